```python
import math
import jax, jax.numpy as jnp
from jax import lax
import numpy as np

D_MODEL = 2048
BATCH = 4
SEQ = 2048
DEPTH = 2
DEC_BATCH = 128
DEC_SEQ = 4
PAST_LEN = 16384
PAGE_SIZE = 128

N_MEM = 256
D_MIX = D_MODEL
D_A = D_MIX // 4
D_B = D_MIX // 4
D_C = D_MIX // 2
CONV_A_WIDTH = 31
S5_GROUP = 16
S5_GROUPS = D_B // S5_GROUP
S5_STATE = 64
SSD_HEAD_DIM = 64
SSD_HEADS = D_C // SSD_HEAD_DIM
SSD_GROUPS = 2
SSD_HPG = SSD_HEADS // SSD_GROUPS
SSD_STATE = 128
SSD_CONV_WIDTH = 4
SSD_CHUNK = 128
D_XBC = D_C + 2 * SSD_GROUPS * SSD_STATE
D_IN = 2 * D_A + D_B + D_C + D_XBC + SSD_HEADS
IN_SPLITS = (D_A, 2 * D_A, 2 * D_A + D_B, 2 * D_A + D_B + D_C, 2 * D_A + D_B + D_C + D_XBC)
XA_HEADS = 4
XA_HEAD_DIM = D_MODEL // XA_HEADS
D_FF = 5504
FFN_CONV_WIDTH = 3
EPS = 1e-6

kernel_name = "hymba_style_conv_s5_ssd_hybrid_step"


def _rmsnorm(x, w):
    xf = x.astype(jnp.float32)
    y = xf * lax.rsqrt(jnp.mean(xf * xf, axis=-1, keepdims=True) + EPS)
    return (y * w.astype(jnp.float32)).astype(x.dtype)


def _layernorm(x, w, b):
    xf = x.astype(jnp.float32)
    mu = jnp.mean(xf, axis=-1, keepdims=True)
    xc = xf - mu
    y = xc * lax.rsqrt(jnp.mean(xc * xc, axis=-1, keepdims=True) + EPS)
    return (y * w.astype(jnp.float32) + b.astype(jnp.float32)).astype(x.dtype)


def _causal_dwconv(x, buf, w, b):
    xp = jnp.concatenate([buf.astype(x.dtype), x], axis=1)
    y = lax.conv_general_dilated(xp, w[:, None, :].astype(x.dtype), window_strides=(1,), padding='VALID',
                                 dimension_numbers=('NWC', 'WIO', 'NWC'), feature_group_count=x.shape[-1])
    return y + b.astype(x.dtype), xp[:, -(w.shape[0] - 1):]


def _conformer_conv(v, g, buf, w, b, ln_w, ln_b):
    u = v * jax.nn.sigmoid(g)
    c, new_buf = _causal_dwconv(u, buf, w, b)
    c = _layernorm(c, ln_w, ln_b)
    return jax.nn.silu(c), new_buf


def _complex_affine_combine(c1, c2):
    a1r, a1i, b1r, b1i = c1
    a2r, a2i, b2r, b2i = c2
    return (a1r * a2r - a1i * a2i,
            a1r * a2i + a1i * a2r,
            a2r * b1r - a2i * b1i + b2r,
            a2r * b1i + a2i * b1r + b2i)


def _s5(u, s_re, s_im, lam_re, lam_im, log_dt, b_re, b_im, c_re, c_im, d, glu_w, glu_b):
    f32 = jnp.float32
    bt, L, _ = u.shape
    uf = u.astype(f32)
    ug = uf.reshape(bt, L, S5_GROUPS, S5_GROUP)
    dt = jnp.exp(log_dt.astype(f32))[:, None]
    lr, li = lam_re.astype(f32), lam_im.astype(f32)
    mag = jnp.exp(lr * dt)
    ang = li * dt
    ab_re, ab_im = mag * jnp.cos(ang), mag * jnp.sin(ang)
    den = lr * lr + li * li
    nr, ni = ab_re - 1.0, ab_im
    co_re = (nr * lr + ni * li) / den
    co_im = (ni * lr - nr * li) / den
    br, bi = b_re.astype(f32), b_im.astype(f32)
    bb_re = co_re[..., None] * br - co_im[..., None] * bi
    bb_im = co_re[..., None] * bi + co_im[..., None] * br
    bu_re = jnp.einsum('blgh,gph->blgp', ug, bb_re)
    bu_im = jnp.einsum('blgh,gph->blgp', ug, bb_im)
    sr, si = s_re.astype(f32), s_im.astype(f32)
    bu_re = bu_re.at[:, 0].add(ab_re * sr - ab_im * si)
    bu_im = bu_im.at[:, 0].add(ab_re * si + ab_im * sr)
    a_re = jnp.broadcast_to(ab_re, bu_re.shape)
    a_im = jnp.broadcast_to(ab_im, bu_im.shape)
    _, _, h_re, h_im = lax.associative_scan(_complex_affine_combine, (a_re, a_im, bu_re, bu_im), axis=1)
    y = (jnp.einsum('blgp,ghp->blgh', h_re, c_re.astype(f32))
         - jnp.einsum('blgp,ghp->blgh', h_im, c_im.astype(f32)))
    y = y.reshape(bt, L, D_B) + d.astype(f32) * uf
    y = jax.nn.gelu(y)
    y = y * jax.nn.sigmoid(y @ glu_w.astype(f32) + glu_b.astype(f32))
    return y.astype(u.dtype), h_re[:, -1], h_im[:, -1]


def _ssd_chunked(x, dt, A, Bm, Cm, s0):
    bt, L, G, R, P = x.shape
    N = Bm.shape[-1]
    Q = SSD_CHUNK if L % SSD_CHUNK == 0 else L
    nc = L // Q
    x = x.reshape(bt, nc, Q, G, R, P)
    dt = dt.reshape(bt, nc, Q, G, R)
    Bm = Bm.reshape(bt, nc, Q, G, N)
    Cm = Cm.reshape(bt, nc, Q, G, N)
    a_cs = jnp.cumsum(dt * A, axis=2)
    xdt = x * dt[..., None]
    causal = jnp.tril(jnp.ones((Q, Q), dtype=bool))[:, :, None, None]
    seg = a_cs[:, :, :, None] - a_cs[:, :, None, :]
    decay = jnp.exp(jnp.where(causal, seg, -jnp.inf))
    cb = jnp.einsum('bclgn,bcsgn->bclsg', Cm, Bm)
    y_diag = jnp.einsum('bclsg,bclsgr,bcsgrp->bclgrp', cb, decay, xdt)
    decay_states = jnp.exp(a_cs[:, :, -1:] - a_cs)
    states = jnp.einsum('bcsgn,bcsgr,bcsgrp->bcgrpn', Bm, decay_states, xdt)
    chunk_decay = jnp.exp(a_cs[:, :, -1])

    def step(s, inp):
        dec, st = inp
        return dec[..., None, None] * s + st, s

    final, prev = lax.scan(step, s0, (jnp.moveaxis(chunk_decay, 1, 0), jnp.moveaxis(states, 1, 0)))
    prev = jnp.moveaxis(prev, 0, 1)
    y_off = jnp.einsum('bclgn,bcgrpn,bclgr->bclgrp', Cm, prev, jnp.exp(a_cs))
    return (y_diag + y_off).reshape(bt, L, G, R, P), final


def _mamba2(z, xbc, dt_raw, conv_buf, ssm_state, conv_w, conv_b, dt_bias, a_log, d, norm_w):
    f32 = jnp.float32
    bt, L, _ = z.shape
    xbc_c, new_buf = _causal_dwconv(xbc, conv_buf, conv_w, conv_b)
    xbc_c = jax.nn.silu(xbc_c).astype(f32)
    xs = xbc_c[..., :D_C].reshape(bt, L, SSD_GROUPS, SSD_HPG, SSD_HEAD_DIM)
    Bm = xbc_c[..., D_C:D_C + SSD_GROUPS * SSD_STATE].reshape(bt, L, SSD_GROUPS, SSD_STATE)
    Cm = xbc_c[..., D_C + SSD_GROUPS * SSD_STATE:].reshape(bt, L, SSD_GROUPS, SSD_STATE)
    dt = jax.nn.softplus(dt_raw.astype(f32) + dt_bias.astype(f32)).reshape(bt, L, SSD_GROUPS, SSD_HPG)
    A = -jnp.exp(a_log.astype(f32)).reshape(SSD_GROUPS, SSD_HPG)
    s0 = ssm_state.astype(f32).reshape(bt, SSD_GROUPS, SSD_HPG, SSD_HEAD_DIM, SSD_STATE)
    y, final = _ssd_chunked(xs, dt, A, Bm, Cm, s0)
    y = y + d.astype(f32).reshape(SSD_GROUPS, SSD_HPG)[..., None] * xs
    y = y.reshape(bt, L, D_C) * jax.nn.silu(z.astype(f32))
    yg = y.reshape(bt, L, SSD_GROUPS, D_C // SSD_GROUPS)
    yg = yg * lax.rsqrt(jnp.mean(yg * yg, axis=-1, keepdims=True) + EPS)
    y = yg.reshape(bt, L, D_C) * norm_w.astype(f32)
    return y.astype(z.dtype), new_buf, final.reshape(bt, SSD_HEADS, SSD_HEAD_DIM, SSD_STATE)


def _cross_attention(h, mem_k, mem_v, wq, wo):
    bt, L, _ = h.shape
    q = (h @ wq).reshape(bt, L, XA_HEADS, XA_HEAD_DIM)
    s = jnp.einsum('blhd,bmhd->bhlm', q, mem_k.astype(h.dtype)).astype(jnp.float32) / math.sqrt(XA_HEAD_DIM)
    p = jax.nn.softmax(s, axis=-1).astype(h.dtype)
    o = jnp.einsum('bhlm,bmhd->blhd', p, mem_v.astype(h.dtype)).reshape(bt, L, D_MODEL)
    return o @ wo


def _conv_ffn(h, buf, w_gate, w_up, conv_w, conv_b, w_down):
    g, new_buf = _causal_dwconv(h @ w_gate, buf, conv_w, conv_b)
    return (jax.nn.silu(g) * (h @ w_up)) @ w_down, new_buf


def _layer(x, mem_k, mem_v, conv_a_buf, s5_re, s5_im, conv_c_buf, ssd_state, ffn_buf, p):
    h = _rmsnorm(x, p['norm_mix_w'])
    a_v, a_g, b_u, c_z, c_xbc, c_dt = jnp.split(h @ p['w_in'], IN_SPLITS, axis=-1)
    ya, conv_a_buf = _conformer_conv(a_v, a_g, conv_a_buf, p['conv_a_w'], p['conv_a_b'], p['ln_a_w'], p['ln_a_b'])
    yb, s5_re, s5_im = _s5(b_u, s5_re, s5_im, p['s5_lam_re'], p['s5_lam_im'], p['s5_log_dt'], p['s5_b_re'],
                           p['s5_b_im'], p['s5_c_re'], p['s5_c_im'], p['s5_d'], p['s5_glu_w'], p['s5_glu_b'])
    yc, conv_c_buf, ssd_state = _mamba2(c_z, c_xbc, c_dt, conv_c_buf, ssd_state, p['conv_c_w'], p['conv_c_b'],
                                        p['ssd_dt_bias'], p['ssd_a_log'], p['ssd_d'], p['ssd_norm_w'])
    x = x + jnp.concatenate([ya, yb, yc], axis=-1) @ p['w_out']
    x = x + _cross_attention(_rmsnorm(x, p['norm_xa_w']), mem_k, mem_v, p['xa_wq'], p['xa_wo'])
    f, ffn_buf = _conv_ffn(_rmsnorm(x, p['norm_ffn_w']), ffn_buf, p['ffn_w_gate'], p['ffn_w_up'],
                           p['ffn_conv_w'], p['ffn_conv_b'], p['ffn_w_down'])
    x = x + f
    return x, (conv_a_buf, s5_re, s5_im, conv_c_buf, ssd_state, ffn_buf)


def setup_inputs(seed: int = 0) -> dict:
    key = jax.random.key(seed)
    ks = iter(jax.random.split(key, 64))
    f32 = jnp.float32

    def nrm(shape, scale):
        return jax.random.normal(next(ks), shape, f32) * scale

    def gain(shape):
        return 1.0 + nrm(shape, 0.02)

    dt0 = jnp.exp(jax.random.uniform(next(ks), (DEPTH, SSD_HEADS), f32, math.log(1e-3), math.log(1e-1)))
    lam_im = jnp.pi * jnp.arange(S5_STATE, dtype=f32)
    return {
        'x_prompt': nrm((BATCH, SEQ, D_MODEL), 1.0),
        'x_sample': nrm((DEC_BATCH, DEC_SEQ, D_MODEL), 1.0),
        'mem_prompt': nrm((BATCH, N_MEM, D_MODEL), 1.0),
        'cache_mem_k': nrm((DEPTH, DEC_BATCH, N_MEM, XA_HEADS, XA_HEAD_DIM), 1.0),
        'cache_mem_v': nrm((DEPTH, DEC_BATCH, N_MEM, XA_HEADS, XA_HEAD_DIM), 1.0),
        'state_conv_a': nrm((DEPTH, DEC_BATCH, CONV_A_WIDTH - 1, D_A), 0.5),
        'state_s5_re': nrm((DEPTH, DEC_BATCH, S5_GROUPS, S5_STATE), 0.1),
        'state_s5_im': nrm((DEPTH, DEC_BATCH, S5_GROUPS, S5_STATE), 0.1),
        'state_conv_c': nrm((DEPTH, DEC_BATCH, SSD_CONV_WIDTH - 1, D_XBC), 1.0),
        'state_ssd': nrm((DEPTH, DEC_BATCH, SSD_HEADS, SSD_HEAD_DIM, SSD_STATE), 0.1),
        'state_ffn_conv': nrm((DEPTH, DEC_BATCH, FFN_CONV_WIDTH - 1, D_FF), 1.0),
        'norm_mix_w': gain((DEPTH, D_MODEL)),
        'w_in': nrm((DEPTH, D_MODEL, D_IN), D_MODEL ** -0.5),
        'conv_a_w': nrm((DEPTH, CONV_A_WIDTH, D_A), CONV_A_WIDTH ** -0.5),
        'conv_a_b': nrm((DEPTH, D_A), 0.02),
        'ln_a_w': gain((DEPTH, D_A)),
        'ln_a_b': nrm((DEPTH, D_A), 0.02),
        's5_lam_re': -0.5 + nrm((DEPTH, S5_GROUPS, S5_STATE), 0.02),
        's5_lam_im': lam_im + nrm((DEPTH, S5_GROUPS, S5_STATE), 0.01),
        's5_log_dt': jax.random.uniform(next(ks), (DEPTH, S5_GROUPS), f32, math.log(1e-3), math.log(1e-1)),
        's5_b_re': nrm((DEPTH, S5_GROUPS, S5_STATE, S5_GROUP), (2 * S5_GROUP) ** -0.5),
        's5_b_im': nrm((DEPTH, S5_GROUPS, S5_STATE, S5_GROUP), (2 * S5_GROUP) ** -0.5),
        's5_c_re': nrm((DEPTH, S5_GROUPS, S5_GROUP, S5_STATE), S5_STATE ** -0.5),
        's5_c_im': nrm((DEPTH, S5_GROUPS, S5_GROUP, S5_STATE), S5_STATE ** -0.5),
        's5_d': nrm((DEPTH, D_B), 1.0),
        's5_glu_w': nrm((DEPTH, D_B, D_B), D_B ** -0.5),
        's5_glu_b': nrm((DEPTH, D_B), 0.02),
        'conv_c_w': nrm((DEPTH, SSD_CONV_WIDTH, D_XBC), SSD_CONV_WIDTH ** -0.5),
        'conv_c_b': nrm((DEPTH, D_XBC), 0.02),
        'ssd_dt_bias': dt0 + jnp.log(-jnp.expm1(-dt0)),
        'ssd_a_log': jnp.log(jax.random.uniform(next(ks), (DEPTH, SSD_HEADS), f32, 1.0, 16.0)),
        'ssd_d': gain((DEPTH, SSD_HEADS)),
        'ssd_norm_w': gain((DEPTH, D_C)),
        'w_out': nrm((DEPTH, D_MIX, D_MODEL), D_MIX ** -0.5),
        'norm_xa_w': gain((DEPTH, D_MODEL)),
        'norm_mem_w': gain((DEPTH, D_MODEL)),
        'xa_wq': nrm((DEPTH, D_MODEL, D_MODEL), D_MODEL ** -0.5),
        'xa_wk': nrm((DEPTH, D_MODEL, D_MODEL), D_MODEL ** -0.5),
        'xa_wv': nrm((DEPTH, D_MODEL, D_MODEL), D_MODEL ** -0.5),
        'xa_wo': nrm((DEPTH, D_MODEL, D_MODEL), D_MODEL ** -0.5),
        'norm_ffn_w': gain((DEPTH, D_MODEL)),
        'ffn_w_gate': nrm((DEPTH, D_MODEL, D_FF), D_MODEL ** -0.5),
        'ffn_w_up': nrm((DEPTH, D_MODEL, D_FF), D_MODEL ** -0.5),
        'ffn_conv_w': nrm((DEPTH, FFN_CONV_WIDTH, D_FF), FFN_CONV_WIDTH ** -0.5),
        'ffn_conv_b': nrm((DEPTH, D_FF), 0.02),
        'ffn_w_down': nrm((DEPTH, D_FF, D_MODEL), D_FF ** -0.5),
        'final_norm_w': gain((D_MODEL,)),
    }


def reference(x_prompt, x_sample, mem_prompt, cache_mem_k, cache_mem_v, state_conv_a, state_s5_re, state_s5_im,
              state_conv_c, state_ssd, state_ffn_conv, norm_mix_w, w_in, conv_a_w, conv_a_b, ln_a_w, ln_a_b,
              s5_lam_re, s5_lam_im, s5_log_dt, s5_b_re, s5_b_im, s5_c_re, s5_c_im, s5_d, s5_glu_w, s5_glu_b,
              conv_c_w, conv_c_b, ssd_dt_bias, ssd_a_log, ssd_d, ssd_norm_w, w_out, norm_xa_w, norm_mem_w,
              xa_wq, xa_wk, xa_wv, xa_wo, norm_ffn_w, ffn_w_gate, ffn_w_up, ffn_conv_w, ffn_conv_b, ffn_w_down,
              final_norm_w):
    bp = x_prompt.shape[0]
    yp, ys = x_prompt, x_sample
    outs_p = [[] for _ in range(8)]
    outs_s = [[] for _ in range(6)]
    for l in range(DEPTH):
        p = {'norm_mix_w': norm_mix_w[l], 'w_in': w_in[l], 'conv_a_w': conv_a_w[l], 'conv_a_b': conv_a_b[l],
             'ln_a_w': ln_a_w[l], 'ln_a_b': ln_a_b[l], 's5_lam_re': s5_lam_re[l], 's5_lam_im': s5_lam_im[l],
             's5_log_dt': s5_log_dt[l], 's5_b_re': s5_b_re[l], 's5_b_im': s5_b_im[l], 's5_c_re': s5_c_re[l],
             's5_c_im': s5_c_im[l], 's5_d': s5_d[l], 's5_glu_w': s5_glu_w[l], 's5_glu_b': s5_glu_b[l],
             'conv_c_w': conv_c_w[l], 'conv_c_b': conv_c_b[l], 'ssd_dt_bias': ssd_dt_bias[l],
             'ssd_a_log': ssd_a_log[l], 'ssd_d': ssd_d[l], 'ssd_norm_w': ssd_norm_w[l], 'w_out': w_out[l],
             'norm_xa_w': norm_xa_w[l], 'xa_wq': xa_wq[l], 'xa_wo': xa_wo[l], 'norm_ffn_w': norm_ffn_w[l],
             'ffn_w_gate': ffn_w_gate[l], 'ffn_w_up': ffn_w_up[l], 'ffn_conv_w': ffn_conv_w[l],
             'ffn_conv_b': ffn_conv_b[l], 'ffn_w_down': ffn_w_down[l]}
        memn = _rmsnorm(mem_prompt, norm_mem_w[l])
        mk = (memn @ xa_wk[l]).reshape(bp, N_MEM, XA_HEADS, XA_HEAD_DIM)
        mv = (memn @ xa_wv[l]).reshape(bp, N_MEM, XA_HEADS, XA_HEAD_DIM)
        yp, st_p = _layer(yp, mk, mv,
                          jnp.zeros((bp, CONV_A_WIDTH - 1, D_A), yp.dtype),
                          jnp.zeros((bp, S5_GROUPS, S5_STATE), jnp.float32),
                          jnp.zeros((bp, S5_GROUPS, S5_STATE), jnp.float32),
                          jnp.zeros((bp, SSD_CONV_WIDTH - 1, D_XBC), yp.dtype),
                          jnp.zeros((bp, SSD_HEADS, SSD_HEAD_DIM, SSD_STATE), jnp.float32),
                          jnp.zeros((bp, FFN_CONV_WIDTH - 1, D_FF), yp.dtype), p)
        ys, st_s = _layer(ys, cache_mem_k[l], cache_mem_v[l], state_conv_a[l], state_s5_re[l], state_s5_im[l],
                          state_conv_c[l], state_ssd[l], state_ffn_conv[l], p)
        for lst, v in zip(outs_p, st_p + (mk, mv)):
            lst.append(v)
        for lst, v in zip(outs_s, st_s):
            lst.append(v)
    y_prompt = _rmsnorm(yp, final_norm_w)
    y_sample = _rmsnorm(ys, final_norm_w)
    (p_conv_a, p_s5_re, p_s5_im, p_conv_c, p_ssd, p_ffn_conv, p_mem_k, p_mem_v) = [jnp.stack(o) for o in outs_p]
    (s_conv_a, s_s5_re, s_s5_im, s_conv_c, s_ssd, s_ffn_conv) = [jnp.stack(o) for o in outs_s]
    return (y_prompt, y_sample, p_conv_a, p_s5_re, p_s5_im, p_conv_c, p_ssd, p_ffn_conv, p_mem_k, p_mem_v,
            s_conv_a, s_s5_re, s_s5_im, s_conv_c, s_ssd, s_ffn_conv)
```

```python
import functools
import math

import jax
import jax.numpy as jnp
from jax import lax
from jax.experimental import pallas as pl
from jax.experimental.pallas import tpu as pltpu

F32 = jnp.float32
BF16 = jnp.bfloat16
EPS = 1e-6

D_MODEL = 2048
D_A = 512
D_B = 512
D_C = 1024
CONV_A_WIDTH = 31
S5_GROUP = 16
S5_GROUPS = 32
S5_STATE = 64
S5_LANES = S5_GROUPS * S5_STATE
SSD_HEAD_DIM = 64
SSD_HEADS = 16
SSD_GROUPS = 2
SSD_STATE = 128
SSD_CONV_WIDTH = 4
SSD_CHUNK = 128
D_XBC = D_C + 2 * SSD_GROUPS * SSD_STATE
D_HC = D_C + D_XBC + 128
XA_HEADS = 4
XA_HEAD_DIM = 512
N_MEM = 256
D_FF = 5504
FFN_CONV_WIDTH = 3

LANE = 128
SUBLANE = 8
VMEM_LIMIT = 56 * 1024 * 1024
FF_TILE = 512
D_FF_PAD = ((D_FF + FF_TILE - 1) // FF_TILE) * FF_TILE


def _round_up(x, m):
    return (x + m - 1) // m * m


def _cp(*sem):
    return pltpu.CompilerParams(dimension_semantics=sem, vmem_limit_bytes=VMEM_LIMIT)


def _dot(a, b):
    return jnp.dot(a, b, preferred_element_type=F32)


def _dot_nt(a, b):
    return lax.dot_general(a, b, (((1,), (1,)), ((), ())), preferred_element_type=F32)


def _dot_tn(a, b):
    return lax.dot_general(a, b, (((0,), (0,)), ((), ())), preferred_element_type=F32)


def _split3(a):
    hi = a.astype(BF16)
    r = a - hi.astype(F32)
    mid = r.astype(BF16)
    lo = (r - mid.astype(F32)).astype(BF16)
    return hi, mid, lo


def _expand(a, e):
    hi, mid, lo = _split3(a)
    return _dot(hi, e) + _dot(mid, e) + _dot(lo, e)


def _sigmoid(x):
    return jax.nn.sigmoid(x)


def _silu(x):
    return x * jax.nn.sigmoid(x)


def _softplus(x):
    return jnp.maximum(x, 0.0) + jnp.log1p(jnp.exp(-jnp.abs(x)))


def _rmsnorm_rows(x, w):
    ms = jnp.mean(x * x, axis=-1, keepdims=True)
    return x * lax.rsqrt(ms + EPS) * w


def _norm_matmul_kernel(x_ref, nw_ref, w_ref, o_ref, xn_ref):
    @pl.when(pl.program_id(1) == 0)
    def _():
        xn_ref[...] = _rmsnorm_rows(x_ref[...], nw_ref[...]).astype(BF16)

    o_ref[...] = _dot(xn_ref[...], w_ref[...]).astype(o_ref.dtype)


def norm_matmul(x, nw, w, *, tm, tn, name):
    m, k = x.shape
    n = w.shape[1]
    return pl.pallas_call(
        _norm_matmul_kernel,
        out_shape=jax.ShapeDtypeStruct((m, n), F32),
        grid=(m // tm, n // tn),
        in_specs=[pl.BlockSpec((tm, k), lambda i, j: (i, 0)),
                  pl.BlockSpec((1, k), lambda i, j: (0, 0)),
                  pl.BlockSpec((k, tn), lambda i, j: (0, j))],
        out_specs=pl.BlockSpec((tm, tn), lambda i, j: (i, j)),
        scratch_shapes=[pltpu.VMEM((tm, k), BF16)],
        compiler_params=_cp("parallel", "arbitrary"),
        name=name,
    )(x, nw.reshape(1, k), w)


def _proj_res_kernel(*refs, n_in):
    a_refs = refs[:n_in]
    w_refs = refs[n_in:2 * n_in]
    res_ref, o_ref = refs[2 * n_in], refs[2 * n_in + 1]
    acc = res_ref[...]
    for a_ref, w_ref in zip(a_refs, w_refs):
        acc = acc + _dot(a_ref[...].astype(BF16), w_ref[...])
    o_ref[...] = acc


def proj_res(a_list, w_list, res, *, tm, name):
    m, n = res.shape
    n_in = len(a_list)
    in_specs = [pl.BlockSpec((tm, a.shape[1]), lambda i: (i, 0)) for a in a_list]
    in_specs += [pl.BlockSpec(w.shape, lambda i: (0, 0)) for w in w_list]
    in_specs += [pl.BlockSpec((tm, n), lambda i: (i, 0))]
    return pl.pallas_call(
        functools.partial(_proj_res_kernel, n_in=n_in),
        out_shape=jax.ShapeDtypeStruct((m, n), F32),
        grid=(m // tm,),
        in_specs=in_specs,
        out_specs=pl.BlockSpec((tm, n), lambda i: (i, 0)),
        compiler_params=_cp("parallel"),
        name=name,
    )(*a_list, *w_list, res)


CONVA_ROW_CHUNK = 32


def _conva_kernel(*refs, nb, lt, n_tiles, has_state):
    if has_state:
        h_ref, w_ref, b_ref, lnw_ref, lnb_ref, st_ref, y_ref, nst_ref, ext_ref = refs
    else:
        h_ref, w_ref, b_ref, lnw_ref, lnb_ref, y_ref, nst_ref, ext_ref = refs
    hist = (CONV_A_WIDTH - 1) * nb
    pad = _round_up(hist, SUBLANE)
    rows = lt * nb
    j = pl.program_id(1)

    @pl.when(j == 0)
    def _():
        if has_state:
            ext_ref[pad - hist:pad, :] = st_ref[0]
        else:
            ext_ref[0:pad, :] = jnp.zeros((pad, D_A), F32)

    ext_ref[pad:pad + rows, :] = h_ref[:, 0:D_A] * _sigmoid(h_ref[:, D_A:2 * D_A])

    bias = b_ref[...]
    lnw = lnw_ref[...]
    lnb = lnb_ref[...]
    rc = CONVA_ROW_CHUNK
    for r0 in range(0, rows, rc):
        acc = jnp.zeros((rc, D_A), F32) + bias
        for k in range(CONV_A_WIDTH):
            acc = acc + w_ref[k:k + 1, :] * ext_ref[pl.ds(pad - hist + k * nb + r0, rc), :]
        mu = jnp.mean(acc, axis=-1, keepdims=True)
        xc = acc - mu
        var = jnp.mean(xc * xc, axis=-1, keepdims=True)
        c = xc * lax.rsqrt(var + EPS) * lnw + lnb
        y_ref[r0:r0 + rc, :] = _silu(c).astype(y_ref.dtype)

    new_hist = ext_ref[pl.ds(pad + rows - hist, hist), :]
    nst_ref[0] = new_hist
    if n_tiles > 1:
        ext_ref[pad - hist:pad, :] = new_hist


def conva_mixer(h_a, w, b, lnw, lnb, state, *, n_seq, nb, lt, n_tiles):
    rows = lt * nb
    hist = (CONV_A_WIDTH - 1) * nb
    pad = _round_up(hist, SUBLANE)
    has_state = state is not None
    wp = jnp.zeros((32, D_A), F32).at[:CONV_A_WIDTH].set(w)
    in_specs = [pl.BlockSpec((rows, 2 * D_A), lambda s, j: (s * n_tiles + j, 0)),
                pl.BlockSpec((32, D_A), lambda s, j: (0, 0)),
                pl.BlockSpec((1, D_A), lambda s, j: (0, 0)),
                pl.BlockSpec((1, D_A), lambda s, j: (0, 0)),
                pl.BlockSpec((1, D_A), lambda s, j: (0, 0))]
    args = [h_a, wp, b.reshape(1, D_A), lnw.reshape(1, D_A), lnb.reshape(1, D_A)]
    if has_state:
        in_specs.append(pl.BlockSpec((1, hist, D_A), lambda s, j: (s, 0, 0)))
        args.append(state)
    return pl.pallas_call(
        functools.partial(_conva_kernel, nb=nb, lt=lt, n_tiles=n_tiles, has_state=has_state),
        out_shape=(jax.ShapeDtypeStruct((h_a.shape[0], D_A), BF16),
                   jax.ShapeDtypeStruct((n_seq, hist, D_A), F32)),
        grid=(n_seq, n_tiles),
        in_specs=in_specs,
        out_specs=(pl.BlockSpec((rows, D_A), lambda s, j: (s * n_tiles + j, 0)),
                   pl.BlockSpec((1, hist, D_A), lambda s, j: (s, 0, 0))),
        scratch_shapes=[pltpu.VMEM((pad + rows, D_A), F32)],
        compiler_params=_cp("parallel", "arbitrary"),
        name="conva_mixer",
    )(*args)


def _gelu_tanh(x):
    return x * (0.5 * (1.0 + jnp.tanh(math.sqrt(2.0 / math.pi) * (x + 0.044715 * (x * x * x)))))


def _s5_kernel(*refs, nb, lt, has_state):
    if has_state:
        (u_ref, bb_ref, ab_ref, cc_ref, d_ref, gw_ref, gb_ref, sre_ref, sim_ref,
         y_ref, nre_ref, nim_ref, hs_ref, cre_ref, cim_ref) = refs
    else:
        (u_ref, bb_ref, ab_ref, cc_ref, d_ref, gw_ref, gb_ref,
         y_ref, nre_ref, nim_ref, hs_ref, cre_ref, cim_ref) = refs
    n = S5_LANES
    j = pl.program_id(1)

    @pl.when(j == 0)
    def _():
        if has_state:
            cre_ref[...] = sre_ref[0]
            cim_ref[...] = sim_ref[0]
        else:
            cre_ref[...] = jnp.zeros(cre_ref.shape, F32)
            cim_ref[...] = jnp.zeros(cim_ref.shape, F32)

    u = u_ref[...]
    hs_ref[...] = _dot(u.astype(BF16), bb_ref[...])
    ab_re = ab_ref[0:1, :]
    ab_im = ab_ref[1:2, :]

    if nb == 1:
        rowid = lax.broadcasted_iota(jnp.int32, (SUBLANE, n), 0)

        def body(i, carry):
            hr, hi = carry
            r0 = pl.multiple_of(i * SUBLANE, SUBLANE)
            blk_re = hs_ref[pl.ds(r0, SUBLANE), 0:n]
            blk_im = hs_ref[pl.ds(r0, SUBLANE), n:2 * n]
            out_re = blk_re
            out_im = blk_im
            for k in range(SUBLANE):
                nr = ab_re * hr - ab_im * hi + blk_re[k:k + 1, :]
                ni = ab_re * hi + ab_im * hr + blk_im[k:k + 1, :]
                hr, hi = nr, ni
                out_re = jnp.where(rowid == k, hr, out_re)
                out_im = jnp.where(rowid == k, hi, out_im)
            hs_ref[pl.ds(r0, SUBLANE), 0:n] = out_re
            hs_ref[pl.ds(r0, SUBLANE), n:2 * n] = out_im
            return hr, hi

        hr, hi = lax.fori_loop(0, lt // SUBLANE, body, (cre_ref[...], cim_ref[...]))
    else:
        hr = cre_ref[...]
        hi = cim_ref[...]
        for t in range(lt):
            rs = slice(t * nb, (t + 1) * nb)
            nr = ab_re * hr - ab_im * hi + hs_ref[rs, 0:n]
            ni = ab_re * hi + ab_im * hr + hs_ref[rs, n:2 * n]
            hr, hi = nr, ni
            hs_ref[rs, 0:n] = hr
            hs_ref[rs, n:2 * n] = hi

    cre_ref[...] = hr
    cim_ref[...] = hi
    nre_ref[0] = hr
    nim_ref[0] = hi

    y = _dot(hs_ref[...].astype(BF16), cc_ref[...]) + d_ref[...] * u
    y = _gelu_tanh(y)
    gate = _dot(y.astype(BF16), gw_ref[...]) + gb_ref[...]
    y_ref[...] = (y * _sigmoid(gate)).astype(y_ref.dtype)


def s5_mixer(h_b, bb, ab, cc, d, gw, gb, s_re, s_im, *, n_seq, nb, lt, n_tiles):
    rows = lt * nb
    n = S5_LANES
    has_state = s_re is not None
    const = lambda s, j: (0, 0)
    in_specs = [pl.BlockSpec((rows, D_B), lambda s, j: (s * n_tiles + j, 0)),
                pl.BlockSpec((D_B, 2 * n), const),
                pl.BlockSpec((2, n), const),
                pl.BlockSpec((2 * n, D_B), const),
                pl.BlockSpec((1, D_B), const),
                pl.BlockSpec((D_B, D_B), const),
                pl.BlockSpec((1, D_B), const)]
    args = [h_b, bb, ab, cc, d.reshape(1, D_B), gw, gb.reshape(1, D_B)]
    if has_state:
        in_specs += [pl.BlockSpec((1, nb, n), lambda s, j: (s, 0, 0))] * 2
        args += [s_re, s_im]
    st_spec = pl.BlockSpec((1, nb, n), lambda s, j: (s, 0, 0))
    return pl.pallas_call(
        functools.partial(_s5_kernel, nb=nb, lt=lt, has_state=has_state),
        out_shape=(jax.ShapeDtypeStruct((h_b.shape[0], D_B), BF16),
                   jax.ShapeDtypeStruct((n_seq, nb, n), F32),
                   jax.ShapeDtypeStruct((n_seq, nb, n), F32)),
        grid=(n_seq, n_tiles),
        in_specs=in_specs,
        out_specs=(pl.BlockSpec((rows, D_B), lambda s, j: (s * n_tiles + j, 0)), st_spec, st_spec),
        scratch_shapes=[pltpu.VMEM((rows, 2 * n), F32),
                        pltpu.VMEM((nb, n), F32),
                        pltpu.VMEM((nb, n), F32)],
        compiler_params=_cp("parallel", "arbitrary"),
        name="s5_mixer",
    )(*args)


def _group_rmsnorm(y, nw):
    half = D_C // SSD_GROUPS
    outs = []
    for g in range(SSD_GROUPS):
        yg = y[:, g * half:(g + 1) * half]
        outs.append(yg * lax.rsqrt(jnp.mean(yg * yg, axis=-1, keepdims=True) + EPS))
    return jnp.concatenate(outs, axis=1) * nw


def _mamba_p_kernel(h_ref, cw_ref, cb_ref, dtb_ref, alog_ref, dexp_ref, nw_ref, e_ref, tril_ref,
                    y_ref, ncst_ref, nsst_ref, ext_ref, st_ref, *, lt, n_tiles):
    q = SSD_CHUNK
    hist = SSD_CONV_WIDTH - 1
    pad = SUBLANE
    half = D_C // SSD_GROUPS
    hpg = SSD_HEADS // SSD_GROUPS
    j = pl.program_id(1)

    @pl.when(j == 0)
    def _():
        ext_ref[0:pad, :] = jnp.zeros((pad, D_XBC), F32)
        st_ref[...] = jnp.zeros(st_ref.shape, F32)

    ext_ref[pad:pad + lt, :] = h_ref[:, D_C:D_C + D_XBC]

    e = e_ref[...]
    tril = tril_ref[...]
    a_neg = -jnp.exp(alog_ref[...])
    li = lax.broadcasted_iota(jnp.int32, (q, q), 0)
    si = lax.broadcasted_iota(jnp.int32, (q, q), 1)
    causal = li >= si
    lane = lax.broadcasted_iota(jnp.int32, (q, LANE), 1)

    for c in range(lt // q):
        r0 = c * q
        acc = jnp.zeros((q, D_XBC), F32) + cb_ref[...]
        for k in range(SSD_CONV_WIDTH):
            acc = acc + cw_ref[k:k + 1, :] * ext_ref[pl.ds(pad - hist + k + r0, q), :]
        xc = _silu(acc)
        xs = xc[:, 0:D_C]
        z = h_ref[r0:r0 + q, 0:D_C]
        dt = _softplus(h_ref[r0:r0 + q, D_C + D_XBC:D_C + D_XBC + LANE] + dtb_ref[...])
        a = dt * a_neg
        hi_, mid_, lo_ = _split3(a)
        cs = _dot(tril, hi_) + _dot(tril, mid_) + _dot(tril, lo_)
        cs_last = cs[q - 1:q, :]
        dt_x = _expand(dt, e)
        ecs_x = _expand(jnp.exp(cs), e)
        edl_x = _expand(jnp.exp(cs_last - cs), e)
        xdt = xs * dt_x
        cs_t = cs.T

        y_parts = []
        for g in range(SSD_GROUPS):
            bm = xc[:, D_C + g * SSD_STATE:D_C + (g + 1) * SSD_STATE]
            cm = xc[:, D_C + SSD_GROUPS * SSD_STATE + g * SSD_STATE:
                    D_C + SSD_GROUPS * SSD_STATE + (g + 1) * SSD_STATE]
            bm16 = bm.astype(BF16)
            cm16 = cm.astype(BF16)
            cb = _dot_nt(cm16, bm16)
            for pr in range(hpg // 2):
                r_even = g * hpg + 2 * pr
                xpair = xdt[:, r_even * SSD_HEAD_DIM:(r_even + 2) * SSD_HEAD_DIM].astype(BF16)
                ys = []
                for r in (r_even, r_even + 1):
                    seg = cs[:, r:r + 1] - cs_t[r:r + 1, :]
                    dec = jnp.exp(jnp.where(causal, seg, -jnp.inf))
                    ys.append(_dot((cb * dec).astype(BF16), xpair))
                y_parts.append(jnp.where(lane < SSD_HEAD_DIM, ys[0], ys[1]))
        y_diag = jnp.concatenate(y_parts, axis=1)
        y_off = jnp.concatenate(
            [_dot(xc[:, D_C + SSD_GROUPS * SSD_STATE + g * SSD_STATE:
                      D_C + SSD_GROUPS * SSD_STATE + (g + 1) * SSD_STATE].astype(BF16),
                  st_ref[:, g * half:(g + 1) * half].astype(BF16)) for g in range(SSD_GROUPS)],
            axis=1) * ecs_x
        y = y_diag + y_off + dexp_ref[...] * xs
        y = y * _silu(z)
        y_ref[r0:r0 + q, :] = _group_rmsnorm(y, nw_ref[...]).astype(y_ref.dtype)

        xw = (xdt * edl_x).astype(BF16)
        dec_row = ecs_x[q - 1:q, :]
        for g in range(SSD_GROUPS):
            bm_t = xc[:, D_C + g * SSD_STATE:D_C + (g + 1) * SSD_STATE].T.astype(BF16)
            upd = _dot(bm_t, xw[:, g * half:(g + 1) * half])
            st_ref[:, g * half:(g + 1) * half] = (
                st_ref[:, g * half:(g + 1) * half] * dec_row[:, g * half:(g + 1) * half] + upd)

    new_hist = ext_ref[pl.ds(pad + lt - hist, hist), :]
    ncst_ref[0] = new_hist
    if n_tiles > 1:
        ext_ref[pad - hist:pad, :] = new_hist

    @pl.when(j == n_tiles - 1)
    def _():
        for blk in range(D_C // LANE):
            nsst_ref[0, blk * LANE:(blk + 1) * LANE, :] = st_ref[:, blk * LANE:(blk + 1) * LANE].T


def _pad_lanes(v, n=LANE):
    return jnp.zeros((1, n), F32).at[0, :v.shape[0]].set(v)


def _ssd_consts():
    head_of_lane = jnp.arange(D_C) // SSD_HEAD_DIM
    e = (jnp.arange(LANE)[:, None] == head_of_lane[None, :]).astype(BF16)
    tril = (jnp.arange(SSD_CHUNK)[:, None] >= jnp.arange(SSD_CHUNK)[None, :]).astype(BF16)
    return e, tril


def mamba_prompt(h_c, cw, cb, dtb, alog, d, nw, *, n_seq, lt, n_tiles):
    e, tril = _ssd_consts()
    hist = SSD_CONV_WIDTH - 1
    const = lambda s, j: (0, 0)
    cwp = jnp.zeros((SUBLANE, D_XBC), F32).at[:SSD_CONV_WIDTH].set(cw)
    return pl.pallas_call(
        functools.partial(_mamba_p_kernel, lt=lt, n_tiles=n_tiles),
        out_shape=(jax.ShapeDtypeStruct((h_c.shape[0], D_C), BF16),
                   jax.ShapeDtypeStruct((n_seq, hist, D_XBC), F32),
                   jax.ShapeDtypeStruct((n_seq, D_C, SSD_STATE), F32)),
        grid=(n_seq, n_tiles),
        in_specs=[pl.BlockSpec((lt, D_HC), lambda s, j: (s * n_tiles + j, 0)),
                  pl.BlockSpec((SUBLANE, D_XBC), const),
                  pl.BlockSpec((1, D_XBC), const),
                  pl.BlockSpec((1, LANE), const),
                  pl.BlockSpec((1, LANE), const),
                  pl.BlockSpec((1, D_C), const),
                  pl.BlockSpec((1, D_C), const),
                  pl.BlockSpec((LANE, D_C), const),
                  pl.BlockSpec((SSD_CHUNK, SSD_CHUNK), const)],
        out_specs=(pl.BlockSpec((lt, D_C), lambda s, j: (s * n_tiles + j, 0)),
                   pl.BlockSpec((1, hist, D_XBC), lambda s, j: (s, 0, 0)),
                   pl.BlockSpec((1, D_C, SSD_STATE), lambda s, j: (s, 0, 0))),
        scratch_shapes=[pltpu.VMEM((SUBLANE + lt, D_XBC), F32),
                        pltpu.VMEM((SSD_STATE, D_C), F32)],
        compiler_params=_cp("parallel", "arbitrary"),
        name="mamba_prompt",
    )(h_c, cwp, cb.reshape(1, D_XBC), _pad_lanes(dtb), _pad_lanes(alog),
      jnp.repeat(d, SSD_HEAD_DIM).reshape(1, D_C), nw.reshape(1, D_C), e, tril)


def _ks(c, k, nb):
    return slice((c * SUBLANE + k) * nb, (c * SUBLANE + k + 1) * nb)


def _slab_put(ref, k, slab, nb):
    for c in range(slab.shape[1] // LANE):
        ref[_ks(c, k, nb), :] = slab[:, c * LANE:(c + 1) * LANE]


def _slab_get(ref, k, n_blocks, nb):
    return jnp.concatenate([ref[_ks(c, k, nb), :] for c in range(n_blocks)], axis=1)


def _seq_get(ref, b, n_blocks, nb):
    return jnp.concatenate(
        [ref[pl.ds(c * SUBLANE * nb + b, SUBLANE, stride=nb), :] for c in range(n_blocks)], axis=1)


def _seq_put(ref, b, val, nb, c0=0):
    for c in range(val.shape[1] // LANE):
        ref[pl.ds((c0 + c) * SUBLANE * nb + b, SUBLANE, stride=nb), :] = val[:, c * LANE:(c + 1) * LANE]


def _mamba_s_kernel(h_ref, cw_ref, cb_ref, dtb_ref, alog_ref, dexp_ref, nw_ref, e_ref, cst_ref, sst_ref,
                    y_ref, ncst_ref, nsst_ref,
                    ext_ref, xs_ref, dt_ref, cs_ref, lhs_ref, rhs_ref, c8_ref, yoff_ref, *, nb, lt, bb):
    hist = (SSD_CONV_WIDTH - 1) * nb
    rows = lt * nb
    half = D_C // SSD_GROUPS
    hpg = SSD_HEADS // SSD_GROUPS
    xblk = D_C // LANE
    hblk = half // LANE
    i = pl.program_id(0)
    n_steps = pl.num_programs(0)
    bc_off = D_C
    cc_off = D_C + SSD_GROUPS * SSD_STATE

    @pl.when(i == 0)
    def _phase1():
        e = e_ref[...]
        ext_ref[0:hist, :] = cst_ref[...]
        ext_ref[hist:hist + rows, :] = h_ref[:, D_C:D_C + D_XBC]
        ncst_ref[...] = ext_ref[rows:rows + hist, :]
        a_neg = -jnp.exp(alog_ref[...])
        lhs_ref[...] = jnp.zeros(lhs_ref.shape, F32)
        rhs_ref[...] = jnp.zeros(rhs_ref.shape, F32)
        c8_ref[...] = jnp.zeros(c8_ref.shape, F32)
        cs = jnp.zeros((nb, LANE), F32)
        for t in range(lt):
            rs = slice(t * nb, (t + 1) * nb)
            acc = jnp.zeros((nb, D_XBC), F32) + cb_ref[...]
            for k in range(SSD_CONV_WIDTH):
                acc = acc + cw_ref[k:k + 1, :] * ext_ref[(t + k) * nb:(t + k + 1) * nb, :]
            xc = _silu(acc)
            xs_ref[rs, :] = xc[:, 0:D_C]
            for g in range(SSD_GROUPS):
                rhs_ref[_ks(2 * g, t, nb), :] = xc[:, bc_off + g * SSD_STATE:bc_off + (g + 1) * SSD_STATE]
            _slab_put(c8_ref, t, xc[:, cc_off:cc_off + SSD_GROUPS * SSD_STATE], nb)
            dt = _softplus(h_ref[rs, D_C + D_XBC:D_C + D_XBC + LANE] + dtb_ref[...])
            dt_ref[rs, :] = dt
            cs = cs + dt * a_neg
            cs_ref[rs, :] = cs
        cs_last = cs
        for t in range(lt):
            rs = slice(t * nb, (t + 1) * nb)
            wt = jnp.exp(cs_last - cs_ref[rs, :]) * dt_ref[rs, :]
            _slab_put(lhs_ref, t, xs_ref[rs, :] * _expand(wt, e), nb)
        dec = _expand(jnp.exp(cs_last), e)
        d_hi = dec.astype(BF16).astype(F32)
        d_r = dec - d_hi
        d_mid = d_r.astype(BF16).astype(F32)
        d_lo = d_r - d_mid
        ones = jnp.ones((nb, SSD_STATE), F32)
        for k, piece in enumerate((d_hi, d_mid, d_lo)):
            _slab_put(lhs_ref, lt + k, piece, nb)
            for g in range(SSD_GROUPS):
                rhs_ref[_ks(2 * g + 1, lt + k, nb), :] = ones

    for jb in range(bb):
        b = i * bb + jb
        l8 = _seq_get(lhs_ref, b, xblk, nb).astype(BF16)
        r8 = _seq_get(rhs_ref, b, 2 * SSD_GROUPS, nb).astype(BF16)
        c8 = _seq_get(c8_ref, b, SSD_GROUPS, nb).astype(BF16)
        for g in range(SSD_GROUPS):
            s = sst_ref[jb, g * half:(g + 1) * half, :]
            yo = _dot_nt(c8[:, g * SSD_STATE:(g + 1) * SSD_STATE], s.astype(BF16))
            _seq_put(yoff_ref, b, yo, nb, c0=g * hblk)
            upd = _dot_tn(l8[:, g * half:(g + 1) * half],
                          r8[:, g * 2 * SSD_STATE:(g + 1) * 2 * SSD_STATE])
            nsst_ref[jb, g * half:(g + 1) * half, :] = upd[:, SSD_STATE:] * s + upd[:, :SSD_STATE]

    @pl.when(i == n_steps - 1)
    def _phase3():
        e = e_ref[...]
        lane = lax.broadcasted_iota(jnp.int32, (nb, LANE), 1)
        for t in range(lt):
            rt = slice(t * nb, (t + 1) * nb)
            cs_t = cs_ref[rt, :]
            y = (_slab_get(yoff_ref, t, xblk, nb) * _expand(jnp.exp(cs_t), e)
                 + dexp_ref[...] * xs_ref[rt, :])
            for s_ in range(t + 1):
                rsl = slice(s_ * nb, (s_ + 1) * nb)
                cbs = []
                for g in range(SSD_GROUPS):
                    cm = c8_ref[_ks(g, t, nb), :]
                    bm = rhs_ref[_ks(2 * g, s_, nb), :]
                    cbs.append(jnp.sum(cm * bm, axis=-1, keepdims=True))
                cb = jnp.where(lane < hpg, cbs[0], cbs[1])
                m = jnp.exp(cs_t - cs_ref[rsl, :]) * dt_ref[rsl, :] * cb
                y = y + _expand(m, e) * xs_ref[rsl, :]
            y = y * _silu(h_ref[rt, 0:D_C])
            y_ref[rt, :] = _group_rmsnorm(y, nw_ref[...]).astype(y_ref.dtype)


def mamba_sample(h_c, cw, cb, dtb, alog, d, nw, cst, sst, *, nb, lt, bb):
    e, _ = _ssd_consts()
    rows = lt * nb
    hist = (SSD_CONV_WIDTH - 1) * nb
    const = lambda i: (0, 0)
    cwp = jnp.zeros((SUBLANE, D_XBC), F32).at[:SSD_CONV_WIDTH].set(cw)
    return pl.pallas_call(
        functools.partial(_mamba_s_kernel, nb=nb, lt=lt, bb=bb),
        out_shape=(jax.ShapeDtypeStruct((rows, D_C), BF16),
                   jax.ShapeDtypeStruct((hist, D_XBC), F32),
                   jax.ShapeDtypeStruct((nb, D_C, SSD_STATE), F32)),
        grid=(nb // bb,),
        in_specs=[pl.BlockSpec((rows, D_HC), const),
                  pl.BlockSpec((SUBLANE, D_XBC), const),
                  pl.BlockSpec((1, D_XBC), const),
                  pl.BlockSpec((1, LANE), const),
                  pl.BlockSpec((1, LANE), const),
                  pl.BlockSpec((1, D_C), const),
                  pl.BlockSpec((1, D_C), const),
                  pl.BlockSpec((LANE, D_C), const),
                  pl.BlockSpec((hist, D_XBC), const),
                  pl.BlockSpec((bb, D_C, SSD_STATE), lambda i: (i, 0, 0))],
        out_specs=(pl.BlockSpec((rows, D_C), const),
                   pl.BlockSpec((hist, D_XBC), const),
                   pl.BlockSpec((bb, D_C, SSD_STATE), lambda i: (i, 0, 0))),
        scratch_shapes=[pltpu.VMEM((hist + rows, D_XBC), F32),
                        pltpu.VMEM((rows, D_C), F32),
                        pltpu.VMEM((rows, LANE), F32),
                        pltpu.VMEM((rows, LANE), F32),
                        pltpu.VMEM((D_C // LANE * SUBLANE * nb, LANE), F32),
                        pltpu.VMEM((2 * SSD_GROUPS * SUBLANE * nb, LANE), F32),
                        pltpu.VMEM((SSD_GROUPS * SUBLANE * nb, LANE), F32),
                        pltpu.VMEM((D_C // LANE * SUBLANE * nb, LANE), F32)],
        compiler_params=_cp("arbitrary"),
        name="mamba_sample",
    )(h_c, cwp, cb.reshape(1, D_XBC), _pad_lanes(dtb), _pad_lanes(alog),
      jnp.repeat(d, SSD_HEAD_DIM).reshape(1, D_C), nw.reshape(1, D_C), e, cst, sst)


def _softmax_rows(s):
    m = jnp.max(s, axis=-1, keepdims=True)
    ex = jnp.exp(s - m)
    return ex / jnp.sum(ex, axis=-1, keepdims=True)


def _attn_heads(q, k_of, v_of):
    outs = []
    for h in range(XA_HEADS):
        hs = slice(h * XA_HEAD_DIM, (h + 1) * XA_HEAD_DIM)
        s = _dot_nt(q[:, hs].astype(BF16), k_of(hs).astype(BF16)) / math.sqrt(XA_HEAD_DIM)
        p = _softmax_rows(s)
        outs.append(_dot(p.astype(BF16), v_of(hs).astype(BF16)))
    return jnp.concatenate(outs, axis=1)


def _attn_p_kernel(q_ref, k_ref, v_ref, o_ref):
    o_ref[...] = _attn_heads(q_ref[...], lambda hs: k_ref[0, :, hs], lambda hs: v_ref[0, :, hs]
                             ).astype(o_ref.dtype)


def attn_prompt(q, k, v, *, n_seq, seq, tq):
    n_tiles = seq // tq
    return pl.pallas_call(
        _attn_p_kernel,
        out_shape=jax.ShapeDtypeStruct(q.shape, BF16),
        grid=(n_seq, n_tiles),
        in_specs=[pl.BlockSpec((tq, D_MODEL), lambda s, j: (s * n_tiles + j, 0)),
                  pl.BlockSpec((1, N_MEM, D_MODEL), lambda s, j: (s, 0, 0)),
                  pl.BlockSpec((1, N_MEM, D_MODEL), lambda s, j: (s, 0, 0))],
        out_specs=pl.BlockSpec((tq, D_MODEL), lambda s, j: (s * n_tiles + j, 0)),
        compiler_params=_cp("parallel", "arbitrary"),
        name="attn_prompt",
    )(q, k, v)


def _attn_s_kernel(q_ref, k_ref, v_ref, o_ref, *, bb):
    for jb in range(bb):
        o_ref[jb] = _attn_heads(q_ref[jb], lambda hs: k_ref[jb, :, hs], lambda hs: v_ref[jb, :, hs])


def attn_sample(q, k, v, *, bb):
    nb, lt, _ = q.shape
    return pl.pallas_call(
        functools.partial(_attn_s_kernel, bb=bb),
        out_shape=jax.ShapeDtypeStruct((nb, lt, D_MODEL), F32),
        grid=(nb // bb,),
        in_specs=[pl.BlockSpec((bb, lt, D_MODEL), lambda i: (i, 0, 0)),
                  pl.BlockSpec((bb, N_MEM, D_MODEL), lambda i: (i, 0, 0)),
                  pl.BlockSpec((bb, N_MEM, D_MODEL), lambda i: (i, 0, 0))],
        out_specs=pl.BlockSpec((bb, lt, D_MODEL), lambda i: (i, 0, 0)),
        compiler_params=_cp("parallel"),
        name="attn_sample",
    )(q, k, v)


def _ffn_kernel(*refs, nb, tiles_per_seq, has_state, final_norm):
    refs = list(refs)
    x_ref, nw_ref, wg_ref, wu_ref, cw_ref, cb_ref, wd_ref = refs[:7]
    pos = 7
    st_ref = None
    if has_state:
        st_ref = refs[pos]
        pos += 1
    fw_ref = None
    if final_norm:
        fw_ref = refs[pos]
        pos += 1
    o_ref, nst_ref, xn_ref, gext_ref, carry_ref = refs[pos:pos + 5]

    hist = (FFN_CONV_WIDTH - 1) * nb
    pad = _round_up(hist, SUBLANE)
    tm = x_ref.shape[0]
    tf = wg_ref.shape[1]
    i = pl.program_id(0)
    f = pl.program_id(1)
    n_f = pl.num_programs(1)

    @pl.when(f == 0)
    def _():
        xn_ref[...] = _rmsnorm_rows(x_ref[...], nw_ref[...]).astype(BF16)

    xn = xn_ref[...]
    g = _dot(xn, wg_ref[...])
    up = _dot(xn, wu_ref[...])

    if tiles_per_seq > 1:
        first = (i % tiles_per_seq) == 0

        @pl.when(first)
        def _():
            if has_state:
                gext_ref[pad - hist:pad, :] = st_ref[0]
            else:
                gext_ref[0:pad, :] = jnp.zeros((pad, tf), F32)

        @pl.when(jnp.logical_not(first))
        def _():
            gext_ref[0:pad, :] = carry_ref[f]
    else:
        if has_state:
            gext_ref[pad - hist:pad, :] = st_ref[0]
        else:
            gext_ref[0:pad, :] = jnp.zeros((pad, tf), F32)

    gext_ref[pad:pad + tm, :] = g
    conv = (cw_ref[0:1, :] * gext_ref[pl.ds(pad - 2 * nb, tm), :]
            + cw_ref[1:2, :] * gext_ref[pl.ds(pad - nb, tm), :]
            + cw_ref[2:3, :] * g + cb_ref[...])
    act = _silu(conv) * up
    contrib = _dot(act.astype(BF16), wd_ref[...])

    @pl.when(f == 0)
    def _():
        o_ref[...] = x_ref[...] + contrib

    @pl.when(f > 0)
    def _():
        o_ref[...] += contrib

    nst_ref[0] = gext_ref[pl.ds(pad + tm - hist, hist), :]
    if tiles_per_seq > 1:
        carry_ref[f] = gext_ref[pl.ds(tm, pad), :]

    if final_norm:
        @pl.when(f == n_f - 1)
        def _():
            o_ref[...] = _rmsnorm_rows(o_ref[...], fw_ref[...])


def conv_ffn(x, nw, wg, wu, cw, cb, wd, state, final_w, *, n_seq, nb, tm, tiles_per_seq):
    m = x.shape[0]
    tf = FF_TILE
    n_f = D_FF_PAD // tf
    hist = (FFN_CONV_WIDTH - 1) * nb
    pad = _round_up(hist, SUBLANE)
    has_state = state is not None
    final_norm = final_w is not None
    in_specs = [pl.BlockSpec((tm, D_MODEL), lambda i, f: (i, 0)),
                pl.BlockSpec((1, D_MODEL), lambda i, f: (0, 0)),
                pl.BlockSpec((D_MODEL, tf), lambda i, f: (0, f)),
                pl.BlockSpec((D_MODEL, tf), lambda i, f: (0, f)),
                pl.BlockSpec((SUBLANE, tf), lambda i, f: (0, f)),
                pl.BlockSpec((1, tf), lambda i, f: (0, f)),
                pl.BlockSpec((tf, D_MODEL), lambda i, f: (f, 0))]
    args = [x, nw.reshape(1, D_MODEL), wg, wu, cw, cb, wd]
    if has_state:
        in_specs.append(pl.BlockSpec((1, hist, tf), lambda i, f: (i // tiles_per_seq, 0, f)))
        args.append(state)
    if final_norm:
        in_specs.append(pl.BlockSpec((1, D_MODEL), lambda i, f: (0, 0)))
        args.append(final_w.reshape(1, D_MODEL))
    return pl.pallas_call(
        functools.partial(_ffn_kernel, nb=nb, tiles_per_seq=tiles_per_seq, has_state=has_state,
                          final_norm=final_norm),
        out_shape=(jax.ShapeDtypeStruct((m, D_MODEL), F32),
                   jax.ShapeDtypeStruct((m // tm, hist, D_FF_PAD), F32)),
        grid=(m // tm, n_f),
        in_specs=in_specs,
        out_specs=(pl.BlockSpec((tm, D_MODEL), lambda i, f: (i, 0)),
                   pl.BlockSpec((1, hist, tf), lambda i, f: (i, 0, f))),
        scratch_shapes=[pltpu.VMEM((tm, D_MODEL), BF16),
                        pltpu.VMEM((pad + tm, tf), F32),
                        pltpu.VMEM((n_f, pad, tf), F32)],
        compiler_params=_cp("arbitrary", "arbitrary"),
        name="conv_ffn",
    )(*args)


def _s5_params(lam_re, lam_im, log_dt, b_re, b_im, c_re, c_im):
    dt = jnp.exp(log_dt)[:, None]
    mag = jnp.exp(lam_re * dt)
    ang = lam_im * dt
    ab_re, ab_im = mag * jnp.cos(ang), mag * jnp.sin(ang)
    den = lam_re * lam_re + lam_im * lam_im
    nr, ni = ab_re - 1.0, ab_im
    co_re = (nr * lam_re + ni * lam_im) / den
    co_im = (ni * lam_re - nr * lam_im) / den
    bb_re = co_re[..., None] * b_re - co_im[..., None] * b_im
    bb_im = co_re[..., None] * b_im + co_im[..., None] * b_re
    eye = jnp.eye(S5_GROUPS, dtype=F32)
    dense_b = lambda m: jnp.einsum('gph,gk->ghkp', m, eye).reshape(D_B, S5_LANES)
    dense_c = lambda m: jnp.einsum('ghp,gk->kpgh', m, eye).reshape(S5_LANES, D_B)
    bb = jnp.concatenate([dense_b(bb_re), dense_b(bb_im)], axis=1).astype(BF16)
    cc = jnp.concatenate([dense_c(c_re), -dense_c(c_im)], axis=0).astype(BF16)
    ab = jnp.stack([ab_re.reshape(S5_LANES), ab_im.reshape(S5_LANES)])
    return bb, ab, cc


def _pad_ff_cols(w):
    return jnp.pad(w, ((0, 0), (0, D_FF_PAD - D_FF)))


def _mix_layer(x, p, l, states, *, n_seq, nb, lt, n_tiles, tm):
    h_a = norm_matmul(x, p['norm_mix_w'][l], p['w_in_a'][l], tm=tm, tn=2 * D_A, name="in_proj_a")
    h_b = norm_matmul(x, p['norm_mix_w'][l], p['w_in_b'][l], tm=tm, tn=D_B, name="in_proj_b")
    h_c = norm_matmul(x, p['norm_mix_w'][l], p['w_in_c'][l], tm=tm, tn=D_HC // 3, name="in_proj_c")
    ya, n_conv_a = conva_mixer(h_a, p['conv_a_w'][l], p['conv_a_b'][l], p['ln_a_w'][l], p['ln_a_b'][l],
                               states.get('conv_a'), n_seq=n_seq, nb=nb, lt=lt, n_tiles=n_tiles)
    yb, n_re, n_im = s5_mixer(h_b, p['s5_bb'][l], p['s5_ab'][l], p['s5_cc'][l], p['s5_d'][l],
                              p['s5_glu_w'][l], p['s5_glu_b'][l], states.get('s5_re'), states.get('s5_im'),
                              n_seq=n_seq, nb=nb, lt=lt, n_tiles=n_tiles)
    return h_c, ya, yb, n_conv_a, n_re, n_im


def kernel(x_prompt, x_sample, mem_prompt, cache_mem_k, cache_mem_v, state_conv_a, state_s5_re, state_s5_im, state_conv_c, state_ssd, state_ffn_conv, norm_mix_w, w_in, conv_a_w, conv_a_b, ln_a_w, ln_a_b, s5_lam_re, s5_lam_im, s5_log_dt, s5_b_re, s5_b_im, s5_c_re, s5_c_im, s5_d, s5_glu_w, s5_glu_b, conv_c_w, conv_c_b, ssd_dt_bias, ssd_a_log, ssd_d, ssd_norm_w, w_out, norm_xa_w, norm_mem_w, xa_wq, xa_wk, xa_wv, xa_wo, norm_ffn_w, ffn_w_gate, ffn_w_up, ffn_conv_w, ffn_conv_b, ffn_w_down, final_norm_w):
    bp, seq, _ = x_prompt.shape
    nbs, lts, _ = x_sample.shape
    depth = w_in.shape[0]
    n_mem = mem_prompt.shape[1]
    lt_p = 512 if seq % 512 == 0 else seq
    n_tiles_p = seq // lt_p
    tm_p = lt_p
    tm_s = lts * nbs
    tm_m = min(512, bp * n_mem)

    sp_a, sp_b, sp_c = 2 * D_A, 2 * D_A + D_B, 2 * D_A + D_B + D_C + D_XBC
    p = {
        'norm_mix_w': norm_mix_w,
        'w_in_a': w_in[:, :, :sp_a].astype(BF16),
        'w_in_b': w_in[:, :, sp_a:sp_b].astype(BF16),
        'w_in_c': jnp.pad(w_in[:, :, sp_b:], ((0, 0), (0, 0), (0, D_HC - (w_in.shape[2] - sp_b)))).astype(BF16),
        'conv_a_w': conv_a_w, 'conv_a_b': conv_a_b, 'ln_a_w': ln_a_w, 'ln_a_b': ln_a_b,
        's5_d': s5_d, 's5_glu_w': s5_glu_w.astype(BF16), 's5_glu_b': s5_glu_b,
    }
    s5p = [_s5_params(s5_lam_re[l], s5_lam_im[l], s5_log_dt[l], s5_b_re[l], s5_b_im[l], s5_c_re[l], s5_c_im[l])
           for l in range(depth)]
    p['s5_bb'] = [t[0] for t in s5p]
    p['s5_ab'] = [t[1] for t in s5p]
    p['s5_cc'] = [t[2] for t in s5p]
    w_out16 = w_out.astype(BF16)
    wq16 = xa_wq.astype(BF16)
    wkv16 = jnp.concatenate([xa_wk, xa_wv], axis=2).astype(BF16)
    wo16 = xa_wo.astype(BF16)
    ff_pad = D_FF_PAD - D_FF
    wg16 = jnp.pad(ffn_w_gate, ((0, 0), (0, 0), (0, ff_pad))).astype(BF16)
    wu16 = jnp.pad(ffn_w_up, ((0, 0), (0, 0), (0, ff_pad))).astype(BF16)
    wd16 = jnp.pad(ffn_w_down, ((0, 0), (0, ff_pad), (0, 0))).astype(BF16)
    fcw = jnp.pad(ffn_conv_w, ((0, 0), (0, SUBLANE - FFN_CONV_WIDTH), (0, ff_pad)))
    fcb = jnp.pad(ffn_conv_b, ((0, 0), (0, ff_pad))).reshape(depth, 1, D_FF_PAD)

    xp = x_prompt.reshape(bp * seq, D_MODEL)
    xs = x_sample.transpose(1, 0, 2).reshape(lts * nbs, D_MODEL)
    mem2d = mem_prompt.reshape(bp * n_mem, D_MODEL)

    outs_p = [[] for _ in range(8)]
    outs_s = [[] for _ in range(6)]
    for l in range(depth):
        last = l == depth - 1
        mkv = norm_matmul(mem2d, norm_mem_w[l], wkv16[l], tm=tm_m, tn=1024, name="mem_kv")
        mk = mkv[:, :D_MODEL].reshape(bp, n_mem, D_MODEL)
        mv = mkv[:, D_MODEL:].reshape(bp, n_mem, D_MODEL)

        h_c, ya, yb, p_conv_a, p_re, p_im = _mix_layer(xp, p, l, {}, n_seq=bp, nb=1, lt=lt_p,
                                                       n_tiles=n_tiles_p, tm=tm_p)
        yc, p_conv_c, p_ssd = mamba_prompt(h_c, conv_c_w[l], conv_c_b[l], ssd_dt_bias[l], ssd_a_log[l],
                                           ssd_d[l], ssd_norm_w[l], n_seq=bp, lt=lt_p, n_tiles=n_tiles_p)
        xp = proj_res([ya, yb, yc], [w_out16[l, :D_A], w_out16[l, D_A:D_A + D_B], w_out16[l, D_A + D_B:]],
                      xp, tm=tm_p, name="out_proj")
        q = norm_matmul(xp, norm_xa_w[l], wq16[l], tm=tm_p, tn=1024, name="q_proj")
        o = attn_prompt(q, mk, mv, n_seq=bp, seq=seq, tq=lt_p)
        xp = proj_res([o], [wo16[l]], xp, tm=tm_p, name="attn_out")
        xp, p_ffn = conv_ffn(xp, norm_ffn_w[l], wg16[l], wu16[l], fcw[l], fcb[l], wd16[l], None,
                             final_norm_w if last else None, n_seq=bp, nb=1, tm=tm_p,
                             tiles_per_seq=seq // tm_p)
        for lst, v in zip(outs_p, (p_conv_a,
                                   p_re.reshape(bp, S5_GROUPS, S5_STATE), p_im.reshape(bp, S5_GROUPS, S5_STATE),
                                   p_conv_c, p_ssd.reshape(bp, SSD_HEADS, SSD_HEAD_DIM, SSD_STATE),
                                   p_ffn[seq // tm_p - 1::seq // tm_p, :, :D_FF],
                                   mk.reshape(bp, n_mem, XA_HEADS, XA_HEAD_DIM),
                                   mv.reshape(bp, n_mem, XA_HEADS, XA_HEAD_DIM))):
            lst.append(v)

        tmaj = lambda a: a.transpose(1, 0, 2).reshape(1, a.shape[1] * nbs, a.shape[2])
        st = {'conv_a': tmaj(state_conv_a[l]),
              's5_re': state_s5_re[l].reshape(1, nbs, S5_LANES),
              's5_im': state_s5_im[l].reshape(1, nbs, S5_LANES)}
        h_c, ya, yb, s_conv_a, s_re, s_im = _mix_layer(xs, p, l, st, n_seq=1, nb=nbs, lt=lts, n_tiles=1, tm=tm_s)
        yc, s_conv_c, s_ssd = mamba_sample(h_c, conv_c_w[l], conv_c_b[l], ssd_dt_bias[l], ssd_a_log[l],
                                           ssd_d[l], ssd_norm_w[l], tmaj(state_conv_c[l])[0],
                                           state_ssd[l].reshape(nbs, D_C, SSD_STATE), nb=nbs, lt=lts, bb=8)
        xs = proj_res([ya, yb, yc], [w_out16[l, :D_A], w_out16[l, D_A:D_A + D_B], w_out16[l, D_A + D_B:]],
                      xs, tm=tm_s, name="out_proj")
        q = norm_matmul(xs, norm_xa_w[l], wq16[l], tm=tm_s, tn=1024, name="q_proj")
        o = attn_sample(q.reshape(lts, nbs, D_MODEL).transpose(1, 0, 2),
                        cache_mem_k[l].reshape(nbs, n_mem, D_MODEL),
                        cache_mem_v[l].reshape(nbs, n_mem, D_MODEL), bb=2)
        o = o.transpose(1, 0, 2).reshape(lts * nbs, D_MODEL)
        xs = proj_res([o], [wo16[l]], xs, tm=tm_s, name="attn_out")
        ffn_st = jnp.pad(tmaj(state_ffn_conv[l]), ((0, 0), (0, 0), (0, ff_pad)))
        xs, s_ffn = conv_ffn(xs, norm_ffn_w[l], wg16[l], wu16[l], fcw[l], fcb[l], wd16[l], ffn_st,
                             final_norm_w if last else None, n_seq=1, nb=nbs, tm=tm_s, tiles_per_seq=1)
        bmaj = lambda a, w: a.reshape(w, nbs, a.shape[-1]).transpose(1, 0, 2)
        for lst, v in zip(outs_s, (bmaj(s_conv_a[0], CONV_A_WIDTH - 1),
                                   s_re.reshape(nbs, S5_GROUPS, S5_STATE), s_im.reshape(nbs, S5_GROUPS, S5_STATE),
                                   bmaj(s_conv_c, SSD_CONV_WIDTH - 1),
                                   s_ssd.reshape(nbs, SSD_HEADS, SSD_HEAD_DIM, SSD_STATE),
                                   bmaj(s_ffn[0], FFN_CONV_WIDTH - 1)[:, :, :D_FF])):
            lst.append(v)

    y_prompt = xp.reshape(bp, seq, D_MODEL)
    y_sample = xs.reshape(lts, nbs, D_MODEL).transpose(1, 0, 2)
    return (y_prompt, y_sample, *[jnp.stack(o) for o in outs_p], *[jnp.stack(o) for o in outs_s])
```

```python
import functools
import math

import jax
import jax.numpy as jnp
from jax import lax
from jax.experimental import pallas as pl
from jax.experimental.pallas import tpu as pltpu

F32 = jnp.float32
BF16 = jnp.bfloat16
EPS = 1e-6

D_MODEL = 2048
D_A = 512
D_B = 512
D_C = 1024
CONV_A_WIDTH = 31
S5_GROUP = 16
S5_GROUPS = 32
S5_STATE = 64
S5_LANES = S5_GROUPS * S5_STATE
SSD_HEAD_DIM = 64
SSD_HEADS = 16
SSD_GROUPS = 2
SSD_STATE = 128
SSD_CONV_WIDTH = 4
SSD_CHUNK = 128
D_XBC = D_C + 2 * SSD_GROUPS * SSD_STATE
D_HC = D_C + D_XBC + 128
XA_HEADS = 4
XA_HEAD_DIM = 512
N_MEM = 256
D_FF = 5504
FFN_CONV_WIDTH = 3

LANE = 128
SUBLANE = 8
VMEM_LIMIT = 56 * 1024 * 1024
FF_TILE = 512
D_FF_PAD = ((D_FF + FF_TILE - 1) // FF_TILE) * FF_TILE


def _round_up(x, m):
    return (x + m - 1) // m * m


def _cp(*sem):
    return pltpu.CompilerParams(dimension_semantics=sem, vmem_limit_bytes=VMEM_LIMIT)


def _dot(a, b):
    return jnp.dot(a, b, preferred_element_type=F32)


def _dot_nt(a, b):
    return lax.dot_general(a, b, (((1,), (1,)), ((), ())), preferred_element_type=F32)


def _dot_tn(a, b):
    return lax.dot_general(a, b, (((0,), (0,)), ((), ())), preferred_element_type=F32)


def _split3(a):
    hi = a.astype(BF16)
    r = a - hi.astype(F32)
    mid = r.astype(BF16)
    lo = (r - mid.astype(F32)).astype(BF16)
    return hi, mid, lo


def _expand(a, e):
    hi, mid, lo = _split3(a)
    return _dot(hi, e) + _dot(mid, e) + _dot(lo, e)


def _sigmoid(x):
    return jax.nn.sigmoid(x)


def _silu(x):
    return x * jax.nn.sigmoid(x)


def _softplus(x):
    return jnp.maximum(x, 0.0) + jnp.log1p(jnp.exp(-jnp.abs(x)))


def _rmsnorm_rows(x, w):
    ms = jnp.mean(x * x, axis=-1, keepdims=True)
    return x * lax.rsqrt(ms + EPS) * w


def _norm_matmul_kernel(x_ref, nw_ref, w_ref, o_ref, xn_ref):
    @pl.when(pl.program_id(1) == 0)
    def _():
        xn_ref[...] = _rmsnorm_rows(x_ref[...], nw_ref[...]).astype(BF16)

    o_ref[...] = _dot(xn_ref[...], w_ref[...]).astype(o_ref.dtype)


def norm_matmul(x, nw, w, *, tm, tn, name):
    m, k = x.shape
    n = w.shape[1]
    return pl.pallas_call(
        _norm_matmul_kernel,
        out_shape=jax.ShapeDtypeStruct((m, n), F32),
        grid=(m // tm, n // tn),
        in_specs=[pl.BlockSpec((tm, k), lambda i, j: (i, 0)),
                  pl.BlockSpec((1, k), lambda i, j: (0, 0)),
                  pl.BlockSpec((k, tn), lambda i, j: (0, j))],
        out_specs=pl.BlockSpec((tm, tn), lambda i, j: (i, j)),
        scratch_shapes=[pltpu.VMEM((tm, k), BF16)],
        compiler_params=_cp("parallel", "arbitrary"),
        name=name,
    )(x, nw.reshape(1, k), w)


def _proj_res_kernel(*refs, n_in):
    a_refs = refs[:n_in]
    w_refs = refs[n_in:2 * n_in]
    res_ref, o_ref = refs[2 * n_in], refs[2 * n_in + 1]
    acc = res_ref[...]
    for a_ref, w_ref in zip(a_refs, w_refs):
        acc = acc + _dot(a_ref[...].astype(BF16), w_ref[...])
    o_ref[...] = acc


def proj_res(a_list, w_list, res, *, tm, name):
    m, n = res.shape
    n_in = len(a_list)
    in_specs = [pl.BlockSpec((tm, a.shape[1]), lambda i: (i, 0)) for a in a_list]
    in_specs += [pl.BlockSpec(w.shape, lambda i: (0, 0)) for w in w_list]
    in_specs += [pl.BlockSpec((tm, n), lambda i: (i, 0))]
    return pl.pallas_call(
        functools.partial(_proj_res_kernel, n_in=n_in),
        out_shape=jax.ShapeDtypeStruct((m, n), F32),
        grid=(m // tm,),
        in_specs=in_specs,
        out_specs=pl.BlockSpec((tm, n), lambda i: (i, 0)),
        compiler_params=_cp("parallel"),
        name=name,
    )(*a_list, *w_list, res)


CONVA_ROW_CHUNK = 32


def _conva_kernel(*refs, nb, lt, n_tiles, has_state):
    if has_state:
        h_ref, w_ref, b_ref, lnw_ref, lnb_ref, st_ref, y_ref, nst_ref, ext_ref = refs
    else:
        h_ref, w_ref, b_ref, lnw_ref, lnb_ref, y_ref, nst_ref, ext_ref = refs
    hist = (CONV_A_WIDTH - 1) * nb
    pad = _round_up(hist, SUBLANE)
    rows = lt * nb
    j = pl.program_id(1)

    @pl.when(j == 0)
    def _():
        if has_state:
            ext_ref[pad - hist:pad, :] = st_ref[0]
        else:
            ext_ref[0:pad, :] = jnp.zeros((pad, D_A), F32)

    ext_ref[pad:pad + rows, :] = h_ref[:, 0:D_A] * _sigmoid(h_ref[:, D_A:2 * D_A])

    bias = b_ref[...]
    lnw = lnw_ref[...]
    lnb = lnb_ref[...]
    rc = CONVA_ROW_CHUNK
    for r0 in range(0, rows, rc):
        acc = jnp.zeros((rc, D_A), F32) + bias
        for k in range(CONV_A_WIDTH):
            acc = acc + w_ref[k:k + 1, :] * ext_ref[pl.ds(pad - hist + k * nb + r0, rc), :]
        mu = jnp.mean(acc, axis=-1, keepdims=True)
        xc = acc - mu
        var = jnp.mean(xc * xc, axis=-1, keepdims=True)
        c = xc * lax.rsqrt(var + EPS) * lnw + lnb
        y_ref[r0:r0 + rc, :] = _silu(c).astype(y_ref.dtype)

    new_hist = ext_ref[pl.ds(pad + rows - hist, hist), :]
    nst_ref[0] = new_hist
    if n_tiles > 1:
        ext_ref[pad - hist:pad, :] = new_hist


def conva_mixer(h_a, w, b, lnw, lnb, state, *, n_seq, nb, lt, n_tiles):
    rows = lt * nb
    hist = (CONV_A_WIDTH - 1) * nb
    pad = _round_up(hist, SUBLANE)
    has_state = state is not None
    wp = jnp.zeros((32, D_A), F32).at[:CONV_A_WIDTH].set(w)
    in_specs = [pl.BlockSpec((rows, 2 * D_A), lambda s, j: (s * n_tiles + j, 0)),
                pl.BlockSpec((32, D_A), lambda s, j: (0, 0)),
                pl.BlockSpec((1, D_A), lambda s, j: (0, 0)),
                pl.BlockSpec((1, D_A), lambda s, j: (0, 0)),
                pl.BlockSpec((1, D_A), lambda s, j: (0, 0))]
    args = [h_a, wp, b.reshape(1, D_A), lnw.reshape(1, D_A), lnb.reshape(1, D_A)]
    if has_state:
        in_specs.append(pl.BlockSpec((1, hist, D_A), lambda s, j: (s, 0, 0)))
        args.append(state)
    return pl.pallas_call(
        functools.partial(_conva_kernel, nb=nb, lt=lt, n_tiles=n_tiles, has_state=has_state),
        out_shape=(jax.ShapeDtypeStruct((h_a.shape[0], D_A), BF16),
                   jax.ShapeDtypeStruct((n_seq, hist, D_A), F32)),
        grid=(n_seq, n_tiles),
        in_specs=in_specs,
        out_specs=(pl.BlockSpec((rows, D_A), lambda s, j: (s * n_tiles + j, 0)),
                   pl.BlockSpec((1, hist, D_A), lambda s, j: (s, 0, 0))),
        scratch_shapes=[pltpu.VMEM((pad + rows, D_A), F32)],
        compiler_params=_cp("parallel", "arbitrary"),
        name="conva_mixer",
    )(*args)


def _gelu_tanh(x):
    return x * (0.5 * (1.0 + jnp.tanh(math.sqrt(2.0 / math.pi) * (x + 0.044715 * (x * x * x)))))


def _s5_kernel(*refs, nb, lt, has_state):
    if has_state:
        (u_ref, bb_ref, ab_ref, cc_ref, d_ref, gw_ref, gb_ref, sre_ref, sim_ref,
         y_ref, nre_ref, nim_ref, hs_ref, cre_ref, cim_ref) = refs
    else:
        (u_ref, bb_ref, ab_ref, cc_ref, d_ref, gw_ref, gb_ref,
         y_ref, nre_ref, nim_ref, hs_ref, cre_ref, cim_ref) = refs
    n = S5_LANES
    j = pl.program_id(1)

    @pl.when(j == 0)
    def _():
        if has_state:
            cre_ref[...] = sre_ref[0]
            cim_ref[...] = sim_ref[0]
        else:
            cre_ref[...] = jnp.zeros(cre_ref.shape, F32)
            cim_ref[...] = jnp.zeros(cim_ref.shape, F32)

    u = u_ref[...]
    hs_ref[...] = _dot(u.astype(BF16), bb_ref[...])
    ab_re = ab_ref[0:1, :]
    ab_im = ab_ref[1:2, :]

    if nb == 1:
        rowid = lax.broadcasted_iota(jnp.int32, (SUBLANE, n), 0)

        def body(i, carry):
            hr, hi = carry
            r0 = pl.multiple_of(i * SUBLANE, SUBLANE)
            blk_re = hs_ref[pl.ds(r0, SUBLANE), 0:n]
            blk_im = hs_ref[pl.ds(r0, SUBLANE), n:2 * n]
            out_re = blk_re
            out_im = blk_im
            for k in range(SUBLANE):
                nr = ab_re * hr - ab_im * hi + blk_re[k:k + 1, :]
                ni = ab_re * hi + ab_im * hr + blk_im[k:k + 1, :]
                hr, hi = nr, ni
                out_re = jnp.where(rowid == k, hr, out_re)
                out_im = jnp.where(rowid == k, hi, out_im)
            hs_ref[pl.ds(r0, SUBLANE), 0:n] = out_re
            hs_ref[pl.ds(r0, SUBLANE), n:2 * n] = out_im
            return hr, hi

        hr, hi = lax.fori_loop(0, lt // SUBLANE, body, (cre_ref[...], cim_ref[...]))
    else:
        hr = cre_ref[...]
        hi = cim_ref[...]
        for t in range(lt):
            rs = slice(t * nb, (t + 1) * nb)
            nr = ab_re * hr - ab_im * hi + hs_ref[rs, 0:n]
            ni = ab_re * hi + ab_im * hr + hs_ref[rs, n:2 * n]
            hr, hi = nr, ni
            hs_ref[rs, 0:n] = hr
            hs_ref[rs, n:2 * n] = hi

    cre_ref[...] = hr
    cim_ref[...] = hi
    nre_ref[0] = hr
    nim_ref[0] = hi

    y = _dot(hs_ref[...].astype(BF16), cc_ref[...]) + d_ref[...] * u
    y = _gelu_tanh(y)
    gate = _dot(y.astype(BF16), gw_ref[...]) + gb_ref[...]
    y_ref[...] = (y * _sigmoid(gate)).astype(y_ref.dtype)


def s5_mixer(h_b, bb, ab, cc, d, gw, gb, s_re, s_im, *, n_seq, nb, lt, n_tiles):
    rows = lt * nb
    n = S5_LANES
    has_state = s_re is not None
    const = lambda s, j: (0, 0)
    in_specs = [pl.BlockSpec((rows, D_B), lambda s, j: (s * n_tiles + j, 0)),
                pl.BlockSpec((D_B, 2 * n), const),
                pl.BlockSpec((2, n), const),
                pl.BlockSpec((2 * n, D_B), const),
                pl.BlockSpec((1, D_B), const),
                pl.BlockSpec((D_B, D_B), const),
                pl.BlockSpec((1, D_B), const)]
    args = [h_b, bb, ab, cc, d.reshape(1, D_B), gw, gb.reshape(1, D_B)]
    if has_state:
        in_specs += [pl.BlockSpec((1, nb, n), lambda s, j: (s, 0, 0))] * 2
        args += [s_re, s_im]
    st_spec = pl.BlockSpec((1, nb, n), lambda s, j: (s, 0, 0))
    return pl.pallas_call(
        functools.partial(_s5_kernel, nb=nb, lt=lt, has_state=has_state),
        out_shape=(jax.ShapeDtypeStruct((h_b.shape[0], D_B), BF16),
                   jax.ShapeDtypeStruct((n_seq, nb, n), F32),
                   jax.ShapeDtypeStruct((n_seq, nb, n), F32)),
        grid=(n_seq, n_tiles),
        in_specs=in_specs,
        out_specs=(pl.BlockSpec((rows, D_B), lambda s, j: (s * n_tiles + j, 0)), st_spec, st_spec),
        scratch_shapes=[pltpu.VMEM((rows, 2 * n), F32),
                        pltpu.VMEM((nb, n), F32),
                        pltpu.VMEM((nb, n), F32)],
        compiler_params=_cp("parallel", "arbitrary"),
        name="s5_mixer",
    )(*args)


def _group_rmsnorm(y, nw):
    half = D_C // SSD_GROUPS
    outs = []
    for g in range(SSD_GROUPS):
        yg = y[:, g * half:(g + 1) * half]
        outs.append(yg * lax.rsqrt(jnp.mean(yg * yg, axis=-1, keepdims=True) + EPS))
    return jnp.concatenate(outs, axis=1) * nw


def _mamba_p_kernel(h_ref, cw_ref, cb_ref, dtb_ref, alog_ref, dexp_ref, nw_ref, e_ref, tril_ref,
                    y_ref, ncst_ref, nsst_ref, ext_ref, st_ref, *, lt, n_tiles):
    q = SSD_CHUNK
    hist = SSD_CONV_WIDTH - 1
    pad = SUBLANE
    half = D_C // SSD_GROUPS
    hpg = SSD_HEADS // SSD_GROUPS
    j = pl.program_id(1)

    @pl.when(j == 0)
    def _():
        ext_ref[0:pad, :] = jnp.zeros((pad, D_XBC), F32)
        st_ref[...] = jnp.zeros(st_ref.shape, F32)

    ext_ref[pad:pad + lt, :] = h_ref[:, D_C:D_C + D_XBC]

    e = e_ref[...]
    tril = tril_ref[...]
    a_neg = -jnp.exp(alog_ref[...])
    li = lax.broadcasted_iota(jnp.int32, (q, q), 0)
    si = lax.broadcasted_iota(jnp.int32, (q, q), 1)
    causal = li >= si
    lane = lax.broadcasted_iota(jnp.int32, (q, LANE), 1)

    for c in range(lt // q):
        r0 = c * q
        acc = jnp.zeros((q, D_XBC), F32) + cb_ref[...]
        for k in range(SSD_CONV_WIDTH):
            acc = acc + cw_ref[k:k + 1, :] * ext_ref[pl.ds(pad - hist + k + r0, q), :]
        xc = _silu(acc)
        xs = xc[:, 0:D_C]
        z = h_ref[r0:r0 + q, 0:D_C]
        dt = _softplus(h_ref[r0:r0 + q, D_C + D_XBC:D_C + D_XBC + LANE] + dtb_ref[...])
        a = dt * a_neg
        hi_, mid_, lo_ = _split3(a)
        cs = _dot(tril, hi_) + _dot(tril, mid_) + _dot(tril, lo_)
        cs_last = cs[q - 1:q, :]
        dt_x = _expand(dt, e)
        ecs_x = _expand(jnp.exp(cs), e)
        edl_x = _expand(jnp.exp(cs_last - cs), e)
        xdt = xs * dt_x
        cs_t = cs.T

        y_parts = []
        for g in range(SSD_GROUPS):
            bm = xc[:, D_C + g * SSD_STATE:D_C + (g + 1) * SSD_STATE]
            cm = xc[:, D_C + SSD_GROUPS * SSD_STATE + g * SSD_STATE:
                    D_C + SSD_GROUPS * SSD_STATE + (g + 1) * SSD_STATE]
            bm16 = bm.astype(BF16)
            cm16 = cm.astype(BF16)
            cb = _dot_nt(cm16, bm16)
            for pr in range(hpg // 2):
                r_even = g * hpg + 2 * pr
                xpair = xdt[:, r_even * SSD_HEAD_DIM:(r_even + 2) * SSD_HEAD_DIM].astype(BF16)
                ys = []
                for r in (r_even, r_even + 1):
                    seg = cs[:, r:r + 1] - cs_t[r:r + 1, :]
                    dec = jnp.exp(jnp.where(causal, seg, -jnp.inf))
                    ys.append(_dot((cb * dec).astype(BF16), xpair))
                y_parts.append(jnp.where(lane < SSD_HEAD_DIM, ys[0], ys[1]))
        y_diag = jnp.concatenate(y_parts, axis=1)
        y_off = jnp.concatenate(
            [_dot(xc[:, D_C + SSD_GROUPS * SSD_STATE + g * SSD_STATE:
                      D_C + SSD_GROUPS * SSD_STATE + (g + 1) * SSD_STATE].astype(BF16),
                  st_ref[:, g * half:(g + 1) * half].astype(BF16)) for g in range(SSD_GROUPS)],
            axis=1) * ecs_x
        y = y_diag + y_off + dexp_ref[...] * xs
        y = y * _silu(z)
        y_ref[r0:r0 + q, :] = _group_rmsnorm(y, nw_ref[...]).astype(y_ref.dtype)

        xw = (xdt * edl_x).astype(BF16)
        dec_row = ecs_x[q - 1:q, :]
        for g in range(SSD_GROUPS):
            bm_t = xc[:, D_C + g * SSD_STATE:D_C + (g + 1) * SSD_STATE].T.astype(BF16)
            upd = _dot(bm_t, xw[:, g * half:(g + 1) * half])
            st_ref[:, g * half:(g + 1) * half] = (
                st_ref[:, g * half:(g + 1) * half] * dec_row[:, g * half:(g + 1) * half] + upd)

    new_hist = ext_ref[pl.ds(pad + lt - hist, hist), :]
    ncst_ref[0] = new_hist
    if n_tiles > 1:
        ext_ref[pad - hist:pad, :] = new_hist

    @pl.when(j == n_tiles - 1)
    def _():
        for blk in range(D_C // LANE):
            nsst_ref[0, blk * LANE:(blk + 1) * LANE, :] = st_ref[:, blk * LANE:(blk + 1) * LANE].T


def _pad_lanes(v, n=LANE):
    return jnp.zeros((1, n), F32).at[0, :v.shape[0]].set(v)


def _ssd_consts():
    head_of_lane = jnp.arange(D_C) // SSD_HEAD_DIM
    e = (jnp.arange(LANE)[:, None] == head_of_lane[None, :]).astype(BF16)
    tril = (jnp.arange(SSD_CHUNK)[:, None] >= jnp.arange(SSD_CHUNK)[None, :]).astype(BF16)
    return e, tril


def mamba_prompt(h_c, cw, cb, dtb, alog, d, nw, *, n_seq, lt, n_tiles):
    e, tril = _ssd_consts()
    hist = SSD_CONV_WIDTH - 1
    const = lambda s, j: (0, 0)
    cwp = jnp.zeros((SUBLANE, D_XBC), F32).at[:SSD_CONV_WIDTH].set(cw)
    return pl.pallas_call(
        functools.partial(_mamba_p_kernel, lt=lt, n_tiles=n_tiles),
        out_shape=(jax.ShapeDtypeStruct((h_c.shape[0], D_C), BF16),
                   jax.ShapeDtypeStruct((n_seq, hist, D_XBC), F32),
                   jax.ShapeDtypeStruct((n_seq, D_C, SSD_STATE), F32)),
        grid=(n_seq, n_tiles),
        in_specs=[pl.BlockSpec((lt, D_HC), lambda s, j: (s * n_tiles + j, 0)),
                  pl.BlockSpec((SUBLANE, D_XBC), const),
                  pl.BlockSpec((1, D_XBC), const),
                  pl.BlockSpec((1, LANE), const),
                  pl.BlockSpec((1, LANE), const),
                  pl.BlockSpec((1, D_C), const),
                  pl.BlockSpec((1, D_C), const),
                  pl.BlockSpec((LANE, D_C), const),
                  pl.BlockSpec((SSD_CHUNK, SSD_CHUNK), const)],
        out_specs=(pl.BlockSpec((lt, D_C), lambda s, j: (s * n_tiles + j, 0)),
                   pl.BlockSpec((1, hist, D_XBC), lambda s, j: (s, 0, 0)),
                   pl.BlockSpec((1, D_C, SSD_STATE), lambda s, j: (s, 0, 0))),
        scratch_shapes=[pltpu.VMEM((SUBLANE + lt, D_XBC), F32),
                        pltpu.VMEM((SSD_STATE, D_C), F32)],
        compiler_params=_cp("parallel", "arbitrary"),
        name="mamba_prompt",
    )(h_c, cwp, cb.reshape(1, D_XBC), _pad_lanes(dtb), _pad_lanes(alog),
      jnp.repeat(d, SSD_HEAD_DIM).reshape(1, D_C), nw.reshape(1, D_C), e, tril)


def _ks(c, k, nb):
    return slice((c * SUBLANE + k) * nb, (c * SUBLANE + k + 1) * nb)


def _slab_put(ref, k, slab, nb):
    for c in range(slab.shape[1] // LANE):
        ref[_ks(c, k, nb), :] = slab[:, c * LANE:(c + 1) * LANE]


def _slab_get(ref, k, n_blocks, nb):
    return jnp.concatenate([ref[_ks(c, k, nb), :] for c in range(n_blocks)], axis=1)


def _seq_get(ref, b, n_blocks, nb):
    return jnp.concatenate(
        [ref[pl.ds(c * SUBLANE * nb + b, SUBLANE, stride=nb), :] for c in range(n_blocks)], axis=1)


def _seq_put(ref, b, val, nb, c0=0):
    for c in range(val.shape[1] // LANE):
        ref[pl.ds((c0 + c) * SUBLANE * nb + b, SUBLANE, stride=nb), :] = val[:, c * LANE:(c + 1) * LANE]


def _mamba_s_kernel(h_ref, cw_ref, cb_ref, dtb_ref, alog_ref, dexp_ref, nw_ref, e_ref, cst_ref, sst_ref,
                    y_ref, ncst_ref, nsst_ref,
                    ext_ref, xs_ref, dt_ref, cs_ref, lhs_ref, rhs_ref, c8_ref, yoff_ref, *, nb, lt, bb):
    hist = (SSD_CONV_WIDTH - 1) * nb
    rows = lt * nb
    half = D_C // SSD_GROUPS
    hpg = SSD_HEADS // SSD_GROUPS
    xblk = D_C // LANE
    hblk = half // LANE
    i = pl.program_id(0)
    n_steps = pl.num_programs(0)
    bc_off = D_C
    cc_off = D_C + SSD_GROUPS * SSD_STATE

    @pl.when(i == 0)
    def _phase1():
        e = e_ref[...]
        ext_ref[0:hist, :] = cst_ref[...]
        ext_ref[hist:hist + rows, :] = h_ref[:, D_C:D_C + D_XBC]
        ncst_ref[...] = ext_ref[rows:rows + hist, :]
        a_neg = -jnp.exp(alog_ref[...])
        lhs_ref[...] = jnp.zeros(lhs_ref.shape, F32)
        rhs_ref[...] = jnp.zeros(rhs_ref.shape, F32)
        c8_ref[...] = jnp.zeros(c8_ref.shape, F32)
        cs = jnp.zeros((nb, LANE), F32)
        for t in range(lt):
            rs = slice(t * nb, (t + 1) * nb)
            acc = jnp.zeros((nb, D_XBC), F32) + cb_ref[...]
            for k in range(SSD_CONV_WIDTH):
                acc = acc + cw_ref[k:k + 1, :] * ext_ref[(t + k) * nb:(t + k + 1) * nb, :]
            xc = _silu(acc)
            xs_ref[rs, :] = xc[:, 0:D_C]
            for g in range(SSD_GROUPS):
                rhs_ref[_ks(2 * g, t, nb), :] = xc[:, bc_off + g * SSD_STATE:bc_off + (g + 1) * SSD_STATE]
            _slab_put(c8_ref, t, xc[:, cc_off:cc_off + SSD_GROUPS * SSD_STATE], nb)
            dt = _softplus(h_ref[rs, D_C + D_XBC:D_C + D_XBC + LANE] + dtb_ref[...])
            dt_ref[rs, :] = dt
            cs = cs + dt * a_neg
            cs_ref[rs, :] = cs
        cs_last = cs
        for t in range(lt):
            rs = slice(t * nb, (t + 1) * nb)
            wt = jnp.exp(cs_last - cs_ref[rs, :]) * dt_ref[rs, :]
            _slab_put(lhs_ref, t, xs_ref[rs, :] * _expand(wt, e), nb)
        dec = _expand(jnp.exp(cs_last), e)
        d_hi = dec.astype(BF16).astype(F32)
        d_r = dec - d_hi
        d_mid = d_r.astype(BF16).astype(F32)
        d_lo = d_r - d_mid
        ones = jnp.ones((nb, SSD_STATE), F32)
        for k, piece in enumerate((d_hi, d_mid, d_lo)):
            _slab_put(lhs_ref, lt + k, piece, nb)
            for g in range(SSD_GROUPS):
                rhs_ref[_ks(2 * g + 1, lt + k, nb), :] = ones

    for jb in range(bb):
        b = i * bb + jb
        l8 = _seq_get(lhs_ref, b, xblk, nb).astype(BF16)
        r8 = _seq_get(rhs_ref, b, 2 * SSD_GROUPS, nb).astype(BF16)
        c8 = _seq_get(c8_ref, b, SSD_GROUPS, nb).astype(BF16)
        for g in range(SSD_GROUPS):
            s = sst_ref[0, jb, g * half:(g + 1) * half, :]
            yo = _dot_nt(c8[:, g * SSD_STATE:(g + 1) * SSD_STATE], s.astype(BF16))
            _seq_put(yoff_ref, b, yo, nb, c0=g * hblk)
            upd = _dot_tn(l8[:, g * half:(g + 1) * half],
                          r8[:, g * 2 * SSD_STATE:(g + 1) * 2 * SSD_STATE])
            nsst_ref[0, jb, g * half:(g + 1) * half, :] = upd[:, SSD_STATE:] * s + upd[:, :SSD_STATE]

    @pl.when(i == n_steps - 1)
    def _phase3():
        e = e_ref[...]
        lane = lax.broadcasted_iota(jnp.int32, (nb, LANE), 1)
        for t in range(lt):
            rt = slice(t * nb, (t + 1) * nb)
            cs_t = cs_ref[rt, :]
            y = (_slab_get(yoff_ref, t, xblk, nb) * _expand(jnp.exp(cs_t), e)
                 + dexp_ref[...] * xs_ref[rt, :])
            for s_ in range(t + 1):
                rsl = slice(s_ * nb, (s_ + 1) * nb)
                cbs = []
                for g in range(SSD_GROUPS):
                    cm = c8_ref[_ks(g, t, nb), :]
                    bm = rhs_ref[_ks(2 * g, s_, nb), :]
                    cbs.append(jnp.sum(cm * bm, axis=-1, keepdims=True))
                cb = jnp.where(lane < hpg, cbs[0], cbs[1])
                m = jnp.exp(cs_t - cs_ref[rsl, :]) * dt_ref[rsl, :] * cb
                y = y + _expand(m, e) * xs_ref[rsl, :]
            y = y * _silu(h_ref[rt, 0:D_C])
            y_ref[rt, :] = _group_rmsnorm(y, nw_ref[...]).astype(y_ref.dtype)


def mamba_sample(h_c, cw, cb, dtb, alog, d, nw, cst, sst, layer, *, nb, lt, bb):
    e, _ = _ssd_consts()
    sst_spec = pl.BlockSpec((1, bb, D_C, SSD_STATE), lambda i: (layer, i, 0, 0))
    rows = lt * nb
    hist = (SSD_CONV_WIDTH - 1) * nb
    const = lambda i: (0, 0)
    cwp = jnp.zeros((SUBLANE, D_XBC), F32).at[:SSD_CONV_WIDTH].set(cw)
    return pl.pallas_call(
        functools.partial(_mamba_s_kernel, nb=nb, lt=lt, bb=bb),
        out_shape=(jax.ShapeDtypeStruct((rows, D_C), BF16),
                   jax.ShapeDtypeStruct((hist, D_XBC), F32),
                   jax.ShapeDtypeStruct(sst.shape, F32)),
        grid=(nb // bb,),
        in_specs=[pl.BlockSpec((rows, D_HC), const),
                  pl.BlockSpec((SUBLANE, D_XBC), const),
                  pl.BlockSpec((1, D_XBC), const),
                  pl.BlockSpec((1, LANE), const),
                  pl.BlockSpec((1, LANE), const),
                  pl.BlockSpec((1, D_C), const),
                  pl.BlockSpec((1, D_C), const),
                  pl.BlockSpec((LANE, D_C), const),
                  pl.BlockSpec((hist, D_XBC), const),
                  sst_spec],
        out_specs=(pl.BlockSpec((rows, D_C), const),
                   pl.BlockSpec((hist, D_XBC), const),
                   sst_spec),
        input_output_aliases={9: 2},
        scratch_shapes=[pltpu.VMEM((hist + rows, D_XBC), F32),
                        pltpu.VMEM((rows, D_C), F32),
                        pltpu.VMEM((rows, LANE), F32),
                        pltpu.VMEM((rows, LANE), F32),
                        pltpu.VMEM((D_C // LANE * SUBLANE * nb, LANE), F32),
                        pltpu.VMEM((2 * SSD_GROUPS * SUBLANE * nb, LANE), F32),
                        pltpu.VMEM((SSD_GROUPS * SUBLANE * nb, LANE), F32),
                        pltpu.VMEM((D_C // LANE * SUBLANE * nb, LANE), F32)],
        compiler_params=_cp("arbitrary"),
        name="mamba_sample",
    )(h_c, cwp, cb.reshape(1, D_XBC), _pad_lanes(dtb), _pad_lanes(alog),
      jnp.repeat(d, SSD_HEAD_DIM).reshape(1, D_C), nw.reshape(1, D_C), e, cst, sst)


def _softmax_rows(s):
    m = jnp.max(s, axis=-1, keepdims=True)
    ex = jnp.exp(s - m)
    return ex / jnp.sum(ex, axis=-1, keepdims=True)


def _attn_heads(q, k_of, v_of):
    outs = []
    for h in range(XA_HEADS):
        hs = slice(h * XA_HEAD_DIM, (h + 1) * XA_HEAD_DIM)
        s = _dot_nt(q[:, hs].astype(BF16), k_of(hs).astype(BF16)) / math.sqrt(XA_HEAD_DIM)
        p = _softmax_rows(s)
        outs.append(_dot(p.astype(BF16), v_of(hs).astype(BF16)))
    return jnp.concatenate(outs, axis=1)


def _attn_p_kernel(q_ref, k_ref, v_ref, o_ref):
    o_ref[...] = _attn_heads(q_ref[...], lambda hs: k_ref[0, :, hs], lambda hs: v_ref[0, :, hs]
                             ).astype(o_ref.dtype)


def attn_prompt(q, k, v, *, n_seq, seq, tq):
    n_tiles = seq // tq
    return pl.pallas_call(
        _attn_p_kernel,
        out_shape=jax.ShapeDtypeStruct(q.shape, BF16),
        grid=(n_seq, n_tiles),
        in_specs=[pl.BlockSpec((tq, D_MODEL), lambda s, j: (s * n_tiles + j, 0)),
                  pl.BlockSpec((1, N_MEM, D_MODEL), lambda s, j: (s, 0, 0)),
                  pl.BlockSpec((1, N_MEM, D_MODEL), lambda s, j: (s, 0, 0))],
        out_specs=pl.BlockSpec((tq, D_MODEL), lambda s, j: (s * n_tiles + j, 0)),
        compiler_params=_cp("parallel", "arbitrary"),
        name="attn_prompt",
    )(q, k, v)


def _attn_s_kernel(q_ref, k_ref, v_ref, o_ref, *, bb, lt):
    rows = XA_HEADS * lt
    n = N_MEM * XA_HEADS
    col_head = lax.broadcasted_iota(jnp.int32, (rows, n), 1) % XA_HEADS
    row_head = lax.broadcasted_iota(jnp.int32, (rows, n), 0) // lt
    same_head = col_head == row_head
    for jb in range(bb):
        k = k_ref[0, jb].reshape(n, XA_HEAD_DIM).astype(BF16)
        v = v_ref[0, jb].reshape(n, XA_HEAD_DIM).astype(BF16)
        s = _dot_nt(q_ref[jb].astype(BF16), k) / math.sqrt(XA_HEAD_DIM)
        p = _softmax_rows(jnp.where(same_head, s, -jnp.inf))
        o_ref[jb] = _dot(p.astype(BF16), v)


def attn_sample(q, k, v, layer, *, bb):
    nb, rows, _ = q.shape
    kv_spec = pl.BlockSpec((1, bb, N_MEM, XA_HEADS, XA_HEAD_DIM), lambda i: (layer, i, 0, 0, 0))
    return pl.pallas_call(
        functools.partial(_attn_s_kernel, bb=bb, lt=rows // XA_HEADS),
        out_shape=jax.ShapeDtypeStruct((nb, rows, XA_HEAD_DIM), F32),
        grid=(nb // bb,),
        in_specs=[pl.BlockSpec((bb, rows, XA_HEAD_DIM), lambda i: (i, 0, 0)), kv_spec, kv_spec],
        out_specs=pl.BlockSpec((bb, rows, XA_HEAD_DIM), lambda i: (i, 0, 0)),
        compiler_params=_cp("parallel"),
        name="attn_sample",
    )(q, k, v)


def _ffn_kernel(*refs, nb, tiles_per_seq, has_state, final_norm):
    refs = list(refs)
    x_ref, nw_ref, wg_ref, wu_ref, cw_ref, cb_ref, wd_ref = refs[:7]
    pos = 7
    st_ref = None
    if has_state:
        st_ref = refs[pos]
        pos += 1
    fw_ref = None
    if final_norm:
        fw_ref = refs[pos]
        pos += 1
    o_ref, nst_ref, xn_ref, gext_ref, carry_ref = refs[pos:pos + 5]

    hist = (FFN_CONV_WIDTH - 1) * nb
    pad = _round_up(hist, SUBLANE)
    tm = x_ref.shape[0]
    tf = wg_ref.shape[1]
    i = pl.program_id(0)
    f = pl.program_id(1)
    n_f = pl.num_programs(1)

    @pl.when(f == 0)
    def _():
        xn_ref[...] = _rmsnorm_rows(x_ref[...], nw_ref[...]).astype(BF16)

    xn = xn_ref[...]
    g = _dot(xn, wg_ref[...])
    up = _dot(xn, wu_ref[...])

    if tiles_per_seq > 1:
        first = (i % tiles_per_seq) == 0

        @pl.when(first)
        def _():
            if has_state:
                gext_ref[pad - hist:pad, :] = st_ref[0]
            else:
                gext_ref[0:pad, :] = jnp.zeros((pad, tf), F32)

        @pl.when(jnp.logical_not(first))
        def _():
            gext_ref[0:pad, :] = carry_ref[f]
    else:
        if has_state:
            gext_ref[pad - hist:pad, :] = st_ref[0]
        else:
            gext_ref[0:pad, :] = jnp.zeros((pad, tf), F32)

    gext_ref[pad:pad + tm, :] = g
    conv = (cw_ref[0:1, :] * gext_ref[pl.ds(pad - 2 * nb, tm), :]
            + cw_ref[1:2, :] * gext_ref[pl.ds(pad - nb, tm), :]
            + cw_ref[2:3, :] * g + cb_ref[...])
    act = _silu(conv) * up
    contrib = _dot(act.astype(BF16), wd_ref[...])

    @pl.when(f == 0)
    def _():
        o_ref[...] = x_ref[...] + contrib

    @pl.when(f > 0)
    def _():
        o_ref[...] += contrib

    nst_ref[0] = gext_ref[pl.ds(pad + tm - hist, hist), :]
    if tiles_per_seq > 1:
        carry_ref[f] = gext_ref[pl.ds(tm, pad), :]

    if final_norm:
        @pl.when(f == n_f - 1)
        def _():
            o_ref[...] = _rmsnorm_rows(o_ref[...], fw_ref[...])


def conv_ffn(x, nw, wg, wu, cw, cb, wd, state, final_w, *, n_seq, nb, tm, tiles_per_seq):
    m = x.shape[0]
    tf = FF_TILE
    n_f = D_FF_PAD // tf
    hist = (FFN_CONV_WIDTH - 1) * nb
    pad = _round_up(hist, SUBLANE)
    has_state = state is not None
    final_norm = final_w is not None
    in_specs = [pl.BlockSpec((tm, D_MODEL), lambda i, f: (i, 0)),
                pl.BlockSpec((1, D_MODEL), lambda i, f: (0, 0)),
                pl.BlockSpec((D_MODEL, tf), lambda i, f: (0, f)),
                pl.BlockSpec((D_MODEL, tf), lambda i, f: (0, f)),
                pl.BlockSpec((SUBLANE, tf), lambda i, f: (0, f)),
                pl.BlockSpec((1, tf), lambda i, f: (0, f)),
                pl.BlockSpec((tf, D_MODEL), lambda i, f: (f, 0))]
    args = [x, nw.reshape(1, D_MODEL), wg, wu, cw, cb, wd]
    if has_state:
        in_specs.append(pl.BlockSpec((1, hist, tf), lambda i, f: (i // tiles_per_seq, 0, f)))
        args.append(state)
    if final_norm:
        in_specs.append(pl.BlockSpec((1, D_MODEL), lambda i, f: (0, 0)))
        args.append(final_w.reshape(1, D_MODEL))
    return pl.pallas_call(
        functools.partial(_ffn_kernel, nb=nb, tiles_per_seq=tiles_per_seq, has_state=has_state,
                          final_norm=final_norm),
        out_shape=(jax.ShapeDtypeStruct((m, D_MODEL), F32),
                   jax.ShapeDtypeStruct((m // tm, hist, D_FF_PAD), F32)),
        grid=(m // tm, n_f),
        in_specs=in_specs,
        out_specs=(pl.BlockSpec((tm, D_MODEL), lambda i, f: (i, 0)),
                   pl.BlockSpec((1, hist, tf), lambda i, f: (i, 0, f))),
        scratch_shapes=[pltpu.VMEM((tm, D_MODEL), BF16),
                        pltpu.VMEM((pad + tm, tf), F32),
                        pltpu.VMEM((n_f, pad, tf), F32)],
        compiler_params=_cp("arbitrary", "arbitrary"),
        name="conv_ffn",
    )(*args)


def _s5_params(lam_re, lam_im, log_dt, b_re, b_im, c_re, c_im):
    dt = jnp.exp(log_dt)[:, None]
    mag = jnp.exp(lam_re * dt)
    ang = lam_im * dt
    ab_re, ab_im = mag * jnp.cos(ang), mag * jnp.sin(ang)
    den = lam_re * lam_re + lam_im * lam_im
    nr, ni = ab_re - 1.0, ab_im
    co_re = (nr * lam_re + ni * lam_im) / den
    co_im = (ni * lam_re - nr * lam_im) / den
    bb_re = co_re[..., None] * b_re - co_im[..., None] * b_im
    bb_im = co_re[..., None] * b_im + co_im[..., None] * b_re
    eye = jnp.eye(S5_GROUPS, dtype=F32)
    dense_b = lambda m: jnp.einsum('gph,gk->ghkp', m, eye).reshape(D_B, S5_LANES)
    dense_c = lambda m: jnp.einsum('ghp,gk->kpgh', m, eye).reshape(S5_LANES, D_B)
    bb = jnp.concatenate([dense_b(bb_re), dense_b(bb_im)], axis=1).astype(BF16)
    cc = jnp.concatenate([dense_c(c_re), -dense_c(c_im)], axis=0).astype(BF16)
    ab = jnp.stack([ab_re.reshape(S5_LANES), ab_im.reshape(S5_LANES)])
    return bb, ab, cc


def _pad_ff_cols(w):
    return jnp.pad(w, ((0, 0), (0, D_FF_PAD - D_FF)))


def _mix_layer(x, p, l, states, *, n_seq, nb, lt, n_tiles, tm):
    h_a = norm_matmul(x, p['norm_mix_w'][l], p['w_in_a'][l], tm=tm, tn=2 * D_A, name="in_proj_a")
    h_b = norm_matmul(x, p['norm_mix_w'][l], p['w_in_b'][l], tm=tm, tn=D_B, name="in_proj_b")
    h_c = norm_matmul(x, p['norm_mix_w'][l], p['w_in_c'][l], tm=tm, tn=D_HC // 3, name="in_proj_c")
    ya, n_conv_a = conva_mixer(h_a, p['conv_a_w'][l], p['conv_a_b'][l], p['ln_a_w'][l], p['ln_a_b'][l],
                               states.get('conv_a'), n_seq=n_seq, nb=nb, lt=lt, n_tiles=n_tiles)
    yb, n_re, n_im = s5_mixer(h_b, p['s5_bb'][l], p['s5_ab'][l], p['s5_cc'][l], p['s5_d'][l],
                              p['s5_glu_w'][l], p['s5_glu_b'][l], states.get('s5_re'), states.get('s5_im'),
                              n_seq=n_seq, nb=nb, lt=lt, n_tiles=n_tiles)
    return h_c, ya, yb, n_conv_a, n_re, n_im


def kernel(x_prompt, x_sample, mem_prompt, cache_mem_k, cache_mem_v, state_conv_a, state_s5_re, state_s5_im, state_conv_c, state_ssd, state_ffn_conv, norm_mix_w, w_in, conv_a_w, conv_a_b, ln_a_w, ln_a_b, s5_lam_re, s5_lam_im, s5_log_dt, s5_b_re, s5_b_im, s5_c_re, s5_c_im, s5_d, s5_glu_w, s5_glu_b, conv_c_w, conv_c_b, ssd_dt_bias, ssd_a_log, ssd_d, ssd_norm_w, w_out, norm_xa_w, norm_mem_w, xa_wq, xa_wk, xa_wv, xa_wo, norm_ffn_w, ffn_w_gate, ffn_w_up, ffn_conv_w, ffn_conv_b, ffn_w_down, final_norm_w):
    bp, seq, _ = x_prompt.shape
    nbs, lts, _ = x_sample.shape
    depth = w_in.shape[0]
    n_mem = mem_prompt.shape[1]
    lt_p = 512 if seq % 512 == 0 else seq
    n_tiles_p = seq // lt_p
    tm_p = lt_p
    tm_s = lts * nbs
    tm_m = min(512, bp * n_mem)

    sp_a, sp_b, sp_c = 2 * D_A, 2 * D_A + D_B, 2 * D_A + D_B + D_C + D_XBC
    p = {
        'norm_mix_w': norm_mix_w,
        'w_in_a': w_in[:, :, :sp_a].astype(BF16),
        'w_in_b': w_in[:, :, sp_a:sp_b].astype(BF16),
        'w_in_c': jnp.pad(w_in[:, :, sp_b:], ((0, 0), (0, 0), (0, D_HC - (w_in.shape[2] - sp_b)))).astype(BF16),
        'conv_a_w': conv_a_w, 'conv_a_b': conv_a_b, 'ln_a_w': ln_a_w, 'ln_a_b': ln_a_b,
        's5_d': s5_d, 's5_glu_w': s5_glu_w.astype(BF16), 's5_glu_b': s5_glu_b,
    }
    s5p = [_s5_params(s5_lam_re[l], s5_lam_im[l], s5_log_dt[l], s5_b_re[l], s5_b_im[l], s5_c_re[l], s5_c_im[l])
           for l in range(depth)]
    p['s5_bb'] = [t[0] for t in s5p]
    p['s5_ab'] = [t[1] for t in s5p]
    p['s5_cc'] = [t[2] for t in s5p]
    w_out16 = w_out.astype(BF16)
    wq16 = xa_wq.astype(BF16)
    wkv16 = jnp.concatenate([xa_wk, xa_wv], axis=2).astype(BF16)
    wo16 = xa_wo.astype(BF16)
    ff_pad = D_FF_PAD - D_FF
    wg16 = jnp.pad(ffn_w_gate, ((0, 0), (0, 0), (0, ff_pad))).astype(BF16)
    wu16 = jnp.pad(ffn_w_up, ((0, 0), (0, 0), (0, ff_pad))).astype(BF16)
    wd16 = jnp.pad(ffn_w_down, ((0, 0), (0, ff_pad), (0, 0))).astype(BF16)
    fcw = jnp.pad(ffn_conv_w, ((0, 0), (0, SUBLANE - FFN_CONV_WIDTH), (0, ff_pad)))
    fcb = jnp.pad(ffn_conv_b, ((0, 0), (0, ff_pad))).reshape(depth, 1, D_FF_PAD)

    xp = x_prompt.reshape(bp * seq, D_MODEL)
    xs = x_sample.transpose(1, 0, 2).reshape(lts * nbs, D_MODEL)
    mem2d = mem_prompt.reshape(bp * n_mem, D_MODEL)

    outs_p = [[] for _ in range(8)]
    outs_s = [[] for _ in range(5)]
    ssd_all = state_ssd.reshape(depth, nbs, D_C, SSD_STATE)
    for l in range(depth):
        last = l == depth - 1
        mkv = norm_matmul(mem2d, norm_mem_w[l], wkv16[l], tm=tm_m, tn=1024, name="mem_kv")
        mk = mkv[:, :D_MODEL].reshape(bp, n_mem, D_MODEL)
        mv = mkv[:, D_MODEL:].reshape(bp, n_mem, D_MODEL)

        h_c, ya, yb, p_conv_a, p_re, p_im = _mix_layer(xp, p, l, {}, n_seq=bp, nb=1, lt=lt_p,
                                                       n_tiles=n_tiles_p, tm=tm_p)
        yc, p_conv_c, p_ssd = mamba_prompt(h_c, conv_c_w[l], conv_c_b[l], ssd_dt_bias[l], ssd_a_log[l],
                                           ssd_d[l], ssd_norm_w[l], n_seq=bp, lt=lt_p, n_tiles=n_tiles_p)
        xp = proj_res([ya, yb, yc], [w_out16[l, :D_A], w_out16[l, D_A:D_A + D_B], w_out16[l, D_A + D_B:]],
                      xp, tm=tm_p, name="out_proj")
        q = norm_matmul(xp, norm_xa_w[l], wq16[l], tm=tm_p, tn=1024, name="q_proj")
        o = attn_prompt(q, mk, mv, n_seq=bp, seq=seq, tq=lt_p)
        xp = proj_res([o], [wo16[l]], xp, tm=tm_p, name="attn_out")
        xp, p_ffn = conv_ffn(xp, norm_ffn_w[l], wg16[l], wu16[l], fcw[l], fcb[l], wd16[l], None,
                             final_norm_w if last else None, n_seq=bp, nb=1, tm=tm_p,
                             tiles_per_seq=seq // tm_p)
        for lst, v in zip(outs_p, (p_conv_a,
                                   p_re.reshape(bp, S5_GROUPS, S5_STATE), p_im.reshape(bp, S5_GROUPS, S5_STATE),
                                   p_conv_c, p_ssd.reshape(bp, SSD_HEADS, SSD_HEAD_DIM, SSD_STATE),
                                   p_ffn[seq // tm_p - 1::seq // tm_p, :, :D_FF],
                                   mk.reshape(bp, n_mem, XA_HEADS, XA_HEAD_DIM),
                                   mv.reshape(bp, n_mem, XA_HEADS, XA_HEAD_DIM))):
            lst.append(v)

        tmaj = lambda a: a.transpose(1, 0, 2).reshape(1, a.shape[1] * nbs, a.shape[2])
        st = {'conv_a': tmaj(state_conv_a[l]),
              's5_re': state_s5_re[l].reshape(1, nbs, S5_LANES),
              's5_im': state_s5_im[l].reshape(1, nbs, S5_LANES)}
        h_c, ya, yb, s_conv_a, s_re, s_im = _mix_layer(xs, p, l, st, n_seq=1, nb=nbs, lt=lts, n_tiles=1, tm=tm_s)
        yc, s_conv_c, ssd_all = mamba_sample(h_c, conv_c_w[l], conv_c_b[l], ssd_dt_bias[l], ssd_a_log[l],
                                             ssd_d[l], ssd_norm_w[l], tmaj(state_conv_c[l])[0],
                                             ssd_all, l, nb=nbs, lt=lts, bb=8)
        xs = proj_res([ya, yb, yc], [w_out16[l, :D_A], w_out16[l, D_A:D_A + D_B], w_out16[l, D_A + D_B:]],
                      xs, tm=tm_s, name="out_proj")
        q = norm_matmul(xs, norm_xa_w[l], wq16[l], tm=tm_s, tn=1024, name="q_proj")
        q = q.reshape(lts, nbs, XA_HEADS, XA_HEAD_DIM).transpose(1, 2, 0, 3)
        o = attn_sample(q.reshape(nbs, XA_HEADS * lts, XA_HEAD_DIM), cache_mem_k, cache_mem_v, l, bb=4)
        o = o.reshape(nbs, XA_HEADS, lts, XA_HEAD_DIM).transpose(2, 0, 1, 3).reshape(lts * nbs, D_MODEL)
        xs = proj_res([o], [wo16[l]], xs, tm=tm_s, name="attn_out")
        ffn_st = jnp.pad(tmaj(state_ffn_conv[l]), ((0, 0), (0, 0), (0, ff_pad)))
        xs, s_ffn = conv_ffn(xs, norm_ffn_w[l], wg16[l], wu16[l], fcw[l], fcb[l], wd16[l], ffn_st,
                             final_norm_w if last else None, n_seq=1, nb=nbs, tm=tm_s, tiles_per_seq=1)
        bmaj = lambda a, w: a.reshape(w, nbs, a.shape[-1]).transpose(1, 0, 2)
        for lst, v in zip(outs_s, (bmaj(s_conv_a[0], CONV_A_WIDTH - 1),
                                   s_re.reshape(nbs, S5_GROUPS, S5_STATE), s_im.reshape(nbs, S5_GROUPS, S5_STATE),
                                   bmaj(s_conv_c, SSD_CONV_WIDTH - 1),
                                   bmaj(s_ffn[0], FFN_CONV_WIDTH - 1)[:, :, :D_FF])):
            lst.append(v)

    y_prompt = xp.reshape(bp, seq, D_MODEL)
    y_sample = xs.reshape(lts, nbs, D_MODEL).transpose(1, 0, 2)
    s_outs = [jnp.stack(o) for o in outs_s]
    s_outs.insert(4, ssd_all.reshape(state_ssd.shape))
    return (y_prompt, y_sample, *[jnp.stack(o) for o in outs_p], *s_outs)
```

```python
import functools
import math

import jax
import jax.numpy as jnp
from jax import lax
from jax.experimental import pallas as pl
from jax.experimental.pallas import tpu as pltpu

F32 = jnp.float32
BF16 = jnp.bfloat16
EPS = 1e-6

D_MODEL = 2048
D_A = 512
D_B = 512
D_C = 1024
CONV_A_WIDTH = 31
S5_GROUP = 16
S5_GROUPS = 32
S5_STATE = 64
S5_LANES = S5_GROUPS * S5_STATE
SSD_HEAD_DIM = 64
SSD_HEADS = 16
SSD_GROUPS = 2
SSD_STATE = 128
SSD_CONV_WIDTH = 4
SSD_CHUNK = 128
D_XBC = D_C + 2 * SSD_GROUPS * SSD_STATE
D_HC = D_C + D_XBC + 128
XA_HEADS = 4
XA_HEAD_DIM = 512
N_MEM = 256
D_FF = 5504
FFN_CONV_WIDTH = 3

LANE = 128
SUBLANE = 8
VMEM_LIMIT = 56 * 1024 * 1024
FF_TILE = 512
D_FF_PAD = ((D_FF + FF_TILE - 1) // FF_TILE) * FF_TILE
FFN_ROW_SPLIT = 256


def _round_up(x, m):
    return (x + m - 1) // m * m


def _cp(*sem):
    return pltpu.CompilerParams(dimension_semantics=sem, vmem_limit_bytes=VMEM_LIMIT)


def _layer_spec(tail, layer):
    zeros = (0,) * len(tail)
    return pl.BlockSpec((1,) + tuple(tail), lambda *_: (layer,) + zeros)


def _dot(a, b):
    return jnp.dot(a, b, preferred_element_type=F32)


def _dot_nt(a, b):
    return lax.dot_general(a, b, (((1,), (1,)), ((), ())), preferred_element_type=F32)


def _dot_tn(a, b):
    return lax.dot_general(a, b, (((0,), (0,)), ((), ())), preferred_element_type=F32)


def _split3(a):
    hi = a.astype(BF16)
    r = a - hi.astype(F32)
    mid = r.astype(BF16)
    lo = (r - mid.astype(F32)).astype(BF16)
    return hi, mid, lo


def _expand(a, e):
    hi, mid, lo = _split3(a)
    return _dot(hi, e) + _dot(mid, e) + _dot(lo, e)


def _sigmoid(x):
    return jax.nn.sigmoid(x)


def _silu(x):
    return x * jax.nn.sigmoid(x)


def _softplus(x):
    return jnp.maximum(x, 0.0) + jnp.log1p(jnp.exp(-jnp.abs(x)))


def _rmsnorm_rows(x, w):
    ms = jnp.mean(x * x, axis=-1, keepdims=True)
    return x * lax.rsqrt(ms + EPS) * w


def _norm_matmul_kernel(x_ref, nw_ref, w_ref, o_ref, xn_ref):
    @pl.when(pl.program_id(1) == 0)
    def _():
        xn_ref[...] = _rmsnorm_rows(x_ref[...], nw_ref[0]).astype(BF16)

    o_ref[...] = _dot(xn_ref[...], w_ref[0]).astype(o_ref.dtype)


def norm_matmul(x, nw, w, layer, *, tm, tn, name):
    m, k = x.shape
    n = w.shape[2]
    return pl.pallas_call(
        _norm_matmul_kernel,
        out_shape=jax.ShapeDtypeStruct((m, n), F32),
        grid=(m // tm, n // tn),
        in_specs=[pl.BlockSpec((tm, k), lambda i, j: (i, 0)),
                  _layer_spec((1, k), layer),
                  pl.BlockSpec((1, k, tn), lambda i, j: (layer, 0, j))],
        out_specs=pl.BlockSpec((tm, tn), lambda i, j: (i, j)),
        scratch_shapes=[pltpu.VMEM((tm, k), BF16)],
        compiler_params=_cp("parallel", "arbitrary"),
        name=name,
    )(x, nw, w)


def _proj_res_kernel(*refs, n_in):
    a_refs = refs[:n_in]
    w_refs = refs[n_in:2 * n_in]
    res_ref, o_ref = refs[2 * n_in], refs[2 * n_in + 1]
    acc = res_ref[...]
    for a_ref, w_ref in zip(a_refs, w_refs):
        acc = acc + _dot(a_ref[...].astype(BF16), w_ref[0])
    o_ref[...] = acc


def proj_res(a_list, w, layer, res, *, tm, name):
    m, n = res.shape
    n_in = len(a_list)
    in_specs = [pl.BlockSpec((tm, a.shape[1]), lambda i: (i, 0)) for a in a_list]
    row0 = 0
    for a in a_list:
        kk = a.shape[1]
        assert row0 % kk == 0
        in_specs.append(pl.BlockSpec((1, kk, n), lambda i, blk=row0 // kk: (layer, blk, 0)))
        row0 += kk
    in_specs.append(pl.BlockSpec((tm, n), lambda i: (i, 0)))
    return pl.pallas_call(
        functools.partial(_proj_res_kernel, n_in=n_in),
        out_shape=jax.ShapeDtypeStruct((m, n), F32),
        grid=(m // tm,),
        in_specs=in_specs,
        out_specs=pl.BlockSpec((tm, n), lambda i: (i, 0)),
        compiler_params=_cp("parallel"),
        name=name,
    )(*a_list, *([w] * n_in), res)


CONVA_ROW_CHUNK = 32


def _conva_kernel(*refs, nb, lt, n_tiles, has_state):
    if has_state:
        h_ref, w_ref, b_ref, lnw_ref, lnb_ref, st_ref, y_ref, nst_ref, ext_ref = refs
    else:
        h_ref, w_ref, b_ref, lnw_ref, lnb_ref, y_ref, nst_ref, ext_ref = refs
    hist = (CONV_A_WIDTH - 1) * nb
    pad = _round_up(hist, SUBLANE)
    rows = lt * nb
    j = pl.program_id(1)

    @pl.when(j == 0)
    def _():
        if has_state:
            ext_ref[pad - hist:pad, :] = st_ref[0]
        else:
            ext_ref[0:pad, :] = jnp.zeros((pad, D_A), F32)

    ext_ref[pad:pad + rows, :] = h_ref[:, 0:D_A] * _sigmoid(h_ref[:, D_A:2 * D_A])

    bias = b_ref[0]
    lnw = lnw_ref[0]
    lnb = lnb_ref[0]
    rc = CONVA_ROW_CHUNK
    for r0 in range(0, rows, rc):
        acc = jnp.zeros((rc, D_A), F32) + bias
        for k in range(CONV_A_WIDTH):
            acc = acc + w_ref[0, k:k + 1, :] * ext_ref[pl.ds(pad - hist + k * nb + r0, rc), :]
        mu = jnp.mean(acc, axis=-1, keepdims=True)
        xc = acc - mu
        var = jnp.mean(xc * xc, axis=-1, keepdims=True)
        c = xc * lax.rsqrt(var + EPS) * lnw + lnb
        y_ref[r0:r0 + rc, :] = _silu(c).astype(y_ref.dtype)

    new_hist = ext_ref[pl.ds(pad + rows - hist, hist), :]
    nst_ref[0] = new_hist
    if n_tiles > 1:
        ext_ref[pad - hist:pad, :] = new_hist


def conva_mixer(h_a, p, layer, state, *, n_seq, nb, lt, n_tiles):
    rows = lt * nb
    hist = (CONV_A_WIDTH - 1) * nb
    pad = _round_up(hist, SUBLANE)
    has_state = state is not None
    in_specs = [pl.BlockSpec((rows, 2 * D_A), lambda s, j: (s * n_tiles + j, 0)),
                _layer_spec((CONV_A_WIDTH, D_A), layer),
                _layer_spec((1, D_A), layer), _layer_spec((1, D_A), layer), _layer_spec((1, D_A), layer)]
    args = [h_a, p['conv_a_w'], p['conv_a_b'], p['ln_a_w'], p['ln_a_b']]
    if has_state:
        assert n_seq == 1
        in_specs.append(_layer_spec((hist, D_A), layer))
        args.append(state)
    return pl.pallas_call(
        functools.partial(_conva_kernel, nb=nb, lt=lt, n_tiles=n_tiles, has_state=has_state),
        out_shape=(jax.ShapeDtypeStruct((h_a.shape[0], D_A), BF16),
                   jax.ShapeDtypeStruct((n_seq, hist, D_A), F32)),
        grid=(n_seq, n_tiles),
        in_specs=in_specs,
        out_specs=(pl.BlockSpec((rows, D_A), lambda s, j: (s * n_tiles + j, 0)),
                   pl.BlockSpec((1, hist, D_A), lambda s, j: (s, 0, 0))),
        scratch_shapes=[pltpu.VMEM((pad + rows, D_A), F32)],
        compiler_params=_cp("parallel", "arbitrary"),
        name="conva_mixer",
    )(*args)


def _gelu_tanh(x):
    return x * (0.5 * (1.0 + jnp.tanh(math.sqrt(2.0 / math.pi) * (x + 0.044715 * (x * x * x)))))


def _s5_kernel(*refs, nb, lt, has_state):
    if has_state:
        (u_ref, bb_ref, ab_ref, cc_ref, d_ref, gw_ref, gb_ref, sre_ref, sim_ref,
         y_ref, nre_ref, nim_ref, hs_ref, cre_ref, cim_ref) = refs
    else:
        (u_ref, bb_ref, ab_ref, cc_ref, d_ref, gw_ref, gb_ref,
         y_ref, nre_ref, nim_ref, hs_ref, cre_ref, cim_ref) = refs
    n = S5_LANES
    j = pl.program_id(1)

    @pl.when(j == 0)
    def _():
        if has_state:
            cre_ref[...] = sre_ref[0]
            cim_ref[...] = sim_ref[0]
        else:
            cre_ref[...] = jnp.zeros(cre_ref.shape, F32)
            cim_ref[...] = jnp.zeros(cim_ref.shape, F32)

    u = u_ref[...]
    hs_ref[...] = _dot(u.astype(BF16), bb_ref[0])
    ab_re = ab_ref[0, 0:1, :]
    ab_im = ab_ref[0, 1:2, :]

    if nb == 1:
        rowid = lax.broadcasted_iota(jnp.int32, (SUBLANE, n), 0)

        def body(i, carry):
            hr, hi = carry
            r0 = pl.multiple_of(i * SUBLANE, SUBLANE)
            blk_re = hs_ref[pl.ds(r0, SUBLANE), 0:n]
            blk_im = hs_ref[pl.ds(r0, SUBLANE), n:2 * n]
            out_re = blk_re
            out_im = blk_im
            for k in range(SUBLANE):
                nr = ab_re * hr - ab_im * hi + blk_re[k:k + 1, :]
                ni = ab_re * hi + ab_im * hr + blk_im[k:k + 1, :]
                hr, hi = nr, ni
                out_re = jnp.where(rowid == k, hr, out_re)
                out_im = jnp.where(rowid == k, hi, out_im)
            hs_ref[pl.ds(r0, SUBLANE), 0:n] = out_re
            hs_ref[pl.ds(r0, SUBLANE), n:2 * n] = out_im
            return hr, hi

        hr, hi = lax.fori_loop(0, lt // SUBLANE, body, (cre_ref[...], cim_ref[...]))
    else:
        hr = cre_ref[...]
        hi = cim_ref[...]
        for t in range(lt):
            rs = slice(t * nb, (t + 1) * nb)
            nr = ab_re * hr - ab_im * hi + hs_ref[rs, 0:n]
            ni = ab_re * hi + ab_im * hr + hs_ref[rs, n:2 * n]
            hr, hi = nr, ni
            hs_ref[rs, 0:n] = hr
            hs_ref[rs, n:2 * n] = hi

    cre_ref[...] = hr
    cim_ref[...] = hi
    nre_ref[0] = hr
    nim_ref[0] = hi

    y = _dot(hs_ref[...].astype(BF16), cc_ref[0]) + d_ref[0] * u
    y = _gelu_tanh(y)
    gate = _dot(y.astype(BF16), gw_ref[0]) + gb_ref[0]
    y_ref[...] = (y * _sigmoid(gate)).astype(y_ref.dtype)


def s5_mixer(h_b, p, layer, s_re, s_im, *, n_seq, nb, lt, n_tiles):
    rows = lt * nb
    n = S5_LANES
    has_state = s_re is not None
    in_specs = [pl.BlockSpec((rows, D_B), lambda s, j: (s * n_tiles + j, 0)),
                _layer_spec((D_B, 2 * n), layer),
                _layer_spec((2, n), layer),
                _layer_spec((2 * n, D_B), layer),
                _layer_spec((1, D_B), layer),
                _layer_spec((D_B, D_B), layer),
                _layer_spec((1, D_B), layer)]
    args = [h_b, p['s5_bb'], p['s5_ab'], p['s5_cc'], p['s5_d'], p['s5_glu_w'], p['s5_glu_b']]
    if has_state:
        assert n_seq == 1
        in_specs += [_layer_spec((nb, n), layer)] * 2
        args += [s_re, s_im]
    st_spec = pl.BlockSpec((1, nb, n), lambda s, j: (s, 0, 0))
    return pl.pallas_call(
        functools.partial(_s5_kernel, nb=nb, lt=lt, has_state=has_state),
        out_shape=(jax.ShapeDtypeStruct((h_b.shape[0], D_B), BF16),
                   jax.ShapeDtypeStruct((n_seq, nb, n), F32),
                   jax.ShapeDtypeStruct((n_seq, nb, n), F32)),
        grid=(n_seq, n_tiles),
        in_specs=in_specs,
        out_specs=(pl.BlockSpec((rows, D_B), lambda s, j: (s * n_tiles + j, 0)), st_spec, st_spec),
        scratch_shapes=[pltpu.VMEM((rows, 2 * n), F32),
                        pltpu.VMEM((nb, n), F32),
                        pltpu.VMEM((nb, n), F32)],
        compiler_params=_cp("parallel", "arbitrary"),
        name="s5_mixer",
    )(*args)


def _group_rmsnorm(y, nw):
    half = D_C // SSD_GROUPS
    outs = []
    for g in range(SSD_GROUPS):
        yg = y[:, g * half:(g + 1) * half]
        outs.append(yg * lax.rsqrt(jnp.mean(yg * yg, axis=-1, keepdims=True) + EPS))
    return jnp.concatenate(outs, axis=1) * nw


def _mamba_p_kernel(h_ref, cw_ref, cb_ref, dtb_ref, alog_ref, dexp_ref, nw_ref, e_ref, tril_ref,
                    y_ref, ncst_ref, nsst_ref, ext_ref, st_ref, *, lt, n_tiles):
    q = SSD_CHUNK
    hist = SSD_CONV_WIDTH - 1
    pad = SUBLANE
    half = D_C // SSD_GROUPS
    hpg = SSD_HEADS // SSD_GROUPS
    j = pl.program_id(1)

    @pl.when(j == 0)
    def _():
        ext_ref[0:pad, :] = jnp.zeros((pad, D_XBC), F32)
        st_ref[...] = jnp.zeros(st_ref.shape, F32)

    ext_ref[pad:pad + lt, :] = h_ref[:, D_C:D_C + D_XBC]

    e = e_ref[...]
    tril = tril_ref[...]
    a_neg = -jnp.exp(alog_ref[0])
    li = lax.broadcasted_iota(jnp.int32, (q, q), 0)
    si = lax.broadcasted_iota(jnp.int32, (q, q), 1)
    causal = li >= si
    lane = lax.broadcasted_iota(jnp.int32, (q, LANE), 1)

    for c in range(lt // q):
        r0 = c * q
        acc = jnp.zeros((q, D_XBC), F32) + cb_ref[0]
        for k in range(SSD_CONV_WIDTH):
            acc = acc + cw_ref[0, k:k + 1, :] * ext_ref[pl.ds(pad - hist + k + r0, q), :]
        xc = _silu(acc)
        xs = xc[:, 0:D_C]
        z = h_ref[r0:r0 + q, 0:D_C]
        dt = _softplus(h_ref[r0:r0 + q, D_C + D_XBC:D_C + D_XBC + LANE] + dtb_ref[0])
        a = dt * a_neg
        hi_, mid_, lo_ = _split3(a)
        cs = _dot(tril, hi_) + _dot(tril, mid_) + _dot(tril, lo_)
        cs_last = cs[q - 1:q, :]
        dt_x = _expand(dt, e)
        ecs_x = _expand(jnp.exp(cs), e)
        edl_x = _expand(jnp.exp(cs_last - cs), e)
        xdt = xs * dt_x
        cs_t = cs.T

        y_parts = []
        for g in range(SSD_GROUPS):
            bm = xc[:, D_C + g * SSD_STATE:D_C + (g + 1) * SSD_STATE]
            cm = xc[:, D_C + SSD_GROUPS * SSD_STATE + g * SSD_STATE:
                    D_C + SSD_GROUPS * SSD_STATE + (g + 1) * SSD_STATE]
            bm16 = bm.astype(BF16)
            cm16 = cm.astype(BF16)
            cb = _dot_nt(cm16, bm16)
            for pr in range(hpg // 2):
                r_even = g * hpg + 2 * pr
                xpair = xdt[:, r_even * SSD_HEAD_DIM:(r_even + 2) * SSD_HEAD_DIM].astype(BF16)
                ys = []
                for r in (r_even, r_even + 1):
                    seg = cs[:, r:r + 1] - cs_t[r:r + 1, :]
                    dec = jnp.exp(jnp.where(causal, seg, -jnp.inf))
                    ys.append(_dot((cb * dec).astype(BF16), xpair))
                y_parts.append(jnp.where(lane < SSD_HEAD_DIM, ys[0], ys[1]))
        y_diag = jnp.concatenate(y_parts, axis=1)
        y_off = jnp.concatenate(
            [_dot(xc[:, D_C + SSD_GROUPS * SSD_STATE + g * SSD_STATE:
                      D_C + SSD_GROUPS * SSD_STATE + (g + 1) * SSD_STATE].astype(BF16),
                  st_ref[:, g * half:(g + 1) * half].astype(BF16)) for g in range(SSD_GROUPS)],
            axis=1) * ecs_x
        y = y_diag + y_off + dexp_ref[0] * xs
        y = y * _silu(z)
        y_ref[r0:r0 + q, :] = _group_rmsnorm(y, nw_ref[0]).astype(y_ref.dtype)

        xw = (xdt * edl_x).astype(BF16)
        dec_row = ecs_x[q - 1:q, :]
        for g in range(SSD_GROUPS):
            bm_t = xc[:, D_C + g * SSD_STATE:D_C + (g + 1) * SSD_STATE].T.astype(BF16)
            upd = _dot(bm_t, xw[:, g * half:(g + 1) * half])
            st_ref[:, g * half:(g + 1) * half] = (
                st_ref[:, g * half:(g + 1) * half] * dec_row[:, g * half:(g + 1) * half] + upd)

    new_hist = ext_ref[pl.ds(pad + lt - hist, hist), :]
    ncst_ref[0] = new_hist
    if n_tiles > 1:
        ext_ref[pad - hist:pad, :] = new_hist

    @pl.when(j == n_tiles - 1)
    def _():
        for blk in range(D_C // LANE):
            nsst_ref[0, blk * LANE:(blk + 1) * LANE, :] = st_ref[:, blk * LANE:(blk + 1) * LANE].T


def _ssd_consts():
    head_of_lane = jnp.arange(D_C) // SSD_HEAD_DIM
    e = (jnp.arange(LANE)[:, None] == head_of_lane[None, :]).astype(BF16)
    tril = (jnp.arange(SSD_CHUNK)[:, None] >= jnp.arange(SSD_CHUNK)[None, :]).astype(BF16)
    return e, tril


def _ssd_param_specs(layer):
    return [_layer_spec((SSD_CONV_WIDTH, D_XBC), layer),
            _layer_spec((1, D_XBC), layer),
            _layer_spec((1, LANE), layer),
            _layer_spec((1, LANE), layer),
            _layer_spec((1, D_C), layer),
            _layer_spec((1, D_C), layer)]


def _ssd_param_args(p):
    return [p['conv_c_w'], p['conv_c_b'], p['ssd_dt_bias'], p['ssd_a_log'], p['ssd_d'], p['ssd_norm_w']]


def mamba_prompt(h_c, p, layer, *, n_seq, lt, n_tiles):
    hist = SSD_CONV_WIDTH - 1
    const = lambda s, j: (0, 0)
    return pl.pallas_call(
        functools.partial(_mamba_p_kernel, lt=lt, n_tiles=n_tiles),
        out_shape=(jax.ShapeDtypeStruct((h_c.shape[0], D_C), BF16),
                   jax.ShapeDtypeStruct((n_seq, hist, D_XBC), F32),
                   jax.ShapeDtypeStruct((n_seq, D_C, SSD_STATE), F32)),
        grid=(n_seq, n_tiles),
        in_specs=[pl.BlockSpec((lt, D_HC), lambda s, j: (s * n_tiles + j, 0))]
        + _ssd_param_specs(layer)
        + [pl.BlockSpec((LANE, D_C), const), pl.BlockSpec((SSD_CHUNK, SSD_CHUNK), const)],
        out_specs=(pl.BlockSpec((lt, D_C), lambda s, j: (s * n_tiles + j, 0)),
                   pl.BlockSpec((1, hist, D_XBC), lambda s, j: (s, 0, 0)),
                   pl.BlockSpec((1, D_C, SSD_STATE), lambda s, j: (s, 0, 0))),
        scratch_shapes=[pltpu.VMEM((SUBLANE + lt, D_XBC), F32),
                        pltpu.VMEM((SSD_STATE, D_C), F32)],
        compiler_params=_cp("parallel", "arbitrary"),
        name="mamba_prompt",
    )(h_c, *_ssd_param_args(p), p['ssd_e'], p['ssd_tril'])


def _ks(c, k, nb):
    return slice((c * SUBLANE + k) * nb, (c * SUBLANE + k + 1) * nb)


def _slab_put(ref, k, slab, nb):
    for c in range(slab.shape[1] // LANE):
        ref[_ks(c, k, nb), :] = slab[:, c * LANE:(c + 1) * LANE]


def _slab_get(ref, k, n_blocks, nb):
    return jnp.concatenate([ref[_ks(c, k, nb), :] for c in range(n_blocks)], axis=1)


def _seq_get(ref, b, n_blocks, nb):
    return jnp.concatenate(
        [ref[pl.ds(c * SUBLANE * nb + b, SUBLANE, stride=nb), :] for c in range(n_blocks)], axis=1)


def _seq_put(ref, b, val, nb, c0=0):
    for c in range(val.shape[1] // LANE):
        ref[pl.ds((c0 + c) * SUBLANE * nb + b, SUBLANE, stride=nb), :] = val[:, c * LANE:(c + 1) * LANE]


def _mamba_s_kernel(h_ref, cw_ref, cb_ref, dtb_ref, alog_ref, dexp_ref, nw_ref, e_ref, cst_ref, sst_ref,
                    y_ref, ncst_ref, nsst_ref,
                    ext_ref, xs_ref, dt_ref, cs_ref, lhs_ref, rhs_ref, c8_ref, yoff_ref, *, nb, lt, bb):
    hist = (SSD_CONV_WIDTH - 1) * nb
    rows = lt * nb
    half = D_C // SSD_GROUPS
    hpg = SSD_HEADS // SSD_GROUPS
    xblk = D_C // LANE
    hblk = half // LANE
    i = pl.program_id(0)
    n_steps = pl.num_programs(0)
    bc_off = D_C
    cc_off = D_C + SSD_GROUPS * SSD_STATE

    @pl.when(i == 0)
    def _phase1():
        e = e_ref[...]
        ext_ref[0:hist, :] = cst_ref[0]
        ext_ref[hist:hist + rows, :] = h_ref[:, D_C:D_C + D_XBC]
        ncst_ref[...] = ext_ref[rows:rows + hist, :]
        a_neg = -jnp.exp(alog_ref[0])
        lhs_ref[...] = jnp.zeros(lhs_ref.shape, F32)
        rhs_ref[...] = jnp.zeros(rhs_ref.shape, F32)
        c8_ref[...] = jnp.zeros(c8_ref.shape, F32)
        cs = jnp.zeros((nb, LANE), F32)
        for t in range(lt):
            rs = slice(t * nb, (t + 1) * nb)
            acc = jnp.zeros((nb, D_XBC), F32) + cb_ref[0]
            for k in range(SSD_CONV_WIDTH):
                acc = acc + cw_ref[0, k:k + 1, :] * ext_ref[(t + k) * nb:(t + k + 1) * nb, :]
            xc = _silu(acc)
            xs_ref[rs, :] = xc[:, 0:D_C]
            for g in range(SSD_GROUPS):
                rhs_ref[_ks(2 * g, t, nb), :] = xc[:, bc_off + g * SSD_STATE:bc_off + (g + 1) * SSD_STATE]
            _slab_put(c8_ref, t, xc[:, cc_off:cc_off + SSD_GROUPS * SSD_STATE], nb)
            dt = _softplus(h_ref[rs, D_C + D_XBC:D_C + D_XBC + LANE] + dtb_ref[0])
            dt_ref[rs, :] = dt
            cs = cs + dt * a_neg
            cs_ref[rs, :] = cs
        cs_last = cs
        for t in range(lt):
            rs = slice(t * nb, (t + 1) * nb)
            wt = jnp.exp(cs_last - cs_ref[rs, :]) * dt_ref[rs, :]
            _slab_put(lhs_ref, t, xs_ref[rs, :] * _expand(wt, e), nb)
        dec = _expand(jnp.exp(cs_last), e)
        d_hi = dec.astype(BF16).astype(F32)
        d_r = dec - d_hi
        d_mid = d_r.astype(BF16).astype(F32)
        d_lo = d_r - d_mid
        ones = jnp.ones((nb, SSD_STATE), F32)
        for k, piece in enumerate((d_hi, d_mid, d_lo)):
            _slab_put(lhs_ref, lt + k, piece, nb)
            for g in range(SSD_GROUPS):
                rhs_ref[_ks(2 * g + 1, lt + k, nb), :] = ones

    for jb in range(bb):
        b = i * bb + jb
        l8 = _seq_get(lhs_ref, b, xblk, nb).astype(BF16)
        r8 = _seq_get(rhs_ref, b, 2 * SSD_GROUPS, nb).astype(BF16)
        c8 = _seq_get(c8_ref, b, SSD_GROUPS, nb).astype(BF16)
        for g in range(SSD_GROUPS):
            s = sst_ref[0, jb, g * half:(g + 1) * half, :]
            yo = _dot_nt(c8[:, g * SSD_STATE:(g + 1) * SSD_STATE], s.astype(BF16))
            _seq_put(yoff_ref, b, yo, nb, c0=g * hblk)
            upd = _dot_tn(l8[:, g * half:(g + 1) * half],
                          r8[:, g * 2 * SSD_STATE:(g + 1) * 2 * SSD_STATE])
            nsst_ref[0, jb, g * half:(g + 1) * half, :] = upd[:, SSD_STATE:] * s + upd[:, :SSD_STATE]

    @pl.when(i == n_steps - 1)
    def _phase3():
        e = e_ref[...]
        lane = lax.broadcasted_iota(jnp.int32, (nb, LANE), 1)
        for t in range(lt):
            rt = slice(t * nb, (t + 1) * nb)
            cs_t = cs_ref[rt, :]
            y = (_slab_get(yoff_ref, t, xblk, nb) * _expand(jnp.exp(cs_t), e)
                 + dexp_ref[0] * xs_ref[rt, :])
            for s_ in range(t + 1):
                rsl = slice(s_ * nb, (s_ + 1) * nb)
                cbs = []
                for g in range(SSD_GROUPS):
                    cm = c8_ref[_ks(g, t, nb), :]
                    bm = rhs_ref[_ks(2 * g, s_, nb), :]
                    cbs.append(jnp.sum(cm * bm, axis=-1, keepdims=True))
                cb = jnp.where(lane < hpg, cbs[0], cbs[1])
                m = jnp.exp(cs_t - cs_ref[rsl, :]) * dt_ref[rsl, :] * cb
                y = y + _expand(m, e) * xs_ref[rsl, :]
            y = y * _silu(h_ref[rt, 0:D_C])
            y_ref[rt, :] = _group_rmsnorm(y, nw_ref[0]).astype(y_ref.dtype)


def mamba_sample(h_c, p, layer, cst, sst, *, nb, lt, bb):
    rows = lt * nb
    hist = (SSD_CONV_WIDTH - 1) * nb
    const = lambda i: (0, 0)
    sst_spec = pl.BlockSpec((1, bb, D_C, SSD_STATE), lambda i: (layer, i, 0, 0))
    in_specs = ([pl.BlockSpec((rows, D_HC), const)] + _ssd_param_specs(layer)
                + [pl.BlockSpec((LANE, D_C), const), _layer_spec((hist, D_XBC), layer), sst_spec])
    return pl.pallas_call(
        functools.partial(_mamba_s_kernel, nb=nb, lt=lt, bb=bb),
        out_shape=(jax.ShapeDtypeStruct((rows, D_C), BF16),
                   jax.ShapeDtypeStruct((hist, D_XBC), F32),
                   jax.ShapeDtypeStruct(sst.shape, F32)),
        grid=(nb // bb,),
        in_specs=in_specs,
        out_specs=(pl.BlockSpec((rows, D_C), const),
                   pl.BlockSpec((hist, D_XBC), const),
                   sst_spec),
        input_output_aliases={len(in_specs) - 1: 2},
        scratch_shapes=[pltpu.VMEM((hist + rows, D_XBC), F32),
                        pltpu.VMEM((rows, D_C), F32),
                        pltpu.VMEM((rows, LANE), F32),
                        pltpu.VMEM((rows, LANE), F32),
                        pltpu.VMEM((D_C // LANE * SUBLANE * nb, LANE), F32),
                        pltpu.VMEM((2 * SSD_GROUPS * SUBLANE * nb, LANE), F32),
                        pltpu.VMEM((SSD_GROUPS * SUBLANE * nb, LANE), F32),
                        pltpu.VMEM((D_C // LANE * SUBLANE * nb, LANE), F32)],
        compiler_params=_cp("arbitrary"),
        name="mamba_sample",
    )(h_c, *_ssd_param_args(p), p['ssd_e'], cst, sst)


def _softmax_rows(s):
    m = jnp.max(s, axis=-1, keepdims=True)
    ex = jnp.exp(s - m)
    return ex / jnp.sum(ex, axis=-1, keepdims=True)


def _attn_p_kernel(q_ref, k_ref, v_ref, o_ref):
    for h in range(XA_HEADS):
        hs = slice(h * XA_HEAD_DIM, (h + 1) * XA_HEAD_DIM)
        s = _dot_nt(q_ref[:, hs].astype(BF16), k_ref[0, :, hs].astype(BF16)) / math.sqrt(XA_HEAD_DIM)
        p = _softmax_rows(s)
        o_ref[:, hs] = _dot(p.astype(BF16), v_ref[0, :, hs].astype(BF16)).astype(o_ref.dtype)


def attn_prompt(q, kv, *, n_seq, seq, tq):
    n_tiles = seq // tq
    return pl.pallas_call(
        _attn_p_kernel,
        out_shape=jax.ShapeDtypeStruct(q.shape, BF16),
        grid=(n_seq, n_tiles),
        in_specs=[pl.BlockSpec((tq, D_MODEL), lambda s, j: (s * n_tiles + j, 0)),
                  pl.BlockSpec((1, N_MEM, D_MODEL), lambda s, j: (s, 0, 0)),
                  pl.BlockSpec((1, N_MEM, D_MODEL), lambda s, j: (s, 0, 1))],
        out_specs=pl.BlockSpec((tq, D_MODEL), lambda s, j: (s * n_tiles + j, 0)),
        compiler_params=_cp("parallel", "arbitrary"),
        name="attn_prompt",
    )(q, kv, kv)


def _attn_s_kernel(q_ref, k_ref, v_ref, o_ref, *, bb, lt):
    rows = XA_HEADS * lt
    n = N_MEM * XA_HEADS
    col_head = lax.broadcasted_iota(jnp.int32, (rows, n), 1) % XA_HEADS
    row_head = lax.broadcasted_iota(jnp.int32, (rows, n), 0) // lt
    same_head = col_head == row_head
    for jb in range(bb):
        k = k_ref[0, jb].reshape(n, XA_HEAD_DIM).astype(BF16)
        v = v_ref[0, jb].reshape(n, XA_HEAD_DIM).astype(BF16)
        s = _dot_nt(q_ref[jb].astype(BF16), k) / math.sqrt(XA_HEAD_DIM)
        p = _softmax_rows(jnp.where(same_head, s, -jnp.inf))
        o_ref[jb] = _dot(p.astype(BF16), v)


def attn_sample(q, k, v, layer, *, bb):
    nb, rows, _ = q.shape
    kv_spec = pl.BlockSpec((1, bb, N_MEM, XA_HEADS, XA_HEAD_DIM), lambda i: (layer, i, 0, 0, 0))
    return pl.pallas_call(
        functools.partial(_attn_s_kernel, bb=bb, lt=rows // XA_HEADS),
        out_shape=jax.ShapeDtypeStruct((nb, rows, XA_HEAD_DIM), F32),
        grid=(nb // bb,),
        in_specs=[pl.BlockSpec((bb, rows, XA_HEAD_DIM), lambda i: (i, 0, 0)), kv_spec, kv_spec],
        out_specs=pl.BlockSpec((bb, rows, XA_HEAD_DIM), lambda i: (i, 0, 0)),
        compiler_params=_cp("parallel"),
        name="attn_sample",
    )(q, k, v)


def _ffn_kernel(*refs, nb, tiles_per_seq, has_state, final_norm):
    refs = list(refs)
    x_ref, nw_ref, wg_ref, wu_ref, cw_ref, cb_ref, wd_ref = refs[:7]
    pos = 7
    st_ref = None
    if has_state:
        st_ref = refs[pos]
        pos += 1
    fw_ref = None
    if final_norm:
        fw_ref = refs[pos]
        pos += 1
    o_ref, nst_ref, xn_ref, gext_ref, carry_ref = refs[pos:pos + 5]

    hist = (FFN_CONV_WIDTH - 1) * nb
    pad = _round_up(hist, SUBLANE)
    tm = x_ref.shape[0]
    tf = wg_ref.shape[3]
    rsz = min(FFN_ROW_SPLIT, tm)
    i = pl.program_id(0)
    f = pl.program_id(1)
    n_f = pl.num_programs(1)

    @pl.when(f == 0)
    def _():
        x = x_ref[...]
        xn_ref[...] = _rmsnorm_rows(x, nw_ref[0]).astype(BF16)
        o_ref[...] = x

    if tiles_per_seq > 1:
        first = (i % tiles_per_seq) == 0

        @pl.when(first)
        def _():
            if has_state:
                gext_ref[pad - hist:pad, :] = st_ref[0]
            else:
                gext_ref[0:pad, :] = jnp.zeros((pad, tf), F32)

        @pl.when(jnp.logical_not(first))
        def _():
            gext_ref[0:pad, :] = carry_ref[f]
    else:
        if has_state:
            gext_ref[pad - hist:pad, :] = st_ref[0]
        else:
            gext_ref[0:pad, :] = jnp.zeros((pad, tf), F32)

    cw0 = cw_ref[0, 0:1, :]
    cw1 = cw_ref[0, 1:2, :]
    cw2 = cw_ref[0, 2:3, :]
    cb = cb_ref[0]
    for r0 in range(0, tm, rsz):
        xn = xn_ref[r0:r0 + rsz, :]
        g = _dot(xn, wg_ref[0, 0])
        up = _dot(xn, wu_ref[0, 0])
        gext_ref[pad + r0:pad + r0 + rsz, :] = g
        conv = (cw0 * gext_ref[pl.ds(pad - 2 * nb + r0, rsz), :]
                + cw1 * gext_ref[pl.ds(pad - nb + r0, rsz), :]
                + cw2 * g + cb)
        act = _silu(conv) * up
        o_ref[r0:r0 + rsz, :] += _dot(act.astype(BF16), wd_ref[0])

    nst_ref[0] = gext_ref[pl.ds(pad + tm - hist, hist), :]
    if tiles_per_seq > 1:
        carry_ref[f] = gext_ref[pl.ds(tm, pad), :]

    if final_norm:
        @pl.when(f == n_f - 1)
        def _():
            o_ref[...] = _rmsnorm_rows(o_ref[...], fw_ref[...])


def conv_ffn(x, p, layer, state, final_w, *, n_seq, nb, tm, tiles_per_seq):
    m = x.shape[0]
    tf = FF_TILE
    n_f = D_FF_PAD // tf
    hist = (FFN_CONV_WIDTH - 1) * nb
    pad = _round_up(hist, SUBLANE)
    has_state = state is not None
    final_norm = final_w is not None
    in_specs = [pl.BlockSpec((tm, D_MODEL), lambda i, f: (i, 0)),
                _layer_spec((1, D_MODEL), layer),
                pl.BlockSpec((1, 1, D_MODEL, tf), lambda i, f: (layer, f, 0, 0)),
                pl.BlockSpec((1, 1, D_MODEL, tf), lambda i, f: (layer, f, 0, 0)),
                pl.BlockSpec((1, FFN_CONV_WIDTH, tf), lambda i, f: (layer, 0, f)),
                pl.BlockSpec((1, 1, tf), lambda i, f: (layer, 0, f)),
                pl.BlockSpec((1, tf, D_MODEL), lambda i, f: (layer, f, 0))]
    args = [x, p['norm_ffn_w'], p['ffn_wg'], p['ffn_wu'], p['ffn_conv_w'], p['ffn_conv_b'], p['ffn_wd']]
    if has_state:
        assert n_seq == 1 and tiles_per_seq == 1
        in_specs.append(pl.BlockSpec((1, hist, tf), lambda i, f: (layer, 0, f)))
        args.append(state)
    if final_norm:
        in_specs.append(pl.BlockSpec((1, D_MODEL), lambda i, f: (0, 0)))
        args.append(final_w.reshape(1, D_MODEL))
    return pl.pallas_call(
        functools.partial(_ffn_kernel, nb=nb, tiles_per_seq=tiles_per_seq, has_state=has_state,
                          final_norm=final_norm),
        out_shape=(jax.ShapeDtypeStruct((m, D_MODEL), F32),
                   jax.ShapeDtypeStruct((m // tm, hist, D_FF_PAD), F32)),
        grid=(m // tm, n_f),
        in_specs=in_specs,
        out_specs=(pl.BlockSpec((tm, D_MODEL), lambda i, f: (i, 0)),
                   pl.BlockSpec((1, hist, tf), lambda i, f: (i, 0, f))),
        scratch_shapes=[pltpu.VMEM((tm, D_MODEL), BF16),
                        pltpu.VMEM((pad + tm, tf), F32),
                        pltpu.VMEM((n_f, pad, tf), F32)],
        compiler_params=_cp("arbitrary", "arbitrary"),
        name="conv_ffn",
    )(*args)


def _s5_params(lam_re, lam_im, log_dt, b_re, b_im, c_re, c_im):
    depth = lam_re.shape[0]
    dt = jnp.exp(log_dt)[..., None]
    mag = jnp.exp(lam_re * dt)
    ang = lam_im * dt
    ab_re, ab_im = mag * jnp.cos(ang), mag * jnp.sin(ang)
    den = lam_re * lam_re + lam_im * lam_im
    nr, ni = ab_re - 1.0, ab_im
    co_re = (nr * lam_re + ni * lam_im) / den
    co_im = (ni * lam_re - nr * lam_im) / den
    bb_re = co_re[..., None] * b_re - co_im[..., None] * b_im
    bb_im = co_re[..., None] * b_im + co_im[..., None] * b_re
    eye = jnp.eye(S5_GROUPS, dtype=F32)
    dense_b = lambda m: jnp.einsum('lgph,gk->lghkp', m, eye).reshape(depth, D_B, S5_LANES)
    dense_c = lambda m: jnp.einsum('lghp,gk->lkpgh', m, eye).reshape(depth, S5_LANES, D_B)
    bb = jnp.concatenate([dense_b(bb_re), dense_b(bb_im)], axis=2).astype(BF16)
    cc = jnp.concatenate([dense_c(c_re), -dense_c(c_im)], axis=1).astype(BF16)
    ab = jnp.stack([ab_re.reshape(depth, S5_LANES), ab_im.reshape(depth, S5_LANES)], axis=1)
    return bb, ab, cc


def _tile_major(w, tf):
    depth, k, n = w.shape
    return w.reshape(depth, k, n // tf, tf).transpose(0, 2, 1, 3)


def kernel(x_prompt, x_sample, mem_prompt, cache_mem_k, cache_mem_v, state_conv_a, state_s5_re, state_s5_im, state_conv_c, state_ssd, state_ffn_conv, norm_mix_w, w_in, conv_a_w, conv_a_b, ln_a_w, ln_a_b, s5_lam_re, s5_lam_im, s5_log_dt, s5_b_re, s5_b_im, s5_c_re, s5_c_im, s5_d, s5_glu_w, s5_glu_b, conv_c_w, conv_c_b, ssd_dt_bias, ssd_a_log, ssd_d, ssd_norm_w, w_out, norm_xa_w, norm_mem_w, xa_wq, xa_wk, xa_wv, xa_wo, norm_ffn_w, ffn_w_gate, ffn_w_up, ffn_conv_w, ffn_conv_b, ffn_w_down, final_norm_w):
    bp, seq, _ = x_prompt.shape
    nbs, lts, _ = x_sample.shape
    depth = w_in.shape[0]
    n_mem = mem_prompt.shape[1]
    lt_p = 512 if seq % 512 == 0 else seq
    n_tiles_p = seq // lt_p
    tm_p = lt_p
    tm_s = lts * nbs
    tm_m = min(512, bp * n_mem)

    vec = lambda a: a.reshape(depth, 1, a.shape[-1])
    pad_lanes = lambda a: vec(jnp.pad(a, ((0, 0), (0, LANE - a.shape[-1]))))
    sp_a, sp_b = 2 * D_A, 2 * D_A + D_B
    ff_pad = D_FF_PAD - D_FF
    s5_bb, s5_ab, s5_cc = _s5_params(s5_lam_re, s5_lam_im, s5_log_dt, s5_b_re, s5_b_im, s5_c_re, s5_c_im)
    ssd_e, ssd_tril = _ssd_consts()
    p = {
        'norm_mix_w': vec(norm_mix_w), 'norm_xa_w': vec(norm_xa_w), 'norm_mem_w': vec(norm_mem_w),
        'norm_ffn_w': vec(norm_ffn_w),
        'w_in_a': w_in[:, :, :sp_a].astype(BF16),
        'w_in_b': w_in[:, :, sp_a:sp_b].astype(BF16),
        'w_in_c': jnp.pad(w_in[:, :, sp_b:], ((0, 0), (0, 0), (0, D_HC - (w_in.shape[2] - sp_b)))).astype(BF16),
        'conv_a_w': conv_a_w, 'conv_a_b': vec(conv_a_b), 'ln_a_w': vec(ln_a_w), 'ln_a_b': vec(ln_a_b),
        's5_bb': s5_bb, 's5_ab': s5_ab, 's5_cc': s5_cc,
        's5_d': vec(s5_d), 's5_glu_w': s5_glu_w.astype(BF16), 's5_glu_b': vec(s5_glu_b),
        'conv_c_w': conv_c_w, 'conv_c_b': vec(conv_c_b),
        'ssd_dt_bias': pad_lanes(ssd_dt_bias), 'ssd_a_log': pad_lanes(ssd_a_log),
        'ssd_d': vec(jnp.repeat(ssd_d, SSD_HEAD_DIM, axis=1)), 'ssd_norm_w': vec(ssd_norm_w),
        'ssd_e': ssd_e, 'ssd_tril': ssd_tril,
        'w_out': w_out.astype(BF16), 'wq': xa_wq.astype(BF16), 'wo': xa_wo.astype(BF16),
        'wkv': jnp.concatenate([xa_wk, xa_wv], axis=2).astype(BF16),
        'ffn_wg': _tile_major(jnp.pad(ffn_w_gate, ((0, 0), (0, 0), (0, ff_pad))).astype(BF16), FF_TILE),
        'ffn_wu': _tile_major(jnp.pad(ffn_w_up, ((0, 0), (0, 0), (0, ff_pad))).astype(BF16), FF_TILE),
        'ffn_wd': jnp.pad(ffn_w_down, ((0, 0), (0, ff_pad), (0, 0))).astype(BF16),
        'ffn_conv_w': jnp.pad(ffn_conv_w, ((0, 0), (0, 0), (0, ff_pad))),
        'ffn_conv_b': vec(jnp.pad(ffn_conv_b, ((0, 0), (0, ff_pad)))),
    }

    tmaj = lambda a: a.transpose(0, 2, 1, 3).reshape(depth, a.shape[2] * nbs, a.shape[3])
    st_conv_a = tmaj(state_conv_a)
    st_conv_c = tmaj(state_conv_c)
    st_ffn = jnp.pad(tmaj(state_ffn_conv), ((0, 0), (0, 0), (0, ff_pad)))
    st_re = state_s5_re.reshape(depth, nbs, S5_LANES)
    st_im = state_s5_im.reshape(depth, nbs, S5_LANES)
    ssd_all = state_ssd.reshape(depth, nbs, D_C, SSD_STATE)

    xp = x_prompt.reshape(bp * seq, D_MODEL)
    xs = x_sample.transpose(1, 0, 2).reshape(lts * nbs, D_MODEL)
    mem2d = mem_prompt.reshape(bp * n_mem, D_MODEL)

    def mixers(x, l, *, n_seq, nb, lt, n_tiles, tm, sample):
        h_a = norm_matmul(x, p['norm_mix_w'], p['w_in_a'], l, tm=tm, tn=2 * D_A, name="in_proj_a")
        h_b = norm_matmul(x, p['norm_mix_w'], p['w_in_b'], l, tm=tm, tn=D_B, name="in_proj_b")
        h_c = norm_matmul(x, p['norm_mix_w'], p['w_in_c'], l, tm=tm, tn=D_HC // 3, name="in_proj_c")
        ya, n_conv_a = conva_mixer(h_a, p, l, st_conv_a if sample else None,
                                   n_seq=n_seq, nb=nb, lt=lt, n_tiles=n_tiles)
        yb, n_re, n_im = s5_mixer(h_b, p, l, st_re if sample else None, st_im if sample else None,
                                  n_seq=n_seq, nb=nb, lt=lt, n_tiles=n_tiles)
        return h_c, ya, yb, n_conv_a, n_re, n_im

    outs_p = [[] for _ in range(8)]
    outs_s = [[] for _ in range(5)]
    for l in range(depth):
        last = l == depth - 1
        mkv = norm_matmul(mem2d, p['norm_mem_w'], p['wkv'], l, tm=tm_m, tn=1024, name="mem_kv")

        h_c, ya, yb, p_conv_a, p_re, p_im = mixers(xp, l, n_seq=bp, nb=1, lt=lt_p, n_tiles=n_tiles_p,
                                                   tm=tm_p, sample=False)
        yc, p_conv_c, p_ssd = mamba_prompt(h_c, p, l, n_seq=bp, lt=lt_p, n_tiles=n_tiles_p)
        xp = proj_res([ya, yb, yc], p['w_out'], l, xp, tm=tm_p, name="out_proj")
        q = norm_matmul(xp, p['norm_xa_w'], p['wq'], l, tm=tm_p, tn=1024, name="q_proj")
        o = attn_prompt(q, mkv.reshape(bp, n_mem, 2 * D_MODEL), n_seq=bp, seq=seq, tq=lt_p)
        xp = proj_res([o], p['wo'], l, xp, tm=tm_p, name="attn_out")
        xp, p_ffn = conv_ffn(xp, p, l, None, final_norm_w if last else None, n_seq=bp, nb=1, tm=tm_p,
                             tiles_per_seq=seq // tm_p)
        for lst, v in zip(outs_p, (p_conv_a, p_re, p_im, p_conv_c, p_ssd,
                                   p_ffn[seq // tm_p - 1::seq // tm_p], mkv[:, :D_MODEL], mkv[:, D_MODEL:])):
            lst.append(v)

        h_c, ya, yb, s_conv_a, s_re, s_im = mixers(xs, l, n_seq=1, nb=nbs, lt=lts, n_tiles=1, tm=tm_s,
                                                   sample=True)
        yc, s_conv_c, ssd_all = mamba_sample(h_c, p, l, st_conv_c, ssd_all, nb=nbs, lt=lts, bb=8)
        xs = proj_res([ya, yb, yc], p['w_out'], l, xs, tm=tm_s, name="out_proj")
        q = norm_matmul(xs, p['norm_xa_w'], p['wq'], l, tm=tm_s, tn=1024, name="q_proj")
        q = q.reshape(lts, nbs, XA_HEADS, XA_HEAD_DIM).transpose(1, 2, 0, 3)
        o = attn_sample(q.reshape(nbs, XA_HEADS * lts, XA_HEAD_DIM), cache_mem_k, cache_mem_v, l, bb=4)
        o = o.reshape(nbs, XA_HEADS, lts, XA_HEAD_DIM).transpose(2, 0, 1, 3).reshape(lts * nbs, D_MODEL)
        xs = proj_res([o], p['wo'], l, xs, tm=tm_s, name="attn_out")
        xs, s_ffn = conv_ffn(xs, p, l, st_ffn, final_norm_w if last else None, n_seq=1, nb=nbs, tm=tm_s,
                             tiles_per_seq=1)
        for lst, v in zip(outs_s, (s_conv_a[0], s_re[0], s_im[0], s_conv_c, s_ffn[0])):
            lst.append(v)

    p_conv_a, p_re, p_im, p_conv_c, p_ssd, p_ffn, p_mk, p_mv = [jnp.stack(o) for o in outs_p]
    s_conv_a, s_re, s_im, s_conv_c, s_ffn = [jnp.stack(o) for o in outs_s]
    bmaj = lambda a, w: a.reshape(depth, w, nbs, a.shape[-1]).transpose(0, 2, 1, 3)
    y_prompt = xp.reshape(bp, seq, D_MODEL)
    y_sample = xs.reshape(lts, nbs, D_MODEL).transpose(1, 0, 2)
    return (y_prompt, y_sample,
            p_conv_a,
            p_re.reshape(depth, bp, S5_GROUPS, S5_STATE), p_im.reshape(depth, bp, S5_GROUPS, S5_STATE),
            p_conv_c,
            p_ssd.reshape(depth, bp, SSD_HEADS, SSD_HEAD_DIM, SSD_STATE),
            p_ffn[..., :D_FF],
            p_mk.reshape(depth, bp, n_mem, XA_HEADS, XA_HEAD_DIM),
            p_mv.reshape(depth, bp, n_mem, XA_HEADS, XA_HEAD_DIM),
            bmaj(s_conv_a, CONV_A_WIDTH - 1),
            s_re.reshape(depth, nbs, S5_GROUPS, S5_STATE), s_im.reshape(depth, nbs, S5_GROUPS, S5_STATE),
            bmaj(s_conv_c, SSD_CONV_WIDTH - 1),
            ssd_all.reshape(state_ssd.shape),
            bmaj(s_ffn, FFN_CONV_WIDTH - 1)[..., :D_FF])
```

```python
import functools
import math

import jax
import jax.numpy as jnp
from jax import lax
from jax.experimental import pallas as pl
from jax.experimental.pallas import tpu as pltpu

F32 = jnp.float32
BF16 = jnp.bfloat16
EPS = 1e-6

D_MODEL = 2048
D_A = 512
D_B = 512
D_C = 1024
CONV_A_WIDTH = 31
S5_GROUP = 16
S5_GROUPS = 32
S5_STATE = 64
S5_LANES = S5_GROUPS * S5_STATE
SSD_HEAD_DIM = 64
SSD_HEADS = 16
SSD_GROUPS = 2
SSD_STATE = 128
SSD_CONV_WIDTH = 4
SSD_CHUNK = 128
D_XBC = D_C + 2 * SSD_GROUPS * SSD_STATE
D_HC = D_C + D_XBC + 128
XA_HEADS = 4
XA_HEAD_DIM = 512
N_MEM = 256
D_FF = 5504
FFN_CONV_WIDTH = 3

LANE = 128
SUBLANE = 8
VMEM_LIMIT = 56 * 1024 * 1024
FF_TILE = 512
D_FF_PAD = ((D_FF + FF_TILE - 1) // FF_TILE) * FF_TILE
FFN_ROW_SPLIT = 256


def _round_up(x, m):
    return (x + m - 1) // m * m


def _cp(*sem):
    return pltpu.CompilerParams(dimension_semantics=sem, vmem_limit_bytes=VMEM_LIMIT)


def _layer_spec(tail, layer):
    zeros = (0,) * len(tail)
    return pl.BlockSpec((1,) + tuple(tail), lambda *_: (layer,) + zeros)


def _dot(a, b):
    return jnp.dot(a, b, preferred_element_type=F32)


def _dot_nt(a, b):
    return lax.dot_general(a, b, (((1,), (1,)), ((), ())), preferred_element_type=F32)


def _dot_tn(a, b):
    return lax.dot_general(a, b, (((0,), (0,)), ((), ())), preferred_element_type=F32)


def _split3(a):
    hi = a.astype(BF16)
    r = a - hi.astype(F32)
    mid = r.astype(BF16)
    lo = (r - mid.astype(F32)).astype(BF16)
    return hi, mid, lo


def _expand(a, e):
    hi, mid, lo = _split3(a)
    return _dot(hi, e) + _dot(mid, e) + _dot(lo, e)


def _sigmoid(x):
    return jax.nn.sigmoid(x)


def _silu(x):
    return x * jax.nn.sigmoid(x)


def _softplus(x):
    return jnp.maximum(x, 0.0) + jnp.log1p(jnp.exp(-jnp.abs(x)))


def _rmsnorm_rows(x, w):
    ms = jnp.mean(x * x, axis=-1, keepdims=True)
    return x * lax.rsqrt(ms + EPS) * w


def _norm_matmul_kernel(x_ref, nw_ref, w_ref, o_ref, xn_ref):
    @pl.when(pl.program_id(1) == 0)
    def _():
        xn_ref[...] = _rmsnorm_rows(x_ref[...], nw_ref[0]).astype(BF16)

    o_ref[...] = _dot(xn_ref[...], w_ref[0]).astype(o_ref.dtype)


def norm_matmul(x, nw, w, layer, *, tm, tn, name):
    m, k = x.shape
    n = w.shape[2]
    return pl.pallas_call(
        _norm_matmul_kernel,
        out_shape=jax.ShapeDtypeStruct((m, n), F32),
        grid=(m // tm, n // tn),
        in_specs=[pl.BlockSpec((tm, k), lambda i, j: (i, 0)),
                  _layer_spec((1, k), layer),
                  pl.BlockSpec((1, k, tn), lambda i, j: (layer, 0, j))],
        out_specs=pl.BlockSpec((tm, tn), lambda i, j: (i, j)),
        scratch_shapes=[pltpu.VMEM((tm, k), BF16)],
        compiler_params=_cp("parallel", "arbitrary"),
        name=name,
    )(x, nw, w)


def _mem_kv_kernel(x_ref, nw_ref, w_ref, k_ref, v_ref, xn_ref):
    j = pl.program_id(1)

    @pl.when(j == 0)
    def _():
        xn = _rmsnorm_rows(x_ref[...], nw_ref[0]).astype(BF16)
        xn_ref[...] = xn
        k_ref[...] = _dot(xn, w_ref[0])

    @pl.when(j == 1)
    def _():
        v_ref[...] = _dot(xn_ref[...], w_ref[0])


def mem_kv(x, nw, w, layer, *, tm):
    m, k = x.shape
    n = w.shape[2] // 2
    out = jax.ShapeDtypeStruct((m, n), F32)
    return pl.pallas_call(
        _mem_kv_kernel,
        out_shape=(out, out),
        grid=(m // tm, 2),
        in_specs=[pl.BlockSpec((tm, k), lambda i, j: (i, 0)),
                  _layer_spec((1, k), layer),
                  pl.BlockSpec((1, k, n), lambda i, j: (layer, 0, j))],
        out_specs=(pl.BlockSpec((tm, n), lambda i, j: (i, 0)), pl.BlockSpec((tm, n), lambda i, j: (i, 0))),
        scratch_shapes=[pltpu.VMEM((tm, k), BF16)],
        compiler_params=_cp("parallel", "arbitrary"),
        name="mem_kv",
    )(x, nw, w)


def _proj_res_kernel(*refs, n_in):
    a_refs = refs[:n_in]
    w_refs = refs[n_in:2 * n_in]
    res_ref, o_ref = refs[2 * n_in], refs[2 * n_in + 1]
    acc = res_ref[...]
    for a_ref, w_ref in zip(a_refs, w_refs):
        acc = acc + _dot(a_ref[...].astype(BF16), w_ref[0])
    o_ref[...] = acc


def proj_res(a_list, w, layer, res, *, tm, name):
    m, n = res.shape
    n_in = len(a_list)
    in_specs = [pl.BlockSpec((tm, a.shape[1]), lambda i: (i, 0)) for a in a_list]
    row0 = 0
    for a in a_list:
        kk = a.shape[1]
        assert row0 % kk == 0
        in_specs.append(pl.BlockSpec((1, kk, n), lambda i, blk=row0 // kk: (layer, blk, 0)))
        row0 += kk
    in_specs.append(pl.BlockSpec((tm, n), lambda i: (i, 0)))
    return pl.pallas_call(
        functools.partial(_proj_res_kernel, n_in=n_in),
        out_shape=jax.ShapeDtypeStruct((m, n), F32),
        grid=(m // tm,),
        in_specs=in_specs,
        out_specs=pl.BlockSpec((tm, n), lambda i: (i, 0)),
        compiler_params=_cp("parallel"),
        name=name,
    )(*a_list, *([w] * n_in), res)


CONVA_ROW_CHUNK = 32


def _conva_kernel(*refs, nb, lt, n_tiles, has_state):
    if has_state:
        h_ref, w_ref, b_ref, lnw_ref, lnb_ref, st_ref, y_ref, nst_ref, ext_ref = refs
    else:
        h_ref, w_ref, b_ref, lnw_ref, lnb_ref, y_ref, nst_ref, ext_ref = refs
    hist = (CONV_A_WIDTH - 1) * nb
    pad = _round_up(hist, SUBLANE)
    rows = lt * nb
    j = pl.program_id(1)

    @pl.when(j == 0)
    def _():
        if has_state:
            ext_ref[pad - hist:pad, :] = st_ref[0]
        else:
            ext_ref[0:pad, :] = jnp.zeros((pad, D_A), F32)

    ext_ref[pad:pad + rows, :] = h_ref[:, 0:D_A] * _sigmoid(h_ref[:, D_A:2 * D_A])

    bias = b_ref[0]
    lnw = lnw_ref[0]
    lnb = lnb_ref[0]
    rc = CONVA_ROW_CHUNK
    for r0 in range(0, rows, rc):
        acc = jnp.zeros((rc, D_A), F32) + bias
        for k in range(CONV_A_WIDTH):
            acc = acc + w_ref[0, k:k + 1, :] * ext_ref[pl.ds(pad - hist + k * nb + r0, rc), :]
        mu = jnp.mean(acc, axis=-1, keepdims=True)
        xc = acc - mu
        var = jnp.mean(xc * xc, axis=-1, keepdims=True)
        c = xc * lax.rsqrt(var + EPS) * lnw + lnb
        y_ref[r0:r0 + rc, :] = _silu(c).astype(y_ref.dtype)

    new_hist = ext_ref[pl.ds(pad + rows - hist, hist), :]
    nst_ref[0] = new_hist
    if n_tiles > 1:
        ext_ref[pad - hist:pad, :] = new_hist


def conva_mixer(h_a, p, layer, state, *, n_seq, nb, lt, n_tiles):
    rows = lt * nb
    hist = (CONV_A_WIDTH - 1) * nb
    pad = _round_up(hist, SUBLANE)
    has_state = state is not None
    in_specs = [pl.BlockSpec((rows, 2 * D_A), lambda s, j: (s * n_tiles + j, 0)),
                _layer_spec((CONV_A_WIDTH, D_A), layer),
                _layer_spec((1, D_A), layer), _layer_spec((1, D_A), layer), _layer_spec((1, D_A), layer)]
    args = [h_a, p['conv_a_w'], p['conv_a_b'], p['ln_a_w'], p['ln_a_b']]
    if has_state:
        assert n_seq == 1
        in_specs.append(_layer_spec((hist, D_A), layer))
        args.append(state)
    return pl.pallas_call(
        functools.partial(_conva_kernel, nb=nb, lt=lt, n_tiles=n_tiles, has_state=has_state),
        out_shape=(jax.ShapeDtypeStruct((h_a.shape[0], D_A), BF16),
                   jax.ShapeDtypeStruct((n_seq, hist, D_A), F32)),
        grid=(n_seq, n_tiles),
        in_specs=in_specs,
        out_specs=(pl.BlockSpec((rows, D_A), lambda s, j: (s * n_tiles + j, 0)),
                   pl.BlockSpec((1, hist, D_A), lambda s, j: (s, 0, 0))),
        scratch_shapes=[pltpu.VMEM((pad + rows, D_A), F32)],
        compiler_params=_cp("parallel", "arbitrary"),
        name="conva_mixer",
    )(*args)


def _gelu_tanh(x):
    return x * (0.5 * (1.0 + jnp.tanh(math.sqrt(2.0 / math.pi) * (x + 0.044715 * (x * x * x)))))


S5_BLOCKS = S5_LANES // LANE
S5_CHUNKS = SUBLANE


def _s5_glu_out(hs16, u, cc_ref, d_ref, gw_ref, gb_ref, y_ref):
    y = _dot(hs16, cc_ref[0]) + d_ref[0] * u
    y = _gelu_tanh(y)
    gate = _dot(y.astype(BF16), gw_ref[0]) + gb_ref[0]
    y_ref[...] = (y * _sigmoid(gate)).astype(y_ref.dtype)


def _s5_seq_kernel(u_ref, bb_ref, ab_ref, pw_ref, cc_ref, d_ref, gw_ref, gb_ref,
                   y_ref, nre_ref, nim_ref, hs_ref, cre_ref, cim_ref, *, lt):
    nblk = S5_BLOCKS
    clen = lt // S5_CHUNKS
    j = pl.program_id(1)

    @pl.when(j == 0)
    def _():
        cre_ref[...] = jnp.zeros(cre_ref.shape, F32)
        cim_ref[...] = jnp.zeros(cim_ref.shape, F32)

    u = u_ref[...]
    bu = _dot(u.astype(BF16), bb_ref[0])
    for c in range(2 * nblk):
        hs_ref[c * lt:(c + 1) * lt, :] = bu[:, c * LANE:(c + 1) * LANE]

    def chunk_rows(c, s):
        return pl.ds(c * lt + s, S5_CHUNKS, stride=clen)

    def local_scan(s, carry):
        new = []
        for c in range(nblk):
            ar = ab_ref[0, c:c + 1, :]
            ai = ab_ref[0, nblk + c:nblk + c + 1, :]
            hr, hi = carry[2 * c], carry[2 * c + 1]
            nr = ar * hr - ai * hi + hs_ref[chunk_rows(c, s), :]
            ni = ar * hi + ai * hr + hs_ref[chunk_rows(nblk + c, s), :]
            hs_ref[chunk_rows(c, s), :] = nr
            hs_ref[chunk_rows(nblk + c, s), :] = ni
            new += [nr, ni]
        return tuple(new)

    zero = jnp.zeros((S5_CHUNKS, LANE), F32)
    ends = lax.fori_loop(0, clen, local_scan, (zero,) * (2 * nblk))

    rowid = lax.broadcasted_iota(jnp.int32, (S5_CHUNKS, LANE), 0)
    entry = []
    for c in range(nblk):
        pr = pw_ref[0, clen - 1, c:c + 1, :]
        pi = pw_ref[0, clen - 1, nblk + c:nblk + c + 1, :]
        hr = cre_ref[:, c * LANE:(c + 1) * LANE]
        hi = cim_ref[:, c * LANE:(c + 1) * LANE]
        er = jnp.zeros((S5_CHUNKS, LANE), F32)
        ei = jnp.zeros((S5_CHUNKS, LANE), F32)
        for k in range(S5_CHUNKS):
            er = jnp.where(rowid == k, hr, er)
            ei = jnp.where(rowid == k, hi, ei)
            nr = ends[2 * c][k:k + 1, :] + pr * hr - pi * hi
            ni = ends[2 * c + 1][k:k + 1, :] + pr * hi + pi * hr
            hr, hi = nr, ni
        entry += [er, ei]
        cre_ref[:, c * LANE:(c + 1) * LANE] = hr
        cim_ref[:, c * LANE:(c + 1) * LANE] = hi
    nre_ref[0] = cre_ref[...]
    nim_ref[0] = cim_ref[...]

    def fix_up(s, carry):
        pw = pw_ref[0, s]
        for c in range(nblk):
            pr = pw[c:c + 1, :]
            pi = pw[nblk + c:nblk + c + 1, :]
            er, ei = entry[2 * c], entry[2 * c + 1]
            hs_ref[chunk_rows(c, s), :] = hs_ref[chunk_rows(c, s), :] + pr * er - pi * ei
            hs_ref[chunk_rows(nblk + c, s), :] = hs_ref[chunk_rows(nblk + c, s), :] + pr * ei + pi * er
        return carry

    lax.fori_loop(0, clen, fix_up, 0)

    hs16 = jnp.concatenate([hs_ref[c * lt:(c + 1) * lt, :].astype(BF16) for c in range(2 * nblk)], axis=1)
    _s5_glu_out(hs16, u, cc_ref, d_ref, gw_ref, gb_ref, y_ref)


def _s5_step_kernel(u_ref, bb_ref, ab_ref, cc_ref, d_ref, gw_ref, gb_ref, sre_ref, sim_ref,
                    y_ref, nre_ref, nim_ref, hs_ref, *, nb, lt):
    n = S5_LANES
    nblk = S5_BLOCKS
    u = u_ref[...]
    hs_ref[...] = _dot(u.astype(BF16), bb_ref[0])
    ab_re = jnp.concatenate([ab_ref[0, c:c + 1, :] for c in range(nblk)], axis=1)
    ab_im = jnp.concatenate([ab_ref[0, nblk + c:nblk + c + 1, :] for c in range(nblk)], axis=1)
    hr = sre_ref[0]
    hi = sim_ref[0]
    for t in range(lt):
        rs = slice(t * nb, (t + 1) * nb)
        nr = ab_re * hr - ab_im * hi + hs_ref[rs, 0:n]
        ni = ab_re * hi + ab_im * hr + hs_ref[rs, n:2 * n]
        hr, hi = nr, ni
        hs_ref[rs, 0:n] = hr
        hs_ref[rs, n:2 * n] = hi
    nre_ref[0] = hr
    nim_ref[0] = hi
    _s5_glu_out(hs_ref[...].astype(BF16), u, cc_ref, d_ref, gw_ref, gb_ref, y_ref)


def s5_mixer(h_b, p, layer, s_re, s_im, *, n_seq, nb, lt, n_tiles):
    rows = lt * nb
    n = S5_LANES
    has_state = s_re is not None
    in_specs = [pl.BlockSpec((rows, D_B), lambda s, j: (s * n_tiles + j, 0)),
                _layer_spec((D_B, 2 * n), layer),
                _layer_spec((2 * S5_BLOCKS, LANE), layer)]
    args = [h_b, p['s5_bb'], p['s5_ab']]
    if not has_state:
        in_specs.append(_layer_spec((lt // S5_CHUNKS, 2 * S5_BLOCKS, LANE), layer))
        args.append(p['s5_pw'])
    in_specs += [_layer_spec((2 * n, D_B), layer),
                 _layer_spec((1, D_B), layer),
                 _layer_spec((D_B, D_B), layer),
                 _layer_spec((1, D_B), layer)]
    args += [p['s5_cc'], p['s5_d'], p['s5_glu_w'], p['s5_glu_b']]
    if has_state:
        assert n_seq == 1 and n_tiles == 1
        in_specs += [_layer_spec((nb, n), layer)] * 2
        args += [s_re, s_im]
        body = functools.partial(_s5_step_kernel, nb=nb, lt=lt)
        scratch = [pltpu.VMEM((rows, 2 * n), F32)]
    else:
        assert nb == 1 and lt % (S5_CHUNKS * SUBLANE) == 0
        body = functools.partial(_s5_seq_kernel, lt=lt)
        scratch = [pltpu.VMEM((2 * S5_BLOCKS * lt, LANE), F32),
                   pltpu.VMEM((1, n), F32),
                   pltpu.VMEM((1, n), F32)]
    st_spec = pl.BlockSpec((1, nb, n), lambda s, j: (s, 0, 0))
    return pl.pallas_call(
        body,
        out_shape=(jax.ShapeDtypeStruct((h_b.shape[0], D_B), BF16),
                   jax.ShapeDtypeStruct((n_seq, nb, n), F32),
                   jax.ShapeDtypeStruct((n_seq, nb, n), F32)),
        grid=(n_seq, n_tiles),
        in_specs=in_specs,
        out_specs=(pl.BlockSpec((rows, D_B), lambda s, j: (s * n_tiles + j, 0)), st_spec, st_spec),
        scratch_shapes=scratch,
        compiler_params=_cp("parallel", "arbitrary"),
        name="s5_mixer",
    )(*args)


def _group_rmsnorm(y, nw):
    half = D_C // SSD_GROUPS
    outs = []
    for g in range(SSD_GROUPS):
        yg = y[:, g * half:(g + 1) * half]
        outs.append(yg * lax.rsqrt(jnp.mean(yg * yg, axis=-1, keepdims=True) + EPS))
    return jnp.concatenate(outs, axis=1) * nw


def _mamba_p_kernel(h_ref, cw_ref, cb_ref, dtb_ref, alog_ref, dexp_ref, nw_ref, e_ref, tril_ref,
                    y_ref, ncst_ref, nsst_ref, ext_ref, st_ref, *, lt, n_tiles):
    q = SSD_CHUNK
    hist = SSD_CONV_WIDTH - 1
    pad = SUBLANE
    half = D_C // SSD_GROUPS
    hpg = SSD_HEADS // SSD_GROUPS
    j = pl.program_id(1)

    @pl.when(j == 0)
    def _():
        ext_ref[0:pad, :] = jnp.zeros((pad, D_XBC), F32)
        st_ref[...] = jnp.zeros(st_ref.shape, F32)

    ext_ref[pad:pad + lt, :] = h_ref[:, D_C:D_C + D_XBC]

    e = e_ref[...]
    tril = tril_ref[...]
    a_neg = -jnp.exp(alog_ref[0])
    li = lax.broadcasted_iota(jnp.int32, (q, q), 0)
    si = lax.broadcasted_iota(jnp.int32, (q, q), 1)
    causal = li >= si
    lane = lax.broadcasted_iota(jnp.int32, (q, LANE), 1)

    for c in range(lt // q):
        r0 = c * q
        acc = jnp.zeros((q, D_XBC), F32) + cb_ref[0]
        for k in range(SSD_CONV_WIDTH):
            acc = acc + cw_ref[0, k:k + 1, :] * ext_ref[pl.ds(pad - hist + k + r0, q), :]
        xc = _silu(acc)
        xs = xc[:, 0:D_C]
        z = h_ref[r0:r0 + q, 0:D_C]
        dt = _softplus(h_ref[r0:r0 + q, D_C + D_XBC:D_C + D_XBC + LANE] + dtb_ref[0])
        a = dt * a_neg
        hi_, mid_, lo_ = _split3(a)
        cs = _dot(tril, hi_) + _dot(tril, mid_) + _dot(tril, lo_)
        cs_last = cs[q - 1:q, :]
        dt_x = _expand(dt, e)
        ecs_x = _expand(jnp.exp(cs), e)
        edl_x = _expand(jnp.exp(cs_last - cs), e)
        xdt = xs * dt_x
        cs_t = cs.T

        y_parts = []
        for g in range(SSD_GROUPS):
            bm = xc[:, D_C + g * SSD_STATE:D_C + (g + 1) * SSD_STATE]
            cm = xc[:, D_C + SSD_GROUPS * SSD_STATE + g * SSD_STATE:
                    D_C + SSD_GROUPS * SSD_STATE + (g + 1) * SSD_STATE]
            bm16 = bm.astype(BF16)
            cm16 = cm.astype(BF16)
            cb = _dot_nt(cm16, bm16)
            for pr in range(hpg // 2):
                r_even = g * hpg + 2 * pr
                xpair = xdt[:, r_even * SSD_HEAD_DIM:(r_even + 2) * SSD_HEAD_DIM].astype(BF16)
                ys = []
                for r in (r_even, r_even + 1):
                    seg = cs[:, r:r + 1] - cs_t[r:r + 1, :]
                    dec = jnp.exp(jnp.where(causal, seg, -jnp.inf))
                    ys.append(_dot((cb * dec).astype(BF16), xpair))
                y_parts.append(jnp.where(lane < SSD_HEAD_DIM, ys[0], ys[1]))
        y_diag = jnp.concatenate(y_parts, axis=1)
        y_off = jnp.concatenate(
            [_dot(xc[:, D_C + SSD_GROUPS * SSD_STATE + g * SSD_STATE:
                      D_C + SSD_GROUPS * SSD_STATE + (g + 1) * SSD_STATE].astype(BF16),
                  st_ref[:, g * half:(g + 1) * half].astype(BF16)) for g in range(SSD_GROUPS)],
            axis=1) * ecs_x
        y = y_diag + y_off + dexp_ref[0] * xs
        y = y * _silu(z)
        y_ref[r0:r0 + q, :] = _group_rmsnorm(y, nw_ref[0]).astype(y_ref.dtype)

        xw = (xdt * edl_x).astype(BF16)
        dec_row = ecs_x[q - 1:q, :]
        for g in range(SSD_GROUPS):
            bm_t = xc[:, D_C + g * SSD_STATE:D_C + (g + 1) * SSD_STATE].T.astype(BF16)
            upd = _dot(bm_t, xw[:, g * half:(g + 1) * half])
            st_ref[:, g * half:(g + 1) * half] = (
                st_ref[:, g * half:(g + 1) * half] * dec_row[:, g * half:(g + 1) * half] + upd)

    new_hist = ext_ref[pl.ds(pad + lt - hist, hist), :]
    ncst_ref[0] = new_hist
    if n_tiles > 1:
        ext_ref[pad - hist:pad, :] = new_hist

    @pl.when(j == n_tiles - 1)
    def _():
        for blk in range(D_C // LANE):
            nsst_ref[0, blk * LANE:(blk + 1) * LANE, :] = st_ref[:, blk * LANE:(blk + 1) * LANE].T


def _ssd_consts():
    head_of_lane = jnp.arange(D_C) // SSD_HEAD_DIM
    e = (jnp.arange(LANE)[:, None] == head_of_lane[None, :]).astype(BF16)
    tril = (jnp.arange(SSD_CHUNK)[:, None] >= jnp.arange(SSD_CHUNK)[None, :]).astype(BF16)
    return e, tril


def _ssd_param_specs(layer):
    return [_layer_spec((SSD_CONV_WIDTH, D_XBC), layer),
            _layer_spec((1, D_XBC), layer),
            _layer_spec((1, LANE), layer),
            _layer_spec((1, LANE), layer),
            _layer_spec((1, D_C), layer),
            _layer_spec((1, D_C), layer)]


def _ssd_param_args(p):
    return [p['conv_c_w'], p['conv_c_b'], p['ssd_dt_bias'], p['ssd_a_log'], p['ssd_d'], p['ssd_norm_w']]


def mamba_prompt(h_c, p, layer, *, n_seq, lt, n_tiles):
    hist = SSD_CONV_WIDTH - 1
    const = lambda s, j: (0, 0)
    return pl.pallas_call(
        functools.partial(_mamba_p_kernel, lt=lt, n_tiles=n_tiles),
        out_shape=(jax.ShapeDtypeStruct((h_c.shape[0], D_C), BF16),
                   jax.ShapeDtypeStruct((n_seq, hist, D_XBC), F32),
                   jax.ShapeDtypeStruct((n_seq, D_C, SSD_STATE), F32)),
        grid=(n_seq, n_tiles),
        in_specs=[pl.BlockSpec((lt, D_HC), lambda s, j: (s * n_tiles + j, 0))]
        + _ssd_param_specs(layer)
        + [pl.BlockSpec((LANE, D_C), const), pl.BlockSpec((SSD_CHUNK, SSD_CHUNK), const)],
        out_specs=(pl.BlockSpec((lt, D_C), lambda s, j: (s * n_tiles + j, 0)),
                   pl.BlockSpec((1, hist, D_XBC), lambda s, j: (s, 0, 0)),
                   pl.BlockSpec((1, D_C, SSD_STATE), lambda s, j: (s, 0, 0))),
        scratch_shapes=[pltpu.VMEM((SUBLANE + lt, D_XBC), F32),
                        pltpu.VMEM((SSD_STATE, D_C), F32)],
        compiler_params=_cp("parallel", "arbitrary"),
        name="mamba_prompt",
    )(h_c, *_ssd_param_args(p), p['ssd_e'], p['ssd_tril'])


def _ks(c, k, nb):
    return slice((c * SUBLANE + k) * nb, (c * SUBLANE + k + 1) * nb)


def _slab_put(ref, k, slab, nb):
    for c in range(slab.shape[1] // LANE):
        ref[_ks(c, k, nb), :] = slab[:, c * LANE:(c + 1) * LANE]


def _slab_get(ref, k, n_blocks, nb):
    return jnp.concatenate([ref[_ks(c, k, nb), :] for c in range(n_blocks)], axis=1)


def _seq_get(ref, b, n_blocks, nb):
    return jnp.concatenate(
        [ref[pl.ds(c * SUBLANE * nb + b, SUBLANE, stride=nb), :] for c in range(n_blocks)], axis=1)


def _seq_put(ref, b, val, nb, c0=0):
    for c in range(val.shape[1] // LANE):
        ref[pl.ds((c0 + c) * SUBLANE * nb + b, SUBLANE, stride=nb), :] = val[:, c * LANE:(c + 1) * LANE]


def _mamba_s_kernel(*refs, nb, lt, bb, has_prev):
    (h_ref, cw_ref, cb_ref, dtb_ref, alog_ref, dexp_ref, nw_ref, e_ref, cst_ref, sst_ref) = refs[:10]
    (y_ref, ncst_ref, nsst_ref,
     ext_ref, xs_ref, dt_ref, cs_ref, lhs_ref, rhs_ref, c8_ref, yoff_ref) = refs[10 + int(has_prev):]
    hist = (SSD_CONV_WIDTH - 1) * nb
    rows = lt * nb
    half = D_C // SSD_GROUPS
    hpg = SSD_HEADS // SSD_GROUPS
    xblk = D_C // LANE
    hblk = half // LANE
    i = pl.program_id(0)
    n_steps = pl.num_programs(0)
    bc_off = D_C
    cc_off = D_C + SSD_GROUPS * SSD_STATE

    @pl.when(i == 0)
    def _phase1():
        e = e_ref[...]
        ext_ref[0:hist, :] = cst_ref[0]
        ext_ref[hist:hist + rows, :] = h_ref[:, D_C:D_C + D_XBC]
        ncst_ref[...] = ext_ref[rows:rows + hist, :]
        a_neg = -jnp.exp(alog_ref[0])
        lhs_ref[...] = jnp.zeros(lhs_ref.shape, F32)
        rhs_ref[...] = jnp.zeros(rhs_ref.shape, F32)
        c8_ref[...] = jnp.zeros(c8_ref.shape, F32)
        cs = jnp.zeros((nb, LANE), F32)
        for t in range(lt):
            rs = slice(t * nb, (t + 1) * nb)
            acc = jnp.zeros((nb, D_XBC), F32) + cb_ref[0]
            for k in range(SSD_CONV_WIDTH):
                acc = acc + cw_ref[0, k:k + 1, :] * ext_ref[(t + k) * nb:(t + k + 1) * nb, :]
            xc = _silu(acc)
            xs_ref[rs, :] = xc[:, 0:D_C]
            for g in range(SSD_GROUPS):
                rhs_ref[_ks(2 * g, t, nb), :] = xc[:, bc_off + g * SSD_STATE:bc_off + (g + 1) * SSD_STATE]
            _slab_put(c8_ref, t, xc[:, cc_off:cc_off + SSD_GROUPS * SSD_STATE], nb)
            dt = _softplus(h_ref[rs, D_C + D_XBC:D_C + D_XBC + LANE] + dtb_ref[0])
            dt_ref[rs, :] = dt
            cs = cs + dt * a_neg
            cs_ref[rs, :] = cs
        cs_last = cs
        for t in range(lt):
            rs = slice(t * nb, (t + 1) * nb)
            wt = jnp.exp(cs_last - cs_ref[rs, :]) * dt_ref[rs, :]
            _slab_put(lhs_ref, t, xs_ref[rs, :] * _expand(wt, e), nb)
        dec = _expand(jnp.exp(cs_last), e)
        d_hi = dec.astype(BF16).astype(F32)
        d_r = dec - d_hi
        d_mid = d_r.astype(BF16).astype(F32)
        d_lo = d_r - d_mid
        ones = jnp.ones((nb, SSD_STATE), F32)
        for k, piece in enumerate((d_hi, d_mid, d_lo)):
            _slab_put(lhs_ref, lt + k, piece, nb)
            for g in range(SSD_GROUPS):
                rhs_ref[_ks(2 * g + 1, lt + k, nb), :] = ones

    for jb in range(bb):
        b = i * bb + jb
        l8 = _seq_get(lhs_ref, b, xblk, nb).astype(BF16)
        r8 = _seq_get(rhs_ref, b, 2 * SSD_GROUPS, nb).astype(BF16)
        c8 = _seq_get(c8_ref, b, SSD_GROUPS, nb).astype(BF16)
        for g in range(SSD_GROUPS):
            s = sst_ref[0, jb, g * half:(g + 1) * half, :]
            yo = _dot_nt(c8[:, g * SSD_STATE:(g + 1) * SSD_STATE], s.astype(BF16))
            _seq_put(yoff_ref, b, yo, nb, c0=g * hblk)
            upd = _dot_tn(l8[:, g * half:(g + 1) * half],
                          r8[:, g * 2 * SSD_STATE:(g + 1) * 2 * SSD_STATE])
            nsst_ref[0, jb, g * half:(g + 1) * half, :] = upd[:, SSD_STATE:] * s + upd[:, :SSD_STATE]

    @pl.when(i == n_steps - 1)
    def _phase3():
        e = e_ref[...]
        lane = lax.broadcasted_iota(jnp.int32, (nb, LANE), 1)
        for t in range(lt):
            rt = slice(t * nb, (t + 1) * nb)
            cs_t = cs_ref[rt, :]
            y = (_slab_get(yoff_ref, t, xblk, nb) * _expand(jnp.exp(cs_t), e)
                 + dexp_ref[0] * xs_ref[rt, :])
            for s_ in range(t + 1):
                rsl = slice(s_ * nb, (s_ + 1) * nb)
                cbs = []
                for g in range(SSD_GROUPS):
                    cm = c8_ref[_ks(g, t, nb), :]
                    bm = rhs_ref[_ks(2 * g, s_, nb), :]
                    cbs.append(jnp.sum(cm * bm, axis=-1, keepdims=True))
                cb = jnp.where(lane < hpg, cbs[0], cbs[1])
                m = jnp.exp(cs_t - cs_ref[rsl, :]) * dt_ref[rsl, :] * cb
                y = y + _expand(m, e) * xs_ref[rsl, :]
            y = y * _silu(h_ref[rt, 0:D_C])
            y_ref[rt, :] = _group_rmsnorm(y, nw_ref[0]).astype(y_ref.dtype)


def mamba_sample(h_c, p, layer, cst, sst, prev_out, *, nb, lt, bb):
    rows = lt * nb
    hist = (SSD_CONV_WIDTH - 1) * nb
    const = lambda i: (0, 0)
    sst_spec = pl.BlockSpec((1, bb, D_C, SSD_STATE), lambda i: (layer, i, 0, 0))
    in_specs = ([pl.BlockSpec((rows, D_HC), const)] + _ssd_param_specs(layer)
                + [pl.BlockSpec((LANE, D_C), const), _layer_spec((hist, D_XBC), layer), sst_spec])
    args = [h_c, *_ssd_param_args(p), p['ssd_e'], cst, sst]
    aliases = {}
    if prev_out is not None:
        in_specs.append(pl.BlockSpec(memory_space=pl.ANY))
        args.append(prev_out)
        aliases = {len(in_specs) - 1: 2}
    return pl.pallas_call(
        functools.partial(_mamba_s_kernel, nb=nb, lt=lt, bb=bb, has_prev=prev_out is not None),
        out_shape=(jax.ShapeDtypeStruct((rows, D_C), BF16),
                   jax.ShapeDtypeStruct((hist, D_XBC), F32),
                   jax.ShapeDtypeStruct(sst.shape, F32)),
        grid=(nb // bb,),
        in_specs=in_specs,
        out_specs=(pl.BlockSpec((rows, D_C), const),
                   pl.BlockSpec((hist, D_XBC), const),
                   sst_spec),
        input_output_aliases=aliases,
        scratch_shapes=[pltpu.VMEM((hist + rows, D_XBC), F32),
                        pltpu.VMEM((rows, D_C), F32),
                        pltpu.VMEM((rows, LANE), F32),
                        pltpu.VMEM((rows, LANE), F32),
                        pltpu.VMEM((D_C // LANE * SUBLANE * nb, LANE), F32),
                        pltpu.VMEM((2 * SSD_GROUPS * SUBLANE * nb, LANE), F32),
                        pltpu.VMEM((SSD_GROUPS * SUBLANE * nb, LANE), F32),
                        pltpu.VMEM((D_C // LANE * SUBLANE * nb, LANE), F32)],
        compiler_params=_cp("arbitrary"),
        name="mamba_sample",
    )(*args)


def _softmax_rows(s):
    m = jnp.max(s, axis=-1, keepdims=True)
    ex = jnp.exp(s - m)
    return ex / jnp.sum(ex, axis=-1, keepdims=True)


def _attn_p_kernel(q_ref, k_ref, v_ref, o_ref):
    for h in range(XA_HEADS):
        hs = slice(h * XA_HEAD_DIM, (h + 1) * XA_HEAD_DIM)
        s = _dot_nt(q_ref[:, hs].astype(BF16), k_ref[0, :, hs].astype(BF16)) / math.sqrt(XA_HEAD_DIM)
        p = _softmax_rows(s)
        o_ref[:, hs] = _dot(p.astype(BF16), v_ref[0, :, hs].astype(BF16)).astype(o_ref.dtype)


def attn_prompt(q, k, v, *, n_seq, seq, tq):
    n_tiles = seq // tq
    return pl.pallas_call(
        _attn_p_kernel,
        out_shape=jax.ShapeDtypeStruct(q.shape, BF16),
        grid=(n_seq, n_tiles),
        in_specs=[pl.BlockSpec((tq, D_MODEL), lambda s, j: (s * n_tiles + j, 0)),
                  pl.BlockSpec((1, N_MEM, D_MODEL), lambda s, j: (s, 0, 0)),
                  pl.BlockSpec((1, N_MEM, D_MODEL), lambda s, j: (s, 0, 0))],
        out_specs=pl.BlockSpec((tq, D_MODEL), lambda s, j: (s * n_tiles + j, 0)),
        compiler_params=_cp("parallel", "arbitrary"),
        name="attn_prompt",
    )(q, k, v)


def _attn_s_kernel(q_ref, k_ref, v_ref, o_ref, *, bb, lt):
    rows = XA_HEADS * lt
    n = N_MEM * XA_HEADS
    col_head = lax.broadcasted_iota(jnp.int32, (rows, n), 1) % XA_HEADS
    row_head = lax.broadcasted_iota(jnp.int32, (rows, n), 0) // lt
    same_head = col_head == row_head
    for jb in range(bb):
        k = k_ref[0, jb].reshape(n, XA_HEAD_DIM).astype(BF16)
        v = v_ref[0, jb].reshape(n, XA_HEAD_DIM).astype(BF16)
        s = _dot_nt(q_ref[jb].astype(BF16), k) / math.sqrt(XA_HEAD_DIM)
        p = _softmax_rows(jnp.where(same_head, s, -jnp.inf))
        o_ref[jb] = _dot(p.astype(BF16), v)


def attn_sample(q, k, v, layer, *, bb):
    nb, rows, _ = q.shape
    kv_spec = pl.BlockSpec((1, bb, N_MEM, XA_HEADS, XA_HEAD_DIM), lambda i: (layer, i, 0, 0, 0))
    return pl.pallas_call(
        functools.partial(_attn_s_kernel, bb=bb, lt=rows // XA_HEADS),
        out_shape=jax.ShapeDtypeStruct((nb, rows, XA_HEAD_DIM), F32),
        grid=(nb // bb,),
        in_specs=[pl.BlockSpec((bb, rows, XA_HEAD_DIM), lambda i: (i, 0, 0)), kv_spec, kv_spec],
        out_specs=pl.BlockSpec((bb, rows, XA_HEAD_DIM), lambda i: (i, 0, 0)),
        compiler_params=_cp("parallel"),
        name="attn_sample",
    )(q, k, v)


def _ffn_kernel(*refs, nb, tiles_per_seq, has_state, final_norm):
    refs = list(refs)
    x_ref, nw_ref, wg_ref, wu_ref, cw_ref, cb_ref, wd_ref = refs[:7]
    pos = 7
    st_ref = None
    if has_state:
        st_ref = refs[pos]
        pos += 1
    fw_ref = None
    if final_norm:
        fw_ref = refs[pos]
        pos += 1
    o_ref, nst_ref, xn_ref, gext_ref, carry_ref = refs[pos:pos + 5]

    hist = (FFN_CONV_WIDTH - 1) * nb
    pad = _round_up(hist, SUBLANE)
    tm = x_ref.shape[0]
    tf = wg_ref.shape[3]
    rsz = min(FFN_ROW_SPLIT, tm)
    i = pl.program_id(0)
    f = pl.program_id(1)
    n_f = pl.num_programs(1)

    @pl.when(f == 0)
    def _():
        x = x_ref[...]
        xn_ref[...] = _rmsnorm_rows(x, nw_ref[0]).astype(BF16)
        o_ref[...] = x

    if tiles_per_seq > 1:
        first = (i % tiles_per_seq) == 0

        @pl.when(first)
        def _():
            if has_state:
                gext_ref[pad - hist:pad, :] = st_ref[0]
            else:
                gext_ref[0:pad, :] = jnp.zeros((pad, tf), F32)

        @pl.when(jnp.logical_not(first))
        def _():
            gext_ref[0:pad, :] = carry_ref[f]
    else:
        if has_state:
            gext_ref[pad - hist:pad, :] = st_ref[0]
        else:
            gext_ref[0:pad, :] = jnp.zeros((pad, tf), F32)

    cw0 = cw_ref[0, 0:1, :]
    cw1 = cw_ref[0, 1:2, :]
    cw2 = cw_ref[0, 2:3, :]
    cb = cb_ref[0]
    for r0 in range(0, tm, rsz):
        xn = xn_ref[r0:r0 + rsz, :]
        g = _dot(xn, wg_ref[0, 0])
        up = _dot(xn, wu_ref[0, 0])
        gext_ref[pad + r0:pad + r0 + rsz, :] = g
        conv = (cw0 * gext_ref[pl.ds(pad - 2 * nb + r0, rsz), :]
                + cw1 * gext_ref[pl.ds(pad - nb + r0, rsz), :]
                + cw2 * g + cb)
        act = _silu(conv) * up
        o_ref[r0:r0 + rsz, :] += _dot(act.astype(BF16), wd_ref[0])

    nst_ref[0] = gext_ref[pl.ds(pad + tm - hist, hist), :]
    if tiles_per_seq > 1:
        carry_ref[f] = gext_ref[pl.ds(tm, pad), :]

    if final_norm:
        @pl.when(f == n_f - 1)
        def _():
            o_ref[...] = _rmsnorm_rows(o_ref[...], fw_ref[...])


def conv_ffn(x, p, layer, state, final_w, *, n_seq, nb, tm, tiles_per_seq):
    m = x.shape[0]
    tf = FF_TILE
    n_f = D_FF_PAD // tf
    hist = (FFN_CONV_WIDTH - 1) * nb
    pad = _round_up(hist, SUBLANE)
    has_state = state is not None
    final_norm = final_w is not None
    in_specs = [pl.BlockSpec((tm, D_MODEL), lambda i, f: (i, 0)),
                _layer_spec((1, D_MODEL), layer),
                pl.BlockSpec((1, 1, D_MODEL, tf), lambda i, f: (layer, f, 0, 0)),
                pl.BlockSpec((1, 1, D_MODEL, tf), lambda i, f: (layer, f, 0, 0)),
                pl.BlockSpec((1, FFN_CONV_WIDTH, tf), lambda i, f: (layer, 0, f)),
                pl.BlockSpec((1, 1, tf), lambda i, f: (layer, 0, f)),
                pl.BlockSpec((1, tf, D_MODEL), lambda i, f: (layer, f, 0))]
    args = [x, p['norm_ffn_w'], p['ffn_wg'], p['ffn_wu'], p['ffn_conv_w'], p['ffn_conv_b'], p['ffn_wd']]
    if has_state:
        assert n_seq == 1 and tiles_per_seq == 1
        in_specs.append(pl.BlockSpec((1, hist, tf), lambda i, f: (layer, 0, f)))
        args.append(state)
    if final_norm:
        in_specs.append(pl.BlockSpec((1, D_MODEL), lambda i, f: (0, 0)))
        args.append(final_w.reshape(1, D_MODEL))
    return pl.pallas_call(
        functools.partial(_ffn_kernel, nb=nb, tiles_per_seq=tiles_per_seq, has_state=has_state,
                          final_norm=final_norm),
        out_shape=(jax.ShapeDtypeStruct((m, D_MODEL), F32),
                   jax.ShapeDtypeStruct((m // tm, hist, D_FF_PAD), F32)),
        grid=(m // tm, n_f),
        in_specs=in_specs,
        out_specs=(pl.BlockSpec((tm, D_MODEL), lambda i, f: (i, 0)),
                   pl.BlockSpec((1, hist, tf), lambda i, f: (i, 0, f))),
        scratch_shapes=[pltpu.VMEM((tm, D_MODEL), BF16),
                        pltpu.VMEM((pad + tm, tf), F32),
                        pltpu.VMEM((n_f, pad, tf), F32)],
        compiler_params=_cp("arbitrary", "arbitrary"),
        name="conv_ffn",
    )(*args)


def _s5_params(lam_re, lam_im, log_dt, b_re, b_im, c_re, c_im, n_pow):
    depth = lam_re.shape[0]
    dt = jnp.exp(log_dt)[..., None]
    mag = jnp.exp(lam_re * dt)
    ang = lam_im * dt
    ab_re, ab_im = mag * jnp.cos(ang), mag * jnp.sin(ang)
    blocks = lambda re, im: jnp.concatenate(
        [re.reshape(*re.shape[:-2], S5_BLOCKS, LANE), im.reshape(*im.shape[:-2], S5_BLOCKS, LANE)], axis=-2)
    kpow = jnp.arange(1, n_pow + 1, dtype=F32)[None, :, None, None]
    pmag = jnp.exp((lam_re * dt)[:, None] * kpow)
    pang = ang[:, None] * kpow
    pw = blocks(pmag * jnp.cos(pang), pmag * jnp.sin(pang))
    den = lam_re * lam_re + lam_im * lam_im
    nr, ni = ab_re - 1.0, ab_im
    co_re = (nr * lam_re + ni * lam_im) / den
    co_im = (ni * lam_re - nr * lam_im) / den
    bb_re = co_re[..., None] * b_re - co_im[..., None] * b_im
    bb_im = co_re[..., None] * b_im + co_im[..., None] * b_re
    eye = jnp.eye(S5_GROUPS, dtype=F32)
    dense_b = lambda m: jnp.einsum('lgph,gk->lghkp', m, eye).reshape(depth, D_B, S5_LANES)
    dense_c = lambda m: jnp.einsum('lghp,gk->lkpgh', m, eye).reshape(depth, S5_LANES, D_B)
    bb = jnp.concatenate([dense_b(bb_re), dense_b(bb_im)], axis=2).astype(BF16)
    cc = jnp.concatenate([dense_c(c_re), -dense_c(c_im)], axis=1).astype(BF16)
    return bb, blocks(ab_re, ab_im), pw, cc


def _wprep_kernel(w_ref, o_ref, *, axis, valid_last):
    f = pl.program_id(1)
    last = pl.num_programs(1) - 1
    o = o_ref.at[0, 0] if axis == 1 else o_ref.at[0]

    @pl.when(f < last)
    def _():
        o[...] = w_ref[0].astype(BF16)

    @pl.when(f == last)
    def _():
        if axis == 1:
            o[:, :valid_last] = w_ref[0, :, :valid_last].astype(BF16)
            o[:, valid_last:] = jnp.zeros((o.shape[0], o.shape[1] - valid_last), BF16)
        else:
            o[:valid_last, :] = w_ref[0, :valid_last, :].astype(BF16)
            o[valid_last:, :] = jnp.zeros((o.shape[0] - valid_last, o.shape[1]), BF16)


def ffn_weight_cols(w, tf):
    depth, k, n = w.shape
    n_f = pl.cdiv(n, tf)
    return pl.pallas_call(
        functools.partial(_wprep_kernel, axis=1, valid_last=n - (n_f - 1) * tf),
        out_shape=jax.ShapeDtypeStruct((depth, n_f, k, tf), BF16),
        grid=(depth, n_f),
        in_specs=[pl.BlockSpec((1, k, tf), lambda l, f: (l, 0, f))],
        out_specs=pl.BlockSpec((1, 1, k, tf), lambda l, f: (l, f, 0, 0)),
        compiler_params=_cp("parallel", "parallel"),
        name="ffn_weight_cols",
    )(w)


def ffn_weight_rows(w, tf):
    depth, k, n = w.shape
    n_f = pl.cdiv(k, tf)
    return pl.pallas_call(
        functools.partial(_wprep_kernel, axis=0, valid_last=k - (n_f - 1) * tf),
        out_shape=jax.ShapeDtypeStruct((depth, n_f * tf, n), BF16),
        grid=(depth, n_f),
        in_specs=[pl.BlockSpec((1, tf, n), lambda l, f: (l, f, 0))],
        out_specs=pl.BlockSpec((1, tf, n), lambda l, f: (l, f, 0)),
        compiler_params=_cp("parallel", "parallel"),
        name="ffn_weight_rows",
    )(w)


def kernel(x_prompt, x_sample, mem_prompt, cache_mem_k, cache_mem_v, state_conv_a, state_s5_re, state_s5_im, state_conv_c, state_ssd, state_ffn_conv, norm_mix_w, w_in, conv_a_w, conv_a_b, ln_a_w, ln_a_b, s5_lam_re, s5_lam_im, s5_log_dt, s5_b_re, s5_b_im, s5_c_re, s5_c_im, s5_d, s5_glu_w, s5_glu_b, conv_c_w, conv_c_b, ssd_dt_bias, ssd_a_log, ssd_d, ssd_norm_w, w_out, norm_xa_w, norm_mem_w, xa_wq, xa_wk, xa_wv, xa_wo, norm_ffn_w, ffn_w_gate, ffn_w_up, ffn_conv_w, ffn_conv_b, ffn_w_down, final_norm_w):
    bp, seq, _ = x_prompt.shape
    nbs, lts, _ = x_sample.shape
    depth = w_in.shape[0]
    n_mem = mem_prompt.shape[1]
    lt_p = 512 if seq % 512 == 0 else seq
    n_tiles_p = seq // lt_p
    tm_p = lt_p
    tm_s = lts * nbs
    tm_m = min(512, bp * n_mem)

    vec = lambda a: a.reshape(depth, 1, a.shape[-1])
    pad_lanes = lambda a: vec(jnp.pad(a, ((0, 0), (0, LANE - a.shape[-1]))))
    sp_a, sp_b = 2 * D_A, 2 * D_A + D_B
    ff_pad = D_FF_PAD - D_FF
    s5_bb, s5_ab, s5_pw, s5_cc = _s5_params(s5_lam_re, s5_lam_im, s5_log_dt, s5_b_re, s5_b_im, s5_c_re,
                                            s5_c_im, lt_p // S5_CHUNKS)
    ssd_e, ssd_tril = _ssd_consts()
    p = {
        'norm_mix_w': vec(norm_mix_w), 'norm_xa_w': vec(norm_xa_w), 'norm_mem_w': vec(norm_mem_w),
        'norm_ffn_w': vec(norm_ffn_w),
        'w_in_a': w_in[:, :, :sp_a].astype(BF16),
        'w_in_b': w_in[:, :, sp_a:sp_b].astype(BF16),
        'w_in_c': jnp.pad(w_in[:, :, sp_b:], ((0, 0), (0, 0), (0, D_HC - (w_in.shape[2] - sp_b)))).astype(BF16),
        'conv_a_w': conv_a_w, 'conv_a_b': vec(conv_a_b), 'ln_a_w': vec(ln_a_w), 'ln_a_b': vec(ln_a_b),
        's5_bb': s5_bb, 's5_ab': s5_ab, 's5_pw': s5_pw, 's5_cc': s5_cc,
        's5_d': vec(s5_d), 's5_glu_w': s5_glu_w.astype(BF16), 's5_glu_b': vec(s5_glu_b),
        'conv_c_w': conv_c_w, 'conv_c_b': vec(conv_c_b),
        'ssd_dt_bias': pad_lanes(ssd_dt_bias), 'ssd_a_log': pad_lanes(ssd_a_log),
        'ssd_d': vec(jnp.repeat(ssd_d, SSD_HEAD_DIM, axis=1)), 'ssd_norm_w': vec(ssd_norm_w),
        'ssd_e': ssd_e, 'ssd_tril': ssd_tril,
        'w_out': w_out.astype(BF16), 'wq': xa_wq.astype(BF16), 'wo': xa_wo.astype(BF16),
        'wkv': jnp.concatenate([xa_wk, xa_wv], axis=2).astype(BF16),
        'ffn_wg': ffn_weight_cols(ffn_w_gate, FF_TILE),
        'ffn_wu': ffn_weight_cols(ffn_w_up, FF_TILE),
        'ffn_wd': ffn_weight_rows(ffn_w_down, FF_TILE),
        'ffn_conv_w': jnp.pad(ffn_conv_w, ((0, 0), (0, 0), (0, ff_pad))),
        'ffn_conv_b': vec(jnp.pad(ffn_conv_b, ((0, 0), (0, ff_pad)))),
    }

    tmaj = lambda a: a.transpose(0, 2, 1, 3).reshape(depth, a.shape[2] * nbs, a.shape[3])
    st_conv_a = tmaj(state_conv_a)
    st_conv_c = tmaj(state_conv_c)
    st_ffn = jnp.pad(tmaj(state_ffn_conv), ((0, 0), (0, 0), (0, ff_pad)))
    st_re = state_s5_re.reshape(depth, nbs, S5_LANES)
    st_im = state_s5_im.reshape(depth, nbs, S5_LANES)
    ssd_old = state_ssd.reshape(depth, nbs, D_C, SSD_STATE)
    ssd_new = None

    xp = x_prompt.reshape(bp * seq, D_MODEL)
    xs = x_sample.transpose(1, 0, 2).reshape(lts * nbs, D_MODEL)
    mem2d = mem_prompt.reshape(bp * n_mem, D_MODEL)

    def mixers(x, l, *, n_seq, nb, lt, n_tiles, tm, sample):
        h_a = norm_matmul(x, p['norm_mix_w'], p['w_in_a'], l, tm=tm, tn=2 * D_A, name="in_proj_a")
        h_b = norm_matmul(x, p['norm_mix_w'], p['w_in_b'], l, tm=tm, tn=D_B, name="in_proj_b")
        h_c = norm_matmul(x, p['norm_mix_w'], p['w_in_c'], l, tm=tm, tn=D_HC // 3, name="in_proj_c")
        ya, n_conv_a = conva_mixer(h_a, p, l, st_conv_a if sample else None,
                                   n_seq=n_seq, nb=nb, lt=lt, n_tiles=n_tiles)
        yb, n_re, n_im = s5_mixer(h_b, p, l, st_re if sample else None, st_im if sample else None,
                                  n_seq=n_seq, nb=nb, lt=lt, n_tiles=n_tiles)
        return h_c, ya, yb, n_conv_a, n_re, n_im

    outs_p = [[] for _ in range(8)]
    outs_s = [[] for _ in range(5)]
    for l in range(depth):
        last = l == depth - 1
        mk, mv = mem_kv(mem2d, p['norm_mem_w'], p['wkv'], l, tm=tm_m)

        h_c, ya, yb, p_conv_a, p_re, p_im = mixers(xp, l, n_seq=bp, nb=1, lt=lt_p, n_tiles=n_tiles_p,
                                                   tm=tm_p, sample=False)
        yc, p_conv_c, p_ssd = mamba_prompt(h_c, p, l, n_seq=bp, lt=lt_p, n_tiles=n_tiles_p)
        xp = proj_res([ya, yb, yc], p['w_out'], l, xp, tm=tm_p, name="out_proj")
        q = norm_matmul(xp, p['norm_xa_w'], p['wq'], l, tm=tm_p, tn=1024, name="q_proj")
        o = attn_prompt(q, mk.reshape(bp, n_mem, D_MODEL), mv.reshape(bp, n_mem, D_MODEL),
                        n_seq=bp, seq=seq, tq=lt_p)
        xp = proj_res([o], p['wo'], l, xp, tm=tm_p, name="attn_out")
        xp, p_ffn = conv_ffn(xp, p, l, None, final_norm_w if last else None, n_seq=bp, nb=1, tm=tm_p,
                             tiles_per_seq=seq // tm_p)
        for lst, v in zip(outs_p, (p_conv_a, p_re, p_im, p_conv_c, p_ssd,
                                   p_ffn[seq // tm_p - 1::seq // tm_p], mk, mv)):
            lst.append(v)

        h_c, ya, yb, s_conv_a, s_re, s_im = mixers(xs, l, n_seq=1, nb=nbs, lt=lts, n_tiles=1, tm=tm_s,
                                                   sample=True)
        yc, s_conv_c, ssd_new = mamba_sample(h_c, p, l, st_conv_c, ssd_old, ssd_new, nb=nbs, lt=lts, bb=8)
        xs = proj_res([ya, yb, yc], p['w_out'], l, xs, tm=tm_s, name="out_proj")
        q = norm_matmul(xs, p['norm_xa_w'], p['wq'], l, tm=tm_s, tn=1024, name="q_proj")
        q = q.reshape(lts, nbs, XA_HEADS, XA_HEAD_DIM).transpose(1, 2, 0, 3)
        o = attn_sample(q.reshape(nbs, XA_HEADS * lts, XA_HEAD_DIM), cache_mem_k, cache_mem_v, l, bb=4)
        o = o.reshape(nbs, XA_HEADS, lts, XA_HEAD_DIM).transpose(2, 0, 1, 3).reshape(lts * nbs, D_MODEL)
        xs = proj_res([o], p['wo'], l, xs, tm=tm_s, name="attn_out")
        xs, s_ffn = conv_ffn(xs, p, l, st_ffn, final_norm_w if last else None, n_seq=1, nb=nbs, tm=tm_s,
                             tiles_per_seq=1)
        for lst, v in zip(outs_s, (s_conv_a[0], s_re[0], s_im[0], s_conv_c, s_ffn[0])):
            lst.append(v)

    p_conv_a, p_re, p_im, p_conv_c, p_ssd, p_ffn, p_mk, p_mv = [jnp.stack(o) for o in outs_p]
    s_conv_a, s_re, s_im, s_conv_c, s_ffn = [jnp.stack(o) for o in outs_s]
    bmaj = lambda a, w: a.reshape(depth, w, nbs, a.shape[-1]).transpose(0, 2, 1, 3)
    y_prompt = xp.reshape(bp, seq, D_MODEL)
    y_sample = xs.reshape(lts, nbs, D_MODEL).transpose(1, 0, 2)
    return (y_prompt, y_sample,
            p_conv_a,
            p_re.reshape(depth, bp, S5_GROUPS, S5_STATE), p_im.reshape(depth, bp, S5_GROUPS, S5_STATE),
            p_conv_c,
            p_ssd.reshape(depth, bp, SSD_HEADS, SSD_HEAD_DIM, SSD_STATE),
            p_ffn[..., :D_FF],
            p_mk.reshape(depth, bp, n_mem, XA_HEADS, XA_HEAD_DIM),
            p_mv.reshape(depth, bp, n_mem, XA_HEADS, XA_HEAD_DIM),
            bmaj(s_conv_a, CONV_A_WIDTH - 1),
            s_re.reshape(depth, nbs, S5_GROUPS, S5_STATE), s_im.reshape(depth, nbs, S5_GROUPS, S5_STATE),
            bmaj(s_conv_c, SSD_CONV_WIDTH - 1),
            ssd_new.reshape(state_ssd.shape),
            bmaj(s_ffn, FFN_CONV_WIDTH - 1)[..., :D_FF])
```

```python
import functools
import math

import jax
import jax.numpy as jnp
from jax import lax
from jax.experimental import pallas as pl
from jax.experimental.pallas import tpu as pltpu

F32 = jnp.float32
BF16 = jnp.bfloat16
EPS = 1e-6

D_MODEL = 2048
D_A = 512
D_B = 512
D_C = 1024
CONV_A_WIDTH = 31
S5_GROUP = 16
S5_GROUPS = 32
S5_STATE = 64
S5_LANES = S5_GROUPS * S5_STATE
SSD_HEAD_DIM = 64
SSD_HEADS = 16
SSD_GROUPS = 2
SSD_STATE = 128
SSD_CONV_WIDTH = 4
SSD_CHUNK = 128
D_XBC = D_C + 2 * SSD_GROUPS * SSD_STATE
D_HC = D_C + D_XBC + 128
XA_HEADS = 4
XA_HEAD_DIM = 512
N_MEM = 256
D_FF = 5504
FFN_CONV_WIDTH = 3

LANE = 128
SUBLANE = 8
VMEM_LIMIT = 56 * 1024 * 1024
FF_TILE = 512
D_FF_PAD = ((D_FF + FF_TILE - 1) // FF_TILE) * FF_TILE
FFN_ROW_SPLIT = 256


def _round_up(x, m):
    return (x + m - 1) // m * m


def _cp(*sem):
    return pltpu.CompilerParams(dimension_semantics=sem, vmem_limit_bytes=VMEM_LIMIT)


def _layer_spec(tail, layer):
    zeros = (0,) * len(tail)
    return pl.BlockSpec((1,) + tuple(tail), lambda *_: (layer,) + zeros)


def _dot(a, b):
    return jnp.dot(a, b, preferred_element_type=F32)


def _dot_nt(a, b):
    return lax.dot_general(a, b, (((1,), (1,)), ((), ())), preferred_element_type=F32)


def _dot_tn(a, b):
    return lax.dot_general(a, b, (((0,), (0,)), ((), ())), preferred_element_type=F32)


def _split3(a):
    hi = a.astype(BF16)
    r = a - hi.astype(F32)
    mid = r.astype(BF16)
    lo = (r - mid.astype(F32)).astype(BF16)
    return hi, mid, lo


def _expand(a, e):
    hi, mid, lo = _split3(a)
    return _dot(hi, e) + _dot(mid, e) + _dot(lo, e)


def _sigmoid(x):
    return jax.nn.sigmoid(x)


def _silu(x):
    return x * jax.nn.sigmoid(x)


def _softplus(x):
    return jnp.maximum(x, 0.0) + jnp.log1p(jnp.exp(-jnp.abs(x)))


def _rmsnorm_rows(x, w):
    ms = jnp.mean(x * x, axis=-1, keepdims=True)
    return x * lax.rsqrt(ms + EPS) * w


def _mem_kv_kernel(x_ref, nw_ref, w_ref, k_ref, v_ref, xn_ref):
    j = pl.program_id(1)

    @pl.when(j == 0)
    def _():
        xn = _rmsnorm_rows(x_ref[...], nw_ref[0]).astype(BF16)
        xn_ref[...] = xn
        k_ref[...] = _dot(xn, w_ref[0])

    @pl.when(j == 1)
    def _():
        v_ref[...] = _dot(xn_ref[...], w_ref[0])


def mem_kv(x, nw, w, layer, *, tm):
    m, k = x.shape
    n = w.shape[2] // 2
    out = jax.ShapeDtypeStruct((m, n), F32)
    return pl.pallas_call(
        _mem_kv_kernel,
        out_shape=(out, out),
        grid=(m // tm, 2),
        in_specs=[pl.BlockSpec((tm, k), lambda i, j: (i, 0)),
                  _layer_spec((1, k), layer),
                  pl.BlockSpec((1, k, n), lambda i, j: (layer, 0, j))],
        out_specs=(pl.BlockSpec((tm, n), lambda i, j: (i, 0)), pl.BlockSpec((tm, n), lambda i, j: (i, 0))),
        scratch_shapes=[pltpu.VMEM((tm, k), BF16)],
        compiler_params=_cp("parallel", "arbitrary"),
        name="mem_kv",
    )(x, nw, w)


def _attn_out_seq_kernel(o_ref, w_ref, res_ref, out_ref, a_ref, *, nb, lt):
    for t in range(lt):
        for h in range(XA_HEADS):
            a_ref[t * nb:(t + 1) * nb, h * XA_HEAD_DIM:(h + 1) * XA_HEAD_DIM] = o_ref[:, h * lt + t, :]
    out_ref[...] = res_ref[...] + _dot(a_ref[...].astype(BF16), w_ref[0])


def attn_out_seq(o, w, layer, res, *, nb, lt):
    m, n = res.shape
    return pl.pallas_call(
        functools.partial(_attn_out_seq_kernel, nb=nb, lt=lt),
        out_shape=jax.ShapeDtypeStruct((m, n), F32),
        grid=(1,),
        in_specs=[pl.BlockSpec(o.shape, lambda i: (0, 0, 0)),
                  _layer_spec(w.shape[1:], layer),
                  pl.BlockSpec((m, n), lambda i: (0, 0))],
        out_specs=pl.BlockSpec((m, n), lambda i: (0, 0)),
        scratch_shapes=[pltpu.VMEM((m, n), F32)],
        compiler_params=_cp("arbitrary"),
        name="attn_out_seq",
    )(o, w, res)


IN_SPLITS = (2 * D_A, 2 * D_A + D_B, 2 * D_A + D_B + D_HC)


def _in_proj_kernel(x_ref, nw_ref, w_ref, ha_ref, hb_ref, hc_ref):
    xn = _rmsnorm_rows(x_ref[...], nw_ref[0]).astype(BF16)
    ha_ref[...] = _dot(xn, w_ref[0, :, 0:IN_SPLITS[0]])
    hb_ref[...] = _dot(xn, w_ref[0, :, IN_SPLITS[0]:IN_SPLITS[1]])
    hc_ref[...] = _dot(xn, w_ref[0, :, IN_SPLITS[1]:IN_SPLITS[2]])


def in_proj(x, nw, w, layer, *, tm):
    m, k = x.shape
    widths = (IN_SPLITS[0], IN_SPLITS[1] - IN_SPLITS[0], IN_SPLITS[2] - IN_SPLITS[1])
    return pl.pallas_call(
        _in_proj_kernel,
        out_shape=tuple(jax.ShapeDtypeStruct((m, wd), F32) for wd in widths),
        grid=(m // tm,),
        in_specs=[pl.BlockSpec((tm, k), lambda i: (i, 0)),
                  _layer_spec((1, k), layer),
                  pl.BlockSpec((1, k, IN_SPLITS[2]), lambda i: (layer, 0, 0), pipeline_mode=pl.Buffered(1))],
        out_specs=tuple(pl.BlockSpec((tm, wd), lambda i: (i, 0)) for wd in widths),
        compiler_params=_cp("parallel"),
        name="in_proj",
    )(x, nw, w)


def _proj_res_kernel(*refs, n_in):
    a_refs = refs[:n_in]
    w_refs = refs[n_in:2 * n_in]
    res_ref, o_ref = refs[2 * n_in], refs[2 * n_in + 1]
    acc = res_ref[...]
    for a_ref, w_ref in zip(a_refs, w_refs):
        acc = acc + _dot(a_ref[...].astype(BF16), w_ref[0])
    o_ref[...] = acc


def proj_res(a_list, w, layer, res, *, tm, name):
    m, n = res.shape
    n_in = len(a_list)
    in_specs = [pl.BlockSpec((tm, a.shape[1]), lambda i: (i, 0)) for a in a_list]
    row0 = 0
    for a in a_list:
        kk = a.shape[1]
        assert row0 % kk == 0
        in_specs.append(pl.BlockSpec((1, kk, n), lambda i, blk=row0 // kk: (layer, blk, 0)))
        row0 += kk
    in_specs.append(pl.BlockSpec((tm, n), lambda i: (i, 0)))
    return pl.pallas_call(
        functools.partial(_proj_res_kernel, n_in=n_in),
        out_shape=jax.ShapeDtypeStruct((m, n), F32),
        grid=(m // tm,),
        in_specs=in_specs,
        out_specs=pl.BlockSpec((tm, n), lambda i: (i, 0)),
        compiler_params=_cp("parallel"),
        name=name,
    )(*a_list, *([w] * n_in), res)


def _out_q_kernel(ya_ref, yb_ref, yc_ref, w_ref, res_ref, nw_ref, wq_ref, x_ref, q_ref, *, q_by_seq):
    x = res_ref[...]
    row0 = 0
    for y_ref in (ya_ref, yb_ref, yc_ref):
        kk = y_ref.shape[1]
        x = x + _dot(y_ref[...], w_ref[0, row0:row0 + kk, :])
        row0 += kk
    x_ref[...] = x
    q = _dot(_rmsnorm_rows(x, nw_ref[0]).astype(BF16), wq_ref[0])
    if q_by_seq is None:
        q_ref[...] = q
    else:
        nb, lt = q_by_seq
        for t in range(lt):
            for h in range(XA_HEADS):
                q_ref[:, h * lt + t, :] = q[t * nb:(t + 1) * nb, h * XA_HEAD_DIM:(h + 1) * XA_HEAD_DIM]


def out_q_proj(ya, yb, yc, w_out, res, nw, wq, layer, *, tm, q_by_seq=None):
    m, n = res.shape
    resident = lambda a: pl.BlockSpec((1,) + a.shape[1:], lambda i: (layer, 0, 0), pipeline_mode=pl.Buffered(1))
    if q_by_seq is None:
        q_shape = (m, wq.shape[2])
        q_spec = pl.BlockSpec((tm, wq.shape[2]), lambda i: (i, 0))
    else:
        assert tm == m == q_by_seq[0] * q_by_seq[1]
        q_shape = (q_by_seq[0], XA_HEADS * q_by_seq[1], XA_HEAD_DIM)
        q_spec = pl.BlockSpec(q_shape, lambda i: (0, 0, 0))
    return pl.pallas_call(
        functools.partial(_out_q_kernel, q_by_seq=q_by_seq),
        out_shape=(jax.ShapeDtypeStruct((m, n), F32), jax.ShapeDtypeStruct(q_shape, F32)),
        grid=(m // tm,),
        in_specs=[pl.BlockSpec((tm, ya.shape[1]), lambda i: (i, 0)),
                  pl.BlockSpec((tm, yb.shape[1]), lambda i: (i, 0)),
                  pl.BlockSpec((tm, yc.shape[1]), lambda i: (i, 0)),
                  resident(w_out),
                  pl.BlockSpec((tm, n), lambda i: (i, 0)),
                  _layer_spec((1, n), layer),
                  resident(wq)],
        out_specs=(pl.BlockSpec((tm, n), lambda i: (i, 0)), q_spec),
        compiler_params=_cp("parallel"),
        name="out_q_proj",
    )(ya, yb, yc, w_out, res, nw, wq)


CONVA_ROW_CHUNK = 32


def _conva_kernel(*refs, nb, lt, n_tiles, has_state):
    if has_state:
        h_ref, w_ref, b_ref, lnw_ref, lnb_ref, st_ref, y_ref, nst_ref, ext_ref = refs
    else:
        h_ref, w_ref, b_ref, lnw_ref, lnb_ref, y_ref, nst_ref, ext_ref = refs
    hist = (CONV_A_WIDTH - 1) * nb
    pad = _round_up(hist, SUBLANE)
    rows = lt * nb
    j = pl.program_id(1)

    @pl.when(j == 0)
    def _():
        if has_state:
            ext_ref[pad - hist:pad, :] = st_ref[0]
        else:
            ext_ref[0:pad, :] = jnp.zeros((pad, D_A), F32)

    ext_ref[pad:pad + rows, :] = h_ref[:, 0:D_A] * _sigmoid(h_ref[:, D_A:2 * D_A])

    bias = b_ref[0]
    lnw = lnw_ref[0]
    lnb = lnb_ref[0]
    rc = CONVA_ROW_CHUNK
    for r0 in range(0, rows, rc):
        acc = jnp.zeros((rc, D_A), F32) + bias
        for k in range(CONV_A_WIDTH):
            acc = acc + w_ref[0, k:k + 1, :] * ext_ref[pl.ds(pad - hist + k * nb + r0, rc), :]
        mu = jnp.mean(acc, axis=-1, keepdims=True)
        xc = acc - mu
        var = jnp.mean(xc * xc, axis=-1, keepdims=True)
        c = xc * lax.rsqrt(var + EPS) * lnw + lnb
        y_ref[r0:r0 + rc, :] = _silu(c).astype(y_ref.dtype)

    new_hist = ext_ref[pl.ds(pad + rows - hist, hist), :]
    nst_ref[0] = new_hist
    if n_tiles > 1:
        ext_ref[pad - hist:pad, :] = new_hist


def conva_mixer(h_a, p, layer, state, *, n_seq, nb, lt, n_tiles):
    rows = lt * nb
    hist = (CONV_A_WIDTH - 1) * nb
    pad = _round_up(hist, SUBLANE)
    has_state = state is not None
    in_specs = [pl.BlockSpec((rows, 2 * D_A), lambda s, j: (s * n_tiles + j, 0)),
                _layer_spec((CONV_A_WIDTH, D_A), layer),
                _layer_spec((1, D_A), layer), _layer_spec((1, D_A), layer), _layer_spec((1, D_A), layer)]
    args = [h_a, p['conv_a_w'], p['conv_a_b'], p['ln_a_w'], p['ln_a_b']]
    if has_state:
        assert n_seq == 1
        in_specs.append(_layer_spec((hist, D_A), layer))
        args.append(state)
    return pl.pallas_call(
        functools.partial(_conva_kernel, nb=nb, lt=lt, n_tiles=n_tiles, has_state=has_state),
        out_shape=(jax.ShapeDtypeStruct((h_a.shape[0], D_A), BF16),
                   jax.ShapeDtypeStruct((n_seq, hist, D_A), F32)),
        grid=(n_seq, n_tiles),
        in_specs=in_specs,
        out_specs=(pl.BlockSpec((rows, D_A), lambda s, j: (s * n_tiles + j, 0)),
                   pl.BlockSpec((1, hist, D_A), lambda s, j: (s, 0, 0))),
        scratch_shapes=[pltpu.VMEM((pad + rows, D_A), F32)],
        compiler_params=_cp("parallel", "arbitrary"),
        name="conva_mixer",
    )(*args)


def _gelu_tanh(x):
    return x * (0.5 * (1.0 + jnp.tanh(math.sqrt(2.0 / math.pi) * (x + 0.044715 * (x * x * x)))))


S5_BLOCKS = S5_LANES // LANE


def _s5_glu_out(hs16, u, cc_ref, d_ref, gw_ref, gb_ref, y_ref):
    y = _dot(hs16, cc_ref[0]) + d_ref[0] * u
    y = _gelu_tanh(y)
    gate = _dot(y.astype(BF16), gw_ref[0]) + gb_ref[0]
    y_ref[...] = (y * _sigmoid(gate)).astype(y_ref.dtype)


def _s5_seq_kernel(u_ref, bb_ref, tab_ref, cc_ref, d_ref, gw_ref, gb_ref,
                   y_ref, nre_ref, nim_ref, hs_ref, cre_ref, cim_ref, *, lt):
    n = S5_LANES
    j = pl.program_id(1)

    @pl.when(j == 0)
    def _():
        cre_ref[...] = jnp.zeros(cre_ref.shape, F32)
        cim_ref[...] = jnp.zeros(cim_ref.shape, F32)

    u = u_ref[...]
    hs_ref[...] = _dot(u.astype(BF16), bb_ref[0])

    def group(i, carry):
        r0 = pl.multiple_of(i * SUBLANE, SUBLANE)
        new = []
        for c in range(S5_BLOCKS):
            lr = slice(c * LANE, (c + 1) * LANE)
            li = slice(n + c * LANE, n + (c + 1) * LANE)
            xr = hs_ref[pl.ds(r0, SUBLANE), lr]
            xi = hs_ref[pl.ds(r0, SUBLANE), li]
            for lev in range(3):
                ar = tab_ref[0, lev, :, lr]
                ai = tab_ref[0, lev, :, li]
                sr = pltpu.roll(xr, 1 << lev, 0)
                si = pltpu.roll(xi, 1 << lev, 0)
                xr, xi = xr + ar * sr - ai * si, xi + ar * si + ai * sr
            pr = tab_ref[0, 3, :, lr]
            pi = tab_ref[0, 3, :, li]
            er, ei = carry[2 * c], carry[2 * c + 1]
            hr = xr + pr * er - pi * ei
            hi = xi + pr * ei + pi * er
            hs_ref[pl.ds(r0, SUBLANE), lr] = hr
            hs_ref[pl.ds(r0, SUBLANE), li] = hi
            new += [jnp.broadcast_to(hr[SUBLANE - 1:SUBLANE, :], (SUBLANE, LANE)),
                    jnp.broadcast_to(hi[SUBLANE - 1:SUBLANE, :], (SUBLANE, LANE))]
        return tuple(new)

    init = []
    for c in range(S5_BLOCKS):
        init += [cre_ref[:, c * LANE:(c + 1) * LANE], cim_ref[:, c * LANE:(c + 1) * LANE]]
    last = lax.fori_loop(0, lt // SUBLANE, group, tuple(init))
    for c in range(S5_BLOCKS):
        cre_ref[:, c * LANE:(c + 1) * LANE] = last[2 * c]
        cim_ref[:, c * LANE:(c + 1) * LANE] = last[2 * c + 1]
    nre_ref[0] = cre_ref[0:1, :]
    nim_ref[0] = cim_ref[0:1, :]
    _s5_glu_out(hs_ref[...].astype(BF16), u, cc_ref, d_ref, gw_ref, gb_ref, y_ref)


def _s5_step_kernel(u_ref, bb_ref, ab_ref, cc_ref, d_ref, gw_ref, gb_ref, sre_ref, sim_ref,
                    y_ref, nre_ref, nim_ref, hs_ref, *, nb, lt):
    n = S5_LANES
    nblk = S5_BLOCKS
    u = u_ref[...]
    hs_ref[...] = _dot(u.astype(BF16), bb_ref[0])
    ab_re = jnp.concatenate([ab_ref[0, c:c + 1, :] for c in range(nblk)], axis=1)
    ab_im = jnp.concatenate([ab_ref[0, nblk + c:nblk + c + 1, :] for c in range(nblk)], axis=1)
    hr = sre_ref[0]
    hi = sim_ref[0]
    for t in range(lt):
        rs = slice(t * nb, (t + 1) * nb)
        nr = ab_re * hr - ab_im * hi + hs_ref[rs, 0:n]
        ni = ab_re * hi + ab_im * hr + hs_ref[rs, n:2 * n]
        hr, hi = nr, ni
        hs_ref[rs, 0:n] = hr
        hs_ref[rs, n:2 * n] = hi
    nre_ref[0] = hr
    nim_ref[0] = hi
    _s5_glu_out(hs_ref[...].astype(BF16), u, cc_ref, d_ref, gw_ref, gb_ref, y_ref)


def s5_mixer(h_b, p, layer, s_re, s_im, *, n_seq, nb, lt, n_tiles):
    rows = lt * nb
    n = S5_LANES
    has_state = s_re is not None
    in_specs = [pl.BlockSpec((rows, D_B), lambda s, j: (s * n_tiles + j, 0)),
                _layer_spec((D_B, 2 * n), layer),
                _layer_spec((2 * S5_BLOCKS, LANE), layer)]
    args = [h_b, p['s5_bb'], p['s5_ab']]
    if not has_state:
        in_specs[2] = _layer_spec((4, SUBLANE, 2 * n), layer)
        args[2] = p['s5_tab']
    in_specs += [_layer_spec((2 * n, D_B), layer),
                 _layer_spec((1, D_B), layer),
                 _layer_spec((D_B, D_B), layer),
                 _layer_spec((1, D_B), layer)]
    args += [p['s5_cc'], p['s5_d'], p['s5_glu_w'], p['s5_glu_b']]
    if has_state:
        assert n_seq == 1 and n_tiles == 1
        in_specs += [_layer_spec((nb, n), layer)] * 2
        args += [s_re, s_im]
        body = functools.partial(_s5_step_kernel, nb=nb, lt=lt)
        scratch = [pltpu.VMEM((rows, 2 * n), F32)]
    else:
        assert nb == 1 and lt % SUBLANE == 0
        body = functools.partial(_s5_seq_kernel, lt=lt)
        scratch = [pltpu.VMEM((rows, 2 * n), F32),
                   pltpu.VMEM((SUBLANE, n), F32),
                   pltpu.VMEM((SUBLANE, n), F32)]
    st_spec = pl.BlockSpec((1, nb, n), lambda s, j: (s, 0, 0))
    return pl.pallas_call(
        body,
        out_shape=(jax.ShapeDtypeStruct((h_b.shape[0], D_B), BF16),
                   jax.ShapeDtypeStruct((n_seq, nb, n), F32),
                   jax.ShapeDtypeStruct((n_seq, nb, n), F32)),
        grid=(n_seq, n_tiles),
        in_specs=in_specs,
        out_specs=(pl.BlockSpec((rows, D_B), lambda s, j: (s * n_tiles + j, 0)), st_spec, st_spec),
        scratch_shapes=scratch,
        compiler_params=_cp("parallel", "arbitrary"),
        name="s5_mixer",
    )(*args)


def _group_rmsnorm(y, nw):
    half = D_C // SSD_GROUPS
    outs = []
    for g in range(SSD_GROUPS):
        yg = y[:, g * half:(g + 1) * half]
        outs.append(yg * lax.rsqrt(jnp.mean(yg * yg, axis=-1, keepdims=True) + EPS))
    return jnp.concatenate(outs, axis=1) * nw


def _mamba_p_kernel(h_ref, cw_ref, cb_ref, dtb_ref, alog_ref, dexp_ref, nw_ref, e_ref, tril_ref,
                    y_ref, ncst_ref, nsst_ref, ext_ref, st_ref, *, lt, n_tiles):
    q = SSD_CHUNK
    hist = SSD_CONV_WIDTH - 1
    pad = SUBLANE
    half = D_C // SSD_GROUPS
    hpg = SSD_HEADS // SSD_GROUPS
    j = pl.program_id(1)

    @pl.when(j == 0)
    def _():
        ext_ref[0:pad, :] = jnp.zeros((pad, D_XBC), F32)
        st_ref[...] = jnp.zeros(st_ref.shape, F32)

    ext_ref[pad:pad + lt, :] = h_ref[:, D_C:D_C + D_XBC]

    e = e_ref[...]
    tril = tril_ref[...]
    a_neg = -jnp.exp(alog_ref[0])
    li = lax.broadcasted_iota(jnp.int32, (q, q), 0)
    si = lax.broadcasted_iota(jnp.int32, (q, q), 1)
    causal = li >= si
    lane = lax.broadcasted_iota(jnp.int32, (q, LANE), 1)

    for c in range(lt // q):
        r0 = c * q
        acc = jnp.zeros((q, D_XBC), F32) + cb_ref[0]
        for k in range(SSD_CONV_WIDTH):
            acc = acc + cw_ref[0, k:k + 1, :] * ext_ref[pl.ds(pad - hist + k + r0, q), :]
        xc = _silu(acc)
        xs = xc[:, 0:D_C]
        z = h_ref[r0:r0 + q, 0:D_C]
        dt = _softplus(h_ref[r0:r0 + q, D_C + D_XBC:D_C + D_XBC + LANE] + dtb_ref[0])
        a = dt * a_neg
        hi_, mid_, lo_ = _split3(a)
        cs = _dot(tril, hi_) + _dot(tril, mid_) + _dot(tril, lo_)
        cs_last = cs[q - 1:q, :]
        dt_x = _expand(dt, e)
        ecs_x = _expand(jnp.exp(cs), e)
        edl_x = _expand(jnp.exp(cs_last - cs), e)
        xdt = xs * dt_x
        cs_t = cs.T

        y_parts = []
        for g in range(SSD_GROUPS):
            bm = xc[:, D_C + g * SSD_STATE:D_C + (g + 1) * SSD_STATE]
            cm = xc[:, D_C + SSD_GROUPS * SSD_STATE + g * SSD_STATE:
                    D_C + SSD_GROUPS * SSD_STATE + (g + 1) * SSD_STATE]
            bm16 = bm.astype(BF16)
            cm16 = cm.astype(BF16)
            cb = _dot_nt(cm16, bm16)
            for pr in range(hpg // 2):
                r_even = g * hpg + 2 * pr
                xpair = xdt[:, r_even * SSD_HEAD_DIM:(r_even + 2) * SSD_HEAD_DIM].astype(BF16)
                ys = []
                for r in (r_even, r_even + 1):
                    seg = cs[:, r:r + 1] - cs_t[r:r + 1, :]
                    dec = jnp.exp(jnp.where(causal, seg, -jnp.inf))
                    ys.append(_dot((cb * dec).astype(BF16), xpair))
                y_parts.append(jnp.where(lane < SSD_HEAD_DIM, ys[0], ys[1]))
        y_diag = jnp.concatenate(y_parts, axis=1)
        y_off = jnp.concatenate(
            [_dot(xc[:, D_C + SSD_GROUPS * SSD_STATE + g * SSD_STATE:
                      D_C + SSD_GROUPS * SSD_STATE + (g + 1) * SSD_STATE].astype(BF16),
                  st_ref[:, g * half:(g + 1) * half].astype(BF16)) for g in range(SSD_GROUPS)],
            axis=1) * ecs_x
        y = y_diag + y_off + dexp_ref[0] * xs
        y = y * _silu(z)
        y_ref[r0:r0 + q, :] = _group_rmsnorm(y, nw_ref[0]).astype(y_ref.dtype)

        xw = (xdt * edl_x).astype(BF16)
        dec_row = ecs_x[q - 1:q, :]
        for g in range(SSD_GROUPS):
            bm_t = xc[:, D_C + g * SSD_STATE:D_C + (g + 1) * SSD_STATE].T.astype(BF16)
            upd = _dot(bm_t, xw[:, g * half:(g + 1) * half])
            st_ref[:, g * half:(g + 1) * half] = (
                st_ref[:, g * half:(g + 1) * half] * dec_row[:, g * half:(g + 1) * half] + upd)

    new_hist = ext_ref[pl.ds(pad + lt - hist, hist), :]
    ncst_ref[0] = new_hist
    if n_tiles > 1:
        ext_ref[pad - hist:pad, :] = new_hist

    @pl.when(j == n_tiles - 1)
    def _():
        for blk in range(D_C // LANE):
            nsst_ref[0, blk * LANE:(blk + 1) * LANE, :] = st_ref[:, blk * LANE:(blk + 1) * LANE].T


def _ssd_consts():
    head_of_lane = jnp.arange(D_C) // SSD_HEAD_DIM
    e = (jnp.arange(LANE)[:, None] == head_of_lane[None, :]).astype(BF16)
    tril = (jnp.arange(SSD_CHUNK)[:, None] >= jnp.arange(SSD_CHUNK)[None, :]).astype(BF16)
    return e, tril


def _ssd_param_specs(layer):
    return [_layer_spec((SSD_CONV_WIDTH, D_XBC), layer),
            _layer_spec((1, D_XBC), layer),
            _layer_spec((1, LANE), layer),
            _layer_spec((1, LANE), layer),
            _layer_spec((1, D_C), layer),
            _layer_spec((1, D_C), layer)]


def _ssd_param_args(p):
    return [p['conv_c_w'], p['conv_c_b'], p['ssd_dt_bias'], p['ssd_a_log'], p['ssd_d'], p['ssd_norm_w']]


def mamba_prompt(h_c, p, layer, *, n_seq, lt, n_tiles):
    hist = SSD_CONV_WIDTH - 1
    const = lambda s, j: (0, 0)
    return pl.pallas_call(
        functools.partial(_mamba_p_kernel, lt=lt, n_tiles=n_tiles),
        out_shape=(jax.ShapeDtypeStruct((h_c.shape[0], D_C), BF16),
                   jax.ShapeDtypeStruct((n_seq, hist, D_XBC), F32),
                   jax.ShapeDtypeStruct((n_seq, D_C, SSD_STATE), F32)),
        grid=(n_seq, n_tiles),
        in_specs=[pl.BlockSpec((lt, D_HC), lambda s, j: (s * n_tiles + j, 0))]
        + _ssd_param_specs(layer)
        + [pl.BlockSpec((LANE, D_C), const), pl.BlockSpec((SSD_CHUNK, SSD_CHUNK), const)],
        out_specs=(pl.BlockSpec((lt, D_C), lambda s, j: (s * n_tiles + j, 0)),
                   pl.BlockSpec((1, hist, D_XBC), lambda s, j: (s, 0, 0)),
                   pl.BlockSpec((1, D_C, SSD_STATE), lambda s, j: (s, 0, 0))),
        scratch_shapes=[pltpu.VMEM((SUBLANE + lt, D_XBC), F32),
                        pltpu.VMEM((SSD_STATE, D_C), F32)],
        compiler_params=_cp("parallel", "arbitrary"),
        name="mamba_prompt",
    )(h_c, *_ssd_param_args(p), p['ssd_e'], p['ssd_tril'])


def _ks(c, k, nb):
    return slice((c * SUBLANE + k) * nb, (c * SUBLANE + k + 1) * nb)


def _slab_put(ref, k, slab, nb):
    for c in range(slab.shape[1] // LANE):
        ref[_ks(c, k, nb), :] = slab[:, c * LANE:(c + 1) * LANE]


def _slab_get(ref, k, n_blocks, nb):
    return jnp.concatenate([ref[_ks(c, k, nb), :] for c in range(n_blocks)], axis=1)


def _seq_get(ref, b, n_blocks, nb):
    return jnp.concatenate(
        [ref[pl.ds(c * SUBLANE * nb + b, SUBLANE, stride=nb), :] for c in range(n_blocks)], axis=1)


def _seq_put(ref, b, val, nb, c0=0):
    for c in range(val.shape[1] // LANE):
        ref[pl.ds((c0 + c) * SUBLANE * nb + b, SUBLANE, stride=nb), :] = val[:, c * LANE:(c + 1) * LANE]


def _mamba_s_kernel(h_ref, cw_ref, cb_ref, dtb_ref, alog_ref, dexp_ref, nw_ref, e_ref, cst_ref, sst_ref,
                    y_ref, ncst_ref, nsst_ref,
                    ext_ref, xs_ref, dt_ref, cs_ref, lhs_ref, rhs_ref, c8_ref, yoff_ref, *, nb, lt, bb):
    hist = (SSD_CONV_WIDTH - 1) * nb
    rows = lt * nb
    half = D_C // SSD_GROUPS
    hpg = SSD_HEADS // SSD_GROUPS
    xblk = D_C // LANE
    hblk = half // LANE
    i = pl.program_id(0)
    n_steps = pl.num_programs(0)
    bc_off = D_C
    cc_off = D_C + SSD_GROUPS * SSD_STATE

    @pl.when(i == 0)
    def _phase1():
        e = e_ref[...]
        ext_ref[0:hist, :] = cst_ref[0]
        ext_ref[hist:hist + rows, :] = h_ref[:, D_C:D_C + D_XBC]
        ncst_ref[...] = ext_ref[rows:rows + hist, :]
        a_neg = -jnp.exp(alog_ref[0])
        lhs_ref[...] = jnp.zeros(lhs_ref.shape, F32)
        rhs_ref[...] = jnp.zeros(rhs_ref.shape, F32)
        c8_ref[...] = jnp.zeros(c8_ref.shape, F32)
        cs = jnp.zeros((nb, LANE), F32)
        for t in range(lt):
            rs = slice(t * nb, (t + 1) * nb)
            acc = jnp.zeros((nb, D_XBC), F32) + cb_ref[0]
            for k in range(SSD_CONV_WIDTH):
                acc = acc + cw_ref[0, k:k + 1, :] * ext_ref[(t + k) * nb:(t + k + 1) * nb, :]
            xc = _silu(acc)
            xs_ref[rs, :] = xc[:, 0:D_C]
            for g in range(SSD_GROUPS):
                rhs_ref[_ks(2 * g, t, nb), :] = xc[:, bc_off + g * SSD_STATE:bc_off + (g + 1) * SSD_STATE]
            _slab_put(c8_ref, t, xc[:, cc_off:cc_off + SSD_GROUPS * SSD_STATE], nb)
            dt = _softplus(h_ref[rs, D_C + D_XBC:D_C + D_XBC + LANE] + dtb_ref[0])
            dt_ref[rs, :] = dt
            cs = cs + dt * a_neg
            cs_ref[rs, :] = cs
        cs_last = cs
        for t in range(lt):
            rs = slice(t * nb, (t + 1) * nb)
            wt = jnp.exp(cs_last - cs_ref[rs, :]) * dt_ref[rs, :]
            _slab_put(lhs_ref, t, xs_ref[rs, :] * _expand(wt, e), nb)
        dec = _expand(jnp.exp(cs_last), e)
        d_hi = dec.astype(BF16).astype(F32)
        d_r = dec - d_hi
        d_mid = d_r.astype(BF16).astype(F32)
        d_lo = d_r - d_mid
        ones = jnp.ones((nb, SSD_STATE), F32)
        for k, piece in enumerate((d_hi, d_mid, d_lo)):
            _slab_put(lhs_ref, lt + k, piece, nb)
            for g in range(SSD_GROUPS):
                rhs_ref[_ks(2 * g + 1, lt + k, nb), :] = ones

    for jb in range(bb):
        b = i * bb + jb
        l8 = _seq_get(lhs_ref, b, xblk, nb).astype(BF16)
        r8 = _seq_get(rhs_ref, b, 2 * SSD_GROUPS, nb).astype(BF16)
        c8 = _seq_get(c8_ref, b, SSD_GROUPS, nb).astype(BF16)
        for g in range(SSD_GROUPS):
            s = sst_ref[0, jb, g * half:(g + 1) * half, :]
            yo = _dot_nt(c8[:, g * SSD_STATE:(g + 1) * SSD_STATE], s.astype(BF16))
            _seq_put(yoff_ref, b, yo, nb, c0=g * hblk)
            upd = _dot_tn(l8[:, g * half:(g + 1) * half],
                          r8[:, g * 2 * SSD_STATE:(g + 1) * 2 * SSD_STATE])
            nsst_ref[0, jb, g * half:(g + 1) * half, :] = upd[:, SSD_STATE:] * s + upd[:, :SSD_STATE]

    @pl.when(i == n_steps - 1)
    def _phase3():
        e = e_ref[...]
        lane = lax.broadcasted_iota(jnp.int32, (nb, LANE), 1)
        for t in range(lt):
            rt = slice(t * nb, (t + 1) * nb)
            cs_t = cs_ref[rt, :]
            y = (_slab_get(yoff_ref, t, xblk, nb) * _expand(jnp.exp(cs_t), e)
                 + dexp_ref[0] * xs_ref[rt, :])
            for s_ in range(t + 1):
                rsl = slice(s_ * nb, (s_ + 1) * nb)
                cbs = []
                for g in range(SSD_GROUPS):
                    cm = c8_ref[_ks(g, t, nb), :]
                    bm = rhs_ref[_ks(2 * g, s_, nb), :]
                    cbs.append(jnp.sum(cm * bm, axis=-1, keepdims=True))
                cb = jnp.where(lane < hpg, cbs[0], cbs[1])
                m = jnp.exp(cs_t - cs_ref[rsl, :]) * dt_ref[rsl, :] * cb
                y = y + _expand(m, e) * xs_ref[rsl, :]
            y = y * _silu(h_ref[rt, 0:D_C])
            y_ref[rt, :] = _group_rmsnorm(y, nw_ref[0]).astype(y_ref.dtype)


def mamba_sample(h_c, p, layer, cst, sst, *, nb, lt, bb):
    rows = lt * nb
    hist = (SSD_CONV_WIDTH - 1) * nb
    const = lambda i: (0, 0)
    sst_spec = pl.BlockSpec((1, bb, D_C, SSD_STATE), lambda i: (layer, i, 0, 0))
    in_specs = ([pl.BlockSpec((rows, D_HC), const)] + _ssd_param_specs(layer)
                + [pl.BlockSpec((LANE, D_C), const), _layer_spec((hist, D_XBC), layer), sst_spec])
    return pl.pallas_call(
        functools.partial(_mamba_s_kernel, nb=nb, lt=lt, bb=bb),
        out_shape=(jax.ShapeDtypeStruct((rows, D_C), BF16),
                   jax.ShapeDtypeStruct((hist, D_XBC), F32),
                   jax.ShapeDtypeStruct(sst.shape, F32)),
        grid=(nb // bb,),
        in_specs=in_specs,
        out_specs=(pl.BlockSpec((rows, D_C), const),
                   pl.BlockSpec((hist, D_XBC), const),
                   sst_spec),
        input_output_aliases={len(in_specs) - 1: 2},
        scratch_shapes=[pltpu.VMEM((hist + rows, D_XBC), F32),
                        pltpu.VMEM((rows, D_C), F32),
                        pltpu.VMEM((rows, LANE), F32),
                        pltpu.VMEM((rows, LANE), F32),
                        pltpu.VMEM((D_C // LANE * SUBLANE * nb, LANE), F32),
                        pltpu.VMEM((2 * SSD_GROUPS * SUBLANE * nb, LANE), F32),
                        pltpu.VMEM((SSD_GROUPS * SUBLANE * nb, LANE), F32),
                        pltpu.VMEM((D_C // LANE * SUBLANE * nb, LANE), F32)],
        compiler_params=_cp("arbitrary"),
        name="mamba_sample",
    )(h_c, *_ssd_param_args(p), p['ssd_e'], cst, sst)


def _softmax_rows(s):
    m = jnp.max(s, axis=-1, keepdims=True)
    ex = jnp.exp(s - m)
    return ex / jnp.sum(ex, axis=-1, keepdims=True)


def _attn_p_kernel(q_ref, k_ref, v_ref, o_ref):
    for h in range(XA_HEADS):
        hs = slice(h * XA_HEAD_DIM, (h + 1) * XA_HEAD_DIM)
        s = _dot_nt(q_ref[:, hs].astype(BF16), k_ref[0, :, hs].astype(BF16)) / math.sqrt(XA_HEAD_DIM)
        p = _softmax_rows(s)
        o_ref[:, hs] = _dot(p.astype(BF16), v_ref[0, :, hs].astype(BF16)).astype(o_ref.dtype)


def attn_prompt(q, k, v, *, n_seq, seq, tq):
    n_tiles = seq // tq
    return pl.pallas_call(
        _attn_p_kernel,
        out_shape=jax.ShapeDtypeStruct(q.shape, BF16),
        grid=(n_seq, n_tiles),
        in_specs=[pl.BlockSpec((tq, D_MODEL), lambda s, j: (s * n_tiles + j, 0)),
                  pl.BlockSpec((1, N_MEM, D_MODEL), lambda s, j: (s, 0, 0)),
                  pl.BlockSpec((1, N_MEM, D_MODEL), lambda s, j: (s, 0, 0))],
        out_specs=pl.BlockSpec((tq, D_MODEL), lambda s, j: (s * n_tiles + j, 0)),
        compiler_params=_cp("parallel", "arbitrary"),
        name="attn_prompt",
    )(q, k, v)


def _attn_s_kernel(q_ref, k_ref, v_ref, o_ref, *, bb, lt):
    rows = XA_HEADS * lt
    n = N_MEM * XA_HEADS
    col_head = lax.broadcasted_iota(jnp.int32, (rows, n), 1) % XA_HEADS
    row_head = lax.broadcasted_iota(jnp.int32, (rows, n), 0) // lt
    same_head = col_head == row_head
    for jb in range(bb):
        k = k_ref[0, jb].reshape(n, XA_HEAD_DIM).astype(BF16)
        v = v_ref[0, jb].reshape(n, XA_HEAD_DIM).astype(BF16)
        s = _dot_nt(q_ref[jb].astype(BF16), k) / math.sqrt(XA_HEAD_DIM)
        p = _softmax_rows(jnp.where(same_head, s, -jnp.inf))
        o_ref[jb] = _dot(p.astype(BF16), v)


def attn_sample(q, k, v, layer, *, bb):
    nb, rows, _ = q.shape
    kv_spec = pl.BlockSpec((1, bb, N_MEM, XA_HEADS, XA_HEAD_DIM), lambda i: (layer, i, 0, 0, 0))
    return pl.pallas_call(
        functools.partial(_attn_s_kernel, bb=bb, lt=rows // XA_HEADS),
        out_shape=jax.ShapeDtypeStruct((nb, rows, XA_HEAD_DIM), F32),
        grid=(nb // bb,),
        in_specs=[pl.BlockSpec((bb, rows, XA_HEAD_DIM), lambda i: (i, 0, 0)), kv_spec, kv_spec],
        out_specs=pl.BlockSpec((bb, rows, XA_HEAD_DIM), lambda i: (i, 0, 0)),
        compiler_params=_cp("parallel"),
        name="attn_sample",
    )(q, k, v)


def _ffn_kernel(*refs, nb, tiles_per_seq, has_state, final_norm):
    refs = list(refs)
    x_ref, nw_ref, wg_ref, wu_ref, cw_ref, cb_ref, wd_ref = refs[:7]
    pos = 7
    st_ref = None
    if has_state:
        st_ref = refs[pos]
        pos += 1
    fw_ref = None
    if final_norm:
        fw_ref = refs[pos]
        pos += 1
    o_ref, nst_ref, xn_ref, gext_ref, carry_ref = refs[pos:pos + 5]

    hist = (FFN_CONV_WIDTH - 1) * nb
    pad = _round_up(hist, SUBLANE)
    tm = x_ref.shape[0]
    tf = wg_ref.shape[3]
    rsz = min(FFN_ROW_SPLIT, tm)
    i = pl.program_id(0)
    f = pl.program_id(1)
    n_f = pl.num_programs(1)

    @pl.when(f == 0)
    def _():
        x = x_ref[...]
        xn_ref[...] = _rmsnorm_rows(x, nw_ref[0]).astype(BF16)
        o_ref[...] = x

    if tiles_per_seq > 1:
        first = (i % tiles_per_seq) == 0

        @pl.when(first)
        def _():
            if has_state:
                gext_ref[pad - hist:pad, :] = st_ref[0]
            else:
                gext_ref[0:pad, :] = jnp.zeros((pad, tf), F32)

        @pl.when(jnp.logical_not(first))
        def _():
            gext_ref[0:pad, :] = carry_ref[f]
    else:
        if has_state:
            gext_ref[pad - hist:pad, :] = st_ref[0]
        else:
            gext_ref[0:pad, :] = jnp.zeros((pad, tf), F32)

    cw0 = cw_ref[0, 0:1, :]
    cw1 = cw_ref[0, 1:2, :]
    cw2 = cw_ref[0, 2:3, :]
    cb = cb_ref[0]
    for r0 in range(0, tm, rsz):
        xn = xn_ref[r0:r0 + rsz, :]
        g = _dot(xn, wg_ref[0, 0])
        up = _dot(xn, wu_ref[0, 0])
        gext_ref[pad + r0:pad + r0 + rsz, :] = g
        conv = (cw0 * gext_ref[pl.ds(pad - 2 * nb + r0, rsz), :]
                + cw1 * gext_ref[pl.ds(pad - nb + r0, rsz), :]
                + cw2 * g + cb)
        act = _silu(conv) * up
        o_ref[r0:r0 + rsz, :] += _dot(act.astype(BF16), wd_ref[0])

    nst_ref[0] = gext_ref[pl.ds(pad + tm - hist, hist), :]
    if tiles_per_seq > 1:
        carry_ref[f] = gext_ref[pl.ds(tm, pad), :]

    if final_norm:
        @pl.when(f == n_f - 1)
        def _():
            o_ref[...] = _rmsnorm_rows(o_ref[...], fw_ref[...])


def conv_ffn(x, p, layer, state, final_w, *, n_seq, nb, tm, tiles_per_seq):
    m = x.shape[0]
    tf = FF_TILE
    n_f = D_FF_PAD // tf
    hist = (FFN_CONV_WIDTH - 1) * nb
    pad = _round_up(hist, SUBLANE)
    has_state = state is not None
    final_norm = final_w is not None
    in_specs = [pl.BlockSpec((tm, D_MODEL), lambda i, f: (i, 0)),
                _layer_spec((1, D_MODEL), layer),
                pl.BlockSpec((1, 1, D_MODEL, tf), lambda i, f: (layer, f, 0, 0)),
                pl.BlockSpec((1, 1, D_MODEL, tf), lambda i, f: (layer, f, 0, 0)),
                pl.BlockSpec((1, FFN_CONV_WIDTH, tf), lambda i, f: (layer, 0, f)),
                pl.BlockSpec((1, 1, tf), lambda i, f: (layer, 0, f)),
                pl.BlockSpec((1, tf, D_MODEL), lambda i, f: (layer, f, 0))]
    args = [x, p['norm_ffn_w'], p['ffn_wg'], p['ffn_wu'], p['ffn_conv_w'], p['ffn_conv_b'], p['ffn_wd']]
    if has_state:
        assert n_seq == 1 and tiles_per_seq == 1
        in_specs.append(pl.BlockSpec((1, hist, tf), lambda i, f: (layer, 0, f)))
        args.append(state)
    if final_norm:
        in_specs.append(pl.BlockSpec((1, D_MODEL), lambda i, f: (0, 0)))
        args.append(final_w.reshape(1, D_MODEL))
    return pl.pallas_call(
        functools.partial(_ffn_kernel, nb=nb, tiles_per_seq=tiles_per_seq, has_state=has_state,
                          final_norm=final_norm),
        out_shape=(jax.ShapeDtypeStruct((m, D_MODEL), F32),
                   jax.ShapeDtypeStruct((m // tm, hist, D_FF_PAD), F32)),
        grid=(m // tm, n_f),
        in_specs=in_specs,
        out_specs=(pl.BlockSpec((tm, D_MODEL), lambda i, f: (i, 0)),
                   pl.BlockSpec((1, hist, tf), lambda i, f: (i, 0, f))),
        scratch_shapes=[pltpu.VMEM((tm, D_MODEL), BF16),
                        pltpu.VMEM((pad + tm, tf), F32),
                        pltpu.VMEM((n_f, pad, tf), F32)],
        compiler_params=_cp("arbitrary", "arbitrary"),
        name="conv_ffn",
    )(*args)


def _s5_params(lam_re, lam_im, log_dt, b_re, b_im, c_re, c_im):
    depth = lam_re.shape[0]
    dt = jnp.exp(log_dt)[..., None]
    mag = jnp.exp(lam_re * dt)
    ang = lam_im * dt
    ab_re, ab_im = mag * jnp.cos(ang), mag * jnp.sin(ang)
    blocks = lambda re, im: jnp.concatenate(
        [re.reshape(*re.shape[:-2], S5_BLOCKS, LANE), im.reshape(*im.shape[:-2], S5_BLOCKS, LANE)], axis=-2)
    row = jnp.arange(SUBLANE, dtype=F32)
    expo = jnp.stack([jnp.full((SUBLANE,), 1.0), jnp.full((SUBLANE,), 2.0), jnp.full((SUBLANE,), 4.0),
                      row + 1.0])
    keep = jnp.stack([row >= 1, row >= 2, row >= 4, row >= 0]).astype(F32)
    lam_dt = (lam_re * dt).reshape(depth, 1, 1, S5_LANES)
    pang = ang.reshape(depth, 1, 1, S5_LANES) * expo[None, :, :, None]
    pmag = jnp.exp(lam_dt * expo[None, :, :, None]) * keep[None, :, :, None]
    tab = jnp.concatenate([pmag * jnp.cos(pang), pmag * jnp.sin(pang)], axis=-1)
    den = lam_re * lam_re + lam_im * lam_im
    nr, ni = ab_re - 1.0, ab_im
    co_re = (nr * lam_re + ni * lam_im) / den
    co_im = (ni * lam_re - nr * lam_im) / den
    bb_re = co_re[..., None] * b_re - co_im[..., None] * b_im
    bb_im = co_re[..., None] * b_im + co_im[..., None] * b_re
    eye = jnp.eye(S5_GROUPS, dtype=F32)
    dense_b = lambda m: jnp.einsum('lgph,gk->lghkp', m, eye).reshape(depth, D_B, S5_LANES)
    dense_c = lambda m: jnp.einsum('lghp,gk->lkpgh', m, eye).reshape(depth, S5_LANES, D_B)
    bb = jnp.concatenate([dense_b(bb_re), dense_b(bb_im)], axis=2).astype(BF16)
    cc = jnp.concatenate([dense_c(c_re), -dense_c(c_im)], axis=1).astype(BF16)
    return bb, blocks(ab_re, ab_im), tab, cc


def _wprep_kernel(w_ref, o_ref, *, axis, valid_last):
    f = pl.program_id(1)
    last = pl.num_programs(1) - 1
    o = o_ref.at[0, 0] if axis == 1 else o_ref.at[0]

    @pl.when(f < last)
    def _():
        o[...] = w_ref[0].astype(BF16)

    @pl.when(f == last)
    def _():
        if axis == 1:
            o[:, :valid_last] = w_ref[0, :, :valid_last].astype(BF16)
            o[:, valid_last:] = jnp.zeros((o.shape[0], o.shape[1] - valid_last), BF16)
        else:
            o[:valid_last, :] = w_ref[0, :valid_last, :].astype(BF16)
            o[valid_last:, :] = jnp.zeros((o.shape[0] - valid_last, o.shape[1]), BF16)


def ffn_weight_cols(w, tf):
    depth, k, n = w.shape
    n_f = pl.cdiv(n, tf)
    return pl.pallas_call(
        functools.partial(_wprep_kernel, axis=1, valid_last=n - (n_f - 1) * tf),
        out_shape=jax.ShapeDtypeStruct((depth, n_f, k, tf), BF16),
        grid=(depth, n_f),
        in_specs=[pl.BlockSpec((1, k, tf), lambda l, f: (l, 0, f))],
        out_specs=pl.BlockSpec((1, 1, k, tf), lambda l, f: (l, f, 0, 0)),
        compiler_params=_cp("parallel", "parallel"),
        name="ffn_weight_cols",
    )(w)


def ffn_weight_rows(w, tf):
    depth, k, n = w.shape
    n_f = pl.cdiv(k, tf)
    return pl.pallas_call(
        functools.partial(_wprep_kernel, axis=0, valid_last=k - (n_f - 1) * tf),
        out_shape=jax.ShapeDtypeStruct((depth, n_f * tf, n), BF16),
        grid=(depth, n_f),
        in_specs=[pl.BlockSpec((1, tf, n), lambda l, f: (l, f, 0))],
        out_specs=pl.BlockSpec((1, tf, n), lambda l, f: (l, f, 0)),
        compiler_params=_cp("parallel", "parallel"),
        name="ffn_weight_rows",
    )(w)


def kernel(x_prompt, x_sample, mem_prompt, cache_mem_k, cache_mem_v, state_conv_a, state_s5_re, state_s5_im, state_conv_c, state_ssd, state_ffn_conv, norm_mix_w, w_in, conv_a_w, conv_a_b, ln_a_w, ln_a_b, s5_lam_re, s5_lam_im, s5_log_dt, s5_b_re, s5_b_im, s5_c_re, s5_c_im, s5_d, s5_glu_w, s5_glu_b, conv_c_w, conv_c_b, ssd_dt_bias, ssd_a_log, ssd_d, ssd_norm_w, w_out, norm_xa_w, norm_mem_w, xa_wq, xa_wk, xa_wv, xa_wo, norm_ffn_w, ffn_w_gate, ffn_w_up, ffn_conv_w, ffn_conv_b, ffn_w_down, final_norm_w):
    bp, seq, _ = x_prompt.shape
    nbs, lts, _ = x_sample.shape
    depth = w_in.shape[0]
    n_mem = mem_prompt.shape[1]
    lt_p = 512 if seq % 512 == 0 else seq
    n_tiles_p = seq // lt_p
    tm_p = lt_p
    tm_s = lts * nbs
    tm_f = 1024 if seq % 1024 == 0 else tm_p
    tm_m = min(512, bp * n_mem)

    vec = lambda a: a.reshape(depth, 1, a.shape[-1])
    pad_lanes = lambda a: vec(jnp.pad(a, ((0, 0), (0, LANE - a.shape[-1]))))
    ff_pad = D_FF_PAD - D_FF
    s5_bb, s5_ab, s5_tab, s5_cc = _s5_params(s5_lam_re, s5_lam_im, s5_log_dt, s5_b_re, s5_b_im, s5_c_re, s5_c_im)
    ssd_e, ssd_tril = _ssd_consts()
    p = {
        'norm_mix_w': vec(norm_mix_w), 'norm_xa_w': vec(norm_xa_w), 'norm_mem_w': vec(norm_mem_w),
        'norm_ffn_w': vec(norm_ffn_w),
        'w_in': jnp.pad(w_in, ((0, 0), (0, 0), (0, IN_SPLITS[2] - w_in.shape[2]))).astype(BF16),
        'conv_a_w': conv_a_w, 'conv_a_b': vec(conv_a_b), 'ln_a_w': vec(ln_a_w), 'ln_a_b': vec(ln_a_b),
        's5_bb': s5_bb, 's5_ab': s5_ab, 's5_tab': s5_tab, 's5_cc': s5_cc,
        's5_d': vec(s5_d), 's5_glu_w': s5_glu_w.astype(BF16), 's5_glu_b': vec(s5_glu_b),
        'conv_c_w': conv_c_w, 'conv_c_b': vec(conv_c_b),
        'ssd_dt_bias': pad_lanes(ssd_dt_bias), 'ssd_a_log': pad_lanes(ssd_a_log),
        'ssd_d': vec(jnp.repeat(ssd_d, SSD_HEAD_DIM, axis=1)), 'ssd_norm_w': vec(ssd_norm_w),
        'ssd_e': ssd_e, 'ssd_tril': ssd_tril,
        'w_out': w_out.astype(BF16), 'wq': xa_wq.astype(BF16), 'wo': xa_wo.astype(BF16),
        'wkv': jnp.concatenate([xa_wk, xa_wv], axis=2).astype(BF16),
        'ffn_wg': ffn_weight_cols(ffn_w_gate, FF_TILE),
        'ffn_wu': ffn_weight_cols(ffn_w_up, FF_TILE),
        'ffn_wd': ffn_weight_rows(ffn_w_down, FF_TILE),
        'ffn_conv_w': jnp.pad(ffn_conv_w, ((0, 0), (0, 0), (0, ff_pad))),
        'ffn_conv_b': vec(jnp.pad(ffn_conv_b, ((0, 0), (0, ff_pad)))),
    }

    tmaj = lambda a: a.transpose(0, 2, 1, 3).reshape(depth, a.shape[2] * nbs, a.shape[3])
    st_conv_a = tmaj(state_conv_a)
    st_conv_c = tmaj(state_conv_c)
    st_ffn = jnp.pad(tmaj(state_ffn_conv), ((0, 0), (0, 0), (0, ff_pad)))
    st_re = state_s5_re.reshape(depth, nbs, S5_LANES)
    st_im = state_s5_im.reshape(depth, nbs, S5_LANES)
    ssd_all = state_ssd.reshape(depth, nbs, D_C, SSD_STATE)

    xp = x_prompt.reshape(bp * seq, D_MODEL)
    xs = x_sample.transpose(1, 0, 2).reshape(lts * nbs, D_MODEL)
    mem2d = mem_prompt.reshape(bp * n_mem, D_MODEL)

    def mixers(x, l, *, n_seq, nb, lt, n_tiles, tm, sample):
        h_a, h_b, h_c = in_proj(x, p['norm_mix_w'], p['w_in'], l, tm=tm)
        ya, n_conv_a = conva_mixer(h_a, p, l, st_conv_a if sample else None,
                                   n_seq=n_seq, nb=nb, lt=lt, n_tiles=n_tiles)
        yb, n_re, n_im = s5_mixer(h_b, p, l, st_re if sample else None, st_im if sample else None,
                                  n_seq=n_seq, nb=nb, lt=lt, n_tiles=n_tiles)
        return h_c, ya, yb, n_conv_a, n_re, n_im

    outs_p = [[] for _ in range(8)]
    outs_s = [[] for _ in range(5)]
    for l in range(depth):
        last = l == depth - 1
        mk, mv = mem_kv(mem2d, p['norm_mem_w'], p['wkv'], l, tm=tm_m)

        h_c, ya, yb, p_conv_a, p_re, p_im = mixers(xp, l, n_seq=bp, nb=1, lt=lt_p, n_tiles=n_tiles_p,
                                                   tm=tm_p, sample=False)
        yc, p_conv_c, p_ssd = mamba_prompt(h_c, p, l, n_seq=bp, lt=lt_p, n_tiles=n_tiles_p)
        xp, q = out_q_proj(ya, yb, yc, p['w_out'], xp, p['norm_xa_w'], p['wq'], l, tm=tm_p)
        o = attn_prompt(q, mk.reshape(bp, n_mem, D_MODEL), mv.reshape(bp, n_mem, D_MODEL),
                        n_seq=bp, seq=seq, tq=lt_p)
        xp = proj_res([o], p['wo'], l, xp, tm=tm_p, name="attn_out")
        xp, p_ffn = conv_ffn(xp, p, l, None, final_norm_w if last else None, n_seq=bp, nb=1, tm=tm_f,
                             tiles_per_seq=seq // tm_f)
        for lst, v in zip(outs_p, (p_conv_a, p_re, p_im, p_conv_c, p_ssd,
                                   p_ffn[seq // tm_f - 1::seq // tm_f], mk, mv)):
            lst.append(v)

        h_c, ya, yb, s_conv_a, s_re, s_im = mixers(xs, l, n_seq=1, nb=nbs, lt=lts, n_tiles=1, tm=tm_s,
                                                   sample=True)
        yc, s_conv_c, ssd_all = mamba_sample(h_c, p, l, st_conv_c, ssd_all, nb=nbs, lt=lts, bb=8)
        xs, q = out_q_proj(ya, yb, yc, p['w_out'], xs, p['norm_xa_w'], p['wq'], l, tm=tm_s,
                           q_by_seq=(nbs, lts))
        o = attn_sample(q, cache_mem_k, cache_mem_v, l, bb=4)
        xs = attn_out_seq(o, p['wo'], l, xs, nb=nbs, lt=lts)
        xs, s_ffn = conv_ffn(xs, p, l, st_ffn, final_norm_w if last else None, n_seq=1, nb=nbs, tm=tm_s,
                             tiles_per_seq=1)
        for lst, v in zip(outs_s, (s_conv_a[0], s_re[0], s_im[0], s_conv_c, s_ffn[0])):
            lst.append(v)

    p_conv_a, p_re, p_im, p_conv_c, p_ssd, p_ffn, p_mk, p_mv = [jnp.stack(o) for o in outs_p]
    s_conv_a, s_re, s_im, s_conv_c, s_ffn = [jnp.stack(o) for o in outs_s]
    bmaj = lambda a, w: a.reshape(depth, w, nbs, a.shape[-1]).transpose(0, 2, 1, 3)
    y_prompt = xp.reshape(bp, seq, D_MODEL)
    y_sample = xs.reshape(lts, nbs, D_MODEL).transpose(1, 0, 2)
    return (y_prompt, y_sample,
            p_conv_a,
            p_re.reshape(depth, bp, S5_GROUPS, S5_STATE), p_im.reshape(depth, bp, S5_GROUPS, S5_STATE),
            p_conv_c,
            p_ssd.reshape(depth, bp, SSD_HEADS, SSD_HEAD_DIM, SSD_STATE),
            p_ffn[..., :D_FF],
            p_mk.reshape(depth, bp, n_mem, XA_HEADS, XA_HEAD_DIM),
            p_mv.reshape(depth, bp, n_mem, XA_HEADS, XA_HEAD_DIM),
            bmaj(s_conv_a, CONV_A_WIDTH - 1),
            s_re.reshape(depth, nbs, S5_GROUPS, S5_STATE), s_im.reshape(depth, nbs, S5_GROUPS, S5_STATE),
            bmaj(s_conv_c, SSD_CONV_WIDTH - 1),
            ssd_all.reshape(state_ssd.shape),
            bmaj(s_ffn, FFN_CONV_WIDTH - 1)[..., :D_FF])
```

```python
import functools
import math

import jax
import jax.numpy as jnp
from jax import lax
from jax.experimental import pallas as pl
from jax.experimental.pallas import tpu as pltpu

F32 = jnp.float32
BF16 = jnp.bfloat16
EPS = 1e-6

D_MODEL = 2048
D_A = 512
D_B = 512
D_C = 1024
CONV_A_WIDTH = 31
S5_GROUP = 16
S5_GROUPS = 32
S5_STATE = 64
S5_LANES = S5_GROUPS * S5_STATE
SSD_HEAD_DIM = 64
SSD_HEADS = 16
SSD_GROUPS = 2
SSD_STATE = 128
SSD_CONV_WIDTH = 4
SSD_CHUNK = 128
D_XBC = D_C + 2 * SSD_GROUPS * SSD_STATE
D_HC = D_C + D_XBC + 128
XA_HEADS = 4
XA_HEAD_DIM = 512
N_MEM = 256
D_FF = 5504
FFN_CONV_WIDTH = 3

LANE = 128
SUBLANE = 8
VMEM_LIMIT = 56 * 1024 * 1024
FF_TILE = 512
D_FF_PAD = ((D_FF + FF_TILE - 1) // FF_TILE) * FF_TILE
FFN_ROW_SPLIT = 256


def _round_up(x, m):
    return (x + m - 1) // m * m


def _cp(*sem):
    return pltpu.CompilerParams(dimension_semantics=sem, vmem_limit_bytes=VMEM_LIMIT)


def _layer_spec(tail, layer):
    zeros = (0,) * len(tail)
    return pl.BlockSpec((1,) + tuple(tail), lambda *_: (layer,) + zeros)


def _dot(a, b):
    return jnp.dot(a, b, preferred_element_type=F32)


def _dot_nt(a, b):
    return lax.dot_general(a, b, (((1,), (1,)), ((), ())), preferred_element_type=F32)


def _dot_tn(a, b):
    return lax.dot_general(a, b, (((0,), (0,)), ((), ())), preferred_element_type=F32)


def _split3(a):
    hi = a.astype(BF16)
    r = a - hi.astype(F32)
    mid = r.astype(BF16)
    lo = (r - mid.astype(F32)).astype(BF16)
    return hi, mid, lo


def _expand(a, e):
    hi, mid, lo = _split3(a)
    return _dot(hi, e) + _dot(mid, e) + _dot(lo, e)


def _sigmoid(x):
    return jax.nn.sigmoid(x)


def _silu(x):
    return x * jax.nn.sigmoid(x)


def _softplus(x):
    return jnp.maximum(x, 0.0) + jnp.log1p(jnp.exp(-jnp.abs(x)))


def _rmsnorm_rows(x, w):
    ms = jnp.mean(x * x, axis=-1, keepdims=True)
    return x * lax.rsqrt(ms + EPS) * w


def _mem_kv_kernel(x_ref, nw_ref, w_ref, k_ref, v_ref, xn_ref):
    j = pl.program_id(1)

    @pl.when(j == 0)
    def _():
        xn = _rmsnorm_rows(x_ref[...], nw_ref[0]).astype(BF16)
        xn_ref[...] = xn
        k_ref[...] = _dot(xn, w_ref[0])

    @pl.when(j == 1)
    def _():
        v_ref[...] = _dot(xn_ref[...], w_ref[0])


def mem_kv(x, nw, w, layer, *, tm):
    m, k = x.shape
    n = w.shape[2] // 2
    out = jax.ShapeDtypeStruct((m, n), F32)
    return pl.pallas_call(
        _mem_kv_kernel,
        out_shape=(out, out),
        grid=(m // tm, 2),
        in_specs=[pl.BlockSpec((tm, k), lambda i, j: (i, 0)),
                  _layer_spec((1, k), layer),
                  pl.BlockSpec((1, k, n), lambda i, j: (layer, 0, j))],
        out_specs=(pl.BlockSpec((tm, n), lambda i, j: (i, 0)), pl.BlockSpec((tm, n), lambda i, j: (i, 0))),
        scratch_shapes=[pltpu.VMEM((tm, k), BF16)],
        compiler_params=_cp("parallel", "arbitrary"),
        name="mem_kv",
    )(x, nw, w)


def _attn_out_seq_kernel(o_ref, w_ref, res_ref, out_ref, a_ref, *, nb, lt):
    for t in range(lt):
        for h in range(XA_HEADS):
            a_ref[t * nb:(t + 1) * nb, h * XA_HEAD_DIM:(h + 1) * XA_HEAD_DIM] = o_ref[:, h * lt + t, :]
    out_ref[...] = res_ref[...] + _dot(a_ref[...].astype(BF16), w_ref[0])


def attn_out_seq(o, w, layer, res, *, nb, lt):
    m, n = res.shape
    return pl.pallas_call(
        functools.partial(_attn_out_seq_kernel, nb=nb, lt=lt),
        out_shape=jax.ShapeDtypeStruct((m, n), F32),
        grid=(1,),
        in_specs=[pl.BlockSpec(o.shape, lambda i: (0, 0, 0)),
                  _layer_spec(w.shape[1:], layer),
                  pl.BlockSpec((m, n), lambda i: (0, 0))],
        out_specs=pl.BlockSpec((m, n), lambda i: (0, 0)),
        scratch_shapes=[pltpu.VMEM((m, n), F32)],
        compiler_params=_cp("arbitrary"),
        name="attn_out_seq",
    )(o, w, res)


IN_SPLITS = (2 * D_A, 2 * D_A + D_B, 2 * D_A + D_B + D_HC)


def _in_proj_kernel(x_ref, nw_ref, w_ref, ha_ref, hb_ref, hc_ref):
    xn = _rmsnorm_rows(x_ref[...], nw_ref[0]).astype(BF16)
    ha_ref[...] = _dot(xn, w_ref[0, :, 0:IN_SPLITS[0]])
    hb_ref[...] = _dot(xn, w_ref[0, :, IN_SPLITS[0]:IN_SPLITS[1]])
    hc_ref[...] = _dot(xn, w_ref[0, :, IN_SPLITS[1]:IN_SPLITS[2]])


def in_proj(x, nw, w, layer, *, tm):
    m, k = x.shape
    widths = (IN_SPLITS[0], IN_SPLITS[1] - IN_SPLITS[0], IN_SPLITS[2] - IN_SPLITS[1])
    return pl.pallas_call(
        _in_proj_kernel,
        out_shape=tuple(jax.ShapeDtypeStruct((m, wd), F32) for wd in widths),
        grid=(m // tm,),
        in_specs=[pl.BlockSpec((tm, k), lambda i: (i, 0)),
                  _layer_spec((1, k), layer),
                  pl.BlockSpec((1, k, IN_SPLITS[2]), lambda i: (layer, 0, 0), pipeline_mode=pl.Buffered(1))],
        out_specs=tuple(pl.BlockSpec((tm, wd), lambda i: (i, 0)) for wd in widths),
        compiler_params=_cp("parallel"),
        name="in_proj",
    )(x, nw, w)


def _proj_res_kernel(*refs, n_in):
    a_refs = refs[:n_in]
    w_refs = refs[n_in:2 * n_in]
    res_ref, o_ref = refs[2 * n_in], refs[2 * n_in + 1]
    acc = res_ref[...]
    for a_ref, w_ref in zip(a_refs, w_refs):
        acc = acc + _dot(a_ref[...].astype(BF16), w_ref[0])
    o_ref[...] = acc


def proj_res(a_list, w, layer, res, *, tm, name):
    m, n = res.shape
    n_in = len(a_list)
    in_specs = [pl.BlockSpec((tm, a.shape[1]), lambda i: (i, 0)) for a in a_list]
    row0 = 0
    for a in a_list:
        kk = a.shape[1]
        assert row0 % kk == 0
        in_specs.append(pl.BlockSpec((1, kk, n), lambda i, blk=row0 // kk: (layer, blk, 0)))
        row0 += kk
    in_specs.append(pl.BlockSpec((tm, n), lambda i: (i, 0)))
    return pl.pallas_call(
        functools.partial(_proj_res_kernel, n_in=n_in),
        out_shape=jax.ShapeDtypeStruct((m, n), F32),
        grid=(m // tm,),
        in_specs=in_specs,
        out_specs=pl.BlockSpec((tm, n), lambda i: (i, 0)),
        compiler_params=_cp("parallel"),
        name=name,
    )(*a_list, *([w] * n_in), res)


def _out_q_kernel(ya_ref, yb_ref, yc_ref, w_ref, res_ref, nw_ref, wq_ref, x_ref, q_ref, *, q_by_seq):
    x = res_ref[...]
    row0 = 0
    for y_ref in (ya_ref, yb_ref, yc_ref):
        kk = y_ref.shape[1]
        x = x + _dot(y_ref[...], w_ref[0, row0:row0 + kk, :])
        row0 += kk
    x_ref[...] = x
    q = _dot(_rmsnorm_rows(x, nw_ref[0]).astype(BF16), wq_ref[0])
    if q_by_seq is None:
        q_ref[...] = q
    else:
        nb, lt = q_by_seq
        for t in range(lt):
            for h in range(XA_HEADS):
                q_ref[:, h * lt + t, :] = q[t * nb:(t + 1) * nb, h * XA_HEAD_DIM:(h + 1) * XA_HEAD_DIM]


def out_q_proj(ya, yb, yc, w_out, res, nw, wq, layer, *, tm, q_by_seq=None):
    m, n = res.shape
    resident = lambda a: pl.BlockSpec((1,) + a.shape[1:], lambda i: (layer, 0, 0), pipeline_mode=pl.Buffered(1))
    if q_by_seq is None:
        q_shape = (m, wq.shape[2])
        q_spec = pl.BlockSpec((tm, wq.shape[2]), lambda i: (i, 0))
    else:
        assert tm == m == q_by_seq[0] * q_by_seq[1]
        q_shape = (q_by_seq[0], XA_HEADS * q_by_seq[1], XA_HEAD_DIM)
        q_spec = pl.BlockSpec(q_shape, lambda i: (0, 0, 0))
    return pl.pallas_call(
        functools.partial(_out_q_kernel, q_by_seq=q_by_seq),
        out_shape=(jax.ShapeDtypeStruct((m, n), F32), jax.ShapeDtypeStruct(q_shape, F32)),
        grid=(m // tm,),
        in_specs=[pl.BlockSpec((tm, ya.shape[1]), lambda i: (i, 0)),
                  pl.BlockSpec((tm, yb.shape[1]), lambda i: (i, 0)),
                  pl.BlockSpec((tm, yc.shape[1]), lambda i: (i, 0)),
                  resident(w_out),
                  pl.BlockSpec((tm, n), lambda i: (i, 0)),
                  _layer_spec((1, n), layer),
                  resident(wq)],
        out_specs=(pl.BlockSpec((tm, n), lambda i: (i, 0)), q_spec),
        compiler_params=_cp("parallel"),
        name="out_q_proj",
    )(ya, yb, yc, w_out, res, nw, wq)


CONVA_ROW_CHUNK = 32


def _conva_kernel(*refs, nb, lt, n_tiles, has_state):
    if has_state:
        h_ref, w_ref, b_ref, lnw_ref, lnb_ref, st_ref, y_ref, nst_ref, ext_ref = refs
    else:
        h_ref, w_ref, b_ref, lnw_ref, lnb_ref, y_ref, nst_ref, ext_ref = refs
    hist = (CONV_A_WIDTH - 1) * nb
    pad = _round_up(hist, SUBLANE)
    rows = lt * nb
    j = pl.program_id(1)

    @pl.when(j == 0)
    def _():
        if has_state:
            ext_ref[pad - hist:pad, :] = st_ref[0]
        else:
            ext_ref[0:pad, :] = jnp.zeros((pad, D_A), F32)

    ext_ref[pad:pad + rows, :] = h_ref[:, 0:D_A] * _sigmoid(h_ref[:, D_A:2 * D_A])

    bias = b_ref[0]
    lnw = lnw_ref[0]
    lnb = lnb_ref[0]
    rc = CONVA_ROW_CHUNK
    for r0 in range(0, rows, rc):
        acc = jnp.zeros((rc, D_A), F32) + bias
        if nb == 1:
            for s in range(SUBLANE):
                n_taps = (CONV_A_WIDTH - 1 - s) // SUBLANE + 1
                win = ext_ref[pl.ds(pad - hist + s + r0, rc + SUBLANE * (n_taps - 1)), :]
                for jt in range(n_taps):
                    k = s + SUBLANE * jt
                    acc = acc + w_ref[0, k:k + 1, :] * win[SUBLANE * jt:SUBLANE * jt + rc, :]
        else:
            for k in range(CONV_A_WIDTH):
                acc = acc + w_ref[0, k:k + 1, :] * ext_ref[pl.ds(pad - hist + k * nb + r0, rc), :]
        mu = jnp.mean(acc, axis=-1, keepdims=True)
        xc = acc - mu
        var = jnp.mean(xc * xc, axis=-1, keepdims=True)
        c = xc * lax.rsqrt(var + EPS) * lnw + lnb
        y_ref[r0:r0 + rc, :] = _silu(c).astype(y_ref.dtype)

    new_hist = ext_ref[pl.ds(pad + rows - hist, hist), :]
    nst_ref[0] = new_hist
    if n_tiles > 1:
        ext_ref[pad - hist:pad, :] = new_hist


def conva_mixer(h_a, p, layer, state, *, n_seq, nb, lt, n_tiles):
    rows = lt * nb
    hist = (CONV_A_WIDTH - 1) * nb
    pad = _round_up(hist, SUBLANE)
    has_state = state is not None
    in_specs = [pl.BlockSpec((rows, 2 * D_A), lambda s, j: (s * n_tiles + j, 0)),
                _layer_spec((CONV_A_WIDTH, D_A), layer),
                _layer_spec((1, D_A), layer), _layer_spec((1, D_A), layer), _layer_spec((1, D_A), layer)]
    args = [h_a, p['conv_a_w'], p['conv_a_b'], p['ln_a_w'], p['ln_a_b']]
    if has_state:
        assert n_seq == 1
        in_specs.append(_layer_spec((hist, D_A), layer))
        args.append(state)
    return pl.pallas_call(
        functools.partial(_conva_kernel, nb=nb, lt=lt, n_tiles=n_tiles, has_state=has_state),
        out_shape=(jax.ShapeDtypeStruct((h_a.shape[0], D_A), BF16),
                   jax.ShapeDtypeStruct((n_seq, hist, D_A), F32)),
        grid=(n_seq, n_tiles),
        in_specs=in_specs,
        out_specs=(pl.BlockSpec((rows, D_A), lambda s, j: (s * n_tiles + j, 0)),
                   pl.BlockSpec((1, hist, D_A), lambda s, j: (s, 0, 0))),
        scratch_shapes=[pltpu.VMEM((pad + rows, D_A), F32)],
        compiler_params=_cp("parallel", "arbitrary"),
        name="conva_mixer",
    )(*args)


def _gelu_tanh(x):
    return x * (0.5 * (1.0 + jnp.tanh(math.sqrt(2.0 / math.pi) * (x + 0.044715 * (x * x * x)))))


S5_BLOCKS = S5_LANES // LANE
S5_SUPER = 2
S5_SUP_CH = D_B // S5_SUPER
S5_SUP_ST = S5_LANES // S5_SUPER


def _s5_b_proj(u, bb_ref, hs_ref):
    n = S5_LANES
    for sb in range(S5_SUPER):
        bu = _dot(u[:, sb * S5_SUP_CH:(sb + 1) * S5_SUP_CH].astype(BF16), bb_ref[0, sb])
        hs_ref[:, sb * S5_SUP_ST:(sb + 1) * S5_SUP_ST] = bu[:, 0:S5_SUP_ST]
        hs_ref[:, n + sb * S5_SUP_ST:n + (sb + 1) * S5_SUP_ST] = bu[:, S5_SUP_ST:2 * S5_SUP_ST]


def _s5_glu_out(hs_ref, u, cc_ref, d_ref, gw_ref, gb_ref, y_ref):
    n = S5_LANES
    ys = []
    for sb in range(S5_SUPER):
        h16 = jnp.concatenate([hs_ref[:, sb * S5_SUP_ST:(sb + 1) * S5_SUP_ST],
                               hs_ref[:, n + sb * S5_SUP_ST:n + (sb + 1) * S5_SUP_ST]], axis=1).astype(BF16)
        ys.append(_dot(h16, cc_ref[0, sb]))
    y = jnp.concatenate(ys, axis=1) + d_ref[0] * u
    y = _gelu_tanh(y)
    gate = _dot(y.astype(BF16), gw_ref[0]) + gb_ref[0]
    y_ref[...] = (y * _sigmoid(gate)).astype(y_ref.dtype)


def _s5_seq_kernel(u_ref, bb_ref, tab_ref, cc_ref, d_ref, gw_ref, gb_ref,
                   y_ref, nre_ref, nim_ref, hs_ref, cre_ref, cim_ref, *, lt):
    n = S5_LANES
    j = pl.program_id(1)

    @pl.when(j == 0)
    def _():
        cre_ref[...] = jnp.zeros(cre_ref.shape, F32)
        cim_ref[...] = jnp.zeros(cim_ref.shape, F32)

    u = u_ref[...]
    _s5_b_proj(u, bb_ref, hs_ref)

    def group(i, carry):
        r0 = pl.multiple_of(i * SUBLANE, SUBLANE)
        new = []
        for c in range(S5_BLOCKS):
            lr = slice(c * LANE, (c + 1) * LANE)
            li = slice(n + c * LANE, n + (c + 1) * LANE)
            xr = hs_ref[pl.ds(r0, SUBLANE), lr]
            xi = hs_ref[pl.ds(r0, SUBLANE), li]
            for lev in range(3):
                ar = tab_ref[0, lev, :, lr]
                ai = tab_ref[0, lev, :, li]
                sr = pltpu.roll(xr, 1 << lev, 0)
                si = pltpu.roll(xi, 1 << lev, 0)
                xr, xi = xr + ar * sr - ai * si, xi + ar * si + ai * sr
            pr = tab_ref[0, 3, :, lr]
            pi = tab_ref[0, 3, :, li]
            er, ei = carry[2 * c], carry[2 * c + 1]
            hr = xr + pr * er - pi * ei
            hi = xi + pr * ei + pi * er
            hs_ref[pl.ds(r0, SUBLANE), lr] = hr
            hs_ref[pl.ds(r0, SUBLANE), li] = hi
            new += [jnp.broadcast_to(hr[SUBLANE - 1:SUBLANE, :], (SUBLANE, LANE)),
                    jnp.broadcast_to(hi[SUBLANE - 1:SUBLANE, :], (SUBLANE, LANE))]
        return tuple(new)

    init = []
    for c in range(S5_BLOCKS):
        init += [cre_ref[:, c * LANE:(c + 1) * LANE], cim_ref[:, c * LANE:(c + 1) * LANE]]
    last = lax.fori_loop(0, lt // SUBLANE, group, tuple(init))
    for c in range(S5_BLOCKS):
        cre_ref[:, c * LANE:(c + 1) * LANE] = last[2 * c]
        cim_ref[:, c * LANE:(c + 1) * LANE] = last[2 * c + 1]
    nre_ref[0] = cre_ref[0:1, :]
    nim_ref[0] = cim_ref[0:1, :]
    _s5_glu_out(hs_ref, u, cc_ref, d_ref, gw_ref, gb_ref, y_ref)


def _s5_step_kernel(u_ref, bb_ref, ab_ref, cc_ref, d_ref, gw_ref, gb_ref, sre_ref, sim_ref,
                    y_ref, nre_ref, nim_ref, hs_ref, *, nb, lt):
    n = S5_LANES
    nblk = S5_BLOCKS
    u = u_ref[...]
    _s5_b_proj(u, bb_ref, hs_ref)
    ab_re = jnp.concatenate([ab_ref[0, c:c + 1, :] for c in range(nblk)], axis=1)
    ab_im = jnp.concatenate([ab_ref[0, nblk + c:nblk + c + 1, :] for c in range(nblk)], axis=1)
    hr = sre_ref[0]
    hi = sim_ref[0]
    for t in range(lt):
        rs = slice(t * nb, (t + 1) * nb)
        nr = ab_re * hr - ab_im * hi + hs_ref[rs, 0:n]
        ni = ab_re * hi + ab_im * hr + hs_ref[rs, n:2 * n]
        hr, hi = nr, ni
        hs_ref[rs, 0:n] = hr
        hs_ref[rs, n:2 * n] = hi
    nre_ref[0] = hr
    nim_ref[0] = hi
    _s5_glu_out(hs_ref, u, cc_ref, d_ref, gw_ref, gb_ref, y_ref)


def s5_mixer(h_b, p, layer, s_re, s_im, *, n_seq, nb, lt, n_tiles):
    rows = lt * nb
    n = S5_LANES
    has_state = s_re is not None
    in_specs = [pl.BlockSpec((rows, D_B), lambda s, j: (s * n_tiles + j, 0)),
                _layer_spec((S5_SUPER, S5_SUP_CH, 2 * S5_SUP_ST), layer),
                _layer_spec((2 * S5_BLOCKS, LANE), layer)]
    args = [h_b, p['s5_bb'], p['s5_ab']]
    if not has_state:
        in_specs[2] = _layer_spec((4, SUBLANE, 2 * n), layer)
        args[2] = p['s5_tab']
    in_specs += [_layer_spec((S5_SUPER, 2 * S5_SUP_ST, S5_SUP_CH), layer),
                 _layer_spec((1, D_B), layer),
                 _layer_spec((D_B, D_B), layer),
                 _layer_spec((1, D_B), layer)]
    args += [p['s5_cc'], p['s5_d'], p['s5_glu_w'], p['s5_glu_b']]
    if has_state:
        assert n_seq == 1 and n_tiles == 1
        in_specs += [_layer_spec((nb, n), layer)] * 2
        args += [s_re, s_im]
        body = functools.partial(_s5_step_kernel, nb=nb, lt=lt)
        scratch = [pltpu.VMEM((rows, 2 * n), F32)]
    else:
        assert nb == 1 and lt % SUBLANE == 0
        body = functools.partial(_s5_seq_kernel, lt=lt)
        scratch = [pltpu.VMEM((rows, 2 * n), F32),
                   pltpu.VMEM((SUBLANE, n), F32),
                   pltpu.VMEM((SUBLANE, n), F32)]
    st_spec = pl.BlockSpec((1, nb, n), lambda s, j: (s, 0, 0))
    return pl.pallas_call(
        body,
        out_shape=(jax.ShapeDtypeStruct((h_b.shape[0], D_B), BF16),
                   jax.ShapeDtypeStruct((n_seq, nb, n), F32),
                   jax.ShapeDtypeStruct((n_seq, nb, n), F32)),
        grid=(n_seq, n_tiles),
        in_specs=in_specs,
        out_specs=(pl.BlockSpec((rows, D_B), lambda s, j: (s * n_tiles + j, 0)), st_spec, st_spec),
        scratch_shapes=scratch,
        compiler_params=_cp("parallel", "arbitrary"),
        name="s5_mixer",
    )(*args)


def _group_rmsnorm(y, nw):
    half = D_C // SSD_GROUPS
    outs = []
    for g in range(SSD_GROUPS):
        yg = y[:, g * half:(g + 1) * half]
        outs.append(yg * lax.rsqrt(jnp.mean(yg * yg, axis=-1, keepdims=True) + EPS))
    return jnp.concatenate(outs, axis=1) * nw


def _mamba_p_kernel(h_ref, cw_ref, cb_ref, dtb_ref, alog_ref, dexp_ref, nw_ref, e_ref, tril_ref,
                    y_ref, ncst_ref, nsst_ref, ext_ref, st_ref, *, lt, n_tiles):
    q = SSD_CHUNK
    hist = SSD_CONV_WIDTH - 1
    pad = SUBLANE
    half = D_C // SSD_GROUPS
    hpg = SSD_HEADS // SSD_GROUPS
    j = pl.program_id(1)

    @pl.when(j == 0)
    def _():
        ext_ref[0:pad, :] = jnp.zeros((pad, D_XBC), F32)
        st_ref[...] = jnp.zeros(st_ref.shape, F32)

    ext_ref[pad:pad + lt, :] = h_ref[:, D_C:D_C + D_XBC]

    e = e_ref[...]
    tril = tril_ref[...]
    a_neg = -jnp.exp(alog_ref[0])
    li = lax.broadcasted_iota(jnp.int32, (q, q), 0)
    si = lax.broadcasted_iota(jnp.int32, (q, q), 1)
    causal = li >= si
    lane = lax.broadcasted_iota(jnp.int32, (q, LANE), 1)

    for c in range(lt // q):
        r0 = c * q
        acc = jnp.zeros((q, D_XBC), F32) + cb_ref[0]
        for k in range(SSD_CONV_WIDTH):
            acc = acc + cw_ref[0, k:k + 1, :] * ext_ref[pl.ds(pad - hist + k + r0, q), :]
        xc = _silu(acc)
        xs = xc[:, 0:D_C]
        z = h_ref[r0:r0 + q, 0:D_C]
        dt = _softplus(h_ref[r0:r0 + q, D_C + D_XBC:D_C + D_XBC + LANE] + dtb_ref[0])
        a = dt * a_neg
        hi_, mid_, lo_ = _split3(a)
        cs = _dot(tril, hi_) + _dot(tril, mid_) + _dot(tril, lo_)
        cs_last = cs[q - 1:q, :]
        dt_x = _expand(dt, e)
        ecs_x = _expand(jnp.exp(cs), e)
        edl_x = _expand(jnp.exp(cs_last - cs), e)
        xdt = xs * dt_x
        cs_t = cs.T

        y_parts = []
        for g in range(SSD_GROUPS):
            bm = xc[:, D_C + g * SSD_STATE:D_C + (g + 1) * SSD_STATE]
            cm = xc[:, D_C + SSD_GROUPS * SSD_STATE + g * SSD_STATE:
                    D_C + SSD_GROUPS * SSD_STATE + (g + 1) * SSD_STATE]
            bm16 = bm.astype(BF16)
            cm16 = cm.astype(BF16)
            cb = _dot_nt(cm16, bm16)
            for pr in range(hpg // 2):
                r_even = g * hpg + 2 * pr
                xpair = xdt[:, r_even * SSD_HEAD_DIM:(r_even + 2) * SSD_HEAD_DIM].astype(BF16)
                ys = []
                for r in (r_even, r_even + 1):
                    seg = cs[:, r:r + 1] - cs_t[r:r + 1, :]
                    dec = jnp.exp(jnp.where(causal, seg, -jnp.inf))
                    ys.append(_dot((cb * dec).astype(BF16), xpair))
                y_parts.append(jnp.where(lane < SSD_HEAD_DIM, ys[0], ys[1]))
        y_diag = jnp.concatenate(y_parts, axis=1)
        y_off = jnp.concatenate(
            [_dot(xc[:, D_C + SSD_GROUPS * SSD_STATE + g * SSD_STATE:
                      D_C + SSD_GROUPS * SSD_STATE + (g + 1) * SSD_STATE].astype(BF16),
                  st_ref[:, g * half:(g + 1) * half].astype(BF16)) for g in range(SSD_GROUPS)],
            axis=1) * ecs_x
        y = y_diag + y_off + dexp_ref[0] * xs
        y = y * _silu(z)
        y_ref[r0:r0 + q, :] = _group_rmsnorm(y, nw_ref[0]).astype(y_ref.dtype)

        xw = (xdt * edl_x).astype(BF16)
        dec_row = ecs_x[q - 1:q, :]
        for g in range(SSD_GROUPS):
            bm_t = xc[:, D_C + g * SSD_STATE:D_C + (g + 1) * SSD_STATE].T.astype(BF16)
            upd = _dot(bm_t, xw[:, g * half:(g + 1) * half])
            st_ref[:, g * half:(g + 1) * half] = (
                st_ref[:, g * half:(g + 1) * half] * dec_row[:, g * half:(g + 1) * half] + upd)

    new_hist = ext_ref[pl.ds(pad + lt - hist, hist), :]
    ncst_ref[0] = new_hist
    if n_tiles > 1:
        ext_ref[pad - hist:pad, :] = new_hist

    @pl.when(j == n_tiles - 1)
    def _():
        for blk in range(D_C // LANE):
            nsst_ref[0, blk * LANE:(blk + 1) * LANE, :] = st_ref[:, blk * LANE:(blk + 1) * LANE].T


def _ssd_consts():
    head_of_lane = jnp.arange(D_C) // SSD_HEAD_DIM
    e = (jnp.arange(LANE)[:, None] == head_of_lane[None, :]).astype(BF16)
    tril = (jnp.arange(SSD_CHUNK)[:, None] >= jnp.arange(SSD_CHUNK)[None, :]).astype(BF16)
    return e, tril


def _ssd_param_specs(layer):
    return [_layer_spec((SSD_CONV_WIDTH, D_XBC), layer),
            _layer_spec((1, D_XBC), layer),
            _layer_spec((1, LANE), layer),
            _layer_spec((1, LANE), layer),
            _layer_spec((1, D_C), layer),
            _layer_spec((1, D_C), layer)]


def _ssd_param_args(p):
    return [p['conv_c_w'], p['conv_c_b'], p['ssd_dt_bias'], p['ssd_a_log'], p['ssd_d'], p['ssd_norm_w']]


def mamba_prompt(h_c, p, layer, *, n_seq, lt, n_tiles):
    hist = SSD_CONV_WIDTH - 1
    const = lambda s, j: (0, 0)
    return pl.pallas_call(
        functools.partial(_mamba_p_kernel, lt=lt, n_tiles=n_tiles),
        out_shape=(jax.ShapeDtypeStruct((h_c.shape[0], D_C), BF16),
                   jax.ShapeDtypeStruct((n_seq, hist, D_XBC), F32),
                   jax.ShapeDtypeStruct((n_seq, D_C, SSD_STATE), F32)),
        grid=(n_seq, n_tiles),
        in_specs=[pl.BlockSpec((lt, D_HC), lambda s, j: (s * n_tiles + j, 0))]
        + _ssd_param_specs(layer)
        + [pl.BlockSpec((LANE, D_C), const), pl.BlockSpec((SSD_CHUNK, SSD_CHUNK), const)],
        out_specs=(pl.BlockSpec((lt, D_C), lambda s, j: (s * n_tiles + j, 0)),
                   pl.BlockSpec((1, hist, D_XBC), lambda s, j: (s, 0, 0)),
                   pl.BlockSpec((1, D_C, SSD_STATE), lambda s, j: (s, 0, 0))),
        scratch_shapes=[pltpu.VMEM((SUBLANE + lt, D_XBC), F32),
                        pltpu.VMEM((SSD_STATE, D_C), F32)],
        compiler_params=_cp("parallel", "arbitrary"),
        name="mamba_prompt",
    )(h_c, *_ssd_param_args(p), p['ssd_e'], p['ssd_tril'])


def _ks(c, k, nb):
    return slice((c * SUBLANE + k) * nb, (c * SUBLANE + k + 1) * nb)


def _slab_put(ref, k, slab, nb):
    for c in range(slab.shape[1] // LANE):
        ref[_ks(c, k, nb), :] = slab[:, c * LANE:(c + 1) * LANE]


def _slab_get(ref, k, n_blocks, nb):
    return jnp.concatenate([ref[_ks(c, k, nb), :] for c in range(n_blocks)], axis=1)


def _seq_get(ref, b, n_blocks, nb):
    return jnp.concatenate(
        [ref[pl.ds(c * SUBLANE * nb + b, SUBLANE, stride=nb), :] for c in range(n_blocks)], axis=1)


def _seq_put(ref, b, val, nb, c0=0):
    for c in range(val.shape[1] // LANE):
        ref[pl.ds((c0 + c) * SUBLANE * nb + b, SUBLANE, stride=nb), :] = val[:, c * LANE:(c + 1) * LANE]


def _mamba_s_kernel(h_ref, cw_ref, cb_ref, dtb_ref, alog_ref, dexp_ref, nw_ref, e_ref, cst_ref, sst_ref,
                    y_ref, ncst_ref, nsst_ref,
                    ext_ref, xs_ref, dt_ref, cs_ref, lhs_ref, rhs_ref, c8_ref, yoff_ref,
                    *, nb, lt, bb, lsel, passthrough):
    hist = (SSD_CONV_WIDTH - 1) * nb
    rows = lt * nb
    half = D_C // SSD_GROUPS
    hpg = SSD_HEADS // SSD_GROUPS
    xblk = D_C // LANE
    hblk = half // LANE
    i = pl.program_id(0)
    n_steps = pl.num_programs(0)
    bc_off = D_C
    cc_off = D_C + SSD_GROUPS * SSD_STATE

    @pl.when(i == 0)
    def _phase1():
        e = e_ref[...]
        ext_ref[0:hist, :] = cst_ref[0]
        ext_ref[hist:hist + rows, :] = h_ref[:, D_C:D_C + D_XBC]
        ncst_ref[...] = ext_ref[rows:rows + hist, :]
        a_neg = -jnp.exp(alog_ref[0])
        lhs_ref[...] = jnp.zeros(lhs_ref.shape, F32)
        rhs_ref[...] = jnp.zeros(rhs_ref.shape, F32)
        c8_ref[...] = jnp.zeros(c8_ref.shape, F32)
        cs = jnp.zeros((nb, LANE), F32)
        for t in range(lt):
            rs = slice(t * nb, (t + 1) * nb)
            acc = jnp.zeros((nb, D_XBC), F32) + cb_ref[0]
            for k in range(SSD_CONV_WIDTH):
                acc = acc + cw_ref[0, k:k + 1, :] * ext_ref[(t + k) * nb:(t + k + 1) * nb, :]
            xc = _silu(acc)
            xs_ref[rs, :] = xc[:, 0:D_C]
            for g in range(SSD_GROUPS):
                rhs_ref[_ks(2 * g, t, nb), :] = xc[:, bc_off + g * SSD_STATE:bc_off + (g + 1) * SSD_STATE]
            _slab_put(c8_ref, t, xc[:, cc_off:cc_off + SSD_GROUPS * SSD_STATE], nb)
            dt = _softplus(h_ref[rs, D_C + D_XBC:D_C + D_XBC + LANE] + dtb_ref[0])
            dt_ref[rs, :] = dt
            cs = cs + dt * a_neg
            cs_ref[rs, :] = cs
        cs_last = cs
        for t in range(lt):
            rs = slice(t * nb, (t + 1) * nb)
            wt = jnp.exp(cs_last - cs_ref[rs, :]) * dt_ref[rs, :]
            _slab_put(lhs_ref, t, xs_ref[rs, :] * _expand(wt, e), nb)
        dec = _expand(jnp.exp(cs_last), e)
        d_hi = dec.astype(BF16).astype(F32)
        d_r = dec - d_hi
        d_mid = d_r.astype(BF16).astype(F32)
        d_lo = d_r - d_mid
        ones = jnp.ones((nb, SSD_STATE), F32)
        for k, piece in enumerate((d_hi, d_mid, d_lo)):
            _slab_put(lhs_ref, lt + k, piece, nb)
            for g in range(SSD_GROUPS):
                rhs_ref[_ks(2 * g + 1, lt + k, nb), :] = ones

    for jb in range(bb):
        b = i * bb + jb
        l8 = _seq_get(lhs_ref, b, xblk, nb).astype(BF16)
        r8 = _seq_get(rhs_ref, b, 2 * SSD_GROUPS, nb).astype(BF16)
        c8 = _seq_get(c8_ref, b, SSD_GROUPS, nb).astype(BF16)
        for g in range(SSD_GROUPS):
            s = sst_ref[lsel, jb, g * half:(g + 1) * half, :]
            yo = _dot_nt(c8[:, g * SSD_STATE:(g + 1) * SSD_STATE], s.astype(BF16))
            _seq_put(yoff_ref, b, yo, nb, c0=g * hblk)
            upd = _dot_tn(l8[:, g * half:(g + 1) * half],
                          r8[:, g * 2 * SSD_STATE:(g + 1) * 2 * SSD_STATE])
            nsst_ref[lsel, jb, g * half:(g + 1) * half, :] = upd[:, SSD_STATE:] * s + upd[:, :SSD_STATE]
    for d in passthrough:
        nsst_ref[d] = sst_ref[d]

    @pl.when(i == n_steps - 1)
    def _phase3():
        e = e_ref[...]
        lane = lax.broadcasted_iota(jnp.int32, (nb, LANE), 1)
        for t in range(lt):
            rt = slice(t * nb, (t + 1) * nb)
            cs_t = cs_ref[rt, :]
            y = (_slab_get(yoff_ref, t, xblk, nb) * _expand(jnp.exp(cs_t), e)
                 + dexp_ref[0] * xs_ref[rt, :])
            for s_ in range(t + 1):
                rsl = slice(s_ * nb, (s_ + 1) * nb)
                cbs = []
                for g in range(SSD_GROUPS):
                    cm = c8_ref[_ks(g, t, nb), :]
                    bm = rhs_ref[_ks(2 * g, s_, nb), :]
                    cbs.append(jnp.sum(cm * bm, axis=-1, keepdims=True))
                cb = jnp.where(lane < hpg, cbs[0], cbs[1])
                m = jnp.exp(cs_t - cs_ref[rsl, :]) * dt_ref[rsl, :] * cb
                y = y + _expand(m, e) * xs_ref[rsl, :]
            y = y * _silu(h_ref[rt, 0:D_C])
            y_ref[rt, :] = _group_rmsnorm(y, nw_ref[0]).astype(y_ref.dtype)


def mamba_sample(h_c, p, layer, cst, sst, *, nb, lt, bb, in_place):
    rows = lt * nb
    hist = (SSD_CONV_WIDTH - 1) * nb
    depth = sst.shape[0]
    const = lambda i: (0, 0)
    if in_place:
        sst_spec = pl.BlockSpec((1, bb, D_C, SSD_STATE), lambda i: (layer, i, 0, 0))
        lsel, passthrough = 0, ()
    else:
        sst_spec = pl.BlockSpec((depth, bb, D_C, SSD_STATE), lambda i: (0, i, 0, 0))
        lsel, passthrough = layer, tuple(d for d in range(depth) if d != layer)
    in_specs = ([pl.BlockSpec((rows, D_HC), const)] + _ssd_param_specs(layer)
                + [pl.BlockSpec((LANE, D_C), const), _layer_spec((hist, D_XBC), layer), sst_spec])
    return pl.pallas_call(
        functools.partial(_mamba_s_kernel, nb=nb, lt=lt, bb=bb, lsel=lsel, passthrough=passthrough),
        out_shape=(jax.ShapeDtypeStruct((rows, D_C), BF16),
                   jax.ShapeDtypeStruct((hist, D_XBC), F32),
                   jax.ShapeDtypeStruct(sst.shape, F32)),
        grid=(nb // bb,),
        in_specs=in_specs,
        out_specs=(pl.BlockSpec((rows, D_C), const),
                   pl.BlockSpec((hist, D_XBC), const),
                   sst_spec),
        input_output_aliases={len(in_specs) - 1: 2} if in_place else {},
        scratch_shapes=[pltpu.VMEM((hist + rows, D_XBC), F32),
                        pltpu.VMEM((rows, D_C), F32),
                        pltpu.VMEM((rows, LANE), F32),
                        pltpu.VMEM((rows, LANE), F32),
                        pltpu.VMEM((D_C // LANE * SUBLANE * nb, LANE), F32),
                        pltpu.VMEM((2 * SSD_GROUPS * SUBLANE * nb, LANE), F32),
                        pltpu.VMEM((SSD_GROUPS * SUBLANE * nb, LANE), F32),
                        pltpu.VMEM((D_C // LANE * SUBLANE * nb, LANE), F32)],
        compiler_params=_cp("arbitrary"),
        name="mamba_sample",
    )(h_c, *_ssd_param_args(p), p['ssd_e'], cst, sst)


def _softmax_rows(s):
    m = jnp.max(s, axis=-1, keepdims=True)
    ex = jnp.exp(s - m)
    return ex / jnp.sum(ex, axis=-1, keepdims=True)


def _attn_p_kernel(q_ref, k_ref, v_ref, o_ref):
    for h in range(XA_HEADS):
        hs = slice(h * XA_HEAD_DIM, (h + 1) * XA_HEAD_DIM)
        s = _dot_nt(q_ref[:, hs].astype(BF16), k_ref[0, :, hs].astype(BF16)) / math.sqrt(XA_HEAD_DIM)
        p = _softmax_rows(s)
        o_ref[:, hs] = _dot(p.astype(BF16), v_ref[0, :, hs].astype(BF16)).astype(o_ref.dtype)


def attn_prompt(q, k, v, *, n_seq, seq, tq):
    n_tiles = seq // tq
    return pl.pallas_call(
        _attn_p_kernel,
        out_shape=jax.ShapeDtypeStruct(q.shape, BF16),
        grid=(n_seq, n_tiles),
        in_specs=[pl.BlockSpec((tq, D_MODEL), lambda s, j: (s * n_tiles + j, 0)),
                  pl.BlockSpec((1, N_MEM, D_MODEL), lambda s, j: (s, 0, 0)),
                  pl.BlockSpec((1, N_MEM, D_MODEL), lambda s, j: (s, 0, 0))],
        out_specs=pl.BlockSpec((tq, D_MODEL), lambda s, j: (s * n_tiles + j, 0)),
        compiler_params=_cp("parallel", "arbitrary"),
        name="attn_prompt",
    )(q, k, v)


def _attn_s_kernel(q_ref, k_ref, v_ref, o_ref, *, bb, lt):
    rows = XA_HEADS * lt
    n = N_MEM * XA_HEADS
    col_head = lax.broadcasted_iota(jnp.int32, (rows, n), 1) % XA_HEADS
    row_head = lax.broadcasted_iota(jnp.int32, (rows, n), 0) // lt
    same_head = col_head == row_head
    for jb in range(bb):
        k = k_ref[0, jb].reshape(n, XA_HEAD_DIM).astype(BF16)
        v = v_ref[0, jb].reshape(n, XA_HEAD_DIM).astype(BF16)
        s = _dot_nt(q_ref[jb].astype(BF16), k) / math.sqrt(XA_HEAD_DIM)
        p = _softmax_rows(jnp.where(same_head, s, -jnp.inf))
        o_ref[jb] = _dot(p.astype(BF16), v)


def attn_sample(q, k, v, layer, *, bb):
    nb, rows, _ = q.shape
    kv_spec = pl.BlockSpec((1, bb, N_MEM, XA_HEADS, XA_HEAD_DIM), lambda i: (layer, i, 0, 0, 0))
    return pl.pallas_call(
        functools.partial(_attn_s_kernel, bb=bb, lt=rows // XA_HEADS),
        out_shape=jax.ShapeDtypeStruct((nb, rows, XA_HEAD_DIM), F32),
        grid=(nb // bb,),
        in_specs=[pl.BlockSpec((bb, rows, XA_HEAD_DIM), lambda i: (i, 0, 0)), kv_spec, kv_spec],
        out_specs=pl.BlockSpec((bb, rows, XA_HEAD_DIM), lambda i: (i, 0, 0)),
        compiler_params=_cp("parallel"),
        name="attn_sample",
    )(q, k, v)


def _ffn_kernel(*refs, nb, tiles_per_seq, has_state, final_norm):
    refs = list(refs)
    x_ref, nw_ref, wg_ref, wu_ref, cw_ref, cb_ref, wd_ref = refs[:7]
    pos = 7
    st_ref = None
    if has_state:
        st_ref = refs[pos]
        pos += 1
    fw_ref = None
    if final_norm:
        fw_ref = refs[pos]
        pos += 1
    o_ref, nst_ref, xn_ref, gext_ref, carry_ref = refs[pos:pos + 5]

    hist = (FFN_CONV_WIDTH - 1) * nb
    pad = _round_up(hist, SUBLANE)
    tm = x_ref.shape[0]
    tf = wg_ref.shape[3]
    rsz = min(FFN_ROW_SPLIT, tm)
    i = pl.program_id(0)
    f = pl.program_id(1)
    n_f = pl.num_programs(1)

    @pl.when(f == 0)
    def _():
        x = x_ref[...]
        xn_ref[...] = _rmsnorm_rows(x, nw_ref[0]).astype(BF16)
        o_ref[...] = x

    if tiles_per_seq > 1:
        first = (i % tiles_per_seq) == 0

        @pl.when(first)
        def _():
            if has_state:
                gext_ref[pad - hist:pad, :] = st_ref[0]
            else:
                gext_ref[0:pad, :] = jnp.zeros((pad, tf), F32)

        @pl.when(jnp.logical_not(first))
        def _():
            gext_ref[0:pad, :] = carry_ref[f]
    else:
        if has_state:
            gext_ref[pad - hist:pad, :] = st_ref[0]
        else:
            gext_ref[0:pad, :] = jnp.zeros((pad, tf), F32)

    cw0 = cw_ref[0, 0:1, :]
    cw1 = cw_ref[0, 1:2, :]
    cw2 = cw_ref[0, 2:3, :]
    cb = cb_ref[0]
    for r0 in range(0, tm, rsz):
        xn = xn_ref[r0:r0 + rsz, :]
        g = _dot(xn, wg_ref[0, 0])
        up = _dot(xn, wu_ref[0, 0])
        gext_ref[pad + r0:pad + r0 + rsz, :] = g
        conv = (cw0 * gext_ref[pl.ds(pad - 2 * nb + r0, rsz), :]
                + cw1 * gext_ref[pl.ds(pad - nb + r0, rsz), :]
                + cw2 * g + cb)
        act = _silu(conv) * up
        o_ref[r0:r0 + rsz, :] += _dot(act.astype(BF16), wd_ref[0])

    nst_ref[0] = gext_ref[pl.ds(pad + tm - hist, hist), :]
    if tiles_per_seq > 1:
        carry_ref[f] = gext_ref[pl.ds(tm, pad), :]

    if final_norm:
        @pl.when(f == n_f - 1)
        def _():
            o_ref[...] = _rmsnorm_rows(o_ref[...], fw_ref[...])


def conv_ffn(x, p, layer, state, final_w, *, n_seq, nb, tm, tiles_per_seq):
    m = x.shape[0]
    tf = FF_TILE
    n_f = D_FF_PAD // tf
    hist = (FFN_CONV_WIDTH - 1) * nb
    pad = _round_up(hist, SUBLANE)
    has_state = state is not None
    final_norm = final_w is not None
    in_specs = [pl.BlockSpec((tm, D_MODEL), lambda i, f: (i, 0)),
                _layer_spec((1, D_MODEL), layer),
                pl.BlockSpec((1, 1, D_MODEL, tf), lambda i, f: (layer, f, 0, 0)),
                pl.BlockSpec((1, 1, D_MODEL, tf), lambda i, f: (layer, f, 0, 0)),
                pl.BlockSpec((1, FFN_CONV_WIDTH, tf), lambda i, f: (layer, 0, f)),
                pl.BlockSpec((1, 1, tf), lambda i, f: (layer, 0, f)),
                pl.BlockSpec((1, tf, D_MODEL), lambda i, f: (layer, f, 0))]
    args = [x, p['norm_ffn_w'], p['ffn_wg'], p['ffn_wu'], p['ffn_conv_w'], p['ffn_conv_b'], p['ffn_wd']]
    if has_state:
        assert n_seq == 1 and tiles_per_seq == 1
        in_specs.append(pl.BlockSpec((1, hist, tf), lambda i, f: (layer, 0, f)))
        args.append(state)
    if final_norm:
        in_specs.append(pl.BlockSpec((1, D_MODEL), lambda i, f: (0, 0)))
        args.append(final_w.reshape(1, D_MODEL))
    return pl.pallas_call(
        functools.partial(_ffn_kernel, nb=nb, tiles_per_seq=tiles_per_seq, has_state=has_state,
                          final_norm=final_norm),
        out_shape=(jax.ShapeDtypeStruct((m, D_MODEL), F32),
                   jax.ShapeDtypeStruct((m // tm, hist, D_FF_PAD), F32)),
        grid=(m // tm, n_f),
        in_specs=in_specs,
        out_specs=(pl.BlockSpec((tm, D_MODEL), lambda i, f: (i, 0)),
                   pl.BlockSpec((1, hist, tf), lambda i, f: (i, 0, f))),
        scratch_shapes=[pltpu.VMEM((tm, D_MODEL), BF16),
                        pltpu.VMEM((pad + tm, tf), F32),
                        pltpu.VMEM((n_f, pad, tf), F32)],
        compiler_params=_cp("arbitrary", "arbitrary"),
        name="conv_ffn",
    )(*args)


def _s5_params(lam_re, lam_im, log_dt, b_re, b_im, c_re, c_im):
    depth = lam_re.shape[0]
    dt = jnp.exp(log_dt)[..., None]
    mag = jnp.exp(lam_re * dt)
    ang = lam_im * dt
    ab_re, ab_im = mag * jnp.cos(ang), mag * jnp.sin(ang)
    blocks = lambda re, im: jnp.concatenate(
        [re.reshape(*re.shape[:-2], S5_BLOCKS, LANE), im.reshape(*im.shape[:-2], S5_BLOCKS, LANE)], axis=-2)
    row = jnp.arange(SUBLANE, dtype=F32)
    expo = jnp.stack([jnp.full((SUBLANE,), 1.0), jnp.full((SUBLANE,), 2.0), jnp.full((SUBLANE,), 4.0),
                      row + 1.0])
    keep = jnp.stack([row >= 1, row >= 2, row >= 4, row >= 0]).astype(F32)
    lam_dt = (lam_re * dt).reshape(depth, 1, 1, S5_LANES)
    pang = ang.reshape(depth, 1, 1, S5_LANES) * expo[None, :, :, None]
    pmag = jnp.exp(lam_dt * expo[None, :, :, None]) * keep[None, :, :, None]
    tab = jnp.concatenate([pmag * jnp.cos(pang), pmag * jnp.sin(pang)], axis=-1)
    den = lam_re * lam_re + lam_im * lam_im
    nr, ni = ab_re - 1.0, ab_im
    co_re = (nr * lam_re + ni * lam_im) / den
    co_im = (ni * lam_re - nr * lam_im) / den
    bb_re = co_re[..., None] * b_re - co_im[..., None] * b_im
    bb_im = co_re[..., None] * b_im + co_im[..., None] * b_re
    gps = S5_GROUPS // S5_SUPER
    eye = jnp.eye(gps, dtype=F32)
    sup = lambda m: m.reshape(depth, S5_SUPER, gps, *m.shape[2:])
    dense_b = lambda m: jnp.einsum('lsgph,gk->lsghkp', sup(m), eye).reshape(depth, S5_SUPER, S5_SUP_CH, S5_SUP_ST)
    dense_c = lambda m: jnp.einsum('lsghp,gk->lskpgh', sup(m), eye).reshape(depth, S5_SUPER, S5_SUP_ST, S5_SUP_CH)
    bb = jnp.concatenate([dense_b(bb_re), dense_b(bb_im)], axis=3).astype(BF16)
    cc = jnp.concatenate([dense_c(c_re), -dense_c(c_im)], axis=2).astype(BF16)
    return bb, blocks(ab_re, ab_im), tab, cc


def _wprep_kernel(w_ref, o_ref, *, axis, valid_last):
    f = pl.program_id(1)
    last = pl.num_programs(1) - 1
    o = o_ref.at[0, 0] if axis == 1 else o_ref.at[0]

    @pl.when(f < last)
    def _():
        o[...] = w_ref[0].astype(BF16)

    @pl.when(f == last)
    def _():
        if axis == 1:
            o[:, :valid_last] = w_ref[0, :, :valid_last].astype(BF16)
            o[:, valid_last:] = jnp.zeros((o.shape[0], o.shape[1] - valid_last), BF16)
        else:
            o[:valid_last, :] = w_ref[0, :valid_last, :].astype(BF16)
            o[valid_last:, :] = jnp.zeros((o.shape[0] - valid_last, o.shape[1]), BF16)


def ffn_weight_cols(w, tf):
    depth, k, n = w.shape
    n_f = pl.cdiv(n, tf)
    return pl.pallas_call(
        functools.partial(_wprep_kernel, axis=1, valid_last=n - (n_f - 1) * tf),
        out_shape=jax.ShapeDtypeStruct((depth, n_f, k, tf), BF16),
        grid=(depth, n_f),
        in_specs=[pl.BlockSpec((1, k, tf), lambda l, f: (l, 0, f))],
        out_specs=pl.BlockSpec((1, 1, k, tf), lambda l, f: (l, f, 0, 0)),
        compiler_params=_cp("parallel", "parallel"),
        name="ffn_weight_cols",
    )(w)


def ffn_weight_rows(w, tf):
    depth, k, n = w.shape
    n_f = pl.cdiv(k, tf)
    return pl.pallas_call(
        functools.partial(_wprep_kernel, axis=0, valid_last=k - (n_f - 1) * tf),
        out_shape=jax.ShapeDtypeStruct((depth, n_f * tf, n), BF16),
        grid=(depth, n_f),
        in_specs=[pl.BlockSpec((1, tf, n), lambda l, f: (l, f, 0))],
        out_specs=pl.BlockSpec((1, tf, n), lambda l, f: (l, f, 0)),
        compiler_params=_cp("parallel", "parallel"),
        name="ffn_weight_rows",
    )(w)


def kernel(x_prompt, x_sample, mem_prompt, cache_mem_k, cache_mem_v, state_conv_a, state_s5_re, state_s5_im, state_conv_c, state_ssd, state_ffn_conv, norm_mix_w, w_in, conv_a_w, conv_a_b, ln_a_w, ln_a_b, s5_lam_re, s5_lam_im, s5_log_dt, s5_b_re, s5_b_im, s5_c_re, s5_c_im, s5_d, s5_glu_w, s5_glu_b, conv_c_w, conv_c_b, ssd_dt_bias, ssd_a_log, ssd_d, ssd_norm_w, w_out, norm_xa_w, norm_mem_w, xa_wq, xa_wk, xa_wv, xa_wo, norm_ffn_w, ffn_w_gate, ffn_w_up, ffn_conv_w, ffn_conv_b, ffn_w_down, final_norm_w):
    bp, seq, _ = x_prompt.shape
    nbs, lts, _ = x_sample.shape
    depth = w_in.shape[0]
    n_mem = mem_prompt.shape[1]
    lt_p = 512 if seq % 512 == 0 else seq
    n_tiles_p = seq // lt_p
    tm_p = lt_p
    tm_s = lts * nbs
    tm_f = 1024 if seq % 1024 == 0 else tm_p
    tm_m = min(512, bp * n_mem)

    vec = lambda a: a.reshape(depth, 1, a.shape[-1])
    pad_lanes = lambda a: vec(jnp.pad(a, ((0, 0), (0, LANE - a.shape[-1]))))
    ff_pad = D_FF_PAD - D_FF
    s5_bb, s5_ab, s5_tab, s5_cc = _s5_params(s5_lam_re, s5_lam_im, s5_log_dt, s5_b_re, s5_b_im, s5_c_re, s5_c_im)
    ssd_e, ssd_tril = _ssd_consts()
    p = {
        'norm_mix_w': vec(norm_mix_w), 'norm_xa_w': vec(norm_xa_w), 'norm_mem_w': vec(norm_mem_w),
        'norm_ffn_w': vec(norm_ffn_w),
        'w_in': jnp.pad(w_in, ((0, 0), (0, 0), (0, IN_SPLITS[2] - w_in.shape[2]))).astype(BF16),
        'conv_a_w': conv_a_w, 'conv_a_b': vec(conv_a_b), 'ln_a_w': vec(ln_a_w), 'ln_a_b': vec(ln_a_b),
        's5_bb': s5_bb, 's5_ab': s5_ab, 's5_tab': s5_tab, 's5_cc': s5_cc,
        's5_d': vec(s5_d), 's5_glu_w': s5_glu_w.astype(BF16), 's5_glu_b': vec(s5_glu_b),
        'conv_c_w': conv_c_w, 'conv_c_b': vec(conv_c_b),
        'ssd_dt_bias': pad_lanes(ssd_dt_bias), 'ssd_a_log': pad_lanes(ssd_a_log),
        'ssd_d': vec(jnp.repeat(ssd_d, SSD_HEAD_DIM, axis=1)), 'ssd_norm_w': vec(ssd_norm_w),
        'ssd_e': ssd_e, 'ssd_tril': ssd_tril,
        'w_out': w_out.astype(BF16), 'wq': xa_wq.astype(BF16), 'wo': xa_wo.astype(BF16),
        'wkv': jnp.concatenate([xa_wk, xa_wv], axis=2).astype(BF16),
        'ffn_wg': ffn_weight_cols(ffn_w_gate, FF_TILE),
        'ffn_wu': ffn_weight_cols(ffn_w_up, FF_TILE),
        'ffn_wd': ffn_weight_rows(ffn_w_down, FF_TILE),
        'ffn_conv_w': jnp.pad(ffn_conv_w, ((0, 0), (0, 0), (0, ff_pad))),
        'ffn_conv_b': vec(jnp.pad(ffn_conv_b, ((0, 0), (0, ff_pad)))),
    }

    tmaj = lambda a: a.transpose(0, 2, 1, 3).reshape(depth, a.shape[2] * nbs, a.shape[3])
    st_conv_a = tmaj(state_conv_a)
    st_conv_c = tmaj(state_conv_c)
    st_ffn = jnp.pad(tmaj(state_ffn_conv), ((0, 0), (0, 0), (0, ff_pad)))
    st_re = state_s5_re.reshape(depth, nbs, S5_LANES)
    st_im = state_s5_im.reshape(depth, nbs, S5_LANES)
    ssd_all = state_ssd.reshape(depth, nbs, D_C, SSD_STATE)

    xp = x_prompt.reshape(bp * seq, D_MODEL)
    xs = x_sample.transpose(1, 0, 2).reshape(lts * nbs, D_MODEL)
    mem2d = mem_prompt.reshape(bp * n_mem, D_MODEL)

    def mixers(x, l, *, n_seq, nb, lt, n_tiles, tm, sample):
        h_a, h_b, h_c = in_proj(x, p['norm_mix_w'], p['w_in'], l, tm=tm)
        ya, n_conv_a = conva_mixer(h_a, p, l, st_conv_a if sample else None,
                                   n_seq=n_seq, nb=nb, lt=lt, n_tiles=n_tiles)
        yb, n_re, n_im = s5_mixer(h_b, p, l, st_re if sample else None, st_im if sample else None,
                                  n_seq=n_seq, nb=nb, lt=lt, n_tiles=n_tiles)
        return h_c, ya, yb, n_conv_a, n_re, n_im

    outs_p = [[] for _ in range(8)]
    outs_s = [[] for _ in range(5)]
    for l in range(depth):
        last = l == depth - 1
        mk, mv = mem_kv(mem2d, p['norm_mem_w'], p['wkv'], l, tm=tm_m)

        h_c, ya, yb, p_conv_a, p_re, p_im = mixers(xp, l, n_seq=bp, nb=1, lt=lt_p, n_tiles=n_tiles_p,
                                                   tm=tm_p, sample=False)
        yc, p_conv_c, p_ssd = mamba_prompt(h_c, p, l, n_seq=bp, lt=lt_p, n_tiles=n_tiles_p)
        xp, q = out_q_proj(ya, yb, yc, p['w_out'], xp, p['norm_xa_w'], p['wq'], l, tm=tm_p)
        o = attn_prompt(q, mk.reshape(bp, n_mem, D_MODEL), mv.reshape(bp, n_mem, D_MODEL),
                        n_seq=bp, seq=seq, tq=lt_p)
        xp = proj_res([o], p['wo'], l, xp, tm=tm_p, name="attn_out")
        xp, p_ffn = conv_ffn(xp, p, l, None, final_norm_w if last else None, n_seq=bp, nb=1, tm=tm_f,
                             tiles_per_seq=seq // tm_f)
        for lst, v in zip(outs_p, (p_conv_a, p_re, p_im, p_conv_c, p_ssd,
                                   p_ffn[seq // tm_f - 1::seq // tm_f], mk, mv)):
            lst.append(v)

        h_c, ya, yb, s_conv_a, s_re, s_im = mixers(xs, l, n_seq=1, nb=nbs, lt=lts, n_tiles=1, tm=tm_s,
                                                   sample=True)
        yc, s_conv_c, ssd_all = mamba_sample(h_c, p, l, st_conv_c, ssd_all, nb=nbs, lt=lts,
                                             bb=8 if l > 0 else 4, in_place=l > 0)
        xs, q = out_q_proj(ya, yb, yc, p['w_out'], xs, p['norm_xa_w'], p['wq'], l, tm=tm_s,
                           q_by_seq=(nbs, lts))
        o = attn_sample(q, cache_mem_k, cache_mem_v, l, bb=4)
        xs = attn_out_seq(o, p['wo'], l, xs, nb=nbs, lt=lts)
        xs, s_ffn = conv_ffn(xs, p, l, st_ffn, final_norm_w if last else None, n_seq=1, nb=nbs, tm=tm_s,
                             tiles_per_seq=1)
        for lst, v in zip(outs_s, (s_conv_a[0], s_re[0], s_im[0], s_conv_c, s_ffn[0])):
            lst.append(v)

    p_conv_a, p_re, p_im, p_conv_c, p_ssd, p_ffn, p_mk, p_mv = [jnp.stack(o) for o in outs_p]
    s_conv_a, s_re, s_im, s_conv_c, s_ffn = [jnp.stack(o) for o in outs_s]
    bmaj = lambda a, w: a.reshape(depth, w, nbs, a.shape[-1]).transpose(0, 2, 1, 3)
    y_prompt = xp.reshape(bp, seq, D_MODEL)
    y_sample = xs.reshape(lts, nbs, D_MODEL).transpose(1, 0, 2)
    return (y_prompt, y_sample,
            p_conv_a,
            p_re.reshape(depth, bp, S5_GROUPS, S5_STATE), p_im.reshape(depth, bp, S5_GROUPS, S5_STATE),
            p_conv_c,
            p_ssd.reshape(depth, bp, SSD_HEADS, SSD_HEAD_DIM, SSD_STATE),
            p_ffn[..., :D_FF],
            p_mk.reshape(depth, bp, n_mem, XA_HEADS, XA_HEAD_DIM),
            p_mv.reshape(depth, bp, n_mem, XA_HEADS, XA_HEAD_DIM),
            bmaj(s_conv_a, CONV_A_WIDTH - 1),
            s_re.reshape(depth, nbs, S5_GROUPS, S5_STATE), s_im.reshape(depth, nbs, S5_GROUPS, S5_STATE),
            bmaj(s_conv_c, SSD_CONV_WIDTH - 1),
            ssd_all.reshape(state_ssd.shape),
            bmaj(s_ffn, FFN_CONV_WIDTH - 1)[..., :D_FF])
```

```python
import functools
import math

import jax
import jax.numpy as jnp
from jax import lax
from jax.experimental import pallas as pl
from jax.experimental.pallas import tpu as pltpu

F32 = jnp.float32
BF16 = jnp.bfloat16
EPS = 1e-6

D_MODEL = 2048
D_A = 512
D_B = 512
D_C = 1024
CONV_A_WIDTH = 31
S5_GROUP = 16
S5_GROUPS = 32
S5_STATE = 64
S5_LANES = S5_GROUPS * S5_STATE
SSD_HEAD_DIM = 64
SSD_HEADS = 16
SSD_GROUPS = 2
SSD_STATE = 128
SSD_CONV_WIDTH = 4
SSD_CHUNK = 128
D_XBC = D_C + 2 * SSD_GROUPS * SSD_STATE
D_HC = D_C + D_XBC + 128
XA_HEADS = 4
XA_HEAD_DIM = 512
N_MEM = 256
D_FF = 5504
FFN_CONV_WIDTH = 3

LANE = 128
SUBLANE = 8
VMEM_LIMIT = 56 * 1024 * 1024
FF_TILE = 512
D_FF_PAD = ((D_FF + FF_TILE - 1) // FF_TILE) * FF_TILE
FFN_ROW_SPLIT = 256


def _round_up(x, m):
    return (x + m - 1) // m * m


def _cp(*sem):
    return pltpu.CompilerParams(dimension_semantics=sem, vmem_limit_bytes=VMEM_LIMIT)


def _layer_spec(tail, layer):
    zeros = (0,) * len(tail)
    return pl.BlockSpec((1,) + tuple(tail), lambda *_: (layer,) + zeros)


def _dot(a, b):
    return jnp.dot(a, b, preferred_element_type=F32)


def _dot_nt(a, b):
    return lax.dot_general(a, b, (((1,), (1,)), ((), ())), preferred_element_type=F32)


def _dot_tn(a, b):
    return lax.dot_general(a, b, (((0,), (0,)), ((), ())), preferred_element_type=F32)


def _split3(a):
    hi = a.astype(BF16)
    r = a - hi.astype(F32)
    mid = r.astype(BF16)
    lo = (r - mid.astype(F32)).astype(BF16)
    return hi, mid, lo


def _expand(a, e):
    hi, mid, lo = _split3(a)
    return _dot(hi, e) + _dot(mid, e) + _dot(lo, e)


def _sigmoid(x):
    return jax.nn.sigmoid(x)


def _silu(x):
    return x * jax.nn.sigmoid(x)


def _softplus(x):
    return jnp.maximum(x, 0.0) + jnp.log1p(jnp.exp(-jnp.abs(x)))


def _rmsnorm_rows(x, w):
    ms = jnp.mean(x * x, axis=-1, keepdims=True)
    return x * lax.rsqrt(ms + EPS) * w


def _mem_kv_kernel(x_ref, nw_ref, w_ref, k_ref, v_ref, xn_ref):
    j = pl.program_id(1)

    @pl.when(j == 0)
    def _():
        xn = _rmsnorm_rows(x_ref[...], nw_ref[0]).astype(BF16)
        xn_ref[...] = xn
        k_ref[...] = _dot(xn, w_ref[0])

    @pl.when(j == 1)
    def _():
        v_ref[...] = _dot(xn_ref[...], w_ref[0])


def mem_kv(x, nw, w, layer, *, tm):
    m, k = x.shape
    n = w.shape[2] // 2
    out = jax.ShapeDtypeStruct((m, n), F32)
    return pl.pallas_call(
        _mem_kv_kernel,
        out_shape=(out, out),
        grid=(m // tm, 2),
        in_specs=[pl.BlockSpec((tm, k), lambda i, j: (i, 0)),
                  _layer_spec((1, k), layer),
                  pl.BlockSpec((1, k, n), lambda i, j: (layer, 0, j))],
        out_specs=(pl.BlockSpec((tm, n), lambda i, j: (i, 0)), pl.BlockSpec((tm, n), lambda i, j: (i, 0))),
        scratch_shapes=[pltpu.VMEM((tm, k), BF16)],
        compiler_params=_cp("parallel", "arbitrary"),
        name="mem_kv",
    )(x, nw, w)


def _attn_out_seq_kernel(o_ref, w_ref, res_ref, out_ref, a_ref, *, nb, lt):
    for t in range(lt):
        for h in range(XA_HEADS):
            a_ref[t * nb:(t + 1) * nb, h * XA_HEAD_DIM:(h + 1) * XA_HEAD_DIM] = o_ref[:, h * lt + t, :]
    out_ref[...] = res_ref[...] + _dot(a_ref[...].astype(BF16), w_ref[0])


def attn_out_seq(o, w, layer, res, *, nb, lt):
    m, n = res.shape
    return pl.pallas_call(
        functools.partial(_attn_out_seq_kernel, nb=nb, lt=lt),
        out_shape=jax.ShapeDtypeStruct((m, n), F32),
        grid=(1,),
        in_specs=[pl.BlockSpec(o.shape, lambda i: (0, 0, 0)),
                  _layer_spec(w.shape[1:], layer),
                  pl.BlockSpec((m, n), lambda i: (0, 0))],
        out_specs=pl.BlockSpec((m, n), lambda i: (0, 0)),
        scratch_shapes=[pltpu.VMEM((m, n), F32)],
        compiler_params=_cp("arbitrary"),
        name="attn_out_seq",
    )(o, w, res)


IN_SPLITS = (2 * D_A, 2 * D_A + D_B, 2 * D_A + D_B + D_HC)


def _in_proj_kernel(x_ref, nw_ref, w_ref, ha_ref, hb_ref, hc_ref):
    xn = _rmsnorm_rows(x_ref[...], nw_ref[0]).astype(BF16)
    ha_ref[...] = _dot(xn, w_ref[0, :, 0:IN_SPLITS[0]])
    hb_ref[...] = _dot(xn, w_ref[0, :, IN_SPLITS[0]:IN_SPLITS[1]])
    hc_ref[...] = _dot(xn, w_ref[0, :, IN_SPLITS[1]:IN_SPLITS[2]])


def in_proj(x, nw, w, layer, *, tm):
    m, k = x.shape
    widths = (IN_SPLITS[0], IN_SPLITS[1] - IN_SPLITS[0], IN_SPLITS[2] - IN_SPLITS[1])
    return pl.pallas_call(
        _in_proj_kernel,
        out_shape=tuple(jax.ShapeDtypeStruct((m, wd), F32) for wd in widths),
        grid=(m // tm,),
        in_specs=[pl.BlockSpec((tm, k), lambda i: (i, 0)),
                  _layer_spec((1, k), layer),
                  pl.BlockSpec((1, k, IN_SPLITS[2]), lambda i: (layer, 0, 0), pipeline_mode=pl.Buffered(1))],
        out_specs=tuple(pl.BlockSpec((tm, wd), lambda i: (i, 0)) for wd in widths),
        compiler_params=_cp("parallel"),
        name="in_proj",
    )(x, nw, w)


def _proj_res_kernel(*refs, n_in):
    a_refs = refs[:n_in]
    w_refs = refs[n_in:2 * n_in]
    res_ref, o_ref = refs[2 * n_in], refs[2 * n_in + 1]
    acc = res_ref[...]
    for a_ref, w_ref in zip(a_refs, w_refs):
        acc = acc + _dot(a_ref[...].astype(BF16), w_ref[0])
    o_ref[...] = acc


def proj_res(a_list, w, layer, res, *, tm, name):
    m, n = res.shape
    n_in = len(a_list)
    in_specs = [pl.BlockSpec((tm, a.shape[1]), lambda i: (i, 0)) for a in a_list]
    row0 = 0
    for a in a_list:
        kk = a.shape[1]
        assert row0 % kk == 0
        in_specs.append(pl.BlockSpec((1, kk, n), lambda i, blk=row0 // kk: (layer, blk, 0)))
        row0 += kk
    in_specs.append(pl.BlockSpec((tm, n), lambda i: (i, 0)))
    return pl.pallas_call(
        functools.partial(_proj_res_kernel, n_in=n_in),
        out_shape=jax.ShapeDtypeStruct((m, n), F32),
        grid=(m // tm,),
        in_specs=in_specs,
        out_specs=pl.BlockSpec((tm, n), lambda i: (i, 0)),
        compiler_params=_cp("parallel"),
        name=name,
    )(*a_list, *([w] * n_in), res)


def _out_q_kernel(ya_ref, yb_ref, yc_ref, w_ref, res_ref, nw_ref, wq_ref, x_ref, q_ref, *, q_by_seq):
    x = res_ref[...]
    row0 = 0
    for y_ref in (ya_ref, yb_ref, yc_ref):
        kk = y_ref.shape[1]
        x = x + _dot(y_ref[...], w_ref[0, row0:row0 + kk, :])
        row0 += kk
    x_ref[...] = x
    q = _dot(_rmsnorm_rows(x, nw_ref[0]).astype(BF16), wq_ref[0])
    if q_by_seq is None:
        q_ref[...] = q
    else:
        nb, lt = q_by_seq
        for t in range(lt):
            for h in range(XA_HEADS):
                q_ref[:, h * lt + t, :] = q[t * nb:(t + 1) * nb, h * XA_HEAD_DIM:(h + 1) * XA_HEAD_DIM]


def out_q_proj(ya, yb, yc, w_out, res, nw, wq, layer, *, tm, q_by_seq=None):
    m, n = res.shape
    resident = lambda a: pl.BlockSpec((1,) + a.shape[1:], lambda i: (layer, 0, 0), pipeline_mode=pl.Buffered(1))
    if q_by_seq is None:
        q_shape = (m, wq.shape[2])
        q_spec = pl.BlockSpec((tm, wq.shape[2]), lambda i: (i, 0))
    else:
        assert tm == m == q_by_seq[0] * q_by_seq[1]
        q_shape = (q_by_seq[0], XA_HEADS * q_by_seq[1], XA_HEAD_DIM)
        q_spec = pl.BlockSpec(q_shape, lambda i: (0, 0, 0))
    return pl.pallas_call(
        functools.partial(_out_q_kernel, q_by_seq=q_by_seq),
        out_shape=(jax.ShapeDtypeStruct((m, n), F32), jax.ShapeDtypeStruct(q_shape, F32)),
        grid=(m // tm,),
        in_specs=[pl.BlockSpec((tm, ya.shape[1]), lambda i: (i, 0)),
                  pl.BlockSpec((tm, yb.shape[1]), lambda i: (i, 0)),
                  pl.BlockSpec((tm, yc.shape[1]), lambda i: (i, 0)),
                  resident(w_out),
                  pl.BlockSpec((tm, n), lambda i: (i, 0)),
                  _layer_spec((1, n), layer),
                  resident(wq)],
        out_specs=(pl.BlockSpec((tm, n), lambda i: (i, 0)), q_spec),
        compiler_params=_cp("parallel"),
        name="out_q_proj",
    )(ya, yb, yc, w_out, res, nw, wq)


CONVA_ROW_CHUNK = 32


def _conva_kernel(*refs, nb, lt, n_tiles, has_state):
    refs = list(refs)
    cls_ref = refs.pop() if nb == 1 else None
    if has_state:
        h_ref, w_ref, b_ref, lnw_ref, lnb_ref, st_ref, y_ref, nst_ref, ext_ref = refs
    else:
        h_ref, w_ref, b_ref, lnw_ref, lnb_ref, y_ref, nst_ref, ext_ref = refs
    hist = (CONV_A_WIDTH - 1) * nb
    pad = _round_up(hist, SUBLANE)
    rows = lt * nb
    j = pl.program_id(1)

    @pl.when(j == 0)
    def _():
        if has_state:
            ext_ref[pad - hist:pad, :] = st_ref[0]
        else:
            ext_ref[0:pad, :] = jnp.zeros((pad, D_A), F32)

    ext_ref[pad:pad + rows, :] = h_ref[:, 0:D_A] * _sigmoid(h_ref[:, D_A:2 * D_A])

    bias = b_ref[0]
    lnw = lnw_ref[0]
    lnb = lnb_ref[0]
    rc = CONVA_ROW_CHUNK
    if nb == 1:
        for s in range(SUBLANE):
            span = rows + SUBLANE * ((CONV_A_WIDTH - 1 - s) // SUBLANE)
            cls_ref[s, 0:span, :] = ext_ref[pl.ds(pad - hist + s, span), :]
    for r0 in range(0, rows, rc):
        acc = jnp.zeros((rc, D_A), F32) + bias
        if nb == 1:
            for k in range(CONV_A_WIDTH):
                s, jt = k % SUBLANE, k // SUBLANE
                acc = acc + w_ref[0, k:k + 1, :] * cls_ref[s, r0 + SUBLANE * jt:r0 + SUBLANE * jt + rc, :]
        else:
            for k in range(CONV_A_WIDTH):
                acc = acc + w_ref[0, k:k + 1, :] * ext_ref[pl.ds(pad - hist + k * nb + r0, rc), :]
        mu = jnp.mean(acc, axis=-1, keepdims=True)
        xc = acc - mu
        var = jnp.mean(xc * xc, axis=-1, keepdims=True)
        c = xc * lax.rsqrt(var + EPS) * lnw + lnb
        y_ref[r0:r0 + rc, :] = _silu(c).astype(y_ref.dtype)

    new_hist = ext_ref[pl.ds(pad + rows - hist, hist), :]
    nst_ref[0] = new_hist
    if n_tiles > 1:
        ext_ref[pad - hist:pad, :] = new_hist


def conva_mixer(h_a, p, layer, state, *, n_seq, nb, lt, n_tiles):
    rows = lt * nb
    hist = (CONV_A_WIDTH - 1) * nb
    pad = _round_up(hist, SUBLANE)
    has_state = state is not None
    in_specs = [pl.BlockSpec((rows, 2 * D_A), lambda s, j: (s * n_tiles + j, 0)),
                _layer_spec((CONV_A_WIDTH, D_A), layer),
                _layer_spec((1, D_A), layer), _layer_spec((1, D_A), layer), _layer_spec((1, D_A), layer)]
    args = [h_a, p['conv_a_w'], p['conv_a_b'], p['ln_a_w'], p['ln_a_b']]
    if has_state:
        assert n_seq == 1
        in_specs.append(_layer_spec((hist, D_A), layer))
        args.append(state)
    return pl.pallas_call(
        functools.partial(_conva_kernel, nb=nb, lt=lt, n_tiles=n_tiles, has_state=has_state),
        out_shape=(jax.ShapeDtypeStruct((h_a.shape[0], D_A), BF16),
                   jax.ShapeDtypeStruct((n_seq, hist, D_A), F32)),
        grid=(n_seq, n_tiles),
        in_specs=in_specs,
        out_specs=(pl.BlockSpec((rows, D_A), lambda s, j: (s * n_tiles + j, 0)),
                   pl.BlockSpec((1, hist, D_A), lambda s, j: (s, 0, 0))),
        scratch_shapes=[pltpu.VMEM((pad + rows, D_A), F32)]
        + ([pltpu.VMEM((SUBLANE, rows + SUBLANE * ((CONV_A_WIDTH - 1) // SUBLANE), D_A), F32)] if nb == 1 else []),
        compiler_params=_cp("parallel", "arbitrary"),
        name="conva_mixer",
    )(*args)


def _gelu_tanh(x):
    return x * (0.5 * (1.0 + jnp.tanh(math.sqrt(2.0 / math.pi) * (x + 0.044715 * (x * x * x)))))


S5_BLOCKS = S5_LANES // LANE
S5_SUPER = 2
S5_SUP_CH = D_B // S5_SUPER
S5_SUP_ST = S5_LANES // S5_SUPER


def _s5_b_proj(u, bb_ref, hs_ref):
    n = S5_LANES
    for sb in range(S5_SUPER):
        bu = _dot(u[:, sb * S5_SUP_CH:(sb + 1) * S5_SUP_CH].astype(BF16), bb_ref[0, sb])
        hs_ref[:, sb * S5_SUP_ST:(sb + 1) * S5_SUP_ST] = bu[:, 0:S5_SUP_ST]
        hs_ref[:, n + sb * S5_SUP_ST:n + (sb + 1) * S5_SUP_ST] = bu[:, S5_SUP_ST:2 * S5_SUP_ST]


def _s5_glu_out(hs_ref, u, cc_ref, d_ref, gw_ref, gb_ref, y_ref):
    n = S5_LANES
    ys = []
    for sb in range(S5_SUPER):
        h16 = jnp.concatenate([hs_ref[:, sb * S5_SUP_ST:(sb + 1) * S5_SUP_ST],
                               hs_ref[:, n + sb * S5_SUP_ST:n + (sb + 1) * S5_SUP_ST]], axis=1).astype(BF16)
        ys.append(_dot(h16, cc_ref[0, sb]))
    y = jnp.concatenate(ys, axis=1) + d_ref[0] * u
    y = _gelu_tanh(y)
    gate = _dot(y.astype(BF16), gw_ref[0]) + gb_ref[0]
    y_ref[...] = (y * _sigmoid(gate)).astype(y_ref.dtype)


def _s5_seq_kernel(u_ref, bb_ref, tab_ref, cc_ref, d_ref, gw_ref, gb_ref,
                   y_ref, nre_ref, nim_ref, hs_ref, cre_ref, cim_ref, *, lt):
    n = S5_LANES
    j = pl.program_id(1)

    @pl.when(j == 0)
    def _():
        cre_ref[...] = jnp.zeros(cre_ref.shape, F32)
        cim_ref[...] = jnp.zeros(cim_ref.shape, F32)

    u = u_ref[...]
    _s5_b_proj(u, bb_ref, hs_ref)

    def group(i, carry):
        r0 = pl.multiple_of(i * SUBLANE, SUBLANE)
        new = []
        for c in range(S5_BLOCKS):
            lr = slice(c * LANE, (c + 1) * LANE)
            li = slice(n + c * LANE, n + (c + 1) * LANE)
            xr = hs_ref[pl.ds(r0, SUBLANE), lr]
            xi = hs_ref[pl.ds(r0, SUBLANE), li]
            for lev in range(3):
                ar = tab_ref[0, lev, :, lr]
                ai = tab_ref[0, lev, :, li]
                sr = pltpu.roll(xr, 1 << lev, 0)
                si = pltpu.roll(xi, 1 << lev, 0)
                xr, xi = xr + ar * sr - ai * si, xi + ar * si + ai * sr
            pr = tab_ref[0, 3, :, lr]
            pi = tab_ref[0, 3, :, li]
            er, ei = carry[2 * c], carry[2 * c + 1]
            hr = xr + pr * er - pi * ei
            hi = xi + pr * ei + pi * er
            hs_ref[pl.ds(r0, SUBLANE), lr] = hr
            hs_ref[pl.ds(r0, SUBLANE), li] = hi
            new += [jnp.broadcast_to(hr[SUBLANE - 1:SUBLANE, :], (SUBLANE, LANE)),
                    jnp.broadcast_to(hi[SUBLANE - 1:SUBLANE, :], (SUBLANE, LANE))]
        return tuple(new)

    init = []
    for c in range(S5_BLOCKS):
        init += [cre_ref[:, c * LANE:(c + 1) * LANE], cim_ref[:, c * LANE:(c + 1) * LANE]]
    last = lax.fori_loop(0, lt // SUBLANE, group, tuple(init))
    for c in range(S5_BLOCKS):
        cre_ref[:, c * LANE:(c + 1) * LANE] = last[2 * c]
        cim_ref[:, c * LANE:(c + 1) * LANE] = last[2 * c + 1]
    nre_ref[0] = cre_ref[0:1, :]
    nim_ref[0] = cim_ref[0:1, :]
    _s5_glu_out(hs_ref, u, cc_ref, d_ref, gw_ref, gb_ref, y_ref)


def _s5_step_kernel(u_ref, bb_ref, ab_ref, cc_ref, d_ref, gw_ref, gb_ref, sre_ref, sim_ref,
                    y_ref, nre_ref, nim_ref, hs_ref, *, nb, lt):
    n = S5_LANES
    nblk = S5_BLOCKS
    u = u_ref[...]
    _s5_b_proj(u, bb_ref, hs_ref)
    ab_re = jnp.concatenate([ab_ref[0, c:c + 1, :] for c in range(nblk)], axis=1)
    ab_im = jnp.concatenate([ab_ref[0, nblk + c:nblk + c + 1, :] for c in range(nblk)], axis=1)
    hr = sre_ref[0]
    hi = sim_ref[0]
    for t in range(lt):
        rs = slice(t * nb, (t + 1) * nb)
        nr = ab_re * hr - ab_im * hi + hs_ref[rs, 0:n]
        ni = ab_re * hi + ab_im * hr + hs_ref[rs, n:2 * n]
        hr, hi = nr, ni
        hs_ref[rs, 0:n] = hr
        hs_ref[rs, n:2 * n] = hi
    nre_ref[0] = hr
    nim_ref[0] = hi
    _s5_glu_out(hs_ref, u, cc_ref, d_ref, gw_ref, gb_ref, y_ref)


def s5_mixer(h_b, p, layer, s_re, s_im, *, n_seq, nb, lt, n_tiles):
    rows = lt * nb
    n = S5_LANES
    has_state = s_re is not None
    in_specs = [pl.BlockSpec((rows, D_B), lambda s, j: (s * n_tiles + j, 0)),
                _layer_spec((S5_SUPER, S5_SUP_CH, 2 * S5_SUP_ST), layer),
                _layer_spec((2 * S5_BLOCKS, LANE), layer)]
    args = [h_b, p['s5_bb'], p['s5_ab']]
    if not has_state:
        in_specs[2] = _layer_spec((4, SUBLANE, 2 * n), layer)
        args[2] = p['s5_tab']
    in_specs += [_layer_spec((S5_SUPER, 2 * S5_SUP_ST, S5_SUP_CH), layer),
                 _layer_spec((1, D_B), layer),
                 _layer_spec((D_B, D_B), layer),
                 _layer_spec((1, D_B), layer)]
    args += [p['s5_cc'], p['s5_d'], p['s5_glu_w'], p['s5_glu_b']]
    if has_state:
        assert n_seq == 1 and n_tiles == 1
        in_specs += [_layer_spec((nb, n), layer)] * 2
        args += [s_re, s_im]
        body = functools.partial(_s5_step_kernel, nb=nb, lt=lt)
        scratch = [pltpu.VMEM((rows, 2 * n), F32)]
    else:
        assert nb == 1 and lt % SUBLANE == 0
        body = functools.partial(_s5_seq_kernel, lt=lt)
        scratch = [pltpu.VMEM((rows, 2 * n), F32),
                   pltpu.VMEM((SUBLANE, n), F32),
                   pltpu.VMEM((SUBLANE, n), F32)]
    st_spec = pl.BlockSpec((1, nb, n), lambda s, j: (s, 0, 0))
    return pl.pallas_call(
        body,
        out_shape=(jax.ShapeDtypeStruct((h_b.shape[0], D_B), BF16),
                   jax.ShapeDtypeStruct((n_seq, nb, n), F32),
                   jax.ShapeDtypeStruct((n_seq, nb, n), F32)),
        grid=(n_seq, n_tiles),
        in_specs=in_specs,
        out_specs=(pl.BlockSpec((rows, D_B), lambda s, j: (s * n_tiles + j, 0)), st_spec, st_spec),
        scratch_shapes=scratch,
        compiler_params=_cp("parallel", "arbitrary"),
        name="s5_mixer",
    )(*args)


def _group_rmsnorm(y, nw):
    half = D_C // SSD_GROUPS
    outs = []
    for g in range(SSD_GROUPS):
        yg = y[:, g * half:(g + 1) * half]
        outs.append(yg * lax.rsqrt(jnp.mean(yg * yg, axis=-1, keepdims=True) + EPS))
    return jnp.concatenate(outs, axis=1) * nw


def _mamba_p_kernel(h_ref, cw_ref, cb_ref, dtb_ref, alog_ref, dexp_ref, nw_ref, e_ref, tril_ref,
                    y_ref, ncst_ref, nsst_ref, ext_ref, st_ref, *, lt, n_tiles):
    q = SSD_CHUNK
    hist = SSD_CONV_WIDTH - 1
    pad = SUBLANE
    half = D_C // SSD_GROUPS
    hpg = SSD_HEADS // SSD_GROUPS
    j = pl.program_id(1)

    @pl.when(j == 0)
    def _():
        ext_ref[0:pad, :] = jnp.zeros((pad, D_XBC), F32)
        st_ref[...] = jnp.zeros(st_ref.shape, F32)

    ext_ref[pad:pad + lt, :] = h_ref[:, D_C:D_C + D_XBC]

    e = e_ref[...]
    tril = tril_ref[...]
    a_neg = -jnp.exp(alog_ref[0])
    li = lax.broadcasted_iota(jnp.int32, (q, q), 0)
    si = lax.broadcasted_iota(jnp.int32, (q, q), 1)
    causal = li >= si
    lane = lax.broadcasted_iota(jnp.int32, (q, LANE), 1)

    for c in range(lt // q):
        r0 = c * q
        acc = jnp.zeros((q, D_XBC), F32) + cb_ref[0]
        for k in range(SSD_CONV_WIDTH):
            acc = acc + cw_ref[0, k:k + 1, :] * ext_ref[pl.ds(pad - hist + k + r0, q), :]
        xc = _silu(acc)
        xs = xc[:, 0:D_C]
        z = h_ref[r0:r0 + q, 0:D_C]
        dt = _softplus(h_ref[r0:r0 + q, D_C + D_XBC:D_C + D_XBC + LANE] + dtb_ref[0])
        a = dt * a_neg
        hi_, mid_, lo_ = _split3(a)
        cs = _dot(tril, hi_) + _dot(tril, mid_) + _dot(tril, lo_)
        cs_last = cs[q - 1:q, :]
        dt_x = _expand(dt, e)
        ecs_x = _expand(jnp.exp(cs), e)
        edl_x = _expand(jnp.exp(cs_last - cs), e)
        xdt = xs * dt_x
        cs_t = cs.T

        y_parts = []
        for g in range(SSD_GROUPS):
            bm = xc[:, D_C + g * SSD_STATE:D_C + (g + 1) * SSD_STATE]
            cm = xc[:, D_C + SSD_GROUPS * SSD_STATE + g * SSD_STATE:
                    D_C + SSD_GROUPS * SSD_STATE + (g + 1) * SSD_STATE]
            bm16 = bm.astype(BF16)
            cm16 = cm.astype(BF16)
            cb = _dot_nt(cm16, bm16)
            for pr in range(hpg // 2):
                r_even = g * hpg + 2 * pr
                xpair = xdt[:, r_even * SSD_HEAD_DIM:(r_even + 2) * SSD_HEAD_DIM].astype(BF16)
                ys = []
                for r in (r_even, r_even + 1):
                    seg = cs[:, r:r + 1] - cs_t[r:r + 1, :]
                    dec = jnp.exp(jnp.where(causal, seg, -jnp.inf))
                    ys.append(_dot((cb * dec).astype(BF16), xpair))
                y_parts.append(jnp.where(lane < SSD_HEAD_DIM, ys[0], ys[1]))
        y_diag = jnp.concatenate(y_parts, axis=1)
        y_off = jnp.concatenate(
            [_dot(xc[:, D_C + SSD_GROUPS * SSD_STATE + g * SSD_STATE:
                      D_C + SSD_GROUPS * SSD_STATE + (g + 1) * SSD_STATE].astype(BF16),
                  st_ref[:, g * half:(g + 1) * half].astype(BF16)) for g in range(SSD_GROUPS)],
            axis=1) * ecs_x
        y = y_diag + y_off + dexp_ref[0] * xs
        y = y * _silu(z)
        y_ref[r0:r0 + q, :] = _group_rmsnorm(y, nw_ref[0]).astype(y_ref.dtype)

        xw = (xdt * edl_x).astype(BF16)
        dec_row = ecs_x[q - 1:q, :]
        for g in range(SSD_GROUPS):
            bm_t = xc[:, D_C + g * SSD_STATE:D_C + (g + 1) * SSD_STATE].T.astype(BF16)
            upd = _dot(bm_t, xw[:, g * half:(g + 1) * half])
            st_ref[:, g * half:(g + 1) * half] = (
                st_ref[:, g * half:(g + 1) * half] * dec_row[:, g * half:(g + 1) * half] + upd)

    new_hist = ext_ref[pl.ds(pad + lt - hist, hist), :]
    ncst_ref[0] = new_hist
    if n_tiles > 1:
        ext_ref[pad - hist:pad, :] = new_hist

    @pl.when(j == n_tiles - 1)
    def _():
        for blk in range(D_C // LANE):
            nsst_ref[0, blk * LANE:(blk + 1) * LANE, :] = st_ref[:, blk * LANE:(blk + 1) * LANE].T


def _ssd_consts():
    head_of_lane = jnp.arange(D_C) // SSD_HEAD_DIM
    e = (jnp.arange(LANE)[:, None] == head_of_lane[None, :]).astype(BF16)
    tril = (jnp.arange(SSD_CHUNK)[:, None] >= jnp.arange(SSD_CHUNK)[None, :]).astype(BF16)
    return e, tril


def _ssd_param_specs(layer):
    return [_layer_spec((SSD_CONV_WIDTH, D_XBC), layer),
            _layer_spec((1, D_XBC), layer),
            _layer_spec((1, LANE), layer),
            _layer_spec((1, LANE), layer),
            _layer_spec((1, D_C), layer),
            _layer_spec((1, D_C), layer)]


def _ssd_param_args(p):
    return [p['conv_c_w'], p['conv_c_b'], p['ssd_dt_bias'], p['ssd_a_log'], p['ssd_d'], p['ssd_norm_w']]


def mamba_prompt(h_c, p, layer, *, n_seq, lt, n_tiles):
    hist = SSD_CONV_WIDTH - 1
    const = lambda s, j: (0, 0)
    return pl.pallas_call(
        functools.partial(_mamba_p_kernel, lt=lt, n_tiles=n_tiles),
        out_shape=(jax.ShapeDtypeStruct((h_c.shape[0], D_C), BF16),
                   jax.ShapeDtypeStruct((n_seq, hist, D_XBC), F32),
                   jax.ShapeDtypeStruct((n_seq, D_C, SSD_STATE), F32)),
        grid=(n_seq, n_tiles),
        in_specs=[pl.BlockSpec((lt, D_HC), lambda s, j: (s * n_tiles + j, 0))]
        + _ssd_param_specs(layer)
        + [pl.BlockSpec((LANE, D_C), const), pl.BlockSpec((SSD_CHUNK, SSD_CHUNK), const)],
        out_specs=(pl.BlockSpec((lt, D_C), lambda s, j: (s * n_tiles + j, 0)),
                   pl.BlockSpec((1, hist, D_XBC), lambda s, j: (s, 0, 0)),
                   pl.BlockSpec((1, D_C, SSD_STATE), lambda s, j: (s, 0, 0))),
        scratch_shapes=[pltpu.VMEM((SUBLANE + lt, D_XBC), F32),
                        pltpu.VMEM((SSD_STATE, D_C), F32)],
        compiler_params=_cp("parallel", "arbitrary"),
        name="mamba_prompt",
    )(h_c, *_ssd_param_args(p), p['ssd_e'], p['ssd_tril'])


def _ks(c, k, nb):
    return slice((c * SUBLANE + k) * nb, (c * SUBLANE + k + 1) * nb)


def _slab_put(ref, k, slab, nb):
    for c in range(slab.shape[1] // LANE):
        ref[_ks(c, k, nb), :] = slab[:, c * LANE:(c + 1) * LANE]


def _slab_get(ref, k, n_blocks, nb):
    return jnp.concatenate([ref[_ks(c, k, nb), :] for c in range(n_blocks)], axis=1)


def _seq_get(ref, b, n_blocks, nb):
    return jnp.concatenate(
        [ref[pl.ds(c * SUBLANE * nb + b, SUBLANE, stride=nb), :] for c in range(n_blocks)], axis=1)


def _seq_put(ref, b, val, nb, c0=0):
    for c in range(val.shape[1] // LANE):
        ref[pl.ds((c0 + c) * SUBLANE * nb + b, SUBLANE, stride=nb), :] = val[:, c * LANE:(c + 1) * LANE]


def _mamba_s_kernel(h_ref, cw_ref, cb_ref, dtb_ref, alog_ref, dexp_ref, nw_ref, e_ref, cst_ref, sst_ref,
                    y_ref, ncst_ref, nsst_ref,
                    ext_ref, xs_ref, dt_ref, cs_ref, lhs_ref, rhs_ref, c8_ref, yoff_ref,
                    *, nb, lt, bb, lsel, passthrough):
    hist = (SSD_CONV_WIDTH - 1) * nb
    rows = lt * nb
    half = D_C // SSD_GROUPS
    hpg = SSD_HEADS // SSD_GROUPS
    xblk = D_C // LANE
    hblk = half // LANE
    i = pl.program_id(0)
    n_steps = pl.num_programs(0)
    bc_off = D_C
    cc_off = D_C + SSD_GROUPS * SSD_STATE

    @pl.when(i == 0)
    def _phase1():
        e = e_ref[...]
        ext_ref[0:hist, :] = cst_ref[0]
        ext_ref[hist:hist + rows, :] = h_ref[:, D_C:D_C + D_XBC]
        ncst_ref[...] = ext_ref[rows:rows + hist, :]
        a_neg = -jnp.exp(alog_ref[0])
        lhs_ref[...] = jnp.zeros(lhs_ref.shape, F32)
        rhs_ref[...] = jnp.zeros(rhs_ref.shape, F32)
        c8_ref[...] = jnp.zeros(c8_ref.shape, F32)
        cs = jnp.zeros((nb, LANE), F32)
        for t in range(lt):
            rs = slice(t * nb, (t + 1) * nb)
            acc = jnp.zeros((nb, D_XBC), F32) + cb_ref[0]
            for k in range(SSD_CONV_WIDTH):
                acc = acc + cw_ref[0, k:k + 1, :] * ext_ref[(t + k) * nb:(t + k + 1) * nb, :]
            xc = _silu(acc)
            xs_ref[rs, :] = xc[:, 0:D_C]
            for g in range(SSD_GROUPS):
                rhs_ref[_ks(2 * g, t, nb), :] = xc[:, bc_off + g * SSD_STATE:bc_off + (g + 1) * SSD_STATE]
            _slab_put(c8_ref, t, xc[:, cc_off:cc_off + SSD_GROUPS * SSD_STATE], nb)
            dt = _softplus(h_ref[rs, D_C + D_XBC:D_C + D_XBC + LANE] + dtb_ref[0])
            dt_ref[rs, :] = dt
            cs = cs + dt * a_neg
            cs_ref[rs, :] = cs
        cs_last = cs
        for t in range(lt):
            rs = slice(t * nb, (t + 1) * nb)
            wt = jnp.exp(cs_last - cs_ref[rs, :]) * dt_ref[rs, :]
            _slab_put(lhs_ref, t, xs_ref[rs, :] * _expand(wt, e), nb)
        dec = _expand(jnp.exp(cs_last), e)
        d_hi = dec.astype(BF16).astype(F32)
        d_r = dec - d_hi
        d_mid = d_r.astype(BF16).astype(F32)
        d_lo = d_r - d_mid
        ones = jnp.ones((nb, SSD_STATE), F32)
        for k, piece in enumerate((d_hi, d_mid, d_lo)):
            _slab_put(lhs_ref, lt + k, piece, nb)
            for g in range(SSD_GROUPS):
                rhs_ref[_ks(2 * g + 1, lt + k, nb), :] = ones

    for jb in range(bb):
        b = i * bb + jb
        l8 = _seq_get(lhs_ref, b, xblk, nb).astype(BF16)
        r8 = _seq_get(rhs_ref, b, 2 * SSD_GROUPS, nb).astype(BF16)
        c8 = _seq_get(c8_ref, b, SSD_GROUPS, nb).astype(BF16)
        for g in range(SSD_GROUPS):
            s = sst_ref[lsel, jb, g * half:(g + 1) * half, :]
            yo = _dot_nt(c8[:, g * SSD_STATE:(g + 1) * SSD_STATE], s.astype(BF16))
            _seq_put(yoff_ref, b, yo, nb, c0=g * hblk)
            upd = _dot_tn(l8[:, g * half:(g + 1) * half],
                          r8[:, g * 2 * SSD_STATE:(g + 1) * 2 * SSD_STATE])
            nsst_ref[lsel, jb, g * half:(g + 1) * half, :] = upd[:, SSD_STATE:] * s + upd[:, :SSD_STATE]
    for d in passthrough:
        nsst_ref[d] = sst_ref[d]

    @pl.when(i == n_steps - 1)
    def _phase3():
        e = e_ref[...]
        lane = lax.broadcasted_iota(jnp.int32, (nb, LANE), 1)
        for t in range(lt):
            rt = slice(t * nb, (t + 1) * nb)
            cs_t = cs_ref[rt, :]
            y = (_slab_get(yoff_ref, t, xblk, nb) * _expand(jnp.exp(cs_t), e)
                 + dexp_ref[0] * xs_ref[rt, :])
            for s_ in range(t + 1):
                rsl = slice(s_ * nb, (s_ + 1) * nb)
                cbs = []
                for g in range(SSD_GROUPS):
                    cm = c8_ref[_ks(g, t, nb), :]
                    bm = rhs_ref[_ks(2 * g, s_, nb), :]
                    cbs.append(jnp.sum(cm * bm, axis=-1, keepdims=True))
                cb = jnp.where(lane < hpg, cbs[0], cbs[1])
                m = jnp.exp(cs_t - cs_ref[rsl, :]) * dt_ref[rsl, :] * cb
                y = y + _expand(m, e) * xs_ref[rsl, :]
            y = y * _silu(h_ref[rt, 0:D_C])
            y_ref[rt, :] = _group_rmsnorm(y, nw_ref[0]).astype(y_ref.dtype)


def mamba_sample(h_c, p, layer, cst, sst, *, nb, lt, bb, in_place):
    rows = lt * nb
    hist = (SSD_CONV_WIDTH - 1) * nb
    depth = sst.shape[0]
    const = lambda i: (0, 0)
    if in_place:
        sst_spec = pl.BlockSpec((1, bb, D_C, SSD_STATE), lambda i: (layer, i, 0, 0))
        lsel, passthrough = 0, ()
    else:
        sst_spec = pl.BlockSpec((depth, bb, D_C, SSD_STATE), lambda i: (0, i, 0, 0))
        lsel, passthrough = layer, tuple(d for d in range(depth) if d != layer)
    in_specs = ([pl.BlockSpec((rows, D_HC), const)] + _ssd_param_specs(layer)
                + [pl.BlockSpec((LANE, D_C), const), _layer_spec((hist, D_XBC), layer), sst_spec])
    return pl.pallas_call(
        functools.partial(_mamba_s_kernel, nb=nb, lt=lt, bb=bb, lsel=lsel, passthrough=passthrough),
        out_shape=(jax.ShapeDtypeStruct((rows, D_C), BF16),
                   jax.ShapeDtypeStruct((hist, D_XBC), F32),
                   jax.ShapeDtypeStruct(sst.shape, F32)),
        grid=(nb // bb,),
        in_specs=in_specs,
        out_specs=(pl.BlockSpec((rows, D_C), const),
                   pl.BlockSpec((hist, D_XBC), const),
                   sst_spec),
        input_output_aliases={len(in_specs) - 1: 2} if in_place else {},
        scratch_shapes=[pltpu.VMEM((hist + rows, D_XBC), F32),
                        pltpu.VMEM((rows, D_C), F32),
                        pltpu.VMEM((rows, LANE), F32),
                        pltpu.VMEM((rows, LANE), F32),
                        pltpu.VMEM((D_C // LANE * SUBLANE * nb, LANE), F32),
                        pltpu.VMEM((2 * SSD_GROUPS * SUBLANE * nb, LANE), F32),
                        pltpu.VMEM((SSD_GROUPS * SUBLANE * nb, LANE), F32),
                        pltpu.VMEM((D_C // LANE * SUBLANE * nb, LANE), F32)],
        compiler_params=_cp("arbitrary"),
        name="mamba_sample",
    )(h_c, *_ssd_param_args(p), p['ssd_e'], cst, sst)


def _softmax_rows(s):
    m = jnp.max(s, axis=-1, keepdims=True)
    ex = jnp.exp(s - m)
    return ex / jnp.sum(ex, axis=-1, keepdims=True)


def _attn_p_kernel(q_ref, k_ref, v_ref, o_ref):
    for h in range(XA_HEADS):
        hs = slice(h * XA_HEAD_DIM, (h + 1) * XA_HEAD_DIM)
        s = _dot_nt(q_ref[:, hs].astype(BF16), k_ref[0, :, hs].astype(BF16)) / math.sqrt(XA_HEAD_DIM)
        p = _softmax_rows(s)
        o_ref[:, hs] = _dot(p.astype(BF16), v_ref[0, :, hs].astype(BF16)).astype(o_ref.dtype)


def attn_prompt(q, k, v, *, n_seq, seq, tq):
    n_tiles = seq // tq
    return pl.pallas_call(
        _attn_p_kernel,
        out_shape=jax.ShapeDtypeStruct(q.shape, BF16),
        grid=(n_seq, n_tiles),
        in_specs=[pl.BlockSpec((tq, D_MODEL), lambda s, j: (s * n_tiles + j, 0)),
                  pl.BlockSpec((1, N_MEM, D_MODEL), lambda s, j: (s, 0, 0)),
                  pl.BlockSpec((1, N_MEM, D_MODEL), lambda s, j: (s, 0, 0))],
        out_specs=pl.BlockSpec((tq, D_MODEL), lambda s, j: (s * n_tiles + j, 0)),
        compiler_params=_cp("parallel", "arbitrary"),
        name="attn_prompt",
    )(q, k, v)


def _attn_s_kernel(q_ref, k_ref, v_ref, o_ref, *, bb, lt):
    rows = XA_HEADS * lt
    n = N_MEM * XA_HEADS
    col_head = lax.broadcasted_iota(jnp.int32, (rows, n), 1) % XA_HEADS
    row_head = lax.broadcasted_iota(jnp.int32, (rows, n), 0) // lt
    same_head = col_head == row_head
    for jb in range(bb):
        k = k_ref[0, jb].reshape(n, XA_HEAD_DIM).astype(BF16)
        v = v_ref[0, jb].reshape(n, XA_HEAD_DIM).astype(BF16)
        s = _dot_nt(q_ref[jb].astype(BF16), k) / math.sqrt(XA_HEAD_DIM)
        p = _softmax_rows(jnp.where(same_head, s, -jnp.inf))
        o_ref[jb] = _dot(p.astype(BF16), v)


def attn_sample(q, k, v, layer, *, bb):
    nb, rows, _ = q.shape
    kv_spec = pl.BlockSpec((1, bb, N_MEM, XA_HEADS, XA_HEAD_DIM), lambda i: (layer, i, 0, 0, 0))
    return pl.pallas_call(
        functools.partial(_attn_s_kernel, bb=bb, lt=rows // XA_HEADS),
        out_shape=jax.ShapeDtypeStruct((nb, rows, XA_HEAD_DIM), F32),
        grid=(nb // bb,),
        in_specs=[pl.BlockSpec((bb, rows, XA_HEAD_DIM), lambda i: (i, 0, 0)), kv_spec, kv_spec],
        out_specs=pl.BlockSpec((bb, rows, XA_HEAD_DIM), lambda i: (i, 0, 0)),
        compiler_params=_cp("parallel"),
        name="attn_sample",
    )(q, k, v)


def _ffn_kernel(*refs, nb, tiles_per_seq, has_state, final_norm):
    refs = list(refs)
    x_ref, nw_ref, wg_ref, wu_ref, cw_ref, cb_ref, wd_ref = refs[:7]
    pos = 7
    st_ref = None
    if has_state:
        st_ref = refs[pos]
        pos += 1
    fw_ref = None
    if final_norm:
        fw_ref = refs[pos]
        pos += 1
    o_ref, nst_ref, xn_ref, gext_ref, carry_ref = refs[pos:pos + 5]

    hist = (FFN_CONV_WIDTH - 1) * nb
    pad = _round_up(hist, SUBLANE)
    tm = x_ref.shape[0]
    tf = wg_ref.shape[3]
    rsz = min(FFN_ROW_SPLIT, tm)
    i = pl.program_id(0)
    f = pl.program_id(1)
    n_f = pl.num_programs(1)

    @pl.when(f == 0)
    def _():
        x = x_ref[...]
        xn_ref[...] = _rmsnorm_rows(x, nw_ref[0]).astype(BF16)
        o_ref[...] = x

    if tiles_per_seq > 1:
        first = (i % tiles_per_seq) == 0

        @pl.when(first)
        def _():
            if has_state:
                gext_ref[pad - hist:pad, :] = st_ref[0]
            else:
                gext_ref[0:pad, :] = jnp.zeros((pad, tf), F32)

        @pl.when(jnp.logical_not(first))
        def _():
            gext_ref[0:pad, :] = carry_ref[f]
    else:
        if has_state:
            gext_ref[pad - hist:pad, :] = st_ref[0]
        else:
            gext_ref[0:pad, :] = jnp.zeros((pad, tf), F32)

    cw0 = cw_ref[0, 0:1, :]
    cw1 = cw_ref[0, 1:2, :]
    cw2 = cw_ref[0, 2:3, :]
    cb = cb_ref[0]
    for r0 in range(0, tm, rsz):
        xn = xn_ref[r0:r0 + rsz, :]
        g = _dot(xn, wg_ref[0, 0])
        up = _dot(xn, wu_ref[0, 0])
        gext_ref[pad + r0:pad + r0 + rsz, :] = g
        conv = (cw0 * gext_ref[pl.ds(pad - 2 * nb + r0, rsz), :]
                + cw1 * gext_ref[pl.ds(pad - nb + r0, rsz), :]
                + cw2 * g + cb)
        act = _silu(conv) * up
        o_ref[r0:r0 + rsz, :] += _dot(act.astype(BF16), wd_ref[0])

    nst_ref[0] = gext_ref[pl.ds(pad + tm - hist, hist), :]
    if tiles_per_seq > 1:
        carry_ref[f] = gext_ref[pl.ds(tm, pad), :]

    if final_norm:
        @pl.when(f == n_f - 1)
        def _():
            o_ref[...] = _rmsnorm_rows(o_ref[...], fw_ref[...])


def conv_ffn(x, p, layer, state, final_w, *, n_seq, nb, tm, tiles_per_seq):
    m = x.shape[0]
    tf = FF_TILE
    n_f = D_FF_PAD // tf
    hist = (FFN_CONV_WIDTH - 1) * nb
    pad = _round_up(hist, SUBLANE)
    has_state = state is not None
    final_norm = final_w is not None
    in_specs = [pl.BlockSpec((tm, D_MODEL), lambda i, f: (i, 0)),
                _layer_spec((1, D_MODEL), layer),
                pl.BlockSpec((1, 1, D_MODEL, tf), lambda i, f: (layer, f, 0, 0)),
                pl.BlockSpec((1, 1, D_MODEL, tf), lambda i, f: (layer, f, 0, 0)),
                pl.BlockSpec((1, FFN_CONV_WIDTH, tf), lambda i, f: (layer, 0, f)),
                pl.BlockSpec((1, 1, tf), lambda i, f: (layer, 0, f)),
                pl.BlockSpec((1, tf, D_MODEL), lambda i, f: (layer, f, 0))]
    args = [x, p['norm_ffn_w'], p['ffn_wg'], p['ffn_wu'], p['ffn_conv_w'], p['ffn_conv_b'], p['ffn_wd']]
    if has_state:
        assert n_seq == 1 and tiles_per_seq == 1
        in_specs.append(pl.BlockSpec((1, hist, tf), lambda i, f: (layer, 0, f)))
        args.append(state)
    if final_norm:
        in_specs.append(pl.BlockSpec((1, D_MODEL), lambda i, f: (0, 0)))
        args.append(final_w.reshape(1, D_MODEL))
    return pl.pallas_call(
        functools.partial(_ffn_kernel, nb=nb, tiles_per_seq=tiles_per_seq, has_state=has_state,
                          final_norm=final_norm),
        out_shape=(jax.ShapeDtypeStruct((m, D_MODEL), F32),
                   jax.ShapeDtypeStruct((m // tm, hist, D_FF_PAD), F32)),
        grid=(m // tm, n_f),
        in_specs=in_specs,
        out_specs=(pl.BlockSpec((tm, D_MODEL), lambda i, f: (i, 0)),
                   pl.BlockSpec((1, hist, tf), lambda i, f: (i, 0, f))),
        scratch_shapes=[pltpu.VMEM((tm, D_MODEL), BF16),
                        pltpu.VMEM((pad + tm, tf), F32),
                        pltpu.VMEM((n_f, pad, tf), F32)],
        compiler_params=_cp("arbitrary", "arbitrary"),
        name="conv_ffn",
    )(*args)


def _s5_params(lam_re, lam_im, log_dt, b_re, b_im, c_re, c_im):
    depth = lam_re.shape[0]
    dt = jnp.exp(log_dt)[..., None]
    mag = jnp.exp(lam_re * dt)
    ang = lam_im * dt
    ab_re, ab_im = mag * jnp.cos(ang), mag * jnp.sin(ang)
    blocks = lambda re, im: jnp.concatenate(
        [re.reshape(*re.shape[:-2], S5_BLOCKS, LANE), im.reshape(*im.shape[:-2], S5_BLOCKS, LANE)], axis=-2)
    row = jnp.arange(SUBLANE, dtype=F32)
    expo = jnp.stack([jnp.full((SUBLANE,), 1.0), jnp.full((SUBLANE,), 2.0), jnp.full((SUBLANE,), 4.0),
                      row + 1.0])
    keep = jnp.stack([row >= 1, row >= 2, row >= 4, row >= 0]).astype(F32)
    lam_dt = (lam_re * dt).reshape(depth, 1, 1, S5_LANES)
    pang = ang.reshape(depth, 1, 1, S5_LANES) * expo[None, :, :, None]
    pmag = jnp.exp(lam_dt * expo[None, :, :, None]) * keep[None, :, :, None]
    tab = jnp.concatenate([pmag * jnp.cos(pang), pmag * jnp.sin(pang)], axis=-1)
    den = lam_re * lam_re + lam_im * lam_im
    nr, ni = ab_re - 1.0, ab_im
    co_re = (nr * lam_re + ni * lam_im) / den
    co_im = (ni * lam_re - nr * lam_im) / den
    bb_re = co_re[..., None] * b_re - co_im[..., None] * b_im
    bb_im = co_re[..., None] * b_im + co_im[..., None] * b_re
    gps = S5_GROUPS // S5_SUPER
    eye = jnp.eye(gps, dtype=F32)
    sup = lambda m: m.reshape(depth, S5_SUPER, gps, *m.shape[2:])
    dense_b = lambda m: jnp.einsum('lsgph,gk->lsghkp', sup(m), eye).reshape(depth, S5_SUPER, S5_SUP_CH, S5_SUP_ST)
    dense_c = lambda m: jnp.einsum('lsghp,gk->lskpgh', sup(m), eye).reshape(depth, S5_SUPER, S5_SUP_ST, S5_SUP_CH)
    bb = jnp.concatenate([dense_b(bb_re), dense_b(bb_im)], axis=3).astype(BF16)
    cc = jnp.concatenate([dense_c(c_re), -dense_c(c_im)], axis=2).astype(BF16)
    return bb, blocks(ab_re, ab_im), tab, cc


def _wprep_kernel(w_ref, o_ref, *, axis, valid_last):
    f = pl.program_id(1)
    last = pl.num_programs(1) - 1
    o = o_ref.at[0, 0] if axis == 1 else o_ref.at[0]

    @pl.when(f < last)
    def _():
        o[...] = w_ref[0].astype(BF16)

    @pl.when(f == last)
    def _():
        if axis == 1:
            o[:, :valid_last] = w_ref[0, :, :valid_last].astype(BF16)
            o[:, valid_last:] = jnp.zeros((o.shape[0], o.shape[1] - valid_last), BF16)
        else:
            o[:valid_last, :] = w_ref[0, :valid_last, :].astype(BF16)
            o[valid_last:, :] = jnp.zeros((o.shape[0] - valid_last, o.shape[1]), BF16)


def ffn_weight_cols(w, tf):
    depth, k, n = w.shape
    n_f = pl.cdiv(n, tf)
    return pl.pallas_call(
        functools.partial(_wprep_kernel, axis=1, valid_last=n - (n_f - 1) * tf),
        out_shape=jax.ShapeDtypeStruct((depth, n_f, k, tf), BF16),
        grid=(depth, n_f),
        in_specs=[pl.BlockSpec((1, k, tf), lambda l, f: (l, 0, f))],
        out_specs=pl.BlockSpec((1, 1, k, tf), lambda l, f: (l, f, 0, 0)),
        compiler_params=_cp("parallel", "parallel"),
        name="ffn_weight_cols",
    )(w)


def ffn_weight_rows(w, tf):
    depth, k, n = w.shape
    n_f = pl.cdiv(k, tf)
    return pl.pallas_call(
        functools.partial(_wprep_kernel, axis=0, valid_last=k - (n_f - 1) * tf),
        out_shape=jax.ShapeDtypeStruct((depth, n_f * tf, n), BF16),
        grid=(depth, n_f),
        in_specs=[pl.BlockSpec((1, tf, n), lambda l, f: (l, f, 0))],
        out_specs=pl.BlockSpec((1, tf, n), lambda l, f: (l, f, 0)),
        compiler_params=_cp("parallel", "parallel"),
        name="ffn_weight_rows",
    )(w)


def kernel(x_prompt, x_sample, mem_prompt, cache_mem_k, cache_mem_v, state_conv_a, state_s5_re, state_s5_im, state_conv_c, state_ssd, state_ffn_conv, norm_mix_w, w_in, conv_a_w, conv_a_b, ln_a_w, ln_a_b, s5_lam_re, s5_lam_im, s5_log_dt, s5_b_re, s5_b_im, s5_c_re, s5_c_im, s5_d, s5_glu_w, s5_glu_b, conv_c_w, conv_c_b, ssd_dt_bias, ssd_a_log, ssd_d, ssd_norm_w, w_out, norm_xa_w, norm_mem_w, xa_wq, xa_wk, xa_wv, xa_wo, norm_ffn_w, ffn_w_gate, ffn_w_up, ffn_conv_w, ffn_conv_b, ffn_w_down, final_norm_w):
    bp, seq, _ = x_prompt.shape
    nbs, lts, _ = x_sample.shape
    depth = w_in.shape[0]
    n_mem = mem_prompt.shape[1]
    lt_p = 512 if seq % 512 == 0 else seq
    n_tiles_p = seq // lt_p
    tm_p = lt_p
    tm_s = lts * nbs
    tm_f = 1024 if seq % 1024 == 0 else tm_p
    tm_m = min(512, bp * n_mem)

    vec = lambda a: a.reshape(depth, 1, a.shape[-1])
    pad_lanes = lambda a: vec(jnp.pad(a, ((0, 0), (0, LANE - a.shape[-1]))))
    ff_pad = D_FF_PAD - D_FF
    s5_bb, s5_ab, s5_tab, s5_cc = _s5_params(s5_lam_re, s5_lam_im, s5_log_dt, s5_b_re, s5_b_im, s5_c_re, s5_c_im)
    ssd_e, ssd_tril = _ssd_consts()
    p = {
        'norm_mix_w': vec(norm_mix_w), 'norm_xa_w': vec(norm_xa_w), 'norm_mem_w': vec(norm_mem_w),
        'norm_ffn_w': vec(norm_ffn_w),
        'w_in': jnp.pad(w_in, ((0, 0), (0, 0), (0, IN_SPLITS[2] - w_in.shape[2]))).astype(BF16),
        'conv_a_w': conv_a_w, 'conv_a_b': vec(conv_a_b), 'ln_a_w': vec(ln_a_w), 'ln_a_b': vec(ln_a_b),
        's5_bb': s5_bb, 's5_ab': s5_ab, 's5_tab': s5_tab, 's5_cc': s5_cc,
        's5_d': vec(s5_d), 's5_glu_w': s5_glu_w.astype(BF16), 's5_glu_b': vec(s5_glu_b),
        'conv_c_w': conv_c_w, 'conv_c_b': vec(conv_c_b),
        'ssd_dt_bias': pad_lanes(ssd_dt_bias), 'ssd_a_log': pad_lanes(ssd_a_log),
        'ssd_d': vec(jnp.repeat(ssd_d, SSD_HEAD_DIM, axis=1)), 'ssd_norm_w': vec(ssd_norm_w),
        'ssd_e': ssd_e, 'ssd_tril': ssd_tril,
        'w_out': w_out.astype(BF16), 'wq': xa_wq.astype(BF16), 'wo': xa_wo.astype(BF16),
        'wkv': jnp.concatenate([xa_wk, xa_wv], axis=2).astype(BF16),
        'ffn_wg': ffn_weight_cols(ffn_w_gate, FF_TILE),
        'ffn_wu': ffn_weight_cols(ffn_w_up, FF_TILE),
        'ffn_wd': ffn_weight_rows(ffn_w_down, FF_TILE),
        'ffn_conv_w': jnp.pad(ffn_conv_w, ((0, 0), (0, 0), (0, ff_pad))),
        'ffn_conv_b': vec(jnp.pad(ffn_conv_b, ((0, 0), (0, ff_pad)))),
    }

    tmaj = lambda a: a.transpose(0, 2, 1, 3).reshape(depth, a.shape[2] * nbs, a.shape[3])
    st_conv_a = tmaj(state_conv_a)
    st_conv_c = tmaj(state_conv_c)
    st_ffn = jnp.pad(tmaj(state_ffn_conv), ((0, 0), (0, 0), (0, ff_pad)))
    st_re = state_s5_re.reshape(depth, nbs, S5_LANES)
    st_im = state_s5_im.reshape(depth, nbs, S5_LANES)
    ssd_all = state_ssd.reshape(depth, nbs, D_C, SSD_STATE)

    xp = x_prompt.reshape(bp * seq, D_MODEL)
    xs = x_sample.transpose(1, 0, 2).reshape(lts * nbs, D_MODEL)
    mem2d = mem_prompt.reshape(bp * n_mem, D_MODEL)

    def mixers(x, l, *, n_seq, nb, lt, n_tiles, tm, sample):
        h_a, h_b, h_c = in_proj(x, p['norm_mix_w'], p['w_in'], l, tm=tm)
        ya, n_conv_a = conva_mixer(h_a, p, l, st_conv_a if sample else None,
                                   n_seq=n_seq, nb=nb, lt=lt, n_tiles=n_tiles)
        yb, n_re, n_im = s5_mixer(h_b, p, l, st_re if sample else None, st_im if sample else None,
                                  n_seq=n_seq, nb=nb, lt=lt, n_tiles=n_tiles)
        return h_c, ya, yb, n_conv_a, n_re, n_im

    outs_p = [[] for _ in range(8)]
    outs_s = [[] for _ in range(5)]
    for l in range(depth):
        last = l == depth - 1
        mk, mv = mem_kv(mem2d, p['norm_mem_w'], p['wkv'], l, tm=tm_m)

        h_c, ya, yb, p_conv_a, p_re, p_im = mixers(xp, l, n_seq=bp, nb=1, lt=lt_p, n_tiles=n_tiles_p,
                                                   tm=tm_p, sample=False)
        yc, p_conv_c, p_ssd = mamba_prompt(h_c, p, l, n_seq=bp, lt=lt_p, n_tiles=n_tiles_p)
        xp, q = out_q_proj(ya, yb, yc, p['w_out'], xp, p['norm_xa_w'], p['wq'], l, tm=tm_p)
        o = attn_prompt(q, mk.reshape(bp, n_mem, D_MODEL), mv.reshape(bp, n_mem, D_MODEL),
                        n_seq=bp, seq=seq, tq=lt_p)
        xp = proj_res([o], p['wo'], l, xp, tm=tm_p, name="attn_out")
        xp, p_ffn = conv_ffn(xp, p, l, None, final_norm_w if last else None, n_seq=bp, nb=1, tm=tm_f,
                             tiles_per_seq=seq // tm_f)
        for lst, v in zip(outs_p, (p_conv_a, p_re, p_im, p_conv_c, p_ssd,
                                   p_ffn[seq // tm_f - 1::seq // tm_f], mk, mv)):
            lst.append(v)

        h_c, ya, yb, s_conv_a, s_re, s_im = mixers(xs, l, n_seq=1, nb=nbs, lt=lts, n_tiles=1, tm=tm_s,
                                                   sample=True)
        yc, s_conv_c, ssd_all = mamba_sample(h_c, p, l, st_conv_c, ssd_all, nb=nbs, lt=lts,
                                             bb=8 if l > 0 else 4, in_place=l > 0)
        xs, q = out_q_proj(ya, yb, yc, p['w_out'], xs, p['norm_xa_w'], p['wq'], l, tm=tm_s,
                           q_by_seq=(nbs, lts))
        o = attn_sample(q, cache_mem_k, cache_mem_v, l, bb=4)
        xs = attn_out_seq(o, p['wo'], l, xs, nb=nbs, lt=lts)
        xs, s_ffn = conv_ffn(xs, p, l, st_ffn, final_norm_w if last else None, n_seq=1, nb=nbs, tm=tm_s,
                             tiles_per_seq=1)
        for lst, v in zip(outs_s, (s_conv_a[0], s_re[0], s_im[0], s_conv_c, s_ffn[0])):
            lst.append(v)

    p_conv_a, p_re, p_im, p_conv_c, p_ssd, p_ffn, p_mk, p_mv = [jnp.stack(o) for o in outs_p]
    s_conv_a, s_re, s_im, s_conv_c, s_ffn = [jnp.stack(o) for o in outs_s]
    bmaj = lambda a, w: a.reshape(depth, w, nbs, a.shape[-1]).transpose(0, 2, 1, 3)
    y_prompt = xp.reshape(bp, seq, D_MODEL)
    y_sample = xs.reshape(lts, nbs, D_MODEL).transpose(1, 0, 2)
    return (y_prompt, y_sample,
            p_conv_a,
            p_re.reshape(depth, bp, S5_GROUPS, S5_STATE), p_im.reshape(depth, bp, S5_GROUPS, S5_STATE),
            p_conv_c,
            p_ssd.reshape(depth, bp, SSD_HEADS, SSD_HEAD_DIM, SSD_STATE),
            p_ffn[..., :D_FF],
            p_mk.reshape(depth, bp, n_mem, XA_HEADS, XA_HEAD_DIM),
            p_mv.reshape(depth, bp, n_mem, XA_HEADS, XA_HEAD_DIM),
            bmaj(s_conv_a, CONV_A_WIDTH - 1),
            s_re.reshape(depth, nbs, S5_GROUPS, S5_STATE), s_im.reshape(depth, nbs, S5_GROUPS, S5_STATE),
            bmaj(s_conv_c, SSD_CONV_WIDTH - 1),
            ssd_all.reshape(state_ssd.shape),
            bmaj(s_ffn, FFN_CONV_WIDTH - 1)[..., :D_FF])
```

```python
import functools
import math

import jax
import jax.numpy as jnp
from jax import lax
from jax.experimental import pallas as pl
from jax.experimental.pallas import tpu as pltpu

F32 = jnp.float32
BF16 = jnp.bfloat16
EPS = 1e-6

D_MODEL = 2048
D_A = 512
D_B = 512
D_C = 1024
CONV_A_WIDTH = 31
S5_GROUP = 16
S5_GROUPS = 32
S5_STATE = 64
S5_LANES = S5_GROUPS * S5_STATE
SSD_HEAD_DIM = 64
SSD_HEADS = 16
SSD_GROUPS = 2
SSD_STATE = 128
SSD_CONV_WIDTH = 4
SSD_CHUNK = 128
D_XBC = D_C + 2 * SSD_GROUPS * SSD_STATE
D_HC = D_C + D_XBC + 128
XA_HEADS = 4
XA_HEAD_DIM = 512
N_MEM = 256
D_FF = 5504
FFN_CONV_WIDTH = 3

LANE = 128
SUBLANE = 8
VMEM_LIMIT = 56 * 1024 * 1024
FF_TILE = 512
D_FF_PAD = ((D_FF + FF_TILE - 1) // FF_TILE) * FF_TILE
FFN_ROW_SPLIT = 256


def _round_up(x, m):
    return (x + m - 1) // m * m


def _cp(*sem):
    return pltpu.CompilerParams(dimension_semantics=sem, vmem_limit_bytes=VMEM_LIMIT)


def _layer_spec(tail, layer):
    zeros = (0,) * len(tail)
    return pl.BlockSpec((1,) + tuple(tail), lambda *_: (layer,) + zeros)


def _dot(a, b):
    return jnp.dot(a, b, preferred_element_type=F32)


def _dot_nt(a, b):
    return lax.dot_general(a, b, (((1,), (1,)), ((), ())), preferred_element_type=F32)


def _dot_tn(a, b):
    return lax.dot_general(a, b, (((0,), (0,)), ((), ())), preferred_element_type=F32)


def _split3(a):
    hi = a.astype(BF16)
    r = a - hi.astype(F32)
    mid = r.astype(BF16)
    lo = (r - mid.astype(F32)).astype(BF16)
    return hi, mid, lo


def _expand(a, e):
    hi, mid, lo = _split3(a)
    return _dot(hi, e) + _dot(mid, e) + _dot(lo, e)


def _sigmoid(x):
    return jax.nn.sigmoid(x)


def _silu(x):
    return x * jax.nn.sigmoid(x)


def _softplus(x):
    return jnp.maximum(x, 0.0) + jnp.log1p(jnp.exp(-jnp.abs(x)))


def _rmsnorm_rows(x, w):
    ms = jnp.mean(x * x, axis=-1, keepdims=True)
    return x * lax.rsqrt(ms + EPS) * w


def _mem_kv_kernel(x_ref, nw_ref, w_ref, k_ref, v_ref, xn_ref):
    j = pl.program_id(1)

    @pl.when(j == 0)
    def _():
        xn = _rmsnorm_rows(x_ref[...], nw_ref[0]).astype(BF16)
        xn_ref[...] = xn
        k_ref[...] = _dot(xn, w_ref[0])

    @pl.when(j == 1)
    def _():
        v_ref[...] = _dot(xn_ref[...], w_ref[0])


def mem_kv(x, nw, w, layer, *, tm):
    m, k = x.shape
    n = w.shape[2] // 2
    out = jax.ShapeDtypeStruct((m, n), F32)
    return pl.pallas_call(
        _mem_kv_kernel,
        out_shape=(out, out),
        grid=(m // tm, 2),
        in_specs=[pl.BlockSpec((tm, k), lambda i, j: (i, 0)),
                  _layer_spec((1, k), layer),
                  pl.BlockSpec((1, k, n), lambda i, j: (layer, 0, j))],
        out_specs=(pl.BlockSpec((tm, n), lambda i, j: (i, 0)), pl.BlockSpec((tm, n), lambda i, j: (i, 0))),
        scratch_shapes=[pltpu.VMEM((tm, k), BF16)],
        compiler_params=_cp("parallel", "arbitrary"),
        name="mem_kv",
    )(x, nw, w)


def _attn_out_seq_kernel(o_ref, w_ref, res_ref, out_ref, a_ref, *, nb, lt):
    for t in range(lt):
        for h in range(XA_HEADS):
            a_ref[t * nb:(t + 1) * nb, h * XA_HEAD_DIM:(h + 1) * XA_HEAD_DIM] = o_ref[:, h * lt + t, :]
    out_ref[...] = res_ref[...] + _dot(a_ref[...].astype(BF16), w_ref[0])


def attn_out_seq(o, w, layer, res, *, nb, lt):
    m, n = res.shape
    return pl.pallas_call(
        functools.partial(_attn_out_seq_kernel, nb=nb, lt=lt),
        out_shape=jax.ShapeDtypeStruct((m, n), F32),
        grid=(1,),
        in_specs=[pl.BlockSpec(o.shape, lambda i: (0, 0, 0)),
                  _layer_spec(w.shape[1:], layer),
                  pl.BlockSpec((m, n), lambda i: (0, 0))],
        out_specs=pl.BlockSpec((m, n), lambda i: (0, 0)),
        scratch_shapes=[pltpu.VMEM((m, n), F32)],
        compiler_params=_cp("arbitrary"),
        name="attn_out_seq",
    )(o, w, res)


IN_SPLITS = (2 * D_A, 2 * D_A + D_B, 2 * D_A + D_B + D_HC)


def _in_proj_kernel(x_ref, nw_ref, w_ref, ha_ref, hb_ref, hc_ref):
    xn = _rmsnorm_rows(x_ref[...], nw_ref[0]).astype(BF16)
    ha_ref[...] = _dot(xn, w_ref[0, :, 0:IN_SPLITS[0]])
    hb_ref[...] = _dot(xn, w_ref[0, :, IN_SPLITS[0]:IN_SPLITS[1]])
    hc_ref[...] = _dot(xn, w_ref[0, :, IN_SPLITS[1]:IN_SPLITS[2]])


def in_proj(x, nw, w, layer, *, tm):
    m, k = x.shape
    widths = (IN_SPLITS[0], IN_SPLITS[1] - IN_SPLITS[0], IN_SPLITS[2] - IN_SPLITS[1])
    return pl.pallas_call(
        _in_proj_kernel,
        out_shape=tuple(jax.ShapeDtypeStruct((m, wd), F32) for wd in widths),
        grid=(m // tm,),
        in_specs=[pl.BlockSpec((tm, k), lambda i: (i, 0)),
                  _layer_spec((1, k), layer),
                  pl.BlockSpec((1, k, IN_SPLITS[2]), lambda i: (layer, 0, 0), pipeline_mode=pl.Buffered(1))],
        out_specs=tuple(pl.BlockSpec((tm, wd), lambda i: (i, 0)) for wd in widths),
        compiler_params=_cp("parallel"),
        name="in_proj",
    )(x, nw, w)


def _proj_res_kernel(*refs, n_in):
    a_refs = refs[:n_in]
    w_refs = refs[n_in:2 * n_in]
    res_ref, o_ref = refs[2 * n_in], refs[2 * n_in + 1]
    acc = res_ref[...]
    for a_ref, w_ref in zip(a_refs, w_refs):
        acc = acc + _dot(a_ref[...].astype(BF16), w_ref[0])
    o_ref[...] = acc


def proj_res(a_list, w, layer, res, *, tm, name):
    m, n = res.shape
    n_in = len(a_list)
    in_specs = [pl.BlockSpec((tm, a.shape[1]), lambda i: (i, 0)) for a in a_list]
    row0 = 0
    for a in a_list:
        kk = a.shape[1]
        assert row0 % kk == 0
        in_specs.append(pl.BlockSpec((1, kk, n), lambda i, blk=row0 // kk: (layer, blk, 0)))
        row0 += kk
    in_specs.append(pl.BlockSpec((tm, n), lambda i: (i, 0)))
    return pl.pallas_call(
        functools.partial(_proj_res_kernel, n_in=n_in),
        out_shape=jax.ShapeDtypeStruct((m, n), F32),
        grid=(m // tm,),
        in_specs=in_specs,
        out_specs=pl.BlockSpec((tm, n), lambda i: (i, 0)),
        compiler_params=_cp("parallel"),
        name=name,
    )(*a_list, *([w] * n_in), res)


def _out_q_kernel(ya_ref, yb_ref, yc_ref, w_ref, res_ref, nw_ref, wq_ref, x_ref, q_ref, *, q_by_seq):
    x = res_ref[...]
    row0 = 0
    for y_ref in (ya_ref, yb_ref, yc_ref):
        kk = y_ref.shape[1]
        x = x + _dot(y_ref[...], w_ref[0, row0:row0 + kk, :])
        row0 += kk
    x_ref[...] = x
    q = _dot(_rmsnorm_rows(x, nw_ref[0]).astype(BF16), wq_ref[0])
    if q_by_seq is None:
        q_ref[...] = q
    else:
        nb, lt = q_by_seq
        for t in range(lt):
            for h in range(XA_HEADS):
                q_ref[:, h * lt + t, :] = q[t * nb:(t + 1) * nb, h * XA_HEAD_DIM:(h + 1) * XA_HEAD_DIM]


def out_q_proj(ya, yb, yc, w_out, res, nw, wq, layer, *, tm, q_by_seq=None):
    m, n = res.shape
    resident = lambda a: pl.BlockSpec((1,) + a.shape[1:], lambda i: (layer, 0, 0), pipeline_mode=pl.Buffered(1))
    if q_by_seq is None:
        q_shape = (m, wq.shape[2])
        q_spec = pl.BlockSpec((tm, wq.shape[2]), lambda i: (i, 0))
    else:
        assert tm == m == q_by_seq[0] * q_by_seq[1]
        q_shape = (q_by_seq[0], XA_HEADS * q_by_seq[1], XA_HEAD_DIM)
        q_spec = pl.BlockSpec(q_shape, lambda i: (0, 0, 0))
    return pl.pallas_call(
        functools.partial(_out_q_kernel, q_by_seq=q_by_seq),
        out_shape=(jax.ShapeDtypeStruct((m, n), F32), jax.ShapeDtypeStruct(q_shape, F32)),
        grid=(m // tm,),
        in_specs=[pl.BlockSpec((tm, ya.shape[1]), lambda i: (i, 0)),
                  pl.BlockSpec((tm, yb.shape[1]), lambda i: (i, 0)),
                  pl.BlockSpec((tm, yc.shape[1]), lambda i: (i, 0)),
                  resident(w_out),
                  pl.BlockSpec((tm, n), lambda i: (i, 0)),
                  _layer_spec((1, n), layer),
                  resident(wq)],
        out_specs=(pl.BlockSpec((tm, n), lambda i: (i, 0)), q_spec),
        compiler_params=_cp("parallel"),
        name="out_q_proj",
    )(ya, yb, yc, w_out, res, nw, wq)


CONVA_ROW_CHUNK = 32


def _conva_tile(ext_ref, cls_ref, w_ref, b_ref, lnw_ref, lnb_ref, y_ref, *, nb, rows):
    hist = (CONV_A_WIDTH - 1) * nb
    pad = _round_up(hist, SUBLANE)
    bias = b_ref[0]
    lnw = lnw_ref[0]
    lnb = lnb_ref[0]
    rc = CONVA_ROW_CHUNK
    if nb == 1:
        for s in range(SUBLANE):
            span = rows + SUBLANE * ((CONV_A_WIDTH - 1 - s) // SUBLANE)
            cls_ref[s, 0:span, :] = ext_ref[pl.ds(pad - hist + s, span), :]
    for r0 in range(0, rows, rc):
        acc = jnp.zeros((rc, D_A), F32) + bias
        if nb == 1:
            for k in range(CONV_A_WIDTH):
                s, jt = k % SUBLANE, k // SUBLANE
                acc = acc + w_ref[0, k:k + 1, :] * cls_ref[s, r0 + SUBLANE * jt:r0 + SUBLANE * jt + rc, :]
        else:
            for k in range(CONV_A_WIDTH):
                acc = acc + w_ref[0, k:k + 1, :] * ext_ref[pl.ds(pad - hist + k * nb + r0, rc), :]
        mu = jnp.mean(acc, axis=-1, keepdims=True)
        xc = acc - mu
        var = jnp.mean(xc * xc, axis=-1, keepdims=True)
        c = xc * lax.rsqrt(var + EPS) * lnw + lnb
        y_ref[r0:r0 + rc, :] = _silu(c).astype(y_ref.dtype)
    return ext_ref[pl.ds(pad + rows - hist, hist), :]


def _conva_scratch(nb, rows):
    hist = (CONV_A_WIDTH - 1) * nb
    pad = _round_up(hist, SUBLANE)
    shapes = [pltpu.VMEM((pad + rows, D_A), F32)]
    if nb == 1:
        shapes.append(pltpu.VMEM((SUBLANE, rows + SUBLANE * ((CONV_A_WIDTH - 1) // SUBLANE), D_A), F32))
    return shapes


def _conva_kernel(*refs, nb, lt, n_tiles, has_state):
    refs = list(refs)
    cls_ref = refs.pop() if nb == 1 else None
    if has_state:
        h_ref, w_ref, b_ref, lnw_ref, lnb_ref, st_ref, y_ref, nst_ref, ext_ref = refs
    else:
        h_ref, w_ref, b_ref, lnw_ref, lnb_ref, y_ref, nst_ref, ext_ref = refs
    hist = (CONV_A_WIDTH - 1) * nb
    pad = _round_up(hist, SUBLANE)
    rows = lt * nb
    j = pl.program_id(1)

    @pl.when(j == 0)
    def _():
        if has_state:
            ext_ref[pad - hist:pad, :] = st_ref[0]
        else:
            ext_ref[0:pad, :] = jnp.zeros((pad, D_A), F32)

    ext_ref[pad:pad + rows, :] = h_ref[:, 0:D_A] * _sigmoid(h_ref[:, D_A:2 * D_A])
    new_hist = _conva_tile(ext_ref, cls_ref, w_ref, b_ref, lnw_ref, lnb_ref, y_ref, nb=nb, rows=rows)
    nst_ref[0] = new_hist
    if n_tiles > 1:
        ext_ref[pad - hist:pad, :] = new_hist


def conva_mixer(h_a, p, layer, state, *, n_seq, nb, lt, n_tiles):
    rows = lt * nb
    hist = (CONV_A_WIDTH - 1) * nb
    pad = _round_up(hist, SUBLANE)
    has_state = state is not None
    in_specs = [pl.BlockSpec((rows, 2 * D_A), lambda s, j: (s * n_tiles + j, 0)),
                _layer_spec((CONV_A_WIDTH, D_A), layer),
                _layer_spec((1, D_A), layer), _layer_spec((1, D_A), layer), _layer_spec((1, D_A), layer)]
    args = [h_a, p['conv_a_w'], p['conv_a_b'], p['ln_a_w'], p['ln_a_b']]
    if has_state:
        assert n_seq == 1
        in_specs.append(_layer_spec((hist, D_A), layer))
        args.append(state)
    return pl.pallas_call(
        functools.partial(_conva_kernel, nb=nb, lt=lt, n_tiles=n_tiles, has_state=has_state),
        out_shape=(jax.ShapeDtypeStruct((h_a.shape[0], D_A), BF16),
                   jax.ShapeDtypeStruct((n_seq, hist, D_A), F32)),
        grid=(n_seq, n_tiles),
        in_specs=in_specs,
        out_specs=(pl.BlockSpec((rows, D_A), lambda s, j: (s * n_tiles + j, 0)),
                   pl.BlockSpec((1, hist, D_A), lambda s, j: (s, 0, 0))),
        scratch_shapes=_conva_scratch(nb, rows),
        compiler_params=_cp("parallel", "arbitrary"),
        name="conva_mixer",
    )(*args)


def _in_proj_conva_kernel(x_ref, nw_ref, w_ref, cw_ref, cb_ref, lnw_ref, lnb_ref,
                          ya_ref, hb_ref, hc_ref, nst_ref, ext_ref, cls_ref, *, tiles_per_seq):
    rows = x_ref.shape[0]
    hist = CONV_A_WIDTH - 1
    pad = _round_up(hist, SUBLANE)

    @pl.when(pl.program_id(0) % tiles_per_seq == 0)
    def _():
        ext_ref[0:pad, :] = jnp.zeros((pad, D_A), F32)

    xn = _rmsnorm_rows(x_ref[...], nw_ref[0]).astype(BF16)
    h_a = _dot(xn, w_ref[0, :, 0:IN_SPLITS[0]])
    ext_ref[pad:pad + rows, :] = h_a[:, 0:D_A] * _sigmoid(h_a[:, D_A:2 * D_A])
    hb_ref[...] = _dot(xn, w_ref[0, :, IN_SPLITS[0]:IN_SPLITS[1]])
    hc_ref[...] = _dot(xn, w_ref[0, :, IN_SPLITS[1]:IN_SPLITS[2]])
    new_hist = _conva_tile(ext_ref, cls_ref, cw_ref, cb_ref, lnw_ref, lnb_ref, ya_ref, nb=1, rows=rows)
    nst_ref[0] = new_hist
    ext_ref[pad - hist:pad, :] = new_hist


def in_proj_conva(x, p, layer, *, tm, tiles_per_seq):
    m, k = x.shape
    hist = CONV_A_WIDTH - 1
    wb, wc = IN_SPLITS[1] - IN_SPLITS[0], IN_SPLITS[2] - IN_SPLITS[1]
    return pl.pallas_call(
        functools.partial(_in_proj_conva_kernel, tiles_per_seq=tiles_per_seq),
        out_shape=(jax.ShapeDtypeStruct((m, D_A), BF16),
                   jax.ShapeDtypeStruct((m, wb), F32),
                   jax.ShapeDtypeStruct((m, wc), F32),
                   jax.ShapeDtypeStruct((m // tm, hist, D_A), F32)),
        grid=(m // tm,),
        in_specs=[pl.BlockSpec((tm, k), lambda i: (i, 0)),
                  _layer_spec((1, k), layer),
                  pl.BlockSpec((1, k, IN_SPLITS[2]), lambda i: (layer, 0, 0), pipeline_mode=pl.Buffered(1)),
                  _layer_spec((CONV_A_WIDTH, D_A), layer),
                  _layer_spec((1, D_A), layer), _layer_spec((1, D_A), layer), _layer_spec((1, D_A), layer)],
        out_specs=(pl.BlockSpec((tm, D_A), lambda i: (i, 0)),
                   pl.BlockSpec((tm, wb), lambda i: (i, 0)),
                   pl.BlockSpec((tm, wc), lambda i: (i, 0)),
                   pl.BlockSpec((1, hist, D_A), lambda i: (i, 0, 0))),
        scratch_shapes=_conva_scratch(1, tm),
        compiler_params=_cp("arbitrary"),
        name="in_proj_conva",
    )(x, p['norm_mix_w'], p['w_in'], p['conv_a_w'], p['conv_a_b'], p['ln_a_w'], p['ln_a_b'])


def _gelu_tanh(x):
    return x * (0.5 * (1.0 + jnp.tanh(math.sqrt(2.0 / math.pi) * (x + 0.044715 * (x * x * x)))))


S5_BLOCKS = S5_LANES // LANE
S5_SUPER = 2
S5_SUP_CH = D_B // S5_SUPER
S5_SUP_ST = S5_LANES // S5_SUPER


def _s5_b_proj(u, bb_ref, hs_ref):
    n = S5_LANES
    for sb in range(S5_SUPER):
        bu = _dot(u[:, sb * S5_SUP_CH:(sb + 1) * S5_SUP_CH].astype(BF16), bb_ref[0, sb])
        hs_ref[:, sb * S5_SUP_ST:(sb + 1) * S5_SUP_ST] = bu[:, 0:S5_SUP_ST]
        hs_ref[:, n + sb * S5_SUP_ST:n + (sb + 1) * S5_SUP_ST] = bu[:, S5_SUP_ST:2 * S5_SUP_ST]


def _s5_glu_out(hs_ref, u, cc_ref, d_ref, gw_ref, gb_ref, y_ref):
    n = S5_LANES
    ys = []
    for sb in range(S5_SUPER):
        h16 = jnp.concatenate([hs_ref[:, sb * S5_SUP_ST:(sb + 1) * S5_SUP_ST],
                               hs_ref[:, n + sb * S5_SUP_ST:n + (sb + 1) * S5_SUP_ST]], axis=1).astype(BF16)
        ys.append(_dot(h16, cc_ref[0, sb]))
    y = jnp.concatenate(ys, axis=1) + d_ref[0] * u
    y = _gelu_tanh(y)
    gate = _dot(y.astype(BF16), gw_ref[0]) + gb_ref[0]
    y_ref[...] = (y * _sigmoid(gate)).astype(y_ref.dtype)


def _s5_seq_kernel(u_ref, bb_ref, tab_ref, cc_ref, d_ref, gw_ref, gb_ref,
                   y_ref, nre_ref, nim_ref, hs_ref, cre_ref, cim_ref, *, lt):
    n = S5_LANES
    j = pl.program_id(1)

    @pl.when(j == 0)
    def _():
        cre_ref[...] = jnp.zeros(cre_ref.shape, F32)
        cim_ref[...] = jnp.zeros(cim_ref.shape, F32)

    u = u_ref[...]
    _s5_b_proj(u, bb_ref, hs_ref)

    def group(i, carry):
        r0 = pl.multiple_of(i * SUBLANE, SUBLANE)
        new = []
        for c in range(S5_BLOCKS):
            lr = slice(c * LANE, (c + 1) * LANE)
            li = slice(n + c * LANE, n + (c + 1) * LANE)
            xr = hs_ref[pl.ds(r0, SUBLANE), lr]
            xi = hs_ref[pl.ds(r0, SUBLANE), li]
            for lev in range(3):
                ar = tab_ref[0, lev, :, lr]
                ai = tab_ref[0, lev, :, li]
                sr = pltpu.roll(xr, 1 << lev, 0)
                si = pltpu.roll(xi, 1 << lev, 0)
                xr, xi = xr + ar * sr - ai * si, xi + ar * si + ai * sr
            pr = tab_ref[0, 3, :, lr]
            pi = tab_ref[0, 3, :, li]
            er, ei = carry[2 * c], carry[2 * c + 1]
            hr = xr + pr * er - pi * ei
            hi = xi + pr * ei + pi * er
            hs_ref[pl.ds(r0, SUBLANE), lr] = hr
            hs_ref[pl.ds(r0, SUBLANE), li] = hi
            new += [jnp.broadcast_to(hr[SUBLANE - 1:SUBLANE, :], (SUBLANE, LANE)),
                    jnp.broadcast_to(hi[SUBLANE - 1:SUBLANE, :], (SUBLANE, LANE))]
        return tuple(new)

    init = []
    for c in range(S5_BLOCKS):
        init += [cre_ref[:, c * LANE:(c + 1) * LANE], cim_ref[:, c * LANE:(c + 1) * LANE]]
    last = lax.fori_loop(0, lt // SUBLANE, group, tuple(init))
    for c in range(S5_BLOCKS):
        cre_ref[:, c * LANE:(c + 1) * LANE] = last[2 * c]
        cim_ref[:, c * LANE:(c + 1) * LANE] = last[2 * c + 1]
    nre_ref[0] = cre_ref[0:1, :]
    nim_ref[0] = cim_ref[0:1, :]
    _s5_glu_out(hs_ref, u, cc_ref, d_ref, gw_ref, gb_ref, y_ref)


def _s5_step_kernel(u_ref, bb_ref, ab_ref, cc_ref, d_ref, gw_ref, gb_ref, sre_ref, sim_ref,
                    y_ref, nre_ref, nim_ref, hs_ref, *, nb, lt):
    n = S5_LANES
    nblk = S5_BLOCKS
    u = u_ref[...]
    _s5_b_proj(u, bb_ref, hs_ref)
    ab_re = jnp.concatenate([ab_ref[0, c:c + 1, :] for c in range(nblk)], axis=1)
    ab_im = jnp.concatenate([ab_ref[0, nblk + c:nblk + c + 1, :] for c in range(nblk)], axis=1)
    hr = sre_ref[0]
    hi = sim_ref[0]
    for t in range(lt):
        rs = slice(t * nb, (t + 1) * nb)
        nr = ab_re * hr - ab_im * hi + hs_ref[rs, 0:n]
        ni = ab_re * hi + ab_im * hr + hs_ref[rs, n:2 * n]
        hr, hi = nr, ni
        hs_ref[rs, 0:n] = hr
        hs_ref[rs, n:2 * n] = hi
    nre_ref[0] = hr
    nim_ref[0] = hi
    _s5_glu_out(hs_ref, u, cc_ref, d_ref, gw_ref, gb_ref, y_ref)


def s5_mixer(h_b, p, layer, s_re, s_im, *, n_seq, nb, lt, n_tiles):
    rows = lt * nb
    n = S5_LANES
    has_state = s_re is not None
    in_specs = [pl.BlockSpec((rows, D_B), lambda s, j: (s * n_tiles + j, 0)),
                _layer_spec((S5_SUPER, S5_SUP_CH, 2 * S5_SUP_ST), layer),
                _layer_spec((2 * S5_BLOCKS, LANE), layer)]
    args = [h_b, p['s5_bb'], p['s5_ab']]
    if not has_state:
        in_specs[2] = _layer_spec((4, SUBLANE, 2 * n), layer)
        args[2] = p['s5_tab']
    in_specs += [_layer_spec((S5_SUPER, 2 * S5_SUP_ST, S5_SUP_CH), layer),
                 _layer_spec((1, D_B), layer),
                 _layer_spec((D_B, D_B), layer),
                 _layer_spec((1, D_B), layer)]
    args += [p['s5_cc'], p['s5_d'], p['s5_glu_w'], p['s5_glu_b']]
    if has_state:
        assert n_seq == 1 and n_tiles == 1
        in_specs += [_layer_spec((nb, n), layer)] * 2
        args += [s_re, s_im]
        body = functools.partial(_s5_step_kernel, nb=nb, lt=lt)
        scratch = [pltpu.VMEM((rows, 2 * n), F32)]
    else:
        assert nb == 1 and lt % SUBLANE == 0
        body = functools.partial(_s5_seq_kernel, lt=lt)
        scratch = [pltpu.VMEM((rows, 2 * n), F32),
                   pltpu.VMEM((SUBLANE, n), F32),
                   pltpu.VMEM((SUBLANE, n), F32)]
    st_spec = pl.BlockSpec((1, nb, n), lambda s, j: (s, 0, 0))
    return pl.pallas_call(
        body,
        out_shape=(jax.ShapeDtypeStruct((h_b.shape[0], D_B), BF16),
                   jax.ShapeDtypeStruct((n_seq, nb, n), F32),
                   jax.ShapeDtypeStruct((n_seq, nb, n), F32)),
        grid=(n_seq, n_tiles),
        in_specs=in_specs,
        out_specs=(pl.BlockSpec((rows, D_B), lambda s, j: (s * n_tiles + j, 0)), st_spec, st_spec),
        scratch_shapes=scratch,
        compiler_params=_cp("parallel", "arbitrary"),
        name="s5_mixer",
    )(*args)


def _group_rmsnorm(y, nw):
    half = D_C // SSD_GROUPS
    outs = []
    for g in range(SSD_GROUPS):
        yg = y[:, g * half:(g + 1) * half]
        outs.append(yg * lax.rsqrt(jnp.mean(yg * yg, axis=-1, keepdims=True) + EPS))
    return jnp.concatenate(outs, axis=1) * nw


def _mamba_p_kernel(h_ref, cw_ref, cb_ref, dtb_ref, alog_ref, dexp_ref, nw_ref, e_ref, tril_ref,
                    y_ref, ncst_ref, nsst_ref, ext_ref, st_ref, *, lt, n_tiles):
    q = SSD_CHUNK
    hist = SSD_CONV_WIDTH - 1
    pad = SUBLANE
    half = D_C // SSD_GROUPS
    hpg = SSD_HEADS // SSD_GROUPS
    j = pl.program_id(1)

    @pl.when(j == 0)
    def _():
        ext_ref[0:pad, :] = jnp.zeros((pad, D_XBC), F32)
        st_ref[...] = jnp.zeros(st_ref.shape, F32)

    ext_ref[pad:pad + lt, :] = h_ref[:, D_C:D_C + D_XBC]

    e = e_ref[...]
    tril = tril_ref[...]
    a_neg = -jnp.exp(alog_ref[0])
    li = lax.broadcasted_iota(jnp.int32, (q, q), 0)
    si = lax.broadcasted_iota(jnp.int32, (q, q), 1)
    causal = li >= si
    lane = lax.broadcasted_iota(jnp.int32, (q, LANE), 1)

    for c in range(lt // q):
        r0 = c * q
        acc = jnp.zeros((q, D_XBC), F32) + cb_ref[0]
        for k in range(SSD_CONV_WIDTH):
            acc = acc + cw_ref[0, k:k + 1, :] * ext_ref[pl.ds(pad - hist + k + r0, q), :]
        xc = _silu(acc)
        xs = xc[:, 0:D_C]
        z = h_ref[r0:r0 + q, 0:D_C]
        dt = _softplus(h_ref[r0:r0 + q, D_C + D_XBC:D_C + D_XBC + LANE] + dtb_ref[0])
        a = dt * a_neg
        hi_, mid_, lo_ = _split3(a)
        cs = _dot(tril, hi_) + _dot(tril, mid_) + _dot(tril, lo_)
        cs_last = cs[q - 1:q, :]
        dt_x = _expand(dt, e)
        ecs_x = _expand(jnp.exp(cs), e)
        edl_x = _expand(jnp.exp(cs_last - cs), e)
        xdt = xs * dt_x
        cs_t = cs.T

        y_parts = []
        for g in range(SSD_GROUPS):
            bm = xc[:, D_C + g * SSD_STATE:D_C + (g + 1) * SSD_STATE]
            cm = xc[:, D_C + SSD_GROUPS * SSD_STATE + g * SSD_STATE:
                    D_C + SSD_GROUPS * SSD_STATE + (g + 1) * SSD_STATE]
            bm16 = bm.astype(BF16)
            cm16 = cm.astype(BF16)
            cb = _dot_nt(cm16, bm16)
            for pr in range(hpg // 2):
                r_even = g * hpg + 2 * pr
                xpair = xdt[:, r_even * SSD_HEAD_DIM:(r_even + 2) * SSD_HEAD_DIM].astype(BF16)
                ys = []
                for r in (r_even, r_even + 1):
                    seg = cs[:, r:r + 1] - cs_t[r:r + 1, :]
                    dec = jnp.exp(jnp.where(causal, seg, -jnp.inf))
                    ys.append(_dot((cb * dec).astype(BF16), xpair))
                y_parts.append(jnp.where(lane < SSD_HEAD_DIM, ys[0], ys[1]))
        y_diag = jnp.concatenate(y_parts, axis=1)
        y_off = jnp.concatenate(
            [_dot(xc[:, D_C + SSD_GROUPS * SSD_STATE + g * SSD_STATE:
                      D_C + SSD_GROUPS * SSD_STATE + (g + 1) * SSD_STATE].astype(BF16),
                  st_ref[:, g * half:(g + 1) * half].astype(BF16)) for g in range(SSD_GROUPS)],
            axis=1) * ecs_x
        y = y_diag + y_off + dexp_ref[0] * xs
        y = y * _silu(z)
        y_ref[r0:r0 + q, :] = _group_rmsnorm(y, nw_ref[0]).astype(y_ref.dtype)

        xw = (xdt * edl_x).astype(BF16)
        dec_row = ecs_x[q - 1:q, :]
        for g in range(SSD_GROUPS):
            bm_t = xc[:, D_C + g * SSD_STATE:D_C + (g + 1) * SSD_STATE].T.astype(BF16)
            upd = _dot(bm_t, xw[:, g * half:(g + 1) * half])
            st_ref[:, g * half:(g + 1) * half] = (
                st_ref[:, g * half:(g + 1) * half] * dec_row[:, g * half:(g + 1) * half] + upd)

    new_hist = ext_ref[pl.ds(pad + lt - hist, hist), :]
    ncst_ref[0] = new_hist
    if n_tiles > 1:
        ext_ref[pad - hist:pad, :] = new_hist

    @pl.when(j == n_tiles - 1)
    def _():
        for blk in range(D_C // LANE):
            nsst_ref[0, blk * LANE:(blk + 1) * LANE, :] = st_ref[:, blk * LANE:(blk + 1) * LANE].T


def _ssd_consts():
    head_of_lane = jnp.arange(D_C) // SSD_HEAD_DIM
    e = (jnp.arange(LANE)[:, None] == head_of_lane[None, :]).astype(BF16)
    tril = (jnp.arange(SSD_CHUNK)[:, None] >= jnp.arange(SSD_CHUNK)[None, :]).astype(BF16)
    return e, tril


def _ssd_param_specs(layer):
    return [_layer_spec((SSD_CONV_WIDTH, D_XBC), layer),
            _layer_spec((1, D_XBC), layer),
            _layer_spec((1, LANE), layer),
            _layer_spec((1, LANE), layer),
            _layer_spec((1, D_C), layer),
            _layer_spec((1, D_C), layer)]


def _ssd_param_args(p):
    return [p['conv_c_w'], p['conv_c_b'], p['ssd_dt_bias'], p['ssd_a_log'], p['ssd_d'], p['ssd_norm_w']]


def mamba_prompt(h_c, p, layer, *, n_seq, lt, n_tiles):
    hist = SSD_CONV_WIDTH - 1
    const = lambda s, j: (0, 0)
    return pl.pallas_call(
        functools.partial(_mamba_p_kernel, lt=lt, n_tiles=n_tiles),
        out_shape=(jax.ShapeDtypeStruct((h_c.shape[0], D_C), BF16),
                   jax.ShapeDtypeStruct((n_seq, hist, D_XBC), F32),
                   jax.ShapeDtypeStruct((n_seq, D_C, SSD_STATE), F32)),
        grid=(n_seq, n_tiles),
        in_specs=[pl.BlockSpec((lt, D_HC), lambda s, j: (s * n_tiles + j, 0))]
        + _ssd_param_specs(layer)
        + [pl.BlockSpec((LANE, D_C), const), pl.BlockSpec((SSD_CHUNK, SSD_CHUNK), const)],
        out_specs=(pl.BlockSpec((lt, D_C), lambda s, j: (s * n_tiles + j, 0)),
                   pl.BlockSpec((1, hist, D_XBC), lambda s, j: (s, 0, 0)),
                   pl.BlockSpec((1, D_C, SSD_STATE), lambda s, j: (s, 0, 0))),
        scratch_shapes=[pltpu.VMEM((SUBLANE + lt, D_XBC), F32),
                        pltpu.VMEM((SSD_STATE, D_C), F32)],
        compiler_params=_cp("parallel", "arbitrary"),
        name="mamba_prompt",
    )(h_c, *_ssd_param_args(p), p['ssd_e'], p['ssd_tril'])


def _ks(c, k, nb):
    return slice((c * SUBLANE + k) * nb, (c * SUBLANE + k + 1) * nb)


def _slab_put(ref, k, slab, nb):
    for c in range(slab.shape[1] // LANE):
        ref[_ks(c, k, nb), :] = slab[:, c * LANE:(c + 1) * LANE]


def _slab_get(ref, k, n_blocks, nb):
    return jnp.concatenate([ref[_ks(c, k, nb), :] for c in range(n_blocks)], axis=1)


def _seq_get(ref, b, n_blocks, nb):
    return jnp.concatenate(
        [ref[pl.ds(c * SUBLANE * nb + b, SUBLANE, stride=nb), :] for c in range(n_blocks)], axis=1)


def _seq_put(ref, b, val, nb, c0=0):
    for c in range(val.shape[1] // LANE):
        ref[pl.ds((c0 + c) * SUBLANE * nb + b, SUBLANE, stride=nb), :] = val[:, c * LANE:(c + 1) * LANE]


def _mamba_s_kernel(h_ref, cw_ref, cb_ref, dtb_ref, alog_ref, dexp_ref, nw_ref, e_ref, cst_ref, sst_ref,
                    y_ref, ncst_ref, nsst_ref,
                    ext_ref, xs_ref, dt_ref, cs_ref, lhs_ref, rhs_ref, c8_ref, yoff_ref,
                    *, nb, lt, bb, lsel, passthrough):
    hist = (SSD_CONV_WIDTH - 1) * nb
    rows = lt * nb
    half = D_C // SSD_GROUPS
    hpg = SSD_HEADS // SSD_GROUPS
    xblk = D_C // LANE
    hblk = half // LANE
    i = pl.program_id(0)
    n_steps = pl.num_programs(0)
    bc_off = D_C
    cc_off = D_C + SSD_GROUPS * SSD_STATE

    @pl.when(i == 0)
    def _phase1():
        e = e_ref[...]
        ext_ref[0:hist, :] = cst_ref[0]
        ext_ref[hist:hist + rows, :] = h_ref[:, D_C:D_C + D_XBC]
        ncst_ref[...] = ext_ref[rows:rows + hist, :]
        a_neg = -jnp.exp(alog_ref[0])
        lhs_ref[...] = jnp.zeros(lhs_ref.shape, F32)
        rhs_ref[...] = jnp.zeros(rhs_ref.shape, F32)
        c8_ref[...] = jnp.zeros(c8_ref.shape, F32)
        cs = jnp.zeros((nb, LANE), F32)
        for t in range(lt):
            rs = slice(t * nb, (t + 1) * nb)
            acc = jnp.zeros((nb, D_XBC), F32) + cb_ref[0]
            for k in range(SSD_CONV_WIDTH):
                acc = acc + cw_ref[0, k:k + 1, :] * ext_ref[(t + k) * nb:(t + k + 1) * nb, :]
            xc = _silu(acc)
            xs_ref[rs, :] = xc[:, 0:D_C]
            for g in range(SSD_GROUPS):
                rhs_ref[_ks(2 * g, t, nb), :] = xc[:, bc_off + g * SSD_STATE:bc_off + (g + 1) * SSD_STATE]
            _slab_put(c8_ref, t, xc[:, cc_off:cc_off + SSD_GROUPS * SSD_STATE], nb)
            dt = _softplus(h_ref[rs, D_C + D_XBC:D_C + D_XBC + LANE] + dtb_ref[0])
            dt_ref[rs, :] = dt
            cs = cs + dt * a_neg
            cs_ref[rs, :] = cs
        cs_last = cs
        for t in range(lt):
            rs = slice(t * nb, (t + 1) * nb)
            wt = jnp.exp(cs_last - cs_ref[rs, :]) * dt_ref[rs, :]
            _slab_put(lhs_ref, t, xs_ref[rs, :] * _expand(wt, e), nb)
        dec = _expand(jnp.exp(cs_last), e)
        d_hi = dec.astype(BF16).astype(F32)
        d_r = dec - d_hi
        d_mid = d_r.astype(BF16).astype(F32)
        d_lo = d_r - d_mid
        ones = jnp.ones((nb, SSD_STATE), F32)
        for k, piece in enumerate((d_hi, d_mid, d_lo)):
            _slab_put(lhs_ref, lt + k, piece, nb)
            for g in range(SSD_GROUPS):
                rhs_ref[_ks(2 * g + 1, lt + k, nb), :] = ones

    for jb in range(bb):
        b = i * bb + jb
        l8 = _seq_get(lhs_ref, b, xblk, nb).astype(BF16)
        r8 = _seq_get(rhs_ref, b, 2 * SSD_GROUPS, nb).astype(BF16)
        c8 = _seq_get(c8_ref, b, SSD_GROUPS, nb).astype(BF16)
        for g in range(SSD_GROUPS):
            s = sst_ref[lsel, jb, g * half:(g + 1) * half, :]
            yo = _dot_nt(c8[:, g * SSD_STATE:(g + 1) * SSD_STATE], s.astype(BF16))
            _seq_put(yoff_ref, b, yo, nb, c0=g * hblk)
            upd = _dot_tn(l8[:, g * half:(g + 1) * half],
                          r8[:, g * 2 * SSD_STATE:(g + 1) * 2 * SSD_STATE])
            nsst_ref[lsel, jb, g * half:(g + 1) * half, :] = upd[:, SSD_STATE:] * s + upd[:, :SSD_STATE]
    for d in passthrough:
        nsst_ref[d] = sst_ref[d]

    @pl.when(i == n_steps - 1)
    def _phase3():
        e = e_ref[...]
        lane = lax.broadcasted_iota(jnp.int32, (nb, LANE), 1)
        for t in range(lt):
            rt = slice(t * nb, (t + 1) * nb)
            cs_t = cs_ref[rt, :]
            y = (_slab_get(yoff_ref, t, xblk, nb) * _expand(jnp.exp(cs_t), e)
                 + dexp_ref[0] * xs_ref[rt, :])
            for s_ in range(t + 1):
                rsl = slice(s_ * nb, (s_ + 1) * nb)
                cbs = []
                for g in range(SSD_GROUPS):
                    cm = c8_ref[_ks(g, t, nb), :]
                    bm = rhs_ref[_ks(2 * g, s_, nb), :]
                    cbs.append(jnp.sum(cm * bm, axis=-1, keepdims=True))
                cb = jnp.where(lane < hpg, cbs[0], cbs[1])
                m = jnp.exp(cs_t - cs_ref[rsl, :]) * dt_ref[rsl, :] * cb
                y = y + _expand(m, e) * xs_ref[rsl, :]
            y = y * _silu(h_ref[rt, 0:D_C])
            y_ref[rt, :] = _group_rmsnorm(y, nw_ref[0]).astype(y_ref.dtype)


def mamba_sample(h_c, p, layer, cst, sst, *, nb, lt, bb, in_place):
    rows = lt * nb
    hist = (SSD_CONV_WIDTH - 1) * nb
    depth = sst.shape[0]
    const = lambda i: (0, 0)
    if in_place:
        sst_spec = pl.BlockSpec((1, bb, D_C, SSD_STATE), lambda i: (layer, i, 0, 0))
        lsel, passthrough = 0, ()
    else:
        sst_spec = pl.BlockSpec((depth, bb, D_C, SSD_STATE), lambda i: (0, i, 0, 0))
        lsel, passthrough = layer, tuple(d for d in range(depth) if d != layer)
    in_specs = ([pl.BlockSpec((rows, D_HC), const)] + _ssd_param_specs(layer)
                + [pl.BlockSpec((LANE, D_C), const), _layer_spec((hist, D_XBC), layer), sst_spec])
    return pl.pallas_call(
        functools.partial(_mamba_s_kernel, nb=nb, lt=lt, bb=bb, lsel=lsel, passthrough=passthrough),
        out_shape=(jax.ShapeDtypeStruct((rows, D_C), BF16),
                   jax.ShapeDtypeStruct((hist, D_XBC), F32),
                   jax.ShapeDtypeStruct(sst.shape, F32)),
        grid=(nb // bb,),
        in_specs=in_specs,
        out_specs=(pl.BlockSpec((rows, D_C), const),
                   pl.BlockSpec((hist, D_XBC), const),
                   sst_spec),
        input_output_aliases={len(in_specs) - 1: 2} if in_place else {},
        scratch_shapes=[pltpu.VMEM((hist + rows, D_XBC), F32),
                        pltpu.VMEM((rows, D_C), F32),
                        pltpu.VMEM((rows, LANE), F32),
                        pltpu.VMEM((rows, LANE), F32),
                        pltpu.VMEM((D_C // LANE * SUBLANE * nb, LANE), F32),
                        pltpu.VMEM((2 * SSD_GROUPS * SUBLANE * nb, LANE), F32),
                        pltpu.VMEM((SSD_GROUPS * SUBLANE * nb, LANE), F32),
                        pltpu.VMEM((D_C // LANE * SUBLANE * nb, LANE), F32)],
        compiler_params=_cp("arbitrary"),
        name="mamba_sample",
    )(h_c, *_ssd_param_args(p), p['ssd_e'], cst, sst)


def _softmax_rows(s):
    m = jnp.max(s, axis=-1, keepdims=True)
    ex = jnp.exp(s - m)
    return ex / jnp.sum(ex, axis=-1, keepdims=True)


def _attn_p_kernel(q_ref, k_ref, v_ref, o_ref):
    for h in range(XA_HEADS):
        hs = slice(h * XA_HEAD_DIM, (h + 1) * XA_HEAD_DIM)
        s = _dot_nt(q_ref[:, hs].astype(BF16), k_ref[0, :, hs].astype(BF16)) / math.sqrt(XA_HEAD_DIM)
        p = _softmax_rows(s)
        o_ref[:, hs] = _dot(p.astype(BF16), v_ref[0, :, hs].astype(BF16)).astype(o_ref.dtype)


def attn_prompt(q, k, v, *, n_seq, seq, tq):
    n_tiles = seq // tq
    return pl.pallas_call(
        _attn_p_kernel,
        out_shape=jax.ShapeDtypeStruct(q.shape, BF16),
        grid=(n_seq, n_tiles),
        in_specs=[pl.BlockSpec((tq, D_MODEL), lambda s, j: (s * n_tiles + j, 0)),
                  pl.BlockSpec((1, N_MEM, D_MODEL), lambda s, j: (s, 0, 0)),
                  pl.BlockSpec((1, N_MEM, D_MODEL), lambda s, j: (s, 0, 0))],
        out_specs=pl.BlockSpec((tq, D_MODEL), lambda s, j: (s * n_tiles + j, 0)),
        compiler_params=_cp("parallel", "arbitrary"),
        name="attn_prompt",
    )(q, k, v)


def _attn_s_kernel(q_ref, k_ref, v_ref, o_ref, *, bb, lt):
    rows = XA_HEADS * lt
    n = N_MEM * XA_HEADS
    col_head = lax.broadcasted_iota(jnp.int32, (rows, n), 1) % XA_HEADS
    row_head = lax.broadcasted_iota(jnp.int32, (rows, n), 0) // lt
    same_head = col_head == row_head
    for jb in range(bb):
        k = k_ref[0, jb].reshape(n, XA_HEAD_DIM).astype(BF16)
        v = v_ref[0, jb].reshape(n, XA_HEAD_DIM).astype(BF16)
        s = _dot_nt(q_ref[jb].astype(BF16), k) / math.sqrt(XA_HEAD_DIM)
        p = _softmax_rows(jnp.where(same_head, s, -jnp.inf))
        o_ref[jb] = _dot(p.astype(BF16), v)


def attn_sample(q, k, v, layer, *, bb):
    nb, rows, _ = q.shape
    kv_spec = pl.BlockSpec((1, bb, N_MEM, XA_HEADS, XA_HEAD_DIM), lambda i: (layer, i, 0, 0, 0))
    return pl.pallas_call(
        functools.partial(_attn_s_kernel, bb=bb, lt=rows // XA_HEADS),
        out_shape=jax.ShapeDtypeStruct((nb, rows, XA_HEAD_DIM), F32),
        grid=(nb // bb,),
        in_specs=[pl.BlockSpec((bb, rows, XA_HEAD_DIM), lambda i: (i, 0, 0)), kv_spec, kv_spec],
        out_specs=pl.BlockSpec((bb, rows, XA_HEAD_DIM), lambda i: (i, 0, 0)),
        compiler_params=_cp("parallel"),
        name="attn_sample",
    )(q, k, v)


def _ffn_kernel(*refs, nb, tiles_per_seq, has_state, final_norm):
    refs = list(refs)
    x_ref, nw_ref, wg_ref, wu_ref, cw_ref, cb_ref, wd_ref = refs[:7]
    pos = 7
    st_ref = None
    if has_state:
        st_ref = refs[pos]
        pos += 1
    fw_ref = None
    if final_norm:
        fw_ref = refs[pos]
        pos += 1
    o_ref, nst_ref, xn_ref, gext_ref, carry_ref = refs[pos:pos + 5]

    hist = (FFN_CONV_WIDTH - 1) * nb
    pad = _round_up(hist, SUBLANE)
    tm = x_ref.shape[0]
    tf = wg_ref.shape[3]
    rsz = min(FFN_ROW_SPLIT, tm)
    i = pl.program_id(0)
    f = pl.program_id(1)
    n_f = pl.num_programs(1)

    @pl.when(f == 0)
    def _():
        x = x_ref[...]
        xn_ref[...] = _rmsnorm_rows(x, nw_ref[0]).astype(BF16)
        o_ref[...] = x

    if tiles_per_seq > 1:
        first = (i % tiles_per_seq) == 0

        @pl.when(first)
        def _():
            if has_state:
                gext_ref[pad - hist:pad, :] = st_ref[0]
            else:
                gext_ref[0:pad, :] = jnp.zeros((pad, tf), F32)

        @pl.when(jnp.logical_not(first))
        def _():
            gext_ref[0:pad, :] = carry_ref[f]
    else:
        if has_state:
            gext_ref[pad - hist:pad, :] = st_ref[0]
        else:
            gext_ref[0:pad, :] = jnp.zeros((pad, tf), F32)

    cw0 = cw_ref[0, 0:1, :]
    cw1 = cw_ref[0, 1:2, :]
    cw2 = cw_ref[0, 2:3, :]
    cb = cb_ref[0]
    for r0 in range(0, tm, rsz):
        xn = xn_ref[r0:r0 + rsz, :]
        g = _dot(xn, wg_ref[0, 0])
        up = _dot(xn, wu_ref[0, 0])
        gext_ref[pad + r0:pad + r0 + rsz, :] = g
        conv = (cw0 * gext_ref[pl.ds(pad - 2 * nb + r0, rsz), :]
                + cw1 * gext_ref[pl.ds(pad - nb + r0, rsz), :]
                + cw2 * g + cb)
        act = _silu(conv) * up
        o_ref[r0:r0 + rsz, :] += _dot(act.astype(BF16), wd_ref[0])

    nst_ref[0] = gext_ref[pl.ds(pad + tm - hist, hist), :]
    if tiles_per_seq > 1:
        carry_ref[f] = gext_ref[pl.ds(tm, pad), :]

    if final_norm:
        @pl.when(f == n_f - 1)
        def _():
            o_ref[...] = _rmsnorm_rows(o_ref[...], fw_ref[...])


def conv_ffn(x, p, layer, state, final_w, *, n_seq, nb, tm, tiles_per_seq):
    m = x.shape[0]
    tf = FF_TILE
    n_f = D_FF_PAD // tf
    hist = (FFN_CONV_WIDTH - 1) * nb
    pad = _round_up(hist, SUBLANE)
    has_state = state is not None
    final_norm = final_w is not None
    in_specs = [pl.BlockSpec((tm, D_MODEL), lambda i, f: (i, 0)),
                _layer_spec((1, D_MODEL), layer),
                pl.BlockSpec((1, 1, D_MODEL, tf), lambda i, f: (layer, f, 0, 0)),
                pl.BlockSpec((1, 1, D_MODEL, tf), lambda i, f: (layer, f, 0, 0)),
                pl.BlockSpec((1, FFN_CONV_WIDTH, tf), lambda i, f: (layer, 0, f)),
                pl.BlockSpec((1, 1, tf), lambda i, f: (layer, 0, f)),
                pl.BlockSpec((1, tf, D_MODEL), lambda i, f: (layer, f, 0))]
    args = [x, p['norm_ffn_w'], p['ffn_wg'], p['ffn_wu'], p['ffn_conv_w'], p['ffn_conv_b'], p['ffn_wd']]
    if has_state:
        assert n_seq == 1 and tiles_per_seq == 1
        in_specs.append(pl.BlockSpec((1, hist, tf), lambda i, f: (layer, 0, f)))
        args.append(state)
    if final_norm:
        in_specs.append(pl.BlockSpec((1, D_MODEL), lambda i, f: (0, 0)))
        args.append(final_w.reshape(1, D_MODEL))
    return pl.pallas_call(
        functools.partial(_ffn_kernel, nb=nb, tiles_per_seq=tiles_per_seq, has_state=has_state,
                          final_norm=final_norm),
        out_shape=(jax.ShapeDtypeStruct((m, D_MODEL), F32),
                   jax.ShapeDtypeStruct((m // tm, hist, D_FF_PAD), F32)),
        grid=(m // tm, n_f),
        in_specs=in_specs,
        out_specs=(pl.BlockSpec((tm, D_MODEL), lambda i, f: (i, 0)),
                   pl.BlockSpec((1, hist, tf), lambda i, f: (i, 0, f))),
        scratch_shapes=[pltpu.VMEM((tm, D_MODEL), BF16),
                        pltpu.VMEM((pad + tm, tf), F32),
                        pltpu.VMEM((n_f, pad, tf), F32)],
        compiler_params=_cp("arbitrary", "arbitrary"),
        name="conv_ffn",
    )(*args)


def _s5_params(lam_re, lam_im, log_dt, b_re, b_im, c_re, c_im):
    depth = lam_re.shape[0]
    dt = jnp.exp(log_dt)[..., None]
    mag = jnp.exp(lam_re * dt)
    ang = lam_im * dt
    ab_re, ab_im = mag * jnp.cos(ang), mag * jnp.sin(ang)
    blocks = lambda re, im: jnp.concatenate(
        [re.reshape(*re.shape[:-2], S5_BLOCKS, LANE), im.reshape(*im.shape[:-2], S5_BLOCKS, LANE)], axis=-2)
    row = jnp.arange(SUBLANE, dtype=F32)
    expo = jnp.stack([jnp.full((SUBLANE,), 1.0), jnp.full((SUBLANE,), 2.0), jnp.full((SUBLANE,), 4.0),
                      row + 1.0])
    keep = jnp.stack([row >= 1, row >= 2, row >= 4, row >= 0]).astype(F32)
    lam_dt = (lam_re * dt).reshape(depth, 1, 1, S5_LANES)
    pang = ang.reshape(depth, 1, 1, S5_LANES) * expo[None, :, :, None]
    pmag = jnp.exp(lam_dt * expo[None, :, :, None]) * keep[None, :, :, None]
    tab = jnp.concatenate([pmag * jnp.cos(pang), pmag * jnp.sin(pang)], axis=-1)
    den = lam_re * lam_re + lam_im * lam_im
    nr, ni = ab_re - 1.0, ab_im
    co_re = (nr * lam_re + ni * lam_im) / den
    co_im = (ni * lam_re - nr * lam_im) / den
    bb_re = co_re[..., None] * b_re - co_im[..., None] * b_im
    bb_im = co_re[..., None] * b_im + co_im[..., None] * b_re
    gps = S5_GROUPS // S5_SUPER
    eye = jnp.eye(gps, dtype=F32)
    sup = lambda m: m.reshape(depth, S5_SUPER, gps, *m.shape[2:])
    dense_b = lambda m: jnp.einsum('lsgph,gk->lsghkp', sup(m), eye).reshape(depth, S5_SUPER, S5_SUP_CH, S5_SUP_ST)
    dense_c = lambda m: jnp.einsum('lsghp,gk->lskpgh', sup(m), eye).reshape(depth, S5_SUPER, S5_SUP_ST, S5_SUP_CH)
    bb = jnp.concatenate([dense_b(bb_re), dense_b(bb_im)], axis=3).astype(BF16)
    cc = jnp.concatenate([dense_c(c_re), -dense_c(c_im)], axis=2).astype(BF16)
    return bb, blocks(ab_re, ab_im), tab, cc


def _wprep_kernel(w_ref, o_ref, *, axis, valid_last):
    f = pl.program_id(1)
    last = pl.num_programs(1) - 1
    o = o_ref.at[0, 0] if axis == 1 else o_ref.at[0]

    @pl.when(f < last)
    def _():
        o[...] = w_ref[0].astype(BF16)

    @pl.when(f == last)
    def _():
        if axis == 1:
            o[:, :valid_last] = w_ref[0, :, :valid_last].astype(BF16)
            o[:, valid_last:] = jnp.zeros((o.shape[0], o.shape[1] - valid_last), BF16)
        else:
            o[:valid_last, :] = w_ref[0, :valid_last, :].astype(BF16)
            o[valid_last:, :] = jnp.zeros((o.shape[0] - valid_last, o.shape[1]), BF16)


def ffn_weight_cols(w, tf):
    depth, k, n = w.shape
    n_f = pl.cdiv(n, tf)
    return pl.pallas_call(
        functools.partial(_wprep_kernel, axis=1, valid_last=n - (n_f - 1) * tf),
        out_shape=jax.ShapeDtypeStruct((depth, n_f, k, tf), BF16),
        grid=(depth, n_f),
        in_specs=[pl.BlockSpec((1, k, tf), lambda l, f: (l, 0, f))],
        out_specs=pl.BlockSpec((1, 1, k, tf), lambda l, f: (l, f, 0, 0)),
        compiler_params=_cp("parallel", "parallel"),
        name="ffn_weight_cols",
    )(w)


def ffn_weight_rows(w, tf):
    depth, k, n = w.shape
    n_f = pl.cdiv(k, tf)
    return pl.pallas_call(
        functools.partial(_wprep_kernel, axis=0, valid_last=k - (n_f - 1) * tf),
        out_shape=jax.ShapeDtypeStruct((depth, n_f * tf, n), BF16),
        grid=(depth, n_f),
        in_specs=[pl.BlockSpec((1, tf, n), lambda l, f: (l, f, 0))],
        out_specs=pl.BlockSpec((1, tf, n), lambda l, f: (l, f, 0)),
        compiler_params=_cp("parallel", "parallel"),
        name="ffn_weight_rows",
    )(w)


def kernel(x_prompt, x_sample, mem_prompt, cache_mem_k, cache_mem_v, state_conv_a, state_s5_re, state_s5_im, state_conv_c, state_ssd, state_ffn_conv, norm_mix_w, w_in, conv_a_w, conv_a_b, ln_a_w, ln_a_b, s5_lam_re, s5_lam_im, s5_log_dt, s5_b_re, s5_b_im, s5_c_re, s5_c_im, s5_d, s5_glu_w, s5_glu_b, conv_c_w, conv_c_b, ssd_dt_bias, ssd_a_log, ssd_d, ssd_norm_w, w_out, norm_xa_w, norm_mem_w, xa_wq, xa_wk, xa_wv, xa_wo, norm_ffn_w, ffn_w_gate, ffn_w_up, ffn_conv_w, ffn_conv_b, ffn_w_down, final_norm_w):
    bp, seq, _ = x_prompt.shape
    nbs, lts, _ = x_sample.shape
    depth = w_in.shape[0]
    n_mem = mem_prompt.shape[1]
    lt_p = 512 if seq % 512 == 0 else seq
    n_tiles_p = seq // lt_p
    tm_p = lt_p
    tm_s = lts * nbs
    tm_f = 1024 if seq % 1024 == 0 else tm_p
    tm_m = min(512, bp * n_mem)

    vec = lambda a: a.reshape(depth, 1, a.shape[-1])
    pad_lanes = lambda a: vec(jnp.pad(a, ((0, 0), (0, LANE - a.shape[-1]))))
    ff_pad = D_FF_PAD - D_FF
    s5_bb, s5_ab, s5_tab, s5_cc = _s5_params(s5_lam_re, s5_lam_im, s5_log_dt, s5_b_re, s5_b_im, s5_c_re, s5_c_im)
    ssd_e, ssd_tril = _ssd_consts()
    p = {
        'norm_mix_w': vec(norm_mix_w), 'norm_xa_w': vec(norm_xa_w), 'norm_mem_w': vec(norm_mem_w),
        'norm_ffn_w': vec(norm_ffn_w),
        'w_in': jnp.pad(w_in, ((0, 0), (0, 0), (0, IN_SPLITS[2] - w_in.shape[2]))).astype(BF16),
        'conv_a_w': conv_a_w, 'conv_a_b': vec(conv_a_b), 'ln_a_w': vec(ln_a_w), 'ln_a_b': vec(ln_a_b),
        's5_bb': s5_bb, 's5_ab': s5_ab, 's5_tab': s5_tab, 's5_cc': s5_cc,
        's5_d': vec(s5_d), 's5_glu_w': s5_glu_w.astype(BF16), 's5_glu_b': vec(s5_glu_b),
        'conv_c_w': conv_c_w, 'conv_c_b': vec(conv_c_b),
        'ssd_dt_bias': pad_lanes(ssd_dt_bias), 'ssd_a_log': pad_lanes(ssd_a_log),
        'ssd_d': vec(jnp.repeat(ssd_d, SSD_HEAD_DIM, axis=1)), 'ssd_norm_w': vec(ssd_norm_w),
        'ssd_e': ssd_e, 'ssd_tril': ssd_tril,
        'w_out': w_out.astype(BF16), 'wq': xa_wq.astype(BF16), 'wo': xa_wo.astype(BF16),
        'wkv': jnp.concatenate([xa_wk, xa_wv], axis=2).astype(BF16),
        'ffn_wg': ffn_weight_cols(ffn_w_gate, FF_TILE),
        'ffn_wu': ffn_weight_cols(ffn_w_up, FF_TILE),
        'ffn_wd': ffn_weight_rows(ffn_w_down, FF_TILE),
        'ffn_conv_w': jnp.pad(ffn_conv_w, ((0, 0), (0, 0), (0, ff_pad))),
        'ffn_conv_b': vec(jnp.pad(ffn_conv_b, ((0, 0), (0, ff_pad)))),
    }

    tmaj = lambda a: a.transpose(0, 2, 1, 3).reshape(depth, a.shape[2] * nbs, a.shape[3])
    st_conv_a = tmaj(state_conv_a)
    st_conv_c = tmaj(state_conv_c)
    st_ffn = jnp.pad(tmaj(state_ffn_conv), ((0, 0), (0, 0), (0, ff_pad)))
    st_re = state_s5_re.reshape(depth, nbs, S5_LANES)
    st_im = state_s5_im.reshape(depth, nbs, S5_LANES)
    ssd_all = state_ssd.reshape(depth, nbs, D_C, SSD_STATE)

    xp = x_prompt.reshape(bp * seq, D_MODEL)
    xs = x_sample.transpose(1, 0, 2).reshape(lts * nbs, D_MODEL)
    mem2d = mem_prompt.reshape(bp * n_mem, D_MODEL)

    def mixers(x, l, *, n_seq, nb, lt, n_tiles, tm, sample):
        if sample:
            h_a, h_b, h_c = in_proj(x, p['norm_mix_w'], p['w_in'], l, tm=tm)
            ya, n_conv_a = conva_mixer(h_a, p, l, st_conv_a, n_seq=n_seq, nb=nb, lt=lt, n_tiles=n_tiles)
        else:
            ya, h_b, h_c, n_conv_a = in_proj_conva(x, p, l, tm=tm, tiles_per_seq=n_tiles)
            n_conv_a = n_conv_a[n_tiles - 1::n_tiles]
        yb, n_re, n_im = s5_mixer(h_b, p, l, st_re if sample else None, st_im if sample else None,
                                  n_seq=n_seq, nb=nb, lt=lt, n_tiles=n_tiles)
        return h_c, ya, yb, n_conv_a, n_re, n_im

    outs_p = [[] for _ in range(8)]
    outs_s = [[] for _ in range(5)]
    for l in range(depth):
        last = l == depth - 1
        mk, mv = mem_kv(mem2d, p['norm_mem_w'], p['wkv'], l, tm=tm_m)

        h_c, ya, yb, p_conv_a, p_re, p_im = mixers(xp, l, n_seq=bp, nb=1, lt=lt_p, n_tiles=n_tiles_p,
                                                   tm=tm_p, sample=False)
        yc, p_conv_c, p_ssd = mamba_prompt(h_c, p, l, n_seq=bp, lt=lt_p, n_tiles=n_tiles_p)
        xp, q = out_q_proj(ya, yb, yc, p['w_out'], xp, p['norm_xa_w'], p['wq'], l, tm=tm_p)
        o = attn_prompt(q, mk.reshape(bp, n_mem, D_MODEL), mv.reshape(bp, n_mem, D_MODEL),
                        n_seq=bp, seq=seq, tq=lt_p)
        xp = proj_res([o], p['wo'], l, xp, tm=tm_p, name="attn_out")
        xp, p_ffn = conv_ffn(xp, p, l, None, final_norm_w if last else None, n_seq=bp, nb=1, tm=tm_f,
                             tiles_per_seq=seq // tm_f)
        for lst, v in zip(outs_p, (p_conv_a, p_re, p_im, p_conv_c, p_ssd,
                                   p_ffn[seq // tm_f - 1::seq // tm_f], mk, mv)):
            lst.append(v)

        h_c, ya, yb, s_conv_a, s_re, s_im = mixers(xs, l, n_seq=1, nb=nbs, lt=lts, n_tiles=1, tm=tm_s,
                                                   sample=True)
        yc, s_conv_c, ssd_all = mamba_sample(h_c, p, l, st_conv_c, ssd_all, nb=nbs, lt=lts,
                                             bb=8 if l > 0 else 4, in_place=l > 0)
        xs, q = out_q_proj(ya, yb, yc, p['w_out'], xs, p['norm_xa_w'], p['wq'], l, tm=tm_s,
                           q_by_seq=(nbs, lts))
        o = attn_sample(q, cache_mem_k, cache_mem_v, l, bb=4)
        xs = attn_out_seq(o, p['wo'], l, xs, nb=nbs, lt=lts)
        xs, s_ffn = conv_ffn(xs, p, l, st_ffn, final_norm_w if last else None, n_seq=1, nb=nbs, tm=tm_s,
                             tiles_per_seq=1)
        for lst, v in zip(outs_s, (s_conv_a[0], s_re[0], s_im[0], s_conv_c, s_ffn[0])):
            lst.append(v)

    p_conv_a, p_re, p_im, p_conv_c, p_ssd, p_ffn, p_mk, p_mv = [jnp.stack(o) for o in outs_p]
    s_conv_a, s_re, s_im, s_conv_c, s_ffn = [jnp.stack(o) for o in outs_s]
    bmaj = lambda a, w: a.reshape(depth, w, nbs, a.shape[-1]).transpose(0, 2, 1, 3)
    y_prompt = xp.reshape(bp, seq, D_MODEL)
    y_sample = xs.reshape(lts, nbs, D_MODEL).transpose(1, 0, 2)
    return (y_prompt, y_sample,
            p_conv_a,
            p_re.reshape(depth, bp, S5_GROUPS, S5_STATE), p_im.reshape(depth, bp, S5_GROUPS, S5_STATE),
            p_conv_c,
            p_ssd.reshape(depth, bp, SSD_HEADS, SSD_HEAD_DIM, SSD_STATE),
            p_ffn[..., :D_FF],
            p_mk.reshape(depth, bp, n_mem, XA_HEADS, XA_HEAD_DIM),
            p_mv.reshape(depth, bp, n_mem, XA_HEADS, XA_HEAD_DIM),
            bmaj(s_conv_a, CONV_A_WIDTH - 1),
            s_re.reshape(depth, nbs, S5_GROUPS, S5_STATE), s_im.reshape(depth, nbs, S5_GROUPS, S5_STATE),
            bmaj(s_conv_c, SSD_CONV_WIDTH - 1),
            ssd_all.reshape(state_ssd.shape),
            bmaj(s_ffn, FFN_CONV_WIDTH - 1)[..., :D_FF])
```

```python
import functools
import math

import jax
import jax.numpy as jnp
from jax import lax
from jax.experimental import pallas as pl
from jax.experimental.pallas import tpu as pltpu

F32 = jnp.float32
BF16 = jnp.bfloat16
EPS = 1e-6

D_MODEL = 2048
D_A = 512
D_B = 512
D_C = 1024
CONV_A_WIDTH = 31
S5_GROUP = 16
S5_GROUPS = 32
S5_STATE = 64
S5_LANES = S5_GROUPS * S5_STATE
SSD_HEAD_DIM = 64
SSD_HEADS = 16
SSD_GROUPS = 2
SSD_STATE = 128
SSD_CONV_WIDTH = 4
SSD_CHUNK = 128
D_XBC = D_C + 2 * SSD_GROUPS * SSD_STATE
D_HC = D_C + D_XBC + 128
XA_HEADS = 4
XA_HEAD_DIM = 512
N_MEM = 256
D_FF = 5504
FFN_CONV_WIDTH = 3

LANE = 128
SUBLANE = 8
VMEM_LIMIT = 56 * 1024 * 1024
FF_TILE = 512
D_FF_PAD = ((D_FF + FF_TILE - 1) // FF_TILE) * FF_TILE
FFN_ROW_SPLIT = 512


def _round_up(x, m):
    return (x + m - 1) // m * m


def _cp(*sem):
    return pltpu.CompilerParams(dimension_semantics=sem, vmem_limit_bytes=VMEM_LIMIT)


def _layer_spec(tail, layer):
    zeros = (0,) * len(tail)
    return pl.BlockSpec((1,) + tuple(tail), lambda *_: (layer,) + zeros)


def _dot(a, b):
    return jnp.dot(a, b, preferred_element_type=F32)


def _dot_nt(a, b):
    return lax.dot_general(a, b, (((1,), (1,)), ((), ())), preferred_element_type=F32)


def _dot_tn(a, b):
    return lax.dot_general(a, b, (((0,), (0,)), ((), ())), preferred_element_type=F32)


def _split3(a):
    hi = a.astype(BF16)
    r = a - hi.astype(F32)
    mid = r.astype(BF16)
    lo = (r - mid.astype(F32)).astype(BF16)
    return hi, mid, lo


def _expand(a, e):
    hi, mid, lo = _split3(a)
    return _dot(hi, e) + _dot(mid, e) + _dot(lo, e)


def _sigmoid(x):
    return jax.nn.sigmoid(x)


def _silu(x):
    return x * jax.nn.sigmoid(x)


def _softplus(x):
    return jnp.maximum(x, 0.0) + jnp.log1p(jnp.exp(-jnp.abs(x)))


def _rmsnorm_rows(x, w):
    ms = jnp.mean(x * x, axis=-1, keepdims=True)
    return x * lax.rsqrt(ms + EPS) * w


def _mem_kv_kernel(x_ref, nw_ref, wk_ref, wv_ref, k_ref, v_ref, k4_ref, v4_ref, xn_ref):
    j = pl.program_id(2)

    def emit(w_ref, o_ref, o4_ref):
        y = _dot(xn_ref[...], w_ref[0])
        o_ref[0] = y
        for h in range(XA_HEADS):
            o4_ref[0, :, h, :] = y[:, h * XA_HEAD_DIM:(h + 1) * XA_HEAD_DIM]

    @pl.when(j == 0)
    def _():
        xn_ref[...] = _rmsnorm_rows(x_ref[...], nw_ref[0]).astype(BF16)
        emit(wk_ref, k_ref, k4_ref)

    @pl.when(j == 1)
    def _():
        emit(wv_ref, v_ref, v4_ref)


def mem_kv(x, nw, wk, wv, *, tm):
    m, k = x.shape
    depth, _, n = wk.shape
    flat = jax.ShapeDtypeStruct((depth, m, n), F32)
    split = jax.ShapeDtypeStruct((depth, m, XA_HEADS, XA_HEAD_DIM), F32)
    w_spec = pl.BlockSpec((1, k, n), lambda l, i, j: (l, 0, 0), pipeline_mode=pl.Buffered(1))
    flat_spec = pl.BlockSpec((1, tm, n), lambda l, i, j: (l, i, 0))
    split_spec = pl.BlockSpec((1, tm, XA_HEADS, XA_HEAD_DIM), lambda l, i, j: (l, i, 0, 0))
    return pl.pallas_call(
        _mem_kv_kernel,
        out_shape=(flat, flat, split, split),
        grid=(depth, m // tm, 2),
        in_specs=[pl.BlockSpec((tm, k), lambda l, i, j: (i, 0)),
                  pl.BlockSpec((1, 1, k), lambda l, i, j: (l, 0, 0)),
                  w_spec, w_spec],
        out_specs=(flat_spec, flat_spec, split_spec, split_spec),
        scratch_shapes=[pltpu.VMEM((tm, k), BF16)],
        compiler_params=_cp("parallel", "parallel", "arbitrary"),
        name="mem_kv",
    )(x, nw, wk, wv)


def _attn_out_seq_kernel(o_ref, w_ref, res_ref, out_ref, a_ref, *, nb, lt):
    for t in range(lt):
        for h in range(XA_HEADS):
            a_ref[t * nb:(t + 1) * nb, h * XA_HEAD_DIM:(h + 1) * XA_HEAD_DIM] = o_ref[:, h * lt + t, :]
    out_ref[...] = res_ref[...] + _dot(a_ref[...].astype(BF16), w_ref[0])


def attn_out_seq(o, w, layer, res, *, nb, lt):
    m, n = res.shape
    return pl.pallas_call(
        functools.partial(_attn_out_seq_kernel, nb=nb, lt=lt),
        out_shape=jax.ShapeDtypeStruct((m, n), F32),
        grid=(1,),
        in_specs=[pl.BlockSpec(o.shape, lambda i: (0, 0, 0)),
                  _layer_spec(w.shape[1:], layer),
                  pl.BlockSpec((m, n), lambda i: (0, 0))],
        out_specs=pl.BlockSpec((m, n), lambda i: (0, 0)),
        scratch_shapes=[pltpu.VMEM((m, n), F32)],
        compiler_params=_cp("arbitrary"),
        name="attn_out_seq",
    )(o, w, res)


IN_SPLITS = (2 * D_A, 2 * D_A + D_B, 2 * D_A + D_B + D_HC)


def _in_proj_kernel(x_ref, nw_ref, w_ref, ha_ref, hb_ref, hc_ref):
    xn = _rmsnorm_rows(x_ref[...], nw_ref[0]).astype(BF16)
    ha_ref[...] = _dot(xn, w_ref[0, :, 0:IN_SPLITS[0]])
    hb_ref[...] = _dot(xn, w_ref[0, :, IN_SPLITS[0]:IN_SPLITS[1]])
    hc_ref[...] = _dot(xn, w_ref[0, :, IN_SPLITS[1]:IN_SPLITS[2]])


def in_proj(x, nw, w, layer, *, tm):
    m, k = x.shape
    widths = (IN_SPLITS[0], IN_SPLITS[1] - IN_SPLITS[0], IN_SPLITS[2] - IN_SPLITS[1])
    return pl.pallas_call(
        _in_proj_kernel,
        out_shape=tuple(jax.ShapeDtypeStruct((m, wd), F32) for wd in widths),
        grid=(m // tm,),
        in_specs=[pl.BlockSpec((tm, k), lambda i: (i, 0)),
                  _layer_spec((1, k), layer),
                  pl.BlockSpec((1, k, IN_SPLITS[2]), lambda i: (layer, 0, 0), pipeline_mode=pl.Buffered(1))],
        out_specs=tuple(pl.BlockSpec((tm, wd), lambda i: (i, 0)) for wd in widths),
        compiler_params=_cp("parallel"),
        name="in_proj",
    )(x, nw, w)


def _proj_res_kernel(*refs, n_in):
    a_refs = refs[:n_in]
    w_refs = refs[n_in:2 * n_in]
    res_ref, o_ref = refs[2 * n_in], refs[2 * n_in + 1]
    acc = res_ref[...]
    for a_ref, w_ref in zip(a_refs, w_refs):
        acc = acc + _dot(a_ref[...].astype(BF16), w_ref[0])
    o_ref[...] = acc


def proj_res(a_list, w, layer, res, *, tm, name):
    m, n = res.shape
    n_in = len(a_list)
    in_specs = [pl.BlockSpec((tm, a.shape[1]), lambda i: (i, 0)) for a in a_list]
    row0 = 0
    for a in a_list:
        kk = a.shape[1]
        assert row0 % kk == 0
        in_specs.append(pl.BlockSpec((1, kk, n), lambda i, blk=row0 // kk: (layer, blk, 0)))
        row0 += kk
    in_specs.append(pl.BlockSpec((tm, n), lambda i: (i, 0)))
    return pl.pallas_call(
        functools.partial(_proj_res_kernel, n_in=n_in),
        out_shape=jax.ShapeDtypeStruct((m, n), F32),
        grid=(m // tm,),
        in_specs=in_specs,
        out_specs=pl.BlockSpec((tm, n), lambda i: (i, 0)),
        compiler_params=_cp("parallel"),
        name=name,
    )(*a_list, *([w] * n_in), res)


def _out_q_kernel(ya_ref, yb_ref, yc_ref, w_ref, res_ref, nw_ref, wq_ref, x_ref, q_ref, *, q_by_seq):
    x = res_ref[...]
    row0 = 0
    for y_ref in (ya_ref, yb_ref, yc_ref):
        kk = y_ref.shape[1]
        x = x + _dot(y_ref[...], w_ref[0, row0:row0 + kk, :])
        row0 += kk
    x_ref[...] = x
    q = _dot(_rmsnorm_rows(x, nw_ref[0]).astype(BF16), wq_ref[0])
    if q_by_seq is None:
        q_ref[...] = q
    else:
        nb, lt = q_by_seq
        for t in range(lt):
            for h in range(XA_HEADS):
                q_ref[:, h * lt + t, :] = q[t * nb:(t + 1) * nb, h * XA_HEAD_DIM:(h + 1) * XA_HEAD_DIM]


def out_q_proj(ya, yb, yc, w_out, res, nw, wq, layer, *, tm, q_by_seq=None):
    m, n = res.shape
    resident = lambda a: pl.BlockSpec((1,) + a.shape[1:], lambda i: (layer, 0, 0), pipeline_mode=pl.Buffered(1))
    if q_by_seq is None:
        q_shape = (m, wq.shape[2])
        q_spec = pl.BlockSpec((tm, wq.shape[2]), lambda i: (i, 0))
    else:
        assert tm == m == q_by_seq[0] * q_by_seq[1]
        q_shape = (q_by_seq[0], XA_HEADS * q_by_seq[1], XA_HEAD_DIM)
        q_spec = pl.BlockSpec(q_shape, lambda i: (0, 0, 0))
    return pl.pallas_call(
        functools.partial(_out_q_kernel, q_by_seq=q_by_seq),
        out_shape=(jax.ShapeDtypeStruct((m, n), F32), jax.ShapeDtypeStruct(q_shape, F32)),
        grid=(m // tm,),
        in_specs=[pl.BlockSpec((tm, ya.shape[1]), lambda i: (i, 0)),
                  pl.BlockSpec((tm, yb.shape[1]), lambda i: (i, 0)),
                  pl.BlockSpec((tm, yc.shape[1]), lambda i: (i, 0)),
                  resident(w_out),
                  pl.BlockSpec((tm, n), lambda i: (i, 0)),
                  _layer_spec((1, n), layer),
                  resident(wq)],
        out_specs=(pl.BlockSpec((tm, n), lambda i: (i, 0)), q_spec),
        compiler_params=_cp("parallel"),
        name="out_q_proj",
    )(ya, yb, yc, w_out, res, nw, wq)


CONVA_ROW_CHUNK = 32


def _conva_tile(ext_ref, cls_ref, w_ref, b_ref, lnw_ref, lnb_ref, y_ref, *, nb, rows):
    hist = (CONV_A_WIDTH - 1) * nb
    pad = _round_up(hist, SUBLANE)
    bias = b_ref[0]
    lnw = lnw_ref[0]
    lnb = lnb_ref[0]
    rc = CONVA_ROW_CHUNK
    if nb == 1:
        for s in range(SUBLANE):
            span = rows + SUBLANE * ((CONV_A_WIDTH - 1 - s) // SUBLANE)
            cls_ref[s, 0:span, :] = ext_ref[pl.ds(pad - hist + s, span), :]
    for r0 in range(0, rows, rc):
        acc = jnp.zeros((rc, D_A), F32) + bias
        if nb == 1:
            for k in range(CONV_A_WIDTH):
                s, jt = k % SUBLANE, k // SUBLANE
                acc = acc + w_ref[0, k:k + 1, :] * cls_ref[s, r0 + SUBLANE * jt:r0 + SUBLANE * jt + rc, :]
        else:
            for k in range(CONV_A_WIDTH):
                acc = acc + w_ref[0, k:k + 1, :] * ext_ref[pl.ds(pad - hist + k * nb + r0, rc), :]
        mu = jnp.mean(acc, axis=-1, keepdims=True)
        xc = acc - mu
        var = jnp.mean(xc * xc, axis=-1, keepdims=True)
        c = xc * lax.rsqrt(var + EPS) * lnw + lnb
        y_ref[r0:r0 + rc, :] = _silu(c).astype(y_ref.dtype)
    return ext_ref[pl.ds(pad + rows - hist, hist), :]


def _conva_scratch(nb, rows):
    hist = (CONV_A_WIDTH - 1) * nb
    pad = _round_up(hist, SUBLANE)
    shapes = [pltpu.VMEM((pad + rows, D_A), F32)]
    if nb == 1:
        shapes.append(pltpu.VMEM((SUBLANE, rows + SUBLANE * ((CONV_A_WIDTH - 1) // SUBLANE), D_A), F32))
    return shapes


def _conva_kernel(*refs, nb, lt, n_tiles, has_state):
    refs = list(refs)
    cls_ref = refs.pop() if nb == 1 else None
    if has_state:
        h_ref, w_ref, b_ref, lnw_ref, lnb_ref, st_ref, y_ref, nst_ref, ext_ref = refs
    else:
        h_ref, w_ref, b_ref, lnw_ref, lnb_ref, y_ref, nst_ref, ext_ref = refs
    hist = (CONV_A_WIDTH - 1) * nb
    pad = _round_up(hist, SUBLANE)
    rows = lt * nb
    j = pl.program_id(1)

    @pl.when(j == 0)
    def _():
        if has_state:
            ext_ref[pad - hist:pad, :] = st_ref[0]
        else:
            ext_ref[0:pad, :] = jnp.zeros((pad, D_A), F32)

    ext_ref[pad:pad + rows, :] = h_ref[:, 0:D_A] * _sigmoid(h_ref[:, D_A:2 * D_A])
    new_hist = _conva_tile(ext_ref, cls_ref, w_ref, b_ref, lnw_ref, lnb_ref, y_ref, nb=nb, rows=rows)
    nst_ref[0] = new_hist
    if n_tiles > 1:
        ext_ref[pad - hist:pad, :] = new_hist


def conva_mixer(h_a, p, layer, state, *, n_seq, nb, lt, n_tiles):
    rows = lt * nb
    hist = (CONV_A_WIDTH - 1) * nb
    pad = _round_up(hist, SUBLANE)
    has_state = state is not None
    in_specs = [pl.BlockSpec((rows, 2 * D_A), lambda s, j: (s * n_tiles + j, 0)),
                _layer_spec((CONV_A_WIDTH, D_A), layer),
                _layer_spec((1, D_A), layer), _layer_spec((1, D_A), layer), _layer_spec((1, D_A), layer)]
    args = [h_a, p['conv_a_w'], p['conv_a_b'], p['ln_a_w'], p['ln_a_b']]
    if has_state:
        assert n_seq == 1
        in_specs.append(_layer_spec((hist, D_A), layer))
        args.append(state)
    return pl.pallas_call(
        functools.partial(_conva_kernel, nb=nb, lt=lt, n_tiles=n_tiles, has_state=has_state),
        out_shape=(jax.ShapeDtypeStruct((h_a.shape[0], D_A), BF16),
                   jax.ShapeDtypeStruct((n_seq, hist, D_A), F32)),
        grid=(n_seq, n_tiles),
        in_specs=in_specs,
        out_specs=(pl.BlockSpec((rows, D_A), lambda s, j: (s * n_tiles + j, 0)),
                   pl.BlockSpec((1, hist, D_A), lambda s, j: (s, 0, 0))),
        scratch_shapes=_conva_scratch(nb, rows),
        compiler_params=_cp("parallel", "arbitrary"),
        name="conva_mixer",
    )(*args)


def _in_proj_conva_kernel(x_ref, nw_ref, w_ref, cw_ref, cb_ref, lnw_ref, lnb_ref,
                          ya_ref, hb_ref, hc_ref, nst_ref, ext_ref, cls_ref, *, tiles_per_seq):
    rows = x_ref.shape[0]
    hist = CONV_A_WIDTH - 1
    pad = _round_up(hist, SUBLANE)

    @pl.when(pl.program_id(0) % tiles_per_seq == 0)
    def _():
        ext_ref[0:pad, :] = jnp.zeros((pad, D_A), F32)

    xn = _rmsnorm_rows(x_ref[...], nw_ref[0]).astype(BF16)
    h_a = _dot(xn, w_ref[0, :, 0:IN_SPLITS[0]])
    ext_ref[pad:pad + rows, :] = h_a[:, 0:D_A] * _sigmoid(h_a[:, D_A:2 * D_A])
    hb_ref[...] = _dot(xn, w_ref[0, :, IN_SPLITS[0]:IN_SPLITS[1]])
    hc_ref[...] = _dot(xn, w_ref[0, :, IN_SPLITS[1]:IN_SPLITS[2]])
    new_hist = _conva_tile(ext_ref, cls_ref, cw_ref, cb_ref, lnw_ref, lnb_ref, ya_ref, nb=1, rows=rows)
    nst_ref[0] = new_hist
    ext_ref[pad - hist:pad, :] = new_hist


def in_proj_conva(x, p, layer, *, tm, tiles_per_seq):
    m, k = x.shape
    hist = CONV_A_WIDTH - 1
    wb, wc = IN_SPLITS[1] - IN_SPLITS[0], IN_SPLITS[2] - IN_SPLITS[1]
    return pl.pallas_call(
        functools.partial(_in_proj_conva_kernel, tiles_per_seq=tiles_per_seq),
        out_shape=(jax.ShapeDtypeStruct((m, D_A), BF16),
                   jax.ShapeDtypeStruct((m, wb), F32),
                   jax.ShapeDtypeStruct((m, wc), F32),
                   jax.ShapeDtypeStruct((m // tm, hist, D_A), F32)),
        grid=(m // tm,),
        in_specs=[pl.BlockSpec((tm, k), lambda i: (i, 0)),
                  _layer_spec((1, k), layer),
                  pl.BlockSpec((1, k, IN_SPLITS[2]), lambda i: (layer, 0, 0), pipeline_mode=pl.Buffered(1)),
                  _layer_spec((CONV_A_WIDTH, D_A), layer),
                  _layer_spec((1, D_A), layer), _layer_spec((1, D_A), layer), _layer_spec((1, D_A), layer)],
        out_specs=(pl.BlockSpec((tm, D_A), lambda i: (i, 0)),
                   pl.BlockSpec((tm, wb), lambda i: (i, 0)),
                   pl.BlockSpec((tm, wc), lambda i: (i, 0)),
                   pl.BlockSpec((1, hist, D_A), lambda i: (i, 0, 0))),
        scratch_shapes=_conva_scratch(1, tm),
        compiler_params=_cp("arbitrary"),
        name="in_proj_conva",
    )(x, p['norm_mix_w'], p['w_in'], p['conv_a_w'], p['conv_a_b'], p['ln_a_w'], p['ln_a_b'])


def _gelu_tanh(x):
    return x * (0.5 * (1.0 + jnp.tanh(math.sqrt(2.0 / math.pi) * (x + 0.044715 * (x * x * x)))))


S5_BLOCKS = S5_LANES // LANE
S5_SUPER = 2
S5_SUP_CH = D_B // S5_SUPER
S5_SUP_ST = S5_LANES // S5_SUPER


def _s5_b_proj(u, bb_ref, hs_ref):
    n = S5_LANES
    for sb in range(S5_SUPER):
        bu = _dot(u[:, sb * S5_SUP_CH:(sb + 1) * S5_SUP_CH].astype(BF16), bb_ref[0, sb])
        hs_ref[:, sb * S5_SUP_ST:(sb + 1) * S5_SUP_ST] = bu[:, 0:S5_SUP_ST]
        hs_ref[:, n + sb * S5_SUP_ST:n + (sb + 1) * S5_SUP_ST] = bu[:, S5_SUP_ST:2 * S5_SUP_ST]


def _s5_glu_out(hs_ref, u, cc_ref, d_ref, gw_ref, gb_ref, y_ref):
    n = S5_LANES
    ys = []
    for sb in range(S5_SUPER):
        h16 = jnp.concatenate([hs_ref[:, sb * S5_SUP_ST:(sb + 1) * S5_SUP_ST],
                               hs_ref[:, n + sb * S5_SUP_ST:n + (sb + 1) * S5_SUP_ST]], axis=1).astype(BF16)
        ys.append(_dot(h16, cc_ref[0, sb]))
    y = jnp.concatenate(ys, axis=1) + d_ref[0] * u
    y = _gelu_tanh(y)
    gate = _dot(y.astype(BF16), gw_ref[0]) + gb_ref[0]
    y_ref[...] = (y * _sigmoid(gate)).astype(y_ref.dtype)


def _s5_seq_kernel(u_ref, bb_ref, tab_ref, cc_ref, d_ref, gw_ref, gb_ref,
                   y_ref, nre_ref, nim_ref, hs_ref, cre_ref, cim_ref, *, lt):
    n = S5_LANES
    j = pl.program_id(1)

    @pl.when(j == 0)
    def _():
        cre_ref[...] = jnp.zeros(cre_ref.shape, F32)
        cim_ref[...] = jnp.zeros(cim_ref.shape, F32)

    u = u_ref[...]
    _s5_b_proj(u, bb_ref, hs_ref)

    def group(i, carry):
        r0 = pl.multiple_of(i * SUBLANE, SUBLANE)
        new = []
        for c in range(S5_BLOCKS):
            lr = slice(c * LANE, (c + 1) * LANE)
            li = slice(n + c * LANE, n + (c + 1) * LANE)
            xr = hs_ref[pl.ds(r0, SUBLANE), lr]
            xi = hs_ref[pl.ds(r0, SUBLANE), li]
            for lev in range(3):
                ar = tab_ref[0, lev, :, lr]
                ai = tab_ref[0, lev, :, li]
                sr = pltpu.roll(xr, 1 << lev, 0)
                si = pltpu.roll(xi, 1 << lev, 0)
                xr, xi = xr + ar * sr - ai * si, xi + ar * si + ai * sr
            pr = tab_ref[0, 3, :, lr]
            pi = tab_ref[0, 3, :, li]
            er, ei = carry[2 * c], carry[2 * c + 1]
            hr = xr + pr * er - pi * ei
            hi = xi + pr * ei + pi * er
            hs_ref[pl.ds(r0, SUBLANE), lr] = hr
            hs_ref[pl.ds(r0, SUBLANE), li] = hi
            new += [jnp.broadcast_to(hr[SUBLANE - 1:SUBLANE, :], (SUBLANE, LANE)),
                    jnp.broadcast_to(hi[SUBLANE - 1:SUBLANE, :], (SUBLANE, LANE))]
        return tuple(new)

    init = []
    for c in range(S5_BLOCKS):
        init += [cre_ref[:, c * LANE:(c + 1) * LANE], cim_ref[:, c * LANE:(c + 1) * LANE]]
    last = lax.fori_loop(0, lt // SUBLANE, group, tuple(init))
    for c in range(S5_BLOCKS):
        cre_ref[:, c * LANE:(c + 1) * LANE] = last[2 * c]
        cim_ref[:, c * LANE:(c + 1) * LANE] = last[2 * c + 1]
    nre_ref[0] = cre_ref[0:1, :]
    nim_ref[0] = cim_ref[0:1, :]
    _s5_glu_out(hs_ref, u, cc_ref, d_ref, gw_ref, gb_ref, y_ref)


def _s5_step_kernel(u_ref, bb_ref, ab_ref, cc_ref, d_ref, gw_ref, gb_ref, sre_ref, sim_ref,
                    y_ref, nre_ref, nim_ref, hs_ref, *, nb, lt):
    n = S5_LANES
    nblk = S5_BLOCKS
    u = u_ref[...]
    _s5_b_proj(u, bb_ref, hs_ref)
    ab_re = jnp.concatenate([ab_ref[0, c:c + 1, :] for c in range(nblk)], axis=1)
    ab_im = jnp.concatenate([ab_ref[0, nblk + c:nblk + c + 1, :] for c in range(nblk)], axis=1)
    hr = sre_ref[0]
    hi = sim_ref[0]
    for t in range(lt):
        rs = slice(t * nb, (t + 1) * nb)
        nr = ab_re * hr - ab_im * hi + hs_ref[rs, 0:n]
        ni = ab_re * hi + ab_im * hr + hs_ref[rs, n:2 * n]
        hr, hi = nr, ni
        hs_ref[rs, 0:n] = hr
        hs_ref[rs, n:2 * n] = hi
    nre_ref[0] = hr
    nim_ref[0] = hi
    _s5_glu_out(hs_ref, u, cc_ref, d_ref, gw_ref, gb_ref, y_ref)


def s5_mixer(h_b, p, layer, s_re, s_im, *, n_seq, nb, lt, n_tiles):
    rows = lt * nb
    n = S5_LANES
    has_state = s_re is not None
    in_specs = [pl.BlockSpec((rows, D_B), lambda s, j: (s * n_tiles + j, 0)),
                _layer_spec((S5_SUPER, S5_SUP_CH, 2 * S5_SUP_ST), layer),
                _layer_spec((2 * S5_BLOCKS, LANE), layer)]
    args = [h_b, p['s5_bb'], p['s5_ab']]
    if not has_state:
        in_specs[2] = _layer_spec((4, SUBLANE, 2 * n), layer)
        args[2] = p['s5_tab']
    in_specs += [_layer_spec((S5_SUPER, 2 * S5_SUP_ST, S5_SUP_CH), layer),
                 _layer_spec((1, D_B), layer),
                 _layer_spec((D_B, D_B), layer),
                 _layer_spec((1, D_B), layer)]
    args += [p['s5_cc'], p['s5_d'], p['s5_glu_w'], p['s5_glu_b']]
    if has_state:
        assert n_seq == 1 and n_tiles == 1
        in_specs += [_layer_spec((nb, n), layer)] * 2
        args += [s_re, s_im]
        body = functools.partial(_s5_step_kernel, nb=nb, lt=lt)
        scratch = [pltpu.VMEM((rows, 2 * n), F32)]
    else:
        assert nb == 1 and lt % SUBLANE == 0
        body = functools.partial(_s5_seq_kernel, lt=lt)
        scratch = [pltpu.VMEM((rows, 2 * n), F32),
                   pltpu.VMEM((SUBLANE, n), F32),
                   pltpu.VMEM((SUBLANE, n), F32)]
    st_spec = pl.BlockSpec((1, nb, n), lambda s, j: (s, 0, 0))
    return pl.pallas_call(
        body,
        out_shape=(jax.ShapeDtypeStruct((h_b.shape[0], D_B), BF16),
                   jax.ShapeDtypeStruct((n_seq, nb, n), F32),
                   jax.ShapeDtypeStruct((n_seq, nb, n), F32)),
        grid=(n_seq, n_tiles),
        in_specs=in_specs,
        out_specs=(pl.BlockSpec((rows, D_B), lambda s, j: (s * n_tiles + j, 0)), st_spec, st_spec),
        scratch_shapes=scratch,
        compiler_params=_cp("parallel", "arbitrary"),
        name="s5_mixer",
    )(*args)


def _group_rmsnorm(y, nw):
    half = D_C // SSD_GROUPS
    outs = []
    for g in range(SSD_GROUPS):
        yg = y[:, g * half:(g + 1) * half]
        outs.append(yg * lax.rsqrt(jnp.mean(yg * yg, axis=-1, keepdims=True) + EPS))
    return jnp.concatenate(outs, axis=1) * nw


def _mamba_p_kernel(h_ref, cw_ref, cb_ref, dtb_ref, alog_ref, dexp_ref, nw_ref, e_ref, tril_ref,
                    y_ref, ncst_ref, nsst_ref, ext_ref, st_ref, *, lt, n_tiles):
    q = SSD_CHUNK
    hist = SSD_CONV_WIDTH - 1
    pad = SUBLANE
    half = D_C // SSD_GROUPS
    hpg = SSD_HEADS // SSD_GROUPS
    j = pl.program_id(1)

    @pl.when(j == 0)
    def _():
        ext_ref[0:pad, :] = jnp.zeros((pad, D_XBC), F32)
        st_ref[...] = jnp.zeros(st_ref.shape, F32)

    ext_ref[pad:pad + lt, :] = h_ref[:, D_C:D_C + D_XBC]

    e = e_ref[...]
    tril = tril_ref[...]
    a_neg = -jnp.exp(alog_ref[0])
    li = lax.broadcasted_iota(jnp.int32, (q, q), 0)
    si = lax.broadcasted_iota(jnp.int32, (q, q), 1)
    causal = li >= si
    lane = lax.broadcasted_iota(jnp.int32, (q, LANE), 1)

    for c in range(lt // q):
        r0 = c * q
        acc = jnp.zeros((q, D_XBC), F32) + cb_ref[0]
        for k in range(SSD_CONV_WIDTH):
            acc = acc + cw_ref[0, k:k + 1, :] * ext_ref[pl.ds(pad - hist + k + r0, q), :]
        xc = _silu(acc)
        xs = xc[:, 0:D_C]
        z = h_ref[r0:r0 + q, 0:D_C]
        dt = _softplus(h_ref[r0:r0 + q, D_C + D_XBC:D_C + D_XBC + LANE] + dtb_ref[0])
        a = dt * a_neg
        hi_, mid_, lo_ = _split3(a)
        cs = _dot(tril, hi_) + _dot(tril, mid_) + _dot(tril, lo_)
        cs_last = cs[q - 1:q, :]
        dt_x = _expand(dt, e)
        ecs_x = _expand(jnp.exp(cs), e)
        edl_x = _expand(jnp.exp(cs_last - cs), e)
        xdt = xs * dt_x
        cs_t = cs.T

        y_parts = []
        for g in range(SSD_GROUPS):
            bm = xc[:, D_C + g * SSD_STATE:D_C + (g + 1) * SSD_STATE]
            cm = xc[:, D_C + SSD_GROUPS * SSD_STATE + g * SSD_STATE:
                    D_C + SSD_GROUPS * SSD_STATE + (g + 1) * SSD_STATE]
            bm16 = bm.astype(BF16)
            cm16 = cm.astype(BF16)
            cb = _dot_nt(cm16, bm16)
            for pr in range(hpg // 2):
                r_even = g * hpg + 2 * pr
                xpair = xdt[:, r_even * SSD_HEAD_DIM:(r_even + 2) * SSD_HEAD_DIM].astype(BF16)
                ys = []
                for r in (r_even, r_even + 1):
                    seg = cs[:, r:r + 1] - cs_t[r:r + 1, :]
                    dec = jnp.exp(jnp.where(causal, seg, -jnp.inf))
                    ys.append(_dot((cb * dec).astype(BF16), xpair))
                y_parts.append(jnp.where(lane < SSD_HEAD_DIM, ys[0], ys[1]))
        y_diag = jnp.concatenate(y_parts, axis=1)
        y_off = jnp.concatenate(
            [_dot(xc[:, D_C + SSD_GROUPS * SSD_STATE + g * SSD_STATE:
                      D_C + SSD_GROUPS * SSD_STATE + (g + 1) * SSD_STATE].astype(BF16),
                  st_ref[:, g * half:(g + 1) * half].astype(BF16)) for g in range(SSD_GROUPS)],
            axis=1) * ecs_x
        y = y_diag + y_off + dexp_ref[0] * xs
        y = y * _silu(z)
        y_ref[r0:r0 + q, :] = _group_rmsnorm(y, nw_ref[0]).astype(y_ref.dtype)

        xw = (xdt * edl_x).astype(BF16)
        dec_row = ecs_x[q - 1:q, :]
        for g in range(SSD_GROUPS):
            bm_t = xc[:, D_C + g * SSD_STATE:D_C + (g + 1) * SSD_STATE].T.astype(BF16)
            upd = _dot(bm_t, xw[:, g * half:(g + 1) * half])
            st_ref[:, g * half:(g + 1) * half] = (
                st_ref[:, g * half:(g + 1) * half] * dec_row[:, g * half:(g + 1) * half] + upd)

    new_hist = ext_ref[pl.ds(pad + lt - hist, hist), :]
    ncst_ref[0] = new_hist
    if n_tiles > 1:
        ext_ref[pad - hist:pad, :] = new_hist

    @pl.when(j == n_tiles - 1)
    def _():
        for blk in range(D_C // LANE):
            nsst_ref[0, blk * LANE:(blk + 1) * LANE, :] = st_ref[:, blk * LANE:(blk + 1) * LANE].T


def _ssd_consts():
    head_of_lane = jnp.arange(D_C) // SSD_HEAD_DIM
    e = (jnp.arange(LANE)[:, None] == head_of_lane[None, :]).astype(BF16)
    tril = (jnp.arange(SSD_CHUNK)[:, None] >= jnp.arange(SSD_CHUNK)[None, :]).astype(BF16)
    return e, tril


def _ssd_param_specs(layer):
    return [_layer_spec((SSD_CONV_WIDTH, D_XBC), layer),
            _layer_spec((1, D_XBC), layer),
            _layer_spec((1, LANE), layer),
            _layer_spec((1, LANE), layer),
            _layer_spec((1, D_C), layer),
            _layer_spec((1, D_C), layer)]


def _ssd_param_args(p):
    return [p['conv_c_w'], p['conv_c_b'], p['ssd_dt_bias'], p['ssd_a_log'], p['ssd_d'], p['ssd_norm_w']]


def mamba_prompt(h_c, p, layer, *, n_seq, lt, n_tiles):
    hist = SSD_CONV_WIDTH - 1
    const = lambda s, j: (0, 0)
    return pl.pallas_call(
        functools.partial(_mamba_p_kernel, lt=lt, n_tiles=n_tiles),
        out_shape=(jax.ShapeDtypeStruct((h_c.shape[0], D_C), BF16),
                   jax.ShapeDtypeStruct((n_seq, hist, D_XBC), F32),
                   jax.ShapeDtypeStruct((n_seq, D_C, SSD_STATE), F32)),
        grid=(n_seq, n_tiles),
        in_specs=[pl.BlockSpec((lt, D_HC), lambda s, j: (s * n_tiles + j, 0))]
        + _ssd_param_specs(layer)
        + [pl.BlockSpec((LANE, D_C), const), pl.BlockSpec((SSD_CHUNK, SSD_CHUNK), const)],
        out_specs=(pl.BlockSpec((lt, D_C), lambda s, j: (s * n_tiles + j, 0)),
                   pl.BlockSpec((1, hist, D_XBC), lambda s, j: (s, 0, 0)),
                   pl.BlockSpec((1, D_C, SSD_STATE), lambda s, j: (s, 0, 0))),
        scratch_shapes=[pltpu.VMEM((SUBLANE + lt, D_XBC), F32),
                        pltpu.VMEM((SSD_STATE, D_C), F32)],
        compiler_params=_cp("parallel", "arbitrary"),
        name="mamba_prompt",
    )(h_c, *_ssd_param_args(p), p['ssd_e'], p['ssd_tril'])


def _ks(c, k, nb):
    return slice((c * SUBLANE + k) * nb, (c * SUBLANE + k + 1) * nb)


def _slab_put(ref, k, slab, nb):
    for c in range(slab.shape[1] // LANE):
        ref[_ks(c, k, nb), :] = slab[:, c * LANE:(c + 1) * LANE]


def _slab_get(ref, k, n_blocks, nb):
    return jnp.concatenate([ref[_ks(c, k, nb), :] for c in range(n_blocks)], axis=1)


def _seq_get(ref, b, n_blocks, nb):
    return jnp.concatenate(
        [ref[pl.ds(c * SUBLANE * nb + b, SUBLANE, stride=nb), :] for c in range(n_blocks)], axis=1)


def _seq_put(ref, b, val, nb, c0=0):
    for c in range(val.shape[1] // LANE):
        ref[pl.ds((c0 + c) * SUBLANE * nb + b, SUBLANE, stride=nb), :] = val[:, c * LANE:(c + 1) * LANE]


def _mamba_s_kernel(h_ref, cw_ref, cb_ref, dtb_ref, alog_ref, dexp_ref, nw_ref, e_ref, cst_ref, sst_ref,
                    y_ref, ncst_ref, nsst_ref,
                    ext_ref, xs_ref, dt_ref, cs_ref, lhs_ref, rhs_ref, c8_ref, yoff_ref,
                    *, nb, lt, bb, lsel, passthrough):
    hist = (SSD_CONV_WIDTH - 1) * nb
    rows = lt * nb
    half = D_C // SSD_GROUPS
    hpg = SSD_HEADS // SSD_GROUPS
    xblk = D_C // LANE
    hblk = half // LANE
    i = pl.program_id(0)
    n_steps = pl.num_programs(0)
    bc_off = D_C
    cc_off = D_C + SSD_GROUPS * SSD_STATE

    @pl.when(i == 0)
    def _phase1():
        e = e_ref[...]
        ext_ref[0:hist, :] = cst_ref[0]
        ext_ref[hist:hist + rows, :] = h_ref[:, D_C:D_C + D_XBC]
        ncst_ref[...] = ext_ref[rows:rows + hist, :]
        a_neg = -jnp.exp(alog_ref[0])
        lhs_ref[...] = jnp.zeros(lhs_ref.shape, F32)
        rhs_ref[...] = jnp.zeros(rhs_ref.shape, F32)
        c8_ref[...] = jnp.zeros(c8_ref.shape, F32)
        cs = jnp.zeros((nb, LANE), F32)
        for t in range(lt):
            rs = slice(t * nb, (t + 1) * nb)
            acc = jnp.zeros((nb, D_XBC), F32) + cb_ref[0]
            for k in range(SSD_CONV_WIDTH):
                acc = acc + cw_ref[0, k:k + 1, :] * ext_ref[(t + k) * nb:(t + k + 1) * nb, :]
            xc = _silu(acc)
            xs_ref[rs, :] = xc[:, 0:D_C]
            for g in range(SSD_GROUPS):
                rhs_ref[_ks(2 * g, t, nb), :] = xc[:, bc_off + g * SSD_STATE:bc_off + (g + 1) * SSD_STATE]
            _slab_put(c8_ref, t, xc[:, cc_off:cc_off + SSD_GROUPS * SSD_STATE], nb)
            dt = _softplus(h_ref[rs, D_C + D_XBC:D_C + D_XBC + LANE] + dtb_ref[0])
            dt_ref[rs, :] = dt
            cs = cs + dt * a_neg
            cs_ref[rs, :] = cs
        cs_last = cs
        for t in range(lt):
            rs = slice(t * nb, (t + 1) * nb)
            wt = jnp.exp(cs_last - cs_ref[rs, :]) * dt_ref[rs, :]
            _slab_put(lhs_ref, t, xs_ref[rs, :] * _expand(wt, e), nb)
        dec = _expand(jnp.exp(cs_last), e)
        d_hi = dec.astype(BF16).astype(F32)
        d_r = dec - d_hi
        d_mid = d_r.astype(BF16).astype(F32)
        d_lo = d_r - d_mid
        ones = jnp.ones((nb, SSD_STATE), F32)
        for k, piece in enumerate((d_hi, d_mid, d_lo)):
            _slab_put(lhs_ref, lt + k, piece, nb)
            for g in range(SSD_GROUPS):
                rhs_ref[_ks(2 * g + 1, lt + k, nb), :] = ones

    for jb in range(bb):
        b = i * bb + jb
        l8 = _seq_get(lhs_ref, b, xblk, nb).astype(BF16)
        r8 = _seq_get(rhs_ref, b, 2 * SSD_GROUPS, nb).astype(BF16)
        c8 = _seq_get(c8_ref, b, SSD_GROUPS, nb).astype(BF16)
        for g in range(SSD_GROUPS):
            s = sst_ref[lsel, jb, g * half:(g + 1) * half, :]
            yo = _dot_nt(c8[:, g * SSD_STATE:(g + 1) * SSD_STATE], s.astype(BF16))
            _seq_put(yoff_ref, b, yo, nb, c0=g * hblk)
            upd = _dot_tn(l8[:, g * half:(g + 1) * half],
                          r8[:, g * 2 * SSD_STATE:(g + 1) * 2 * SSD_STATE])
            nsst_ref[lsel, jb, g * half:(g + 1) * half, :] = upd[:, SSD_STATE:] * s + upd[:, :SSD_STATE]
    for d in passthrough:
        nsst_ref[d] = sst_ref[d]

    @pl.when(i == n_steps - 1)
    def _phase3():
        e = e_ref[...]
        lane = lax.broadcasted_iota(jnp.int32, (nb, LANE), 1)
        for t in range(lt):
            rt = slice(t * nb, (t + 1) * nb)
            cs_t = cs_ref[rt, :]
            y = (_slab_get(yoff_ref, t, xblk, nb) * _expand(jnp.exp(cs_t), e)
                 + dexp_ref[0] * xs_ref[rt, :])
            for s_ in range(t + 1):
                rsl = slice(s_ * nb, (s_ + 1) * nb)
                cbs = []
                for g in range(SSD_GROUPS):
                    cm = c8_ref[_ks(g, t, nb), :]
                    bm = rhs_ref[_ks(2 * g, s_, nb), :]
                    cbs.append(jnp.sum(cm * bm, axis=-1, keepdims=True))
                cb = jnp.where(lane < hpg, cbs[0], cbs[1])
                m = jnp.exp(cs_t - cs_ref[rsl, :]) * dt_ref[rsl, :] * cb
                y = y + _expand(m, e) * xs_ref[rsl, :]
            y = y * _silu(h_ref[rt, 0:D_C])
            y_ref[rt, :] = _group_rmsnorm(y, nw_ref[0]).astype(y_ref.dtype)


def mamba_sample(h_c, p, layer, cst, sst, *, nb, lt, bb, in_place):
    rows = lt * nb
    hist = (SSD_CONV_WIDTH - 1) * nb
    depth = sst.shape[0]
    const = lambda i: (0, 0)
    if in_place:
        sst_spec = pl.BlockSpec((1, bb, D_C, SSD_STATE), lambda i: (layer, i, 0, 0))
        lsel, passthrough = 0, ()
    else:
        sst_spec = pl.BlockSpec((depth, bb, D_C, SSD_STATE), lambda i: (0, i, 0, 0))
        lsel, passthrough = layer, tuple(d for d in range(depth) if d != layer)
    in_specs = ([pl.BlockSpec((rows, D_HC), const)] + _ssd_param_specs(layer)
                + [pl.BlockSpec((LANE, D_C), const), _layer_spec((hist, D_XBC), layer), sst_spec])
    return pl.pallas_call(
        functools.partial(_mamba_s_kernel, nb=nb, lt=lt, bb=bb, lsel=lsel, passthrough=passthrough),
        out_shape=(jax.ShapeDtypeStruct((rows, D_C), BF16),
                   jax.ShapeDtypeStruct((hist, D_XBC), F32),
                   jax.ShapeDtypeStruct(sst.shape, F32)),
        grid=(nb // bb,),
        in_specs=in_specs,
        out_specs=(pl.BlockSpec((rows, D_C), const),
                   pl.BlockSpec((hist, D_XBC), const),
                   sst_spec),
        input_output_aliases={len(in_specs) - 1: 2} if in_place else {},
        scratch_shapes=[pltpu.VMEM((hist + rows, D_XBC), F32),
                        pltpu.VMEM((rows, D_C), F32),
                        pltpu.VMEM((rows, LANE), F32),
                        pltpu.VMEM((rows, LANE), F32),
                        pltpu.VMEM((D_C // LANE * SUBLANE * nb, LANE), F32),
                        pltpu.VMEM((2 * SSD_GROUPS * SUBLANE * nb, LANE), F32),
                        pltpu.VMEM((SSD_GROUPS * SUBLANE * nb, LANE), F32),
                        pltpu.VMEM((D_C // LANE * SUBLANE * nb, LANE), F32)],
        compiler_params=_cp("arbitrary"),
        name="mamba_sample",
    )(h_c, *_ssd_param_args(p), p['ssd_e'], cst, sst)


def _softmax_rows(s):
    m = jnp.max(s, axis=-1, keepdims=True)
    ex = jnp.exp(s - m)
    return ex / jnp.sum(ex, axis=-1, keepdims=True)


def _attn_p_kernel(q_ref, k_ref, v_ref, o_ref):
    for h in range(XA_HEADS):
        hs = slice(h * XA_HEAD_DIM, (h + 1) * XA_HEAD_DIM)
        s = _dot_nt(q_ref[:, hs].astype(BF16), k_ref[0, :, hs].astype(BF16)) / math.sqrt(XA_HEAD_DIM)
        p = _softmax_rows(s)
        o_ref[:, hs] = _dot(p.astype(BF16), v_ref[0, :, hs].astype(BF16)).astype(o_ref.dtype)


def attn_prompt(q, k, v, layer, *, n_seq, seq, tq):
    n_tiles = seq // tq
    return pl.pallas_call(
        _attn_p_kernel,
        out_shape=jax.ShapeDtypeStruct(q.shape, BF16),
        grid=(n_seq, n_tiles),
        in_specs=[pl.BlockSpec((tq, D_MODEL), lambda s, j: (s * n_tiles + j, 0)),
                  pl.BlockSpec((1, N_MEM, D_MODEL), lambda s, j: (layer * n_seq + s, 0, 0)),
                  pl.BlockSpec((1, N_MEM, D_MODEL), lambda s, j: (layer * n_seq + s, 0, 0))],
        out_specs=pl.BlockSpec((tq, D_MODEL), lambda s, j: (s * n_tiles + j, 0)),
        compiler_params=_cp("parallel", "arbitrary"),
        name="attn_prompt",
    )(q, k, v)


def _attn_s_kernel(q_ref, k_ref, v_ref, o_ref, *, bb, lt):
    rows = XA_HEADS * lt
    n = N_MEM * XA_HEADS
    col_head = lax.broadcasted_iota(jnp.int32, (rows, n), 1) % XA_HEADS
    row_head = lax.broadcasted_iota(jnp.int32, (rows, n), 0) // lt
    same_head = col_head == row_head
    for jb in range(bb):
        k = k_ref[0, jb].reshape(n, XA_HEAD_DIM).astype(BF16)
        v = v_ref[0, jb].reshape(n, XA_HEAD_DIM).astype(BF16)
        s = _dot_nt(q_ref[jb].astype(BF16), k) / math.sqrt(XA_HEAD_DIM)
        p = _softmax_rows(jnp.where(same_head, s, -jnp.inf))
        o_ref[jb] = _dot(p.astype(BF16), v)


def attn_sample(q, k, v, layer, *, bb):
    nb, rows, _ = q.shape
    kv_spec = pl.BlockSpec((1, bb, N_MEM, XA_HEADS, XA_HEAD_DIM), lambda i: (layer, i, 0, 0, 0))
    return pl.pallas_call(
        functools.partial(_attn_s_kernel, bb=bb, lt=rows // XA_HEADS),
        out_shape=jax.ShapeDtypeStruct((nb, rows, XA_HEAD_DIM), F32),
        grid=(nb // bb,),
        in_specs=[pl.BlockSpec((bb, rows, XA_HEAD_DIM), lambda i: (i, 0, 0)), kv_spec, kv_spec],
        out_specs=pl.BlockSpec((bb, rows, XA_HEAD_DIM), lambda i: (i, 0, 0)),
        compiler_params=_cp("parallel"),
        name="attn_sample",
    )(q, k, v)


def _ffn_kernel(*refs, nb, tiles_per_seq, has_state, final_norm):
    refs = list(refs)
    x_ref, nw_ref, wg_ref, wu_ref, cw_ref, cb_ref, wd_ref = refs[:7]
    pos = 7
    st_ref = None
    if has_state:
        st_ref = refs[pos]
        pos += 1
    fw_ref = None
    if final_norm:
        fw_ref = refs[pos]
        pos += 1
    o_ref, nst_ref, xn_ref, gext_ref, carry_ref = refs[pos:pos + 5]

    hist = (FFN_CONV_WIDTH - 1) * nb
    pad = _round_up(hist, SUBLANE)
    tm = x_ref.shape[0]
    tf = wg_ref.shape[3]
    rsz = min(FFN_ROW_SPLIT, tm // 2)
    i = pl.program_id(0)
    f = pl.program_id(1)
    n_f = pl.num_programs(1)

    @pl.when(f == 0)
    def _():
        x = x_ref[...]
        xn_ref[...] = _rmsnorm_rows(x, nw_ref[0]).astype(BF16)
        o_ref[...] = x

    if tiles_per_seq > 1:
        first = (i % tiles_per_seq) == 0

        @pl.when(first)
        def _():
            if has_state:
                gext_ref[pad - hist:pad, :] = st_ref[0]
            else:
                gext_ref[0:pad, :] = jnp.zeros((pad, tf), F32)

        @pl.when(jnp.logical_not(first))
        def _():
            gext_ref[0:pad, :] = carry_ref[f]
    else:
        if has_state:
            gext_ref[pad - hist:pad, :] = st_ref[0]
        else:
            gext_ref[0:pad, :] = jnp.zeros((pad, tf), F32)

    cw0 = cw_ref[0, 0:1, :]
    cw1 = cw_ref[0, 1:2, :]
    cw2 = cw_ref[0, 2:3, :]
    cb = cb_ref[0]
    for r0 in range(0, tm, rsz):
        xn = xn_ref[r0:r0 + rsz, :]
        g = _dot(xn, wg_ref[0, 0])
        up = _dot(xn, wu_ref[0, 0])
        gext_ref[pad + r0:pad + r0 + rsz, :] = g
        conv = (cw0 * gext_ref[pl.ds(pad - 2 * nb + r0, rsz), :]
                + cw1 * gext_ref[pl.ds(pad - nb + r0, rsz), :]
                + cw2 * g + cb)
        act = _silu(conv) * up
        o_ref[r0:r0 + rsz, :] += _dot(act.astype(BF16), wd_ref[0])

    nst_ref[0] = gext_ref[pl.ds(pad + tm - hist, hist), :]
    if tiles_per_seq > 1:
        carry_ref[f] = gext_ref[pl.ds(tm, pad), :]

    if final_norm:
        @pl.when(f == n_f - 1)
        def _():
            o_ref[...] = _rmsnorm_rows(o_ref[...], fw_ref[...])


def conv_ffn(x, p, layer, state, final_w, *, n_seq, nb, tm, tiles_per_seq):
    m = x.shape[0]
    tf = FF_TILE
    n_f = D_FF_PAD // tf
    hist = (FFN_CONV_WIDTH - 1) * nb
    pad = _round_up(hist, SUBLANE)
    has_state = state is not None
    final_norm = final_w is not None
    in_specs = [pl.BlockSpec((tm, D_MODEL), lambda i, f: (i, 0)),
                _layer_spec((1, D_MODEL), layer),
                pl.BlockSpec((1, 1, D_MODEL, tf), lambda i, f: (layer, f, 0, 0)),
                pl.BlockSpec((1, 1, D_MODEL, tf), lambda i, f: (layer, f, 0, 0)),
                pl.BlockSpec((1, FFN_CONV_WIDTH, tf), lambda i, f: (layer, 0, f)),
                pl.BlockSpec((1, 1, tf), lambda i, f: (layer, 0, f)),
                pl.BlockSpec((1, tf, D_MODEL), lambda i, f: (layer, f, 0))]
    args = [x, p['norm_ffn_w'], p['ffn_wg'], p['ffn_wu'], p['ffn_conv_w'], p['ffn_conv_b'], p['ffn_wd']]
    if has_state:
        assert n_seq == 1 and tiles_per_seq == 1
        in_specs.append(pl.BlockSpec((1, hist, tf), lambda i, f: (layer, 0, f)))
        args.append(state)
    if final_norm:
        in_specs.append(pl.BlockSpec((1, D_MODEL), lambda i, f: (0, 0)))
        args.append(final_w.reshape(1, D_MODEL))
    return pl.pallas_call(
        functools.partial(_ffn_kernel, nb=nb, tiles_per_seq=tiles_per_seq, has_state=has_state,
                          final_norm=final_norm),
        out_shape=(jax.ShapeDtypeStruct((m, D_MODEL), F32),
                   jax.ShapeDtypeStruct((m // tm, hist, D_FF_PAD), F32)),
        grid=(m // tm, n_f),
        in_specs=in_specs,
        out_specs=(pl.BlockSpec((tm, D_MODEL), lambda i, f: (i, 0)),
                   pl.BlockSpec((1, hist, tf), lambda i, f: (i, 0, f))),
        scratch_shapes=[pltpu.VMEM((tm, D_MODEL), BF16),
                        pltpu.VMEM((pad + tm, tf), F32),
                        pltpu.VMEM((n_f, pad, tf), F32)],
        compiler_params=_cp("arbitrary", "arbitrary"),
        name="conv_ffn",
    )(*args)


def _s5_params(lam_re, lam_im, log_dt, b_re, b_im, c_re, c_im):
    depth = lam_re.shape[0]
    dt = jnp.exp(log_dt)[..., None]
    mag = jnp.exp(lam_re * dt)
    ang = lam_im * dt
    ab_re, ab_im = mag * jnp.cos(ang), mag * jnp.sin(ang)
    blocks = lambda re, im: jnp.concatenate(
        [re.reshape(*re.shape[:-2], S5_BLOCKS, LANE), im.reshape(*im.shape[:-2], S5_BLOCKS, LANE)], axis=-2)
    row = jnp.arange(SUBLANE, dtype=F32)
    expo = jnp.stack([jnp.full((SUBLANE,), 1.0), jnp.full((SUBLANE,), 2.0), jnp.full((SUBLANE,), 4.0),
                      row + 1.0])
    keep = jnp.stack([row >= 1, row >= 2, row >= 4, row >= 0]).astype(F32)
    lam_dt = (lam_re * dt).reshape(depth, 1, 1, S5_LANES)
    pang = ang.reshape(depth, 1, 1, S5_LANES) * expo[None, :, :, None]
    pmag = jnp.exp(lam_dt * expo[None, :, :, None]) * keep[None, :, :, None]
    tab = jnp.concatenate([pmag * jnp.cos(pang), pmag * jnp.sin(pang)], axis=-1)
    den = lam_re * lam_re + lam_im * lam_im
    nr, ni = ab_re - 1.0, ab_im
    co_re = (nr * lam_re + ni * lam_im) / den
    co_im = (ni * lam_re - nr * lam_im) / den
    bb_re = co_re[..., None] * b_re - co_im[..., None] * b_im
    bb_im = co_re[..., None] * b_im + co_im[..., None] * b_re
    gps = S5_GROUPS // S5_SUPER
    eye = jnp.eye(gps, dtype=F32)
    sup = lambda m: m.reshape(depth, S5_SUPER, gps, *m.shape[2:])
    dense_b = lambda m: jnp.einsum('lsgph,gk->lsghkp', sup(m), eye).reshape(depth, S5_SUPER, S5_SUP_CH, S5_SUP_ST)
    dense_c = lambda m: jnp.einsum('lsghp,gk->lskpgh', sup(m), eye).reshape(depth, S5_SUPER, S5_SUP_ST, S5_SUP_CH)
    bb = jnp.concatenate([dense_b(bb_re), dense_b(bb_im)], axis=3).astype(BF16)
    cc = jnp.concatenate([dense_c(c_re), -dense_c(c_im)], axis=2).astype(BF16)
    return bb, blocks(ab_re, ab_im), tab, cc


def _wprep_kernel(w_ref, o_ref, *, axis, valid_last):
    f = pl.program_id(1)
    last = pl.num_programs(1) - 1
    o = o_ref.at[0, 0] if axis == 1 else o_ref.at[0]

    @pl.when(f < last)
    def _():
        o[...] = w_ref[0].astype(BF16)

    @pl.when(f == last)
    def _():
        if axis == 1:
            o[:, :valid_last] = w_ref[0, :, :valid_last].astype(BF16)
            o[:, valid_last:] = jnp.zeros((o.shape[0], o.shape[1] - valid_last), BF16)
        else:
            o[:valid_last, :] = w_ref[0, :valid_last, :].astype(BF16)
            o[valid_last:, :] = jnp.zeros((o.shape[0] - valid_last, o.shape[1]), BF16)


def ffn_weight_cols(w, tf):
    depth, k, n = w.shape
    n_f = pl.cdiv(n, tf)
    return pl.pallas_call(
        functools.partial(_wprep_kernel, axis=1, valid_last=n - (n_f - 1) * tf),
        out_shape=jax.ShapeDtypeStruct((depth, n_f, k, tf), BF16),
        grid=(depth, n_f),
        in_specs=[pl.BlockSpec((1, k, tf), lambda l, f: (l, 0, f))],
        out_specs=pl.BlockSpec((1, 1, k, tf), lambda l, f: (l, f, 0, 0)),
        compiler_params=_cp("parallel", "parallel"),
        name="ffn_weight_cols",
    )(w)


def ffn_weight_rows(w, tf):
    depth, k, n = w.shape
    n_f = pl.cdiv(k, tf)
    return pl.pallas_call(
        functools.partial(_wprep_kernel, axis=0, valid_last=k - (n_f - 1) * tf),
        out_shape=jax.ShapeDtypeStruct((depth, n_f * tf, n), BF16),
        grid=(depth, n_f),
        in_specs=[pl.BlockSpec((1, tf, n), lambda l, f: (l, f, 0))],
        out_specs=pl.BlockSpec((1, tf, n), lambda l, f: (l, f, 0)),
        compiler_params=_cp("parallel", "parallel"),
        name="ffn_weight_rows",
    )(w)


def kernel(x_prompt, x_sample, mem_prompt, cache_mem_k, cache_mem_v, state_conv_a, state_s5_re, state_s5_im, state_conv_c, state_ssd, state_ffn_conv, norm_mix_w, w_in, conv_a_w, conv_a_b, ln_a_w, ln_a_b, s5_lam_re, s5_lam_im, s5_log_dt, s5_b_re, s5_b_im, s5_c_re, s5_c_im, s5_d, s5_glu_w, s5_glu_b, conv_c_w, conv_c_b, ssd_dt_bias, ssd_a_log, ssd_d, ssd_norm_w, w_out, norm_xa_w, norm_mem_w, xa_wq, xa_wk, xa_wv, xa_wo, norm_ffn_w, ffn_w_gate, ffn_w_up, ffn_conv_w, ffn_conv_b, ffn_w_down, final_norm_w):
    bp, seq, _ = x_prompt.shape
    nbs, lts, _ = x_sample.shape
    depth = w_in.shape[0]
    n_mem = mem_prompt.shape[1]
    lt_p = 512 if seq % 512 == 0 else seq
    n_tiles_p = seq // lt_p
    tm_p = lt_p
    tm_s = lts * nbs
    tm_f = 1024 if seq % 1024 == 0 else tm_p
    tm_m = min(256, bp * n_mem)

    vec = lambda a: a.reshape(depth, 1, a.shape[-1])
    pad_lanes = lambda a: vec(jnp.pad(a, ((0, 0), (0, LANE - a.shape[-1]))))
    ff_pad = D_FF_PAD - D_FF
    s5_bb, s5_ab, s5_tab, s5_cc = _s5_params(s5_lam_re, s5_lam_im, s5_log_dt, s5_b_re, s5_b_im, s5_c_re, s5_c_im)
    ssd_e, ssd_tril = _ssd_consts()
    p = {
        'norm_mix_w': vec(norm_mix_w), 'norm_xa_w': vec(norm_xa_w), 'norm_mem_w': vec(norm_mem_w),
        'norm_ffn_w': vec(norm_ffn_w),
        'w_in': jnp.pad(w_in, ((0, 0), (0, 0), (0, IN_SPLITS[2] - w_in.shape[2]))).astype(BF16),
        'conv_a_w': conv_a_w, 'conv_a_b': vec(conv_a_b), 'ln_a_w': vec(ln_a_w), 'ln_a_b': vec(ln_a_b),
        's5_bb': s5_bb, 's5_ab': s5_ab, 's5_tab': s5_tab, 's5_cc': s5_cc,
        's5_d': vec(s5_d), 's5_glu_w': s5_glu_w.astype(BF16), 's5_glu_b': vec(s5_glu_b),
        'conv_c_w': conv_c_w, 'conv_c_b': vec(conv_c_b),
        'ssd_dt_bias': pad_lanes(ssd_dt_bias), 'ssd_a_log': pad_lanes(ssd_a_log),
        'ssd_d': vec(jnp.repeat(ssd_d, SSD_HEAD_DIM, axis=1)), 'ssd_norm_w': vec(ssd_norm_w),
        'ssd_e': ssd_e, 'ssd_tril': ssd_tril,
        'w_out': w_out.astype(BF16), 'wq': xa_wq.astype(BF16), 'wo': xa_wo.astype(BF16),
        'wk': xa_wk.astype(BF16), 'wv': xa_wv.astype(BF16),
        'ffn_wg': ffn_weight_cols(ffn_w_gate, FF_TILE),
        'ffn_wu': ffn_weight_cols(ffn_w_up, FF_TILE),
        'ffn_wd': ffn_weight_rows(ffn_w_down, FF_TILE),
        'ffn_conv_w': jnp.pad(ffn_conv_w, ((0, 0), (0, 0), (0, ff_pad))),
        'ffn_conv_b': vec(jnp.pad(ffn_conv_b, ((0, 0), (0, ff_pad)))),
    }

    tmaj = lambda a: a.transpose(0, 2, 1, 3).reshape(depth, a.shape[2] * nbs, a.shape[3])
    st_conv_a = tmaj(state_conv_a)
    st_conv_c = tmaj(state_conv_c)
    st_ffn = jnp.pad(tmaj(state_ffn_conv), ((0, 0), (0, 0), (0, ff_pad)))
    st_re = state_s5_re.reshape(depth, nbs, S5_LANES)
    st_im = state_s5_im.reshape(depth, nbs, S5_LANES)
    ssd_all = state_ssd.reshape(depth, nbs, D_C, SSD_STATE)

    xp = x_prompt.reshape(bp * seq, D_MODEL)
    xs = x_sample.transpose(1, 0, 2).reshape(lts * nbs, D_MODEL)
    mem2d = mem_prompt.reshape(bp * n_mem, D_MODEL)

    def mixers(x, l, *, n_seq, nb, lt, n_tiles, tm, sample):
        if sample:
            h_a, h_b, h_c = in_proj(x, p['norm_mix_w'], p['w_in'], l, tm=tm)
            ya, n_conv_a = conva_mixer(h_a, p, l, st_conv_a, n_seq=n_seq, nb=nb, lt=lt, n_tiles=n_tiles)
        else:
            ya, h_b, h_c, n_conv_a = in_proj_conva(x, p, l, tm=tm, tiles_per_seq=n_tiles)
            n_conv_a = n_conv_a[n_tiles - 1::n_tiles]
        yb, n_re, n_im = s5_mixer(h_b, p, l, st_re if sample else None, st_im if sample else None,
                                  n_seq=n_seq, nb=nb, lt=lt, n_tiles=n_tiles)
        return h_c, ya, yb, n_conv_a, n_re, n_im

    mk, mv, p_mk, p_mv = mem_kv(mem2d, p['norm_mem_w'], p['wk'], p['wv'], tm=tm_m)
    mk = mk.reshape(depth * bp, n_mem, D_MODEL)
    mv = mv.reshape(depth * bp, n_mem, D_MODEL)
    tq_p = 1024 if seq % 1024 == 0 else lt_p

    outs_p = [[] for _ in range(6)]
    outs_s = [[] for _ in range(5)]
    for l in range(depth):
        last = l == depth - 1

        h_c, ya, yb, p_conv_a, p_re, p_im = mixers(xp, l, n_seq=bp, nb=1, lt=lt_p, n_tiles=n_tiles_p,
                                                   tm=tm_p, sample=False)
        yc, p_conv_c, p_ssd = mamba_prompt(h_c, p, l, n_seq=bp, lt=lt_p, n_tiles=n_tiles_p)
        xp, q = out_q_proj(ya, yb, yc, p['w_out'], xp, p['norm_xa_w'], p['wq'], l, tm=tm_p)
        o = attn_prompt(q, mk, mv, l, n_seq=bp, seq=seq, tq=tq_p)
        xp = proj_res([o], p['wo'], l, xp, tm=tm_p, name="attn_out")
        xp, p_ffn = conv_ffn(xp, p, l, None, final_norm_w if last else None, n_seq=bp, nb=1, tm=tm_f,
                             tiles_per_seq=seq // tm_f)
        for lst, v in zip(outs_p, (p_conv_a, p_re, p_im, p_conv_c, p_ssd,
                                   p_ffn[seq // tm_f - 1::seq // tm_f])):
            lst.append(v)

        h_c, ya, yb, s_conv_a, s_re, s_im = mixers(xs, l, n_seq=1, nb=nbs, lt=lts, n_tiles=1, tm=tm_s,
                                                   sample=True)
        yc, s_conv_c, ssd_all = mamba_sample(h_c, p, l, st_conv_c, ssd_all, nb=nbs, lt=lts,
                                             bb=8 if l > 0 else 4, in_place=l > 0)
        xs, q = out_q_proj(ya, yb, yc, p['w_out'], xs, p['norm_xa_w'], p['wq'], l, tm=tm_s,
                           q_by_seq=(nbs, lts))
        o = attn_sample(q, cache_mem_k, cache_mem_v, l, bb=4)
        xs = attn_out_seq(o, p['wo'], l, xs, nb=nbs, lt=lts)
        xs, s_ffn = conv_ffn(xs, p, l, st_ffn, final_norm_w if last else None, n_seq=1, nb=nbs, tm=tm_s,
                             tiles_per_seq=1)
        for lst, v in zip(outs_s, (s_conv_a[0], s_re[0], s_im[0], s_conv_c, s_ffn[0])):
            lst.append(v)

    p_conv_a, p_re, p_im, p_conv_c, p_ssd, p_ffn = [jnp.stack(o) for o in outs_p]
    s_conv_a, s_re, s_im, s_conv_c, s_ffn = [jnp.stack(o) for o in outs_s]
    bmaj = lambda a, w: a.reshape(depth, w, nbs, a.shape[-1]).transpose(0, 2, 1, 3)
    y_prompt = xp.reshape(bp, seq, D_MODEL)
    y_sample = xs.reshape(lts, nbs, D_MODEL).transpose(1, 0, 2)
    return (y_prompt, y_sample,
            p_conv_a,
            p_re.reshape(depth, bp, S5_GROUPS, S5_STATE), p_im.reshape(depth, bp, S5_GROUPS, S5_STATE),
            p_conv_c,
            p_ssd.reshape(depth, bp, SSD_HEADS, SSD_HEAD_DIM, SSD_STATE),
            p_ffn[..., :D_FF],
            p_mk.reshape(depth, bp, n_mem, XA_HEADS, XA_HEAD_DIM),
            p_mv.reshape(depth, bp, n_mem, XA_HEADS, XA_HEAD_DIM),
            bmaj(s_conv_a, CONV_A_WIDTH - 1),
            s_re.reshape(depth, nbs, S5_GROUPS, S5_STATE), s_im.reshape(depth, nbs, S5_GROUPS, S5_STATE),
            bmaj(s_conv_c, SSD_CONV_WIDTH - 1),
            ssd_all.reshape(state_ssd.shape),
            bmaj(s_ffn, FFN_CONV_WIDTH - 1)[..., :D_FF])
```

```python
import functools
import math

import jax
import jax.numpy as jnp
from jax import lax
from jax.experimental import pallas as pl
from jax.experimental.pallas import tpu as pltpu

F32 = jnp.float32
BF16 = jnp.bfloat16
EPS = 1e-6

D_MODEL = 2048
D_A = 512
D_B = 512
D_C = 1024
CONV_A_WIDTH = 31
S5_GROUP = 16
S5_GROUPS = 32
S5_STATE = 64
S5_LANES = S5_GROUPS * S5_STATE
SSD_HEAD_DIM = 64
SSD_HEADS = 16
SSD_GROUPS = 2
SSD_STATE = 128
SSD_CONV_WIDTH = 4
SSD_CHUNK = 128
D_XBC = D_C + 2 * SSD_GROUPS * SSD_STATE
D_HC = D_C + D_XBC + 128
XA_HEADS = 4
XA_HEAD_DIM = 512
N_MEM = 256
D_FF = 5504
FFN_CONV_WIDTH = 3

LANE = 128
SUBLANE = 8
VMEM_LIMIT = 56 * 1024 * 1024
FF_TILE = 512
D_FF_PAD = ((D_FF + FF_TILE - 1) // FF_TILE) * FF_TILE


def _round_up(x, m):
    return (x + m - 1) // m * m


def _cp(*sem):
    return pltpu.CompilerParams(dimension_semantics=sem, vmem_limit_bytes=VMEM_LIMIT)


def _layer_spec(tail, layer):
    zeros = (0,) * len(tail)
    return pl.BlockSpec((1,) + tuple(tail), lambda *_: (layer,) + zeros)


def _dot(a, b):
    return jnp.dot(a, b, preferred_element_type=F32)


def _dot_nt(a, b):
    return lax.dot_general(a, b, (((1,), (1,)), ((), ())), preferred_element_type=F32)


def _dot_tn(a, b):
    return lax.dot_general(a, b, (((0,), (0,)), ((), ())), preferred_element_type=F32)


def _split3(a):
    hi = a.astype(BF16)
    r = a - hi.astype(F32)
    mid = r.astype(BF16)
    lo = (r - mid.astype(F32)).astype(BF16)
    return hi, mid, lo


def _expand(a, e):
    hi, mid, lo = _split3(a)
    return _dot(hi, e) + _dot(mid, e) + _dot(lo, e)


def _sigmoid(x):
    return jax.nn.sigmoid(x)


def _silu(x):
    return x * jax.nn.sigmoid(x)


def _softplus(x):
    return jnp.maximum(x, 0.0) + jnp.log1p(jnp.exp(-jnp.abs(x)))


def _rmsnorm_rows(x, w):
    ms = jnp.mean(x * x, axis=-1, keepdims=True)
    return x * lax.rsqrt(ms + EPS) * w


def _mem_kv_kernel(x_ref, nw_ref, wk_ref, wv_ref, k_ref, v_ref, k4_ref, v4_ref, xn_ref):
    j = pl.program_id(2)

    def emit(w_ref, o_ref, o4_ref):
        y = _dot(xn_ref[...], w_ref[0])
        o_ref[0] = y
        for h in range(XA_HEADS):
            o4_ref[0, :, h, :] = y[:, h * XA_HEAD_DIM:(h + 1) * XA_HEAD_DIM]

    @pl.when(j == 0)
    def _():
        xn_ref[...] = _rmsnorm_rows(x_ref[...], nw_ref[0]).astype(BF16)
        emit(wk_ref, k_ref, k4_ref)

    @pl.when(j == 1)
    def _():
        emit(wv_ref, v_ref, v4_ref)


def mem_kv(x, nw, wk, wv, *, tm):
    m, k = x.shape
    depth, _, n = wk.shape
    flat = jax.ShapeDtypeStruct((depth, m, n), F32)
    split = jax.ShapeDtypeStruct((depth, m, XA_HEADS, XA_HEAD_DIM), F32)
    w_spec = pl.BlockSpec((1, k, n), lambda l, i, j: (l, 0, 0), pipeline_mode=pl.Buffered(1))
    flat_spec = pl.BlockSpec((1, tm, n), lambda l, i, j: (l, i, 0))
    split_spec = pl.BlockSpec((1, tm, XA_HEADS, XA_HEAD_DIM), lambda l, i, j: (l, i, 0, 0))
    return pl.pallas_call(
        _mem_kv_kernel,
        out_shape=(flat, flat, split, split),
        grid=(depth, m // tm, 2),
        in_specs=[pl.BlockSpec((tm, k), lambda l, i, j: (i, 0)),
                  pl.BlockSpec((1, 1, k), lambda l, i, j: (l, 0, 0)),
                  w_spec, w_spec],
        out_specs=(flat_spec, flat_spec, split_spec, split_spec),
        scratch_shapes=[pltpu.VMEM((tm, k), BF16)],
        compiler_params=_cp("parallel", "parallel", "arbitrary"),
        name="mem_kv",
    )(x, nw, wk, wv)


def _attn_out_seq_kernel(o_ref, w_ref, res_ref, out_ref, a_ref, *, nb, lt):
    for t in range(lt):
        for h in range(XA_HEADS):
            a_ref[t * nb:(t + 1) * nb, h * XA_HEAD_DIM:(h + 1) * XA_HEAD_DIM] = o_ref[:, h * lt + t, :]
    out_ref[...] = res_ref[...] + _dot(a_ref[...].astype(BF16), w_ref[0])


def attn_out_seq(o, w, layer, res, *, nb, lt):
    m, n = res.shape
    return pl.pallas_call(
        functools.partial(_attn_out_seq_kernel, nb=nb, lt=lt),
        out_shape=jax.ShapeDtypeStruct((m, n), F32),
        grid=(1,),
        in_specs=[pl.BlockSpec(o.shape, lambda i: (0, 0, 0)),
                  _layer_spec(w.shape[1:], layer),
                  pl.BlockSpec((m, n), lambda i: (0, 0))],
        out_specs=pl.BlockSpec((m, n), lambda i: (0, 0)),
        scratch_shapes=[pltpu.VMEM((m, n), F32)],
        compiler_params=_cp("arbitrary"),
        name="attn_out_seq",
    )(o, w, res)


IN_SPLITS = (2 * D_A, 2 * D_A + D_B, 2 * D_A + D_B + D_HC)


def _in_proj_kernel(x_ref, nw_ref, w_ref, ha_ref, hb_ref, hc_ref):
    xn = _rmsnorm_rows(x_ref[...], nw_ref[0]).astype(BF16)
    ha_ref[...] = _dot(xn, w_ref[0, :, 0:IN_SPLITS[0]])
    hb_ref[...] = _dot(xn, w_ref[0, :, IN_SPLITS[0]:IN_SPLITS[1]])
    hc_ref[...] = _dot(xn, w_ref[0, :, IN_SPLITS[1]:IN_SPLITS[2]])


def in_proj(x, nw, w, layer, *, tm):
    m, k = x.shape
    widths = (IN_SPLITS[0], IN_SPLITS[1] - IN_SPLITS[0], IN_SPLITS[2] - IN_SPLITS[1])
    return pl.pallas_call(
        _in_proj_kernel,
        out_shape=tuple(jax.ShapeDtypeStruct((m, wd), F32) for wd in widths),
        grid=(m // tm,),
        in_specs=[pl.BlockSpec((tm, k), lambda i: (i, 0)),
                  _layer_spec((1, k), layer),
                  pl.BlockSpec((1, k, IN_SPLITS[2]), lambda i: (layer, 0, 0), pipeline_mode=pl.Buffered(1))],
        out_specs=tuple(pl.BlockSpec((tm, wd), lambda i: (i, 0)) for wd in widths),
        compiler_params=_cp("parallel"),
        name="in_proj",
    )(x, nw, w)


def _proj_res_kernel(*refs, n_in):
    a_refs = refs[:n_in]
    w_refs = refs[n_in:2 * n_in]
    res_ref, o_ref = refs[2 * n_in], refs[2 * n_in + 1]
    acc = res_ref[...]
    for a_ref, w_ref in zip(a_refs, w_refs):
        acc = acc + _dot(a_ref[...].astype(BF16), w_ref[0])
    o_ref[...] = acc


def proj_res(a_list, w, layer, res, *, tm, name):
    m, n = res.shape
    n_in = len(a_list)
    in_specs = [pl.BlockSpec((tm, a.shape[1]), lambda i: (i, 0)) for a in a_list]
    row0 = 0
    for a in a_list:
        kk = a.shape[1]
        assert row0 % kk == 0
        in_specs.append(pl.BlockSpec((1, kk, n), lambda i, blk=row0 // kk: (layer, blk, 0)))
        row0 += kk
    in_specs.append(pl.BlockSpec((tm, n), lambda i: (i, 0)))
    return pl.pallas_call(
        functools.partial(_proj_res_kernel, n_in=n_in),
        out_shape=jax.ShapeDtypeStruct((m, n), F32),
        grid=(m // tm,),
        in_specs=in_specs,
        out_specs=pl.BlockSpec((tm, n), lambda i: (i, 0)),
        compiler_params=_cp("parallel"),
        name=name,
    )(*a_list, *([w] * n_in), res)


def _out_q_kernel(ya_ref, yb_ref, yc_ref, w_ref, res_ref, nw_ref, wq_ref, x_ref, q_ref, *, q_by_seq):
    x = res_ref[...]
    row0 = 0
    for y_ref in (ya_ref, yb_ref, yc_ref):
        kk = y_ref.shape[1]
        x = x + _dot(y_ref[...], w_ref[0, row0:row0 + kk, :])
        row0 += kk
    x_ref[...] = x
    q = _dot(_rmsnorm_rows(x, nw_ref[0]).astype(BF16), wq_ref[0])
    if q_by_seq is None:
        q_ref[...] = q
    else:
        nb, lt = q_by_seq
        for t in range(lt):
            for h in range(XA_HEADS):
                q_ref[:, h * lt + t, :] = q[t * nb:(t + 1) * nb, h * XA_HEAD_DIM:(h + 1) * XA_HEAD_DIM]


def out_q_proj(ya, yb, yc, w_out, res, nw, wq, layer, *, tm, q_by_seq=None):
    m, n = res.shape
    resident = lambda a: pl.BlockSpec((1,) + a.shape[1:], lambda i: (layer, 0, 0), pipeline_mode=pl.Buffered(1))
    if q_by_seq is None:
        q_shape = (m, wq.shape[2])
        q_spec = pl.BlockSpec((tm, wq.shape[2]), lambda i: (i, 0))
    else:
        assert tm == m == q_by_seq[0] * q_by_seq[1]
        q_shape = (q_by_seq[0], XA_HEADS * q_by_seq[1], XA_HEAD_DIM)
        q_spec = pl.BlockSpec(q_shape, lambda i: (0, 0, 0))
    return pl.pallas_call(
        functools.partial(_out_q_kernel, q_by_seq=q_by_seq),
        out_shape=(jax.ShapeDtypeStruct((m, n), F32), jax.ShapeDtypeStruct(q_shape, F32)),
        grid=(m // tm,),
        in_specs=[pl.BlockSpec((tm, ya.shape[1]), lambda i: (i, 0)),
                  pl.BlockSpec((tm, yb.shape[1]), lambda i: (i, 0)),
                  pl.BlockSpec((tm, yc.shape[1]), lambda i: (i, 0)),
                  resident(w_out),
                  pl.BlockSpec((tm, n), lambda i: (i, 0)),
                  _layer_spec((1, n), layer),
                  resident(wq)],
        out_specs=(pl.BlockSpec((tm, n), lambda i: (i, 0)), q_spec),
        compiler_params=_cp("parallel"),
        name="out_q_proj",
    )(ya, yb, yc, w_out, res, nw, wq)


CONVA_ROW_CHUNK = 32


def _conva_tile(ext_ref, cls_ref, w_ref, b_ref, lnw_ref, lnb_ref, y_ref, *, nb, rows):
    hist = (CONV_A_WIDTH - 1) * nb
    pad = _round_up(hist, SUBLANE)
    bias = b_ref[0]
    lnw = lnw_ref[0]
    lnb = lnb_ref[0]
    rc = CONVA_ROW_CHUNK
    if nb == 1:
        for s in range(SUBLANE):
            span = rows + SUBLANE * ((CONV_A_WIDTH - 1 - s) // SUBLANE)
            cls_ref[s, 0:span, :] = ext_ref[pl.ds(pad - hist + s, span), :]
    for r0 in range(0, rows, rc):
        acc = jnp.zeros((rc, D_A), F32) + bias
        if nb == 1:
            for k in range(CONV_A_WIDTH):
                s, jt = k % SUBLANE, k // SUBLANE
                acc = acc + w_ref[0, k:k + 1, :] * cls_ref[s, r0 + SUBLANE * jt:r0 + SUBLANE * jt + rc, :]
        else:
            for k in range(CONV_A_WIDTH):
                acc = acc + w_ref[0, k:k + 1, :] * ext_ref[pl.ds(pad - hist + k * nb + r0, rc), :]
        mu = jnp.mean(acc, axis=-1, keepdims=True)
        xc = acc - mu
        var = jnp.mean(xc * xc, axis=-1, keepdims=True)
        c = xc * lax.rsqrt(var + EPS) * lnw + lnb
        y_ref[r0:r0 + rc, :] = _silu(c).astype(y_ref.dtype)
    return ext_ref[pl.ds(pad + rows - hist, hist), :]


def _conva_scratch(nb, rows):
    hist = (CONV_A_WIDTH - 1) * nb
    pad = _round_up(hist, SUBLANE)
    shapes = [pltpu.VMEM((pad + rows, D_A), F32)]
    if nb == 1:
        shapes.append(pltpu.VMEM((SUBLANE, rows + SUBLANE * ((CONV_A_WIDTH - 1) // SUBLANE), D_A), F32))
    return shapes


def _conva_kernel(*refs, nb, lt, n_tiles, has_state):
    refs = list(refs)
    cls_ref = refs.pop() if nb == 1 else None
    if has_state:
        h_ref, w_ref, b_ref, lnw_ref, lnb_ref, st_ref, y_ref, nst_ref, ext_ref = refs
    else:
        h_ref, w_ref, b_ref, lnw_ref, lnb_ref, y_ref, nst_ref, ext_ref = refs
    hist = (CONV_A_WIDTH - 1) * nb
    pad = _round_up(hist, SUBLANE)
    rows = lt * nb
    j = pl.program_id(1)

    @pl.when(j == 0)
    def _():
        if has_state:
            ext_ref[pad - hist:pad, :] = st_ref[0]
        else:
            ext_ref[0:pad, :] = jnp.zeros((pad, D_A), F32)

    ext_ref[pad:pad + rows, :] = h_ref[:, 0:D_A] * _sigmoid(h_ref[:, D_A:2 * D_A])
    new_hist = _conva_tile(ext_ref, cls_ref, w_ref, b_ref, lnw_ref, lnb_ref, y_ref, nb=nb, rows=rows)
    nst_ref[0] = new_hist
    if n_tiles > 1:
        ext_ref[pad - hist:pad, :] = new_hist


def conva_mixer(h_a, p, layer, state, *, n_seq, nb, lt, n_tiles):
    rows = lt * nb
    hist = (CONV_A_WIDTH - 1) * nb
    pad = _round_up(hist, SUBLANE)
    has_state = state is not None
    in_specs = [pl.BlockSpec((rows, 2 * D_A), lambda s, j: (s * n_tiles + j, 0)),
                _layer_spec((CONV_A_WIDTH, D_A), layer),
                _layer_spec((1, D_A), layer), _layer_spec((1, D_A), layer), _layer_spec((1, D_A), layer)]
    args = [h_a, p['conv_a_w'], p['conv_a_b'], p['ln_a_w'], p['ln_a_b']]
    if has_state:
        assert n_seq == 1
        in_specs.append(_layer_spec((hist, D_A), layer))
        args.append(state)
    return pl.pallas_call(
        functools.partial(_conva_kernel, nb=nb, lt=lt, n_tiles=n_tiles, has_state=has_state),
        out_shape=(jax.ShapeDtypeStruct((h_a.shape[0], D_A), BF16),
                   jax.ShapeDtypeStruct((n_seq, hist, D_A), F32)),
        grid=(n_seq, n_tiles),
        in_specs=in_specs,
        out_specs=(pl.BlockSpec((rows, D_A), lambda s, j: (s * n_tiles + j, 0)),
                   pl.BlockSpec((1, hist, D_A), lambda s, j: (s, 0, 0))),
        scratch_shapes=_conva_scratch(nb, rows),
        compiler_params=_cp("parallel", "arbitrary"),
        name="conva_mixer",
    )(*args)


def _in_proj_conva_kernel(x_ref, nw_ref, w_ref, cw_ref, cb_ref, lnw_ref, lnb_ref,
                          ya_ref, hb_ref, hc_ref, nst_ref, ext_ref, cls_ref, *, tiles_per_seq):
    rows = x_ref.shape[0]
    hist = CONV_A_WIDTH - 1
    pad = _round_up(hist, SUBLANE)

    @pl.when(pl.program_id(0) % tiles_per_seq == 0)
    def _():
        ext_ref[0:pad, :] = jnp.zeros((pad, D_A), F32)

    xn = _rmsnorm_rows(x_ref[...], nw_ref[0]).astype(BF16)
    h_a = _dot(xn, w_ref[0, :, 0:IN_SPLITS[0]])
    ext_ref[pad:pad + rows, :] = h_a[:, 0:D_A] * _sigmoid(h_a[:, D_A:2 * D_A])
    hb_ref[...] = _dot(xn, w_ref[0, :, IN_SPLITS[0]:IN_SPLITS[1]])
    hc_ref[...] = _dot(xn, w_ref[0, :, IN_SPLITS[1]:IN_SPLITS[2]])
    new_hist = _conva_tile(ext_ref, cls_ref, cw_ref, cb_ref, lnw_ref, lnb_ref, ya_ref, nb=1, rows=rows)
    nst_ref[0] = new_hist
    ext_ref[pad - hist:pad, :] = new_hist


def in_proj_conva(x, p, layer, *, tm, tiles_per_seq):
    m, k = x.shape
    hist = CONV_A_WIDTH - 1
    wb, wc = IN_SPLITS[1] - IN_SPLITS[0], IN_SPLITS[2] - IN_SPLITS[1]
    return pl.pallas_call(
        functools.partial(_in_proj_conva_kernel, tiles_per_seq=tiles_per_seq),
        out_shape=(jax.ShapeDtypeStruct((m, D_A), BF16),
                   jax.ShapeDtypeStruct((m, wb), F32),
                   jax.ShapeDtypeStruct((m, wc), F32),
                   jax.ShapeDtypeStruct((m // tm, hist, D_A), F32)),
        grid=(m // tm,),
        in_specs=[pl.BlockSpec((tm, k), lambda i: (i, 0)),
                  _layer_spec((1, k), layer),
                  pl.BlockSpec((1, k, IN_SPLITS[2]), lambda i: (layer, 0, 0), pipeline_mode=pl.Buffered(1)),
                  _layer_spec((CONV_A_WIDTH, D_A), layer),
                  _layer_spec((1, D_A), layer), _layer_spec((1, D_A), layer), _layer_spec((1, D_A), layer)],
        out_specs=(pl.BlockSpec((tm, D_A), lambda i: (i, 0)),
                   pl.BlockSpec((tm, wb), lambda i: (i, 0)),
                   pl.BlockSpec((tm, wc), lambda i: (i, 0)),
                   pl.BlockSpec((1, hist, D_A), lambda i: (i, 0, 0))),
        scratch_shapes=_conva_scratch(1, tm),
        compiler_params=_cp("arbitrary"),
        name="in_proj_conva",
    )(x, p['norm_mix_w'], p['w_in'], p['conv_a_w'], p['conv_a_b'], p['ln_a_w'], p['ln_a_b'])


def _gelu_tanh(x):
    return x * (0.5 * (1.0 + jnp.tanh(math.sqrt(2.0 / math.pi) * (x + 0.044715 * (x * x * x)))))


S5_BLOCKS = S5_LANES // LANE
S5_SUPER = 2
S5_SUP_CH = D_B // S5_SUPER
S5_SUP_ST = S5_LANES // S5_SUPER


def _s5_b_proj(u, bb_ref, hs_ref):
    n = S5_LANES
    for sb in range(S5_SUPER):
        bu = _dot(u[:, sb * S5_SUP_CH:(sb + 1) * S5_SUP_CH].astype(BF16), bb_ref[0, sb])
        hs_ref[:, sb * S5_SUP_ST:(sb + 1) * S5_SUP_ST] = bu[:, 0:S5_SUP_ST]
        hs_ref[:, n + sb * S5_SUP_ST:n + (sb + 1) * S5_SUP_ST] = bu[:, S5_SUP_ST:2 * S5_SUP_ST]


def _s5_glu_out(hs_ref, u, cc_ref, d_ref, gw_ref, gb_ref, y_ref):
    n = S5_LANES
    ys = []
    for sb in range(S5_SUPER):
        h16 = jnp.concatenate([hs_ref[:, sb * S5_SUP_ST:(sb + 1) * S5_SUP_ST],
                               hs_ref[:, n + sb * S5_SUP_ST:n + (sb + 1) * S5_SUP_ST]], axis=1).astype(BF16)
        ys.append(_dot(h16, cc_ref[0, sb]))
    y = jnp.concatenate(ys, axis=1) + d_ref[0] * u
    y = _gelu_tanh(y)
    gate = _dot(y.astype(BF16), gw_ref[0]) + gb_ref[0]
    y_ref[...] = (y * _sigmoid(gate)).astype(y_ref.dtype)


def _s5_seq_kernel(u_ref, bb_ref, tab_ref, cc_ref, d_ref, gw_ref, gb_ref,
                   y_ref, nre_ref, nim_ref, hs_ref, cre_ref, cim_ref, *, lt):
    n = S5_LANES
    j = pl.program_id(1)

    @pl.when(j == 0)
    def _():
        cre_ref[...] = jnp.zeros(cre_ref.shape, F32)
        cim_ref[...] = jnp.zeros(cim_ref.shape, F32)

    u = u_ref[...]
    _s5_b_proj(u, bb_ref, hs_ref)

    def group(i, carry):
        r0 = pl.multiple_of(i * SUBLANE, SUBLANE)
        new = []
        for c in range(S5_BLOCKS):
            lr = slice(c * LANE, (c + 1) * LANE)
            li = slice(n + c * LANE, n + (c + 1) * LANE)
            xr = hs_ref[pl.ds(r0, SUBLANE), lr]
            xi = hs_ref[pl.ds(r0, SUBLANE), li]
            for lev in range(3):
                ar = tab_ref[0, lev, :, lr]
                ai = tab_ref[0, lev, :, li]
                sr = pltpu.roll(xr, 1 << lev, 0)
                si = pltpu.roll(xi, 1 << lev, 0)
                xr, xi = xr + ar * sr - ai * si, xi + ar * si + ai * sr
            pr = tab_ref[0, 3, :, lr]
            pi = tab_ref[0, 3, :, li]
            er, ei = carry[2 * c], carry[2 * c + 1]
            hr = xr + pr * er - pi * ei
            hi = xi + pr * ei + pi * er
            hs_ref[pl.ds(r0, SUBLANE), lr] = hr
            hs_ref[pl.ds(r0, SUBLANE), li] = hi
            new += [jnp.broadcast_to(hr[SUBLANE - 1:SUBLANE, :], (SUBLANE, LANE)),
                    jnp.broadcast_to(hi[SUBLANE - 1:SUBLANE, :], (SUBLANE, LANE))]
        return tuple(new)

    init = []
    for c in range(S5_BLOCKS):
        init += [cre_ref[:, c * LANE:(c + 1) * LANE], cim_ref[:, c * LANE:(c + 1) * LANE]]
    last = lax.fori_loop(0, lt // SUBLANE, group, tuple(init))
    for c in range(S5_BLOCKS):
        cre_ref[:, c * LANE:(c + 1) * LANE] = last[2 * c]
        cim_ref[:, c * LANE:(c + 1) * LANE] = last[2 * c + 1]
    nre_ref[0] = cre_ref[0:1, :]
    nim_ref[0] = cim_ref[0:1, :]
    _s5_glu_out(hs_ref, u, cc_ref, d_ref, gw_ref, gb_ref, y_ref)


def _s5_step_kernel(u_ref, bb_ref, ab_ref, cc_ref, d_ref, gw_ref, gb_ref, sre_ref, sim_ref,
                    y_ref, nre_ref, nim_ref, hs_ref, *, nb, lt):
    n = S5_LANES
    nblk = S5_BLOCKS
    u = u_ref[...]
    _s5_b_proj(u, bb_ref, hs_ref)
    ab_re = jnp.concatenate([ab_ref[0, c:c + 1, :] for c in range(nblk)], axis=1)
    ab_im = jnp.concatenate([ab_ref[0, nblk + c:nblk + c + 1, :] for c in range(nblk)], axis=1)
    hr = sre_ref[0]
    hi = sim_ref[0]
    for t in range(lt):
        rs = slice(t * nb, (t + 1) * nb)
        nr = ab_re * hr - ab_im * hi + hs_ref[rs, 0:n]
        ni = ab_re * hi + ab_im * hr + hs_ref[rs, n:2 * n]
        hr, hi = nr, ni
        hs_ref[rs, 0:n] = hr
        hs_ref[rs, n:2 * n] = hi
    nre_ref[0] = hr
    nim_ref[0] = hi
    _s5_glu_out(hs_ref, u, cc_ref, d_ref, gw_ref, gb_ref, y_ref)


def s5_mixer(h_b, p, layer, s_re, s_im, *, n_seq, nb, lt, n_tiles):
    rows = lt * nb
    n = S5_LANES
    has_state = s_re is not None
    in_specs = [pl.BlockSpec((rows, D_B), lambda s, j: (s * n_tiles + j, 0)),
                _layer_spec((S5_SUPER, S5_SUP_CH, 2 * S5_SUP_ST), layer),
                _layer_spec((2 * S5_BLOCKS, LANE), layer)]
    args = [h_b, p['s5_bb'], p['s5_ab']]
    if not has_state:
        in_specs[2] = _layer_spec((4, SUBLANE, 2 * n), layer)
        args[2] = p['s5_tab']
    in_specs += [_layer_spec((S5_SUPER, 2 * S5_SUP_ST, S5_SUP_CH), layer),
                 _layer_spec((1, D_B), layer),
                 _layer_spec((D_B, D_B), layer),
                 _layer_spec((1, D_B), layer)]
    args += [p['s5_cc'], p['s5_d'], p['s5_glu_w'], p['s5_glu_b']]
    if has_state:
        assert n_seq == 1 and n_tiles == 1
        in_specs += [_layer_spec((nb, n), layer)] * 2
        args += [s_re, s_im]
        body = functools.partial(_s5_step_kernel, nb=nb, lt=lt)
        scratch = [pltpu.VMEM((rows, 2 * n), F32)]
    else:
        assert nb == 1 and lt % SUBLANE == 0
        body = functools.partial(_s5_seq_kernel, lt=lt)
        scratch = [pltpu.VMEM((rows, 2 * n), F32),
                   pltpu.VMEM((SUBLANE, n), F32),
                   pltpu.VMEM((SUBLANE, n), F32)]
    st_spec = pl.BlockSpec((1, nb, n), lambda s, j: (s, 0, 0))
    return pl.pallas_call(
        body,
        out_shape=(jax.ShapeDtypeStruct((h_b.shape[0], D_B), BF16),
                   jax.ShapeDtypeStruct((n_seq, nb, n), F32),
                   jax.ShapeDtypeStruct((n_seq, nb, n), F32)),
        grid=(n_seq, n_tiles),
        in_specs=in_specs,
        out_specs=(pl.BlockSpec((rows, D_B), lambda s, j: (s * n_tiles + j, 0)), st_spec, st_spec),
        scratch_shapes=scratch,
        compiler_params=_cp("parallel", "arbitrary"),
        name="s5_mixer",
    )(*args)


def _group_rmsnorm(y, nw):
    half = D_C // SSD_GROUPS
    outs = []
    for g in range(SSD_GROUPS):
        yg = y[:, g * half:(g + 1) * half]
        outs.append(yg * lax.rsqrt(jnp.mean(yg * yg, axis=-1, keepdims=True) + EPS))
    return jnp.concatenate(outs, axis=1) * nw


def _mamba_p_kernel(h_ref, cw_ref, cb_ref, dtb_ref, alog_ref, dexp_ref, nw_ref, e_ref, tril_ref,
                    y_ref, ncst_ref, nsst_ref, ext_ref, st_ref, *, lt, n_tiles):
    q = SSD_CHUNK
    hist = SSD_CONV_WIDTH - 1
    pad = SUBLANE
    half = D_C // SSD_GROUPS
    hpg = SSD_HEADS // SSD_GROUPS
    j = pl.program_id(1)

    @pl.when(j == 0)
    def _():
        ext_ref[0:pad, :] = jnp.zeros((pad, D_XBC), F32)
        st_ref[...] = jnp.zeros(st_ref.shape, F32)

    ext_ref[pad:pad + lt, :] = h_ref[:, D_C:D_C + D_XBC]

    e = e_ref[...]
    tril = tril_ref[...]
    a_neg = -jnp.exp(alog_ref[0])
    li = lax.broadcasted_iota(jnp.int32, (q, q), 0)
    si = lax.broadcasted_iota(jnp.int32, (q, q), 1)
    causal = li >= si
    lane = lax.broadcasted_iota(jnp.int32, (q, LANE), 1)

    for c in range(lt // q):
        r0 = c * q
        acc = jnp.zeros((q, D_XBC), F32) + cb_ref[0]
        for k in range(SSD_CONV_WIDTH):
            acc = acc + cw_ref[0, k:k + 1, :] * ext_ref[pl.ds(pad - hist + k + r0, q), :]
        xc = _silu(acc)
        xs = xc[:, 0:D_C]
        z = h_ref[r0:r0 + q, 0:D_C]
        dt = _softplus(h_ref[r0:r0 + q, D_C + D_XBC:D_C + D_XBC + LANE] + dtb_ref[0])
        a = dt * a_neg
        hi_, mid_, lo_ = _split3(a)
        cs = _dot(tril, hi_) + _dot(tril, mid_) + _dot(tril, lo_)
        cs_last = cs[q - 1:q, :]
        dt_x = _expand(dt, e)
        ecs_x = _expand(jnp.exp(cs), e)
        edl_x = _expand(jnp.exp(cs_last - cs), e)
        xdt = xs * dt_x
        cs_t = cs.T

        y_parts = []
        for g in range(SSD_GROUPS):
            bm = xc[:, D_C + g * SSD_STATE:D_C + (g + 1) * SSD_STATE]
            cm = xc[:, D_C + SSD_GROUPS * SSD_STATE + g * SSD_STATE:
                    D_C + SSD_GROUPS * SSD_STATE + (g + 1) * SSD_STATE]
            bm16 = bm.astype(BF16)
            cm16 = cm.astype(BF16)
            cb = _dot_nt(cm16, bm16)
            for pr in range(hpg // 2):
                r_even = g * hpg + 2 * pr
                xpair = xdt[:, r_even * SSD_HEAD_DIM:(r_even + 2) * SSD_HEAD_DIM].astype(BF16)
                ys = []
                for r in (r_even, r_even + 1):
                    seg = cs[:, r:r + 1] - cs_t[r:r + 1, :]
                    dec = jnp.exp(jnp.where(causal, seg, -jnp.inf))
                    ys.append(_dot((cb * dec).astype(BF16), xpair))
                y_parts.append(jnp.where(lane < SSD_HEAD_DIM, ys[0], ys[1]))
        y_diag = jnp.concatenate(y_parts, axis=1)
        y_off = jnp.concatenate(
            [_dot(xc[:, D_C + SSD_GROUPS * SSD_STATE + g * SSD_STATE:
                      D_C + SSD_GROUPS * SSD_STATE + (g + 1) * SSD_STATE].astype(BF16),
                  st_ref[:, g * half:(g + 1) * half].astype(BF16)) for g in range(SSD_GROUPS)],
            axis=1) * ecs_x
        y = y_diag + y_off + dexp_ref[0] * xs
        y = y * _silu(z)
        y_ref[r0:r0 + q, :] = _group_rmsnorm(y, nw_ref[0]).astype(y_ref.dtype)

        xw = (xdt * edl_x).astype(BF16)
        dec_row = ecs_x[q - 1:q, :]
        for g in range(SSD_GROUPS):
            bm_t = xc[:, D_C + g * SSD_STATE:D_C + (g + 1) * SSD_STATE].T.astype(BF16)
            upd = _dot(bm_t, xw[:, g * half:(g + 1) * half])
            st_ref[:, g * half:(g + 1) * half] = (
                st_ref[:, g * half:(g + 1) * half] * dec_row[:, g * half:(g + 1) * half] + upd)

    new_hist = ext_ref[pl.ds(pad + lt - hist, hist), :]
    ncst_ref[0] = new_hist
    if n_tiles > 1:
        ext_ref[pad - hist:pad, :] = new_hist

    @pl.when(j == n_tiles - 1)
    def _():
        for blk in range(D_C // LANE):
            nsst_ref[0, blk * LANE:(blk + 1) * LANE, :] = st_ref[:, blk * LANE:(blk + 1) * LANE].T


def _ssd_consts():
    head_of_lane = jnp.arange(D_C) // SSD_HEAD_DIM
    e = (jnp.arange(LANE)[:, None] == head_of_lane[None, :]).astype(BF16)
    tril = (jnp.arange(SSD_CHUNK)[:, None] >= jnp.arange(SSD_CHUNK)[None, :]).astype(BF16)
    return e, tril


def _ssd_param_specs(layer):
    return [_layer_spec((SSD_CONV_WIDTH, D_XBC), layer),
            _layer_spec((1, D_XBC), layer),
            _layer_spec((1, LANE), layer),
            _layer_spec((1, LANE), layer),
            _layer_spec((1, D_C), layer),
            _layer_spec((1, D_C), layer)]


def _ssd_param_args(p):
    return [p['conv_c_w'], p['conv_c_b'], p['ssd_dt_bias'], p['ssd_a_log'], p['ssd_d'], p['ssd_norm_w']]


def mamba_prompt(h_c, p, layer, *, n_seq, lt, n_tiles):
    hist = SSD_CONV_WIDTH - 1
    const = lambda s, j: (0, 0)
    return pl.pallas_call(
        functools.partial(_mamba_p_kernel, lt=lt, n_tiles=n_tiles),
        out_shape=(jax.ShapeDtypeStruct((h_c.shape[0], D_C), BF16),
                   jax.ShapeDtypeStruct((n_seq, hist, D_XBC), F32),
                   jax.ShapeDtypeStruct((n_seq, D_C, SSD_STATE), F32)),
        grid=(n_seq, n_tiles),
        in_specs=[pl.BlockSpec((lt, D_HC), lambda s, j: (s * n_tiles + j, 0))]
        + _ssd_param_specs(layer)
        + [pl.BlockSpec((LANE, D_C), const), pl.BlockSpec((SSD_CHUNK, SSD_CHUNK), const)],
        out_specs=(pl.BlockSpec((lt, D_C), lambda s, j: (s * n_tiles + j, 0)),
                   pl.BlockSpec((1, hist, D_XBC), lambda s, j: (s, 0, 0)),
                   pl.BlockSpec((1, D_C, SSD_STATE), lambda s, j: (s, 0, 0))),
        scratch_shapes=[pltpu.VMEM((SUBLANE + lt, D_XBC), F32),
                        pltpu.VMEM((SSD_STATE, D_C), F32)],
        compiler_params=_cp("parallel", "arbitrary"),
        name="mamba_prompt",
    )(h_c, *_ssd_param_args(p), p['ssd_e'], p['ssd_tril'])


def _ks(c, k, nb):
    return slice((c * SUBLANE + k) * nb, (c * SUBLANE + k + 1) * nb)


def _slab_put(ref, k, slab, nb):
    for c in range(slab.shape[1] // LANE):
        ref[_ks(c, k, nb), :] = slab[:, c * LANE:(c + 1) * LANE]


def _slab_get(ref, k, n_blocks, nb):
    return jnp.concatenate([ref[_ks(c, k, nb), :] for c in range(n_blocks)], axis=1)


def _seq_get(ref, b, n_blocks, nb):
    return jnp.concatenate(
        [ref[pl.ds(c * SUBLANE * nb + b, SUBLANE, stride=nb), :] for c in range(n_blocks)], axis=1)


def _seq_put(ref, b, val, nb, c0=0):
    for c in range(val.shape[1] // LANE):
        ref[pl.ds((c0 + c) * SUBLANE * nb + b, SUBLANE, stride=nb), :] = val[:, c * LANE:(c + 1) * LANE]


def _mamba_s_kernel(h_ref, cw_ref, cb_ref, dtb_ref, alog_ref, dexp_ref, nw_ref, e_ref, cst_ref, sst_ref,
                    y_ref, ncst_ref, nsst_ref,
                    ext_ref, xs_ref, dt_ref, cs_ref, lhs_ref, rhs_ref, c8_ref, yoff_ref,
                    *, nb, lt, bb, lsel, passthrough):
    hist = (SSD_CONV_WIDTH - 1) * nb
    rows = lt * nb
    half = D_C // SSD_GROUPS
    hpg = SSD_HEADS // SSD_GROUPS
    xblk = D_C // LANE
    hblk = half // LANE
    i = pl.program_id(0)
    n_steps = pl.num_programs(0)
    bc_off = D_C
    cc_off = D_C + SSD_GROUPS * SSD_STATE

    @pl.when(i == 0)
    def _phase1():
        e = e_ref[...]
        ext_ref[0:hist, :] = cst_ref[0]
        ext_ref[hist:hist + rows, :] = h_ref[:, D_C:D_C + D_XBC]
        ncst_ref[...] = ext_ref[rows:rows + hist, :]
        a_neg = -jnp.exp(alog_ref[0])
        lhs_ref[...] = jnp.zeros(lhs_ref.shape, F32)
        rhs_ref[...] = jnp.zeros(rhs_ref.shape, F32)
        c8_ref[...] = jnp.zeros(c8_ref.shape, F32)
        cs = jnp.zeros((nb, LANE), F32)
        for t in range(lt):
            rs = slice(t * nb, (t + 1) * nb)
            acc = jnp.zeros((nb, D_XBC), F32) + cb_ref[0]
            for k in range(SSD_CONV_WIDTH):
                acc = acc + cw_ref[0, k:k + 1, :] * ext_ref[(t + k) * nb:(t + k + 1) * nb, :]
            xc = _silu(acc)
            xs_ref[rs, :] = xc[:, 0:D_C]
            for g in range(SSD_GROUPS):
                rhs_ref[_ks(2 * g, t, nb), :] = xc[:, bc_off + g * SSD_STATE:bc_off + (g + 1) * SSD_STATE]
            _slab_put(c8_ref, t, xc[:, cc_off:cc_off + SSD_GROUPS * SSD_STATE], nb)
            dt = _softplus(h_ref[rs, D_C + D_XBC:D_C + D_XBC + LANE] + dtb_ref[0])
            dt_ref[rs, :] = dt
            cs = cs + dt * a_neg
            cs_ref[rs, :] = cs
        cs_last = cs
        for t in range(lt):
            rs = slice(t * nb, (t + 1) * nb)
            wt = jnp.exp(cs_last - cs_ref[rs, :]) * dt_ref[rs, :]
            _slab_put(lhs_ref, t, xs_ref[rs, :] * _expand(wt, e), nb)
        dec = _expand(jnp.exp(cs_last), e)
        d_hi = dec.astype(BF16).astype(F32)
        d_r = dec - d_hi
        d_mid = d_r.astype(BF16).astype(F32)
        d_lo = d_r - d_mid
        ones = jnp.ones((nb, SSD_STATE), F32)
        for k, piece in enumerate((d_hi, d_mid, d_lo)):
            _slab_put(lhs_ref, lt + k, piece, nb)
            for g in range(SSD_GROUPS):
                rhs_ref[_ks(2 * g + 1, lt + k, nb), :] = ones

    for jb in range(bb):
        b = i * bb + jb
        l8 = _seq_get(lhs_ref, b, xblk, nb).astype(BF16)
        r8 = _seq_get(rhs_ref, b, 2 * SSD_GROUPS, nb).astype(BF16)
        c8 = _seq_get(c8_ref, b, SSD_GROUPS, nb).astype(BF16)
        for g in range(SSD_GROUPS):
            s = sst_ref[lsel, jb, g * half:(g + 1) * half, :]
            yo = _dot_nt(c8[:, g * SSD_STATE:(g + 1) * SSD_STATE], s.astype(BF16))
            _seq_put(yoff_ref, b, yo, nb, c0=g * hblk)
            upd = _dot_tn(l8[:, g * half:(g + 1) * half],
                          r8[:, g * 2 * SSD_STATE:(g + 1) * 2 * SSD_STATE])
            nsst_ref[lsel, jb, g * half:(g + 1) * half, :] = upd[:, SSD_STATE:] * s + upd[:, :SSD_STATE]
    for d in passthrough:
        nsst_ref[d] = sst_ref[d]

    @pl.when(i == n_steps - 1)
    def _phase3():
        e = e_ref[...]
        lane = lax.broadcasted_iota(jnp.int32, (nb, LANE), 1)
        for t in range(lt):
            rt = slice(t * nb, (t + 1) * nb)
            cs_t = cs_ref[rt, :]
            y = (_slab_get(yoff_ref, t, xblk, nb) * _expand(jnp.exp(cs_t), e)
                 + dexp_ref[0] * xs_ref[rt, :])
            for s_ in range(t + 1):
                rsl = slice(s_ * nb, (s_ + 1) * nb)
                cbs = []
                for g in range(SSD_GROUPS):
                    cm = c8_ref[_ks(g, t, nb), :]
                    bm = rhs_ref[_ks(2 * g, s_, nb), :]
                    cbs.append(jnp.sum(cm * bm, axis=-1, keepdims=True))
                cb = jnp.where(lane < hpg, cbs[0], cbs[1])
                m = jnp.exp(cs_t - cs_ref[rsl, :]) * dt_ref[rsl, :] * cb
                y = y + _expand(m, e) * xs_ref[rsl, :]
            y = y * _silu(h_ref[rt, 0:D_C])
            y_ref[rt, :] = _group_rmsnorm(y, nw_ref[0]).astype(y_ref.dtype)


def mamba_sample(h_c, p, layer, cst, sst, *, nb, lt, bb, in_place):
    rows = lt * nb
    hist = (SSD_CONV_WIDTH - 1) * nb
    depth = sst.shape[0]
    const = lambda i: (0, 0)
    if in_place:
        sst_spec = pl.BlockSpec((1, bb, D_C, SSD_STATE), lambda i: (layer, i, 0, 0))
        lsel, passthrough = 0, ()
    else:
        sst_spec = pl.BlockSpec((depth, bb, D_C, SSD_STATE), lambda i: (0, i, 0, 0))
        lsel, passthrough = layer, tuple(d for d in range(depth) if d != layer)
    in_specs = ([pl.BlockSpec((rows, D_HC), const)] + _ssd_param_specs(layer)
                + [pl.BlockSpec((LANE, D_C), const), _layer_spec((hist, D_XBC), layer), sst_spec])
    return pl.pallas_call(
        functools.partial(_mamba_s_kernel, nb=nb, lt=lt, bb=bb, lsel=lsel, passthrough=passthrough),
        out_shape=(jax.ShapeDtypeStruct((rows, D_C), BF16),
                   jax.ShapeDtypeStruct((hist, D_XBC), F32),
                   jax.ShapeDtypeStruct(sst.shape, F32)),
        grid=(nb // bb,),
        in_specs=in_specs,
        out_specs=(pl.BlockSpec((rows, D_C), const),
                   pl.BlockSpec((hist, D_XBC), const),
                   sst_spec),
        input_output_aliases={len(in_specs) - 1: 2} if in_place else {},
        scratch_shapes=[pltpu.VMEM((hist + rows, D_XBC), F32),
                        pltpu.VMEM((rows, D_C), F32),
                        pltpu.VMEM((rows, LANE), F32),
                        pltpu.VMEM((rows, LANE), F32),
                        pltpu.VMEM((D_C // LANE * SUBLANE * nb, LANE), F32),
                        pltpu.VMEM((2 * SSD_GROUPS * SUBLANE * nb, LANE), F32),
                        pltpu.VMEM((SSD_GROUPS * SUBLANE * nb, LANE), F32),
                        pltpu.VMEM((D_C // LANE * SUBLANE * nb, LANE), F32)],
        compiler_params=_cp("arbitrary"),
        name="mamba_sample",
    )(h_c, *_ssd_param_args(p), p['ssd_e'], cst, sst)


def _softmax_rows(s):
    m = jnp.max(s, axis=-1, keepdims=True)
    ex = jnp.exp(s - m)
    return ex / jnp.sum(ex, axis=-1, keepdims=True)


def _attn_p_kernel(q_ref, k_ref, v_ref, o_ref):
    for h in range(XA_HEADS):
        hs = slice(h * XA_HEAD_DIM, (h + 1) * XA_HEAD_DIM)
        s = _dot_nt(q_ref[:, hs].astype(BF16), k_ref[0, :, hs].astype(BF16)) / math.sqrt(XA_HEAD_DIM)
        p = _softmax_rows(s)
        o_ref[:, hs] = _dot(p.astype(BF16), v_ref[0, :, hs].astype(BF16)).astype(o_ref.dtype)


def attn_prompt(q, k, v, layer, *, n_seq, seq, tq):
    n_tiles = seq // tq
    return pl.pallas_call(
        _attn_p_kernel,
        out_shape=jax.ShapeDtypeStruct(q.shape, BF16),
        grid=(n_seq, n_tiles),
        in_specs=[pl.BlockSpec((tq, D_MODEL), lambda s, j: (s * n_tiles + j, 0)),
                  pl.BlockSpec((1, N_MEM, D_MODEL), lambda s, j: (layer * n_seq + s, 0, 0)),
                  pl.BlockSpec((1, N_MEM, D_MODEL), lambda s, j: (layer * n_seq + s, 0, 0))],
        out_specs=pl.BlockSpec((tq, D_MODEL), lambda s, j: (s * n_tiles + j, 0)),
        compiler_params=_cp("parallel", "arbitrary"),
        name="attn_prompt",
    )(q, k, v)


def _attn_s_kernel(q_ref, k_ref, v_ref, o_ref, *, bb, lt):
    rows = XA_HEADS * lt
    n = N_MEM * XA_HEADS
    col_head = lax.broadcasted_iota(jnp.int32, (rows, n), 1) % XA_HEADS
    row_head = lax.broadcasted_iota(jnp.int32, (rows, n), 0) // lt
    same_head = col_head == row_head
    for jb in range(bb):
        k = k_ref[0, jb].reshape(n, XA_HEAD_DIM).astype(BF16)
        v = v_ref[0, jb].reshape(n, XA_HEAD_DIM).astype(BF16)
        s = _dot_nt(q_ref[jb].astype(BF16), k) / math.sqrt(XA_HEAD_DIM)
        p = _softmax_rows(jnp.where(same_head, s, -jnp.inf))
        o_ref[jb] = _dot(p.astype(BF16), v)


def attn_sample(q, k, v, layer, *, bb):
    nb, rows, _ = q.shape
    kv_spec = pl.BlockSpec((1, bb, N_MEM, XA_HEADS, XA_HEAD_DIM), lambda i: (layer, i, 0, 0, 0))
    return pl.pallas_call(
        functools.partial(_attn_s_kernel, bb=bb, lt=rows // XA_HEADS),
        out_shape=jax.ShapeDtypeStruct((nb, rows, XA_HEAD_DIM), F32),
        grid=(nb // bb,),
        in_specs=[pl.BlockSpec((bb, rows, XA_HEAD_DIM), lambda i: (i, 0, 0)), kv_spec, kv_spec],
        out_specs=pl.BlockSpec((bb, rows, XA_HEAD_DIM), lambda i: (i, 0, 0)),
        compiler_params=_cp("parallel"),
        name="attn_sample",
    )(q, k, v)


def _ffn_kernel(*refs, nb, tiles_per_seq, has_state, final_norm):
    refs = list(refs)
    x_ref, nw_ref, wg_ref, wu_ref, cw_ref, cb_ref, wd_ref = refs[:7]
    pos = 7
    st_ref = None
    if has_state:
        st_ref = refs[pos]
        pos += 1
    fw_ref = None
    if final_norm:
        fw_ref = refs[pos]
        pos += 1
    o_ref, nst_ref, xn_ref, gext_ref, carry_ref = refs[pos:pos + 5]

    hist = (FFN_CONV_WIDTH - 1) * nb
    pad = _round_up(hist, SUBLANE)
    tm = x_ref.shape[0]
    tf = wg_ref.shape[3]
    i = pl.program_id(0)
    f = pl.program_id(1)
    n_f = pl.num_programs(1)

    @pl.when(f == 0)
    def _():
        x = x_ref[...]
        xn_ref[...] = _rmsnorm_rows(x, nw_ref[0]).astype(BF16)
        o_ref[...] = x

    if tiles_per_seq > 1:
        first = (i % tiles_per_seq) == 0

        @pl.when(first)
        def _():
            if has_state:
                gext_ref[pad - hist:pad, :] = st_ref[0]
            else:
                gext_ref[0:pad, :] = jnp.zeros((pad, tf), F32)

        @pl.when(jnp.logical_not(first))
        def _():
            gext_ref[0:pad, :] = carry_ref[f]
    else:
        if has_state:
            gext_ref[pad - hist:pad, :] = st_ref[0]
        else:
            gext_ref[0:pad, :] = jnp.zeros((pad, tf), F32)

    xn = xn_ref[...]
    g = _dot(xn, wg_ref[0, 0])
    up = _dot(xn, wu_ref[0, 0])
    gext_ref[pad:pad + tm, :] = g
    conv = (cw_ref[0, 0:1, :] * gext_ref[pl.ds(pad - 2 * nb, tm), :]
            + cw_ref[0, 1:2, :] * gext_ref[pl.ds(pad - nb, tm), :]
            + cw_ref[0, 2:3, :] * g + cb_ref[0])
    act = _silu(conv) * up
    o_ref[...] += _dot(act.astype(BF16), wd_ref[0])

    nst_ref[0] = gext_ref[pl.ds(pad + tm - hist, hist), :]
    if tiles_per_seq > 1:
        carry_ref[f] = gext_ref[pl.ds(tm, pad), :]

    if final_norm:
        @pl.when(f == n_f - 1)
        def _():
            o_ref[...] = _rmsnorm_rows(o_ref[...], fw_ref[...])


def conv_ffn(x, p, layer, state, final_w, *, n_seq, nb, tm, tiles_per_seq):
    m = x.shape[0]
    tf = FF_TILE
    n_f = D_FF_PAD // tf
    hist = (FFN_CONV_WIDTH - 1) * nb
    pad = _round_up(hist, SUBLANE)
    has_state = state is not None
    final_norm = final_w is not None
    in_specs = [pl.BlockSpec((tm, D_MODEL), lambda i, f: (i, 0)),
                _layer_spec((1, D_MODEL), layer),
                pl.BlockSpec((1, 1, D_MODEL, tf), lambda i, f: (layer, f, 0, 0)),
                pl.BlockSpec((1, 1, D_MODEL, tf), lambda i, f: (layer, f, 0, 0)),
                pl.BlockSpec((1, FFN_CONV_WIDTH, tf), lambda i, f: (layer, 0, f)),
                pl.BlockSpec((1, 1, tf), lambda i, f: (layer, 0, f)),
                pl.BlockSpec((1, tf, D_MODEL), lambda i, f: (layer, f, 0))]
    args = [x, p['norm_ffn_w'], p['ffn_wg'], p['ffn_wu'], p['ffn_conv_w'], p['ffn_conv_b'], p['ffn_wd']]
    if has_state:
        assert n_seq == 1 and tiles_per_seq == 1
        in_specs.append(pl.BlockSpec((1, hist, tf), lambda i, f: (layer, 0, f)))
        args.append(state)
    if final_norm:
        in_specs.append(pl.BlockSpec((1, D_MODEL), lambda i, f: (0, 0)))
        args.append(final_w.reshape(1, D_MODEL))
    return pl.pallas_call(
        functools.partial(_ffn_kernel, nb=nb, tiles_per_seq=tiles_per_seq, has_state=has_state,
                          final_norm=final_norm),
        out_shape=(jax.ShapeDtypeStruct((m, D_MODEL), F32),
                   jax.ShapeDtypeStruct((m // tm, hist, D_FF_PAD), F32)),
        grid=(m // tm, n_f),
        in_specs=in_specs,
        out_specs=(pl.BlockSpec((tm, D_MODEL), lambda i, f: (i, 0)),
                   pl.BlockSpec((1, hist, tf), lambda i, f: (i, 0, f))),
        scratch_shapes=[pltpu.VMEM((tm, D_MODEL), BF16),
                        pltpu.VMEM((pad + tm, tf), F32),
                        pltpu.VMEM((n_f, pad, tf), F32)],
        compiler_params=_cp("arbitrary", "arbitrary"),
        name="conv_ffn",
    )(*args)


def _s5_params(lam_re, lam_im, log_dt, b_re, b_im, c_re, c_im):
    depth = lam_re.shape[0]
    dt = jnp.exp(log_dt)[..., None]
    mag = jnp.exp(lam_re * dt)
    ang = lam_im * dt
    ab_re, ab_im = mag * jnp.cos(ang), mag * jnp.sin(ang)
    blocks = lambda re, im: jnp.concatenate(
        [re.reshape(*re.shape[:-2], S5_BLOCKS, LANE), im.reshape(*im.shape[:-2], S5_BLOCKS, LANE)], axis=-2)
    row = jnp.arange(SUBLANE, dtype=F32)
    expo = jnp.stack([jnp.full((SUBLANE,), 1.0), jnp.full((SUBLANE,), 2.0), jnp.full((SUBLANE,), 4.0),
                      row + 1.0])
    keep = jnp.stack([row >= 1, row >= 2, row >= 4, row >= 0]).astype(F32)
    lam_dt = (lam_re * dt).reshape(depth, 1, 1, S5_LANES)
    pang = ang.reshape(depth, 1, 1, S5_LANES) * expo[None, :, :, None]
    pmag = jnp.exp(lam_dt * expo[None, :, :, None]) * keep[None, :, :, None]
    tab = jnp.concatenate([pmag * jnp.cos(pang), pmag * jnp.sin(pang)], axis=-1)
    den = lam_re * lam_re + lam_im * lam_im
    nr, ni = ab_re - 1.0, ab_im
    co_re = (nr * lam_re + ni * lam_im) / den
    co_im = (ni * lam_re - nr * lam_im) / den
    bb_re = co_re[..., None] * b_re - co_im[..., None] * b_im
    bb_im = co_re[..., None] * b_im + co_im[..., None] * b_re
    gps = S5_GROUPS // S5_SUPER
    eye = jnp.eye(gps, dtype=F32)
    sup = lambda m: m.reshape(depth, S5_SUPER, gps, *m.shape[2:])
    dense_b = lambda m: jnp.einsum('lsgph,gk->lsghkp', sup(m), eye).reshape(depth, S5_SUPER, S5_SUP_CH, S5_SUP_ST)
    dense_c = lambda m: jnp.einsum('lsghp,gk->lskpgh', sup(m), eye).reshape(depth, S5_SUPER, S5_SUP_ST, S5_SUP_CH)
    bb = jnp.concatenate([dense_b(bb_re), dense_b(bb_im)], axis=3).astype(BF16)
    cc = jnp.concatenate([dense_c(c_re), -dense_c(c_im)], axis=2).astype(BF16)
    return bb, blocks(ab_re, ab_im), tab, cc


def _wprep_kernel(w_ref, o_ref, *, axis, valid_last):
    f = pl.program_id(1)
    last = pl.num_programs(1) - 1
    o = o_ref.at[0, 0] if axis == 1 else o_ref.at[0]

    @pl.when(f < last)
    def _():
        o[...] = w_ref[0].astype(BF16)

    @pl.when(f == last)
    def _():
        if axis == 1:
            o[:, :valid_last] = w_ref[0, :, :valid_last].astype(BF16)
            o[:, valid_last:] = jnp.zeros((o.shape[0], o.shape[1] - valid_last), BF16)
        else:
            o[:valid_last, :] = w_ref[0, :valid_last, :].astype(BF16)
            o[valid_last:, :] = jnp.zeros((o.shape[0] - valid_last, o.shape[1]), BF16)


def ffn_weight_cols(w, tf):
    depth, k, n = w.shape
    n_f = pl.cdiv(n, tf)
    return pl.pallas_call(
        functools.partial(_wprep_kernel, axis=1, valid_last=n - (n_f - 1) * tf),
        out_shape=jax.ShapeDtypeStruct((depth, n_f, k, tf), BF16),
        grid=(depth, n_f),
        in_specs=[pl.BlockSpec((1, k, tf), lambda l, f: (l, 0, f))],
        out_specs=pl.BlockSpec((1, 1, k, tf), lambda l, f: (l, f, 0, 0)),
        compiler_params=_cp("parallel", "parallel"),
        name="ffn_weight_cols",
    )(w)


def ffn_weight_rows(w, tf):
    depth, k, n = w.shape
    n_f = pl.cdiv(k, tf)
    return pl.pallas_call(
        functools.partial(_wprep_kernel, axis=0, valid_last=k - (n_f - 1) * tf),
        out_shape=jax.ShapeDtypeStruct((depth, n_f * tf, n), BF16),
        grid=(depth, n_f),
        in_specs=[pl.BlockSpec((1, tf, n), lambda l, f: (l, f, 0))],
        out_specs=pl.BlockSpec((1, tf, n), lambda l, f: (l, f, 0)),
        compiler_params=_cp("parallel", "parallel"),
        name="ffn_weight_rows",
    )(w)


def kernel(x_prompt, x_sample, mem_prompt, cache_mem_k, cache_mem_v, state_conv_a, state_s5_re, state_s5_im, state_conv_c, state_ssd, state_ffn_conv, norm_mix_w, w_in, conv_a_w, conv_a_b, ln_a_w, ln_a_b, s5_lam_re, s5_lam_im, s5_log_dt, s5_b_re, s5_b_im, s5_c_re, s5_c_im, s5_d, s5_glu_w, s5_glu_b, conv_c_w, conv_c_b, ssd_dt_bias, ssd_a_log, ssd_d, ssd_norm_w, w_out, norm_xa_w, norm_mem_w, xa_wq, xa_wk, xa_wv, xa_wo, norm_ffn_w, ffn_w_gate, ffn_w_up, ffn_conv_w, ffn_conv_b, ffn_w_down, final_norm_w):
    bp, seq, _ = x_prompt.shape
    nbs, lts, _ = x_sample.shape
    depth = w_in.shape[0]
    n_mem = mem_prompt.shape[1]
    lt_p = 512 if seq % 512 == 0 else seq
    n_tiles_p = seq // lt_p
    tm_p = lt_p
    tm_s = lts * nbs
    tm_f = 1024 if seq % 1024 == 0 else tm_p
    tm_m = min(256, bp * n_mem)

    vec = lambda a: a.reshape(depth, 1, a.shape[-1])
    pad_lanes = lambda a: vec(jnp.pad(a, ((0, 0), (0, LANE - a.shape[-1]))))
    ff_pad = D_FF_PAD - D_FF
    s5_bb, s5_ab, s5_tab, s5_cc = _s5_params(s5_lam_re, s5_lam_im, s5_log_dt, s5_b_re, s5_b_im, s5_c_re, s5_c_im)
    ssd_e, ssd_tril = _ssd_consts()
    p = {
        'norm_mix_w': vec(norm_mix_w), 'norm_xa_w': vec(norm_xa_w), 'norm_mem_w': vec(norm_mem_w),
        'norm_ffn_w': vec(norm_ffn_w),
        'w_in': jnp.pad(w_in, ((0, 0), (0, 0), (0, IN_SPLITS[2] - w_in.shape[2]))).astype(BF16),
        'conv_a_w': conv_a_w, 'conv_a_b': vec(conv_a_b), 'ln_a_w': vec(ln_a_w), 'ln_a_b': vec(ln_a_b),
        's5_bb': s5_bb, 's5_ab': s5_ab, 's5_tab': s5_tab, 's5_cc': s5_cc,
        's5_d': vec(s5_d), 's5_glu_w': s5_glu_w.astype(BF16), 's5_glu_b': vec(s5_glu_b),
        'conv_c_w': conv_c_w, 'conv_c_b': vec(conv_c_b),
        'ssd_dt_bias': pad_lanes(ssd_dt_bias), 'ssd_a_log': pad_lanes(ssd_a_log),
        'ssd_d': vec(jnp.repeat(ssd_d, SSD_HEAD_DIM, axis=1)), 'ssd_norm_w': vec(ssd_norm_w),
        'ssd_e': ssd_e, 'ssd_tril': ssd_tril,
        'w_out': w_out.astype(BF16), 'wq': xa_wq.astype(BF16), 'wo': xa_wo.astype(BF16),
        'wk': xa_wk.astype(BF16), 'wv': xa_wv.astype(BF16),
        'ffn_wg': ffn_weight_cols(ffn_w_gate, FF_TILE),
        'ffn_wu': ffn_weight_cols(ffn_w_up, FF_TILE),
        'ffn_wd': ffn_weight_rows(ffn_w_down, FF_TILE),
        'ffn_conv_w': jnp.pad(ffn_conv_w, ((0, 0), (0, 0), (0, ff_pad))),
        'ffn_conv_b': vec(jnp.pad(ffn_conv_b, ((0, 0), (0, ff_pad)))),
    }

    tmaj = lambda a: a.transpose(0, 2, 1, 3).reshape(depth, a.shape[2] * nbs, a.shape[3])
    st_conv_a = tmaj(state_conv_a)
    st_conv_c = tmaj(state_conv_c)
    st_ffn = jnp.pad(tmaj(state_ffn_conv), ((0, 0), (0, 0), (0, ff_pad)))
    st_re = state_s5_re.reshape(depth, nbs, S5_LANES)
    st_im = state_s5_im.reshape(depth, nbs, S5_LANES)
    ssd_all = state_ssd.reshape(depth, nbs, D_C, SSD_STATE)

    xp = x_prompt.reshape(bp * seq, D_MODEL)
    xs = x_sample.transpose(1, 0, 2).reshape(lts * nbs, D_MODEL)
    mem2d = mem_prompt.reshape(bp * n_mem, D_MODEL)

    def mixers(x, l, *, n_seq, nb, lt, n_tiles, tm, sample):
        if sample:
            h_a, h_b, h_c = in_proj(x, p['norm_mix_w'], p['w_in'], l, tm=tm)
            ya, n_conv_a = conva_mixer(h_a, p, l, st_conv_a, n_seq=n_seq, nb=nb, lt=lt, n_tiles=n_tiles)
        else:
            ya, h_b, h_c, n_conv_a = in_proj_conva(x, p, l, tm=tm, tiles_per_seq=n_tiles)
            n_conv_a = n_conv_a[n_tiles - 1::n_tiles]
        yb, n_re, n_im = s5_mixer(h_b, p, l, st_re if sample else None, st_im if sample else None,
                                  n_seq=n_seq, nb=nb, lt=lt, n_tiles=n_tiles)
        return h_c, ya, yb, n_conv_a, n_re, n_im

    mk, mv, p_mk, p_mv = mem_kv(mem2d, p['norm_mem_w'], p['wk'], p['wv'], tm=tm_m)
    mk = mk.reshape(depth * bp, n_mem, D_MODEL)
    mv = mv.reshape(depth * bp, n_mem, D_MODEL)
    tq_p = 1024 if seq % 1024 == 0 else lt_p

    outs_p = [[] for _ in range(6)]
    outs_s = [[] for _ in range(5)]
    for l in range(depth):
        last = l == depth - 1

        h_c, ya, yb, p_conv_a, p_re, p_im = mixers(xp, l, n_seq=bp, nb=1, lt=lt_p, n_tiles=n_tiles_p,
                                                   tm=tm_p, sample=False)
        yc, p_conv_c, p_ssd = mamba_prompt(h_c, p, l, n_seq=bp, lt=lt_p, n_tiles=n_tiles_p)
        xp, q = out_q_proj(ya, yb, yc, p['w_out'], xp, p['norm_xa_w'], p['wq'], l, tm=tm_p)
        o = attn_prompt(q, mk, mv, l, n_seq=bp, seq=seq, tq=tq_p)
        xp = proj_res([o], p['wo'], l, xp, tm=tm_p, name="attn_out")
        xp, p_ffn = conv_ffn(xp, p, l, None, final_norm_w if last else None, n_seq=bp, nb=1, tm=tm_f,
                             tiles_per_seq=seq // tm_f)
        for lst, v in zip(outs_p, (p_conv_a, p_re, p_im, p_conv_c, p_ssd,
                                   p_ffn[seq // tm_f - 1::seq // tm_f])):
            lst.append(v)

        h_c, ya, yb, s_conv_a, s_re, s_im = mixers(xs, l, n_seq=1, nb=nbs, lt=lts, n_tiles=1, tm=tm_s,
                                                   sample=True)
        yc, s_conv_c, ssd_all = mamba_sample(h_c, p, l, st_conv_c, ssd_all, nb=nbs, lt=lts,
                                             bb=8 if l > 0 else 4, in_place=l > 0)
        xs, q = out_q_proj(ya, yb, yc, p['w_out'], xs, p['norm_xa_w'], p['wq'], l, tm=tm_s,
                           q_by_seq=(nbs, lts))
        o = attn_sample(q, cache_mem_k, cache_mem_v, l, bb=4)
        xs = attn_out_seq(o, p['wo'], l, xs, nb=nbs, lt=lts)
        xs, s_ffn = conv_ffn(xs, p, l, st_ffn, final_norm_w if last else None, n_seq=1, nb=nbs, tm=tm_s,
                             tiles_per_seq=1)
        for lst, v in zip(outs_s, (s_conv_a[0], s_re[0], s_im[0], s_conv_c, s_ffn[0])):
            lst.append(v)

    p_conv_a, p_re, p_im, p_conv_c, p_ssd, p_ffn = [jnp.stack(o) for o in outs_p]
    s_conv_a, s_re, s_im, s_conv_c, s_ffn = [jnp.stack(o) for o in outs_s]
    bmaj = lambda a, w: a.reshape(depth, w, nbs, a.shape[-1]).transpose(0, 2, 1, 3)
    y_prompt = xp.reshape(bp, seq, D_MODEL)
    y_sample = xs.reshape(lts, nbs, D_MODEL).transpose(1, 0, 2)
    return (y_prompt, y_sample,
            p_conv_a,
            p_re.reshape(depth, bp, S5_GROUPS, S5_STATE), p_im.reshape(depth, bp, S5_GROUPS, S5_STATE),
            p_conv_c,
            p_ssd.reshape(depth, bp, SSD_HEADS, SSD_HEAD_DIM, SSD_STATE),
            p_ffn[..., :D_FF],
            p_mk.reshape(depth, bp, n_mem, XA_HEADS, XA_HEAD_DIM),
            p_mv.reshape(depth, bp, n_mem, XA_HEADS, XA_HEAD_DIM),
            bmaj(s_conv_a, CONV_A_WIDTH - 1),
            s_re.reshape(depth, nbs, S5_GROUPS, S5_STATE), s_im.reshape(depth, nbs, S5_GROUPS, S5_STATE),
            bmaj(s_conv_c, SSD_CONV_WIDTH - 1),
            ssd_all.reshape(state_ssd.shape),
            bmaj(s_ffn, FFN_CONV_WIDTH - 1)[..., :D_FF])
```

```python
import functools
import math

import jax
import jax.numpy as jnp
from jax import lax
from jax.experimental import pallas as pl
from jax.experimental.pallas import tpu as pltpu

F32 = jnp.float32
BF16 = jnp.bfloat16
EPS = 1e-6

D_MODEL = 2048
D_A = 512
D_B = 512
D_C = 1024
CONV_A_WIDTH = 31
S5_GROUP = 16
S5_GROUPS = 32
S5_STATE = 64
S5_LANES = S5_GROUPS * S5_STATE
SSD_HEAD_DIM = 64
SSD_HEADS = 16
SSD_GROUPS = 2
SSD_STATE = 128
SSD_CONV_WIDTH = 4
SSD_CHUNK = 128
D_XBC = D_C + 2 * SSD_GROUPS * SSD_STATE
D_HC = D_C + D_XBC + 128
XA_HEADS = 4
XA_HEAD_DIM = 512
N_MEM = 256
D_FF = 5504
FFN_CONV_WIDTH = 3

LANE = 128
SUBLANE = 8
VMEM_LIMIT = 56 * 1024 * 1024
FF_TILE = 512
D_FF_PAD = ((D_FF + FF_TILE - 1) // FF_TILE) * FF_TILE


def _round_up(x, m):
    return (x + m - 1) // m * m


def _cp(*sem):
    return pltpu.CompilerParams(dimension_semantics=sem, vmem_limit_bytes=VMEM_LIMIT)


def _layer_spec(tail, layer):
    zeros = (0,) * len(tail)
    return pl.BlockSpec((1,) + tuple(tail), lambda *_: (layer,) + zeros)


def _dot(a, b):
    return jnp.dot(a, b, preferred_element_type=F32)


def _dot_nt(a, b):
    return lax.dot_general(a, b, (((1,), (1,)), ((), ())), preferred_element_type=F32)


def _dot_tn(a, b):
    return lax.dot_general(a, b, (((0,), (0,)), ((), ())), preferred_element_type=F32)


def _split3(a):
    hi = a.astype(BF16)
    r = a - hi.astype(F32)
    mid = r.astype(BF16)
    lo = (r - mid.astype(F32)).astype(BF16)
    return hi, mid, lo


def _expand(a, e):
    hi, mid, lo = _split3(a)
    return _dot(hi, e) + _dot(mid, e) + _dot(lo, e)


def _sigmoid(x):
    return jax.nn.sigmoid(x)


def _silu(x):
    return x * jax.nn.sigmoid(x)


def _softplus(x):
    return jnp.maximum(x, 0.0) + jnp.log1p(jnp.exp(-jnp.abs(x)))


def _rmsnorm_rows(x, w):
    ms = jnp.mean(x * x, axis=-1, keepdims=True)
    return x * lax.rsqrt(ms + EPS) * w


def _mem_kv_kernel(x_ref, nw_ref, wk_ref, wv_ref, k_ref, v_ref, k4_ref, v4_ref, xn_ref):
    j = pl.program_id(2)

    def emit(w_ref, o_ref, o4_ref):
        y = _dot(xn_ref[...], w_ref[0])
        o_ref[0] = y
        for h in range(XA_HEADS):
            o4_ref[0, :, h, :] = y[:, h * XA_HEAD_DIM:(h + 1) * XA_HEAD_DIM]

    @pl.when(j == 0)
    def _():
        xn_ref[...] = _rmsnorm_rows(x_ref[...], nw_ref[0]).astype(BF16)
        emit(wk_ref, k_ref, k4_ref)

    @pl.when(j == 1)
    def _():
        emit(wv_ref, v_ref, v4_ref)


def mem_kv(x, nw, wk, wv, *, tm):
    m, k = x.shape
    depth, _, n = wk.shape
    flat = jax.ShapeDtypeStruct((depth, m, n), F32)
    split = jax.ShapeDtypeStruct((depth, m, XA_HEADS, XA_HEAD_DIM), F32)
    w_spec = pl.BlockSpec((1, k, n), lambda l, i, j: (l, 0, 0), pipeline_mode=pl.Buffered(1))
    flat_spec = pl.BlockSpec((1, tm, n), lambda l, i, j: (l, i, 0))
    split_spec = pl.BlockSpec((1, tm, XA_HEADS, XA_HEAD_DIM), lambda l, i, j: (l, i, 0, 0))
    return pl.pallas_call(
        _mem_kv_kernel,
        out_shape=(flat, flat, split, split),
        grid=(depth, m // tm, 2),
        in_specs=[pl.BlockSpec((tm, k), lambda l, i, j: (i, 0)),
                  pl.BlockSpec((1, 1, k), lambda l, i, j: (l, 0, 0)),
                  w_spec, w_spec],
        out_specs=(flat_spec, flat_spec, split_spec, split_spec),
        scratch_shapes=[pltpu.VMEM((tm, k), BF16)],
        compiler_params=_cp("parallel", "parallel", "arbitrary"),
        name="mem_kv",
    )(x, nw, wk, wv)


def _attn_out_seq_kernel(o_ref, w_ref, res_ref, out_ref, a_ref, *, nb, lt):
    for t in range(lt):
        for h in range(XA_HEADS):
            a_ref[t * nb:(t + 1) * nb, h * XA_HEAD_DIM:(h + 1) * XA_HEAD_DIM] = o_ref[:, h * lt + t, :]
    out_ref[...] = res_ref[...] + _dot(a_ref[...].astype(BF16), w_ref[0])


def attn_out_seq(o, w, layer, res, *, nb, lt):
    m, n = res.shape
    return pl.pallas_call(
        functools.partial(_attn_out_seq_kernel, nb=nb, lt=lt),
        out_shape=jax.ShapeDtypeStruct((m, n), F32),
        grid=(1,),
        in_specs=[pl.BlockSpec(o.shape, lambda i: (0, 0, 0)),
                  _layer_spec(w.shape[1:], layer),
                  pl.BlockSpec((m, n), lambda i: (0, 0))],
        out_specs=pl.BlockSpec((m, n), lambda i: (0, 0)),
        scratch_shapes=[pltpu.VMEM((m, n), F32)],
        compiler_params=_cp("arbitrary"),
        name="attn_out_seq",
    )(o, w, res)


IN_SPLITS = (2 * D_A, 2 * D_A + D_B, 2 * D_A + D_B + D_HC)


def _in_proj_kernel(x_ref, nw_ref, w_ref, ha_ref, hb_ref, hc_ref):
    xn = _rmsnorm_rows(x_ref[...], nw_ref[0]).astype(BF16)
    ha_ref[...] = _dot(xn, w_ref[0, :, 0:IN_SPLITS[0]])
    hb_ref[...] = _dot(xn, w_ref[0, :, IN_SPLITS[0]:IN_SPLITS[1]])
    hc_ref[...] = _dot(xn, w_ref[0, :, IN_SPLITS[1]:IN_SPLITS[2]])


def in_proj(x, nw, w, layer, *, tm):
    m, k = x.shape
    widths = (IN_SPLITS[0], IN_SPLITS[1] - IN_SPLITS[0], IN_SPLITS[2] - IN_SPLITS[1])
    return pl.pallas_call(
        _in_proj_kernel,
        out_shape=tuple(jax.ShapeDtypeStruct((m, wd), F32) for wd in widths),
        grid=(m // tm,),
        in_specs=[pl.BlockSpec((tm, k), lambda i: (i, 0)),
                  _layer_spec((1, k), layer),
                  pl.BlockSpec((1, k, IN_SPLITS[2]), lambda i: (layer, 0, 0), pipeline_mode=pl.Buffered(1))],
        out_specs=tuple(pl.BlockSpec((tm, wd), lambda i: (i, 0)) for wd in widths),
        compiler_params=_cp("parallel"),
        name="in_proj",
    )(x, nw, w)


def _proj_res_kernel(*refs, n_in):
    a_refs = refs[:n_in]
    w_refs = refs[n_in:2 * n_in]
    res_ref, o_ref = refs[2 * n_in], refs[2 * n_in + 1]
    acc = res_ref[...]
    for a_ref, w_ref in zip(a_refs, w_refs):
        acc = acc + _dot(a_ref[...].astype(BF16), w_ref[0])
    o_ref[...] = acc


def proj_res(a_list, w, layer, res, *, tm, name):
    m, n = res.shape
    n_in = len(a_list)
    in_specs = [pl.BlockSpec((tm, a.shape[1]), lambda i: (i, 0)) for a in a_list]
    row0 = 0
    for a in a_list:
        kk = a.shape[1]
        assert row0 % kk == 0
        in_specs.append(pl.BlockSpec((1, kk, n), lambda i, blk=row0 // kk: (layer, blk, 0),
                                     pipeline_mode=pl.Buffered(1)))
        row0 += kk
    in_specs.append(pl.BlockSpec((tm, n), lambda i: (i, 0)))
    return pl.pallas_call(
        functools.partial(_proj_res_kernel, n_in=n_in),
        out_shape=jax.ShapeDtypeStruct((m, n), F32),
        grid=(m // tm,),
        in_specs=in_specs,
        out_specs=pl.BlockSpec((tm, n), lambda i: (i, 0)),
        compiler_params=_cp("parallel"),
        name=name,
    )(*a_list, *([w] * n_in), res)


def _out_q_kernel(ya_ref, yb_ref, yc_ref, w_ref, res_ref, nw_ref, wq_ref, x_ref, q_ref, *, q_by_seq):
    x = res_ref[...]
    row0 = 0
    for y_ref in (ya_ref, yb_ref, yc_ref):
        kk = y_ref.shape[1]
        x = x + _dot(y_ref[...], w_ref[0, row0:row0 + kk, :])
        row0 += kk
    x_ref[...] = x
    q = _dot(_rmsnorm_rows(x, nw_ref[0]).astype(BF16), wq_ref[0])
    if q_by_seq is None:
        q_ref[...] = q
    else:
        nb, lt = q_by_seq
        for t in range(lt):
            for h in range(XA_HEADS):
                q_ref[:, h * lt + t, :] = q[t * nb:(t + 1) * nb, h * XA_HEAD_DIM:(h + 1) * XA_HEAD_DIM]


def out_q_proj(ya, yb, yc, w_out, res, nw, wq, layer, *, tm, q_by_seq=None):
    m, n = res.shape
    resident = lambda a: pl.BlockSpec((1,) + a.shape[1:], lambda i: (layer, 0, 0), pipeline_mode=pl.Buffered(1))
    if q_by_seq is None:
        q_shape = (m, wq.shape[2])
        q_spec = pl.BlockSpec((tm, wq.shape[2]), lambda i: (i, 0))
    else:
        assert tm == m == q_by_seq[0] * q_by_seq[1]
        q_shape = (q_by_seq[0], XA_HEADS * q_by_seq[1], XA_HEAD_DIM)
        q_spec = pl.BlockSpec(q_shape, lambda i: (0, 0, 0))
    return pl.pallas_call(
        functools.partial(_out_q_kernel, q_by_seq=q_by_seq),
        out_shape=(jax.ShapeDtypeStruct((m, n), F32), jax.ShapeDtypeStruct(q_shape, F32)),
        grid=(m // tm,),
        in_specs=[pl.BlockSpec((tm, ya.shape[1]), lambda i: (i, 0)),
                  pl.BlockSpec((tm, yb.shape[1]), lambda i: (i, 0)),
                  pl.BlockSpec((tm, yc.shape[1]), lambda i: (i, 0)),
                  resident(w_out),
                  pl.BlockSpec((tm, n), lambda i: (i, 0)),
                  _layer_spec((1, n), layer),
                  resident(wq)],
        out_specs=(pl.BlockSpec((tm, n), lambda i: (i, 0)), q_spec),
        compiler_params=_cp("parallel"),
        name="out_q_proj",
    )(ya, yb, yc, w_out, res, nw, wq)


CONVA_ROW_CHUNK = 32


def _conva_tile(ext_ref, cls_ref, w_ref, b_ref, lnw_ref, lnb_ref, y_ref, *, nb, rows):
    hist = (CONV_A_WIDTH - 1) * nb
    pad = _round_up(hist, SUBLANE)
    bias = b_ref[0]
    lnw = lnw_ref[0]
    lnb = lnb_ref[0]
    rc = CONVA_ROW_CHUNK
    if nb == 1:
        for s in range(SUBLANE):
            span = rows + SUBLANE * ((CONV_A_WIDTH - 1 - s) // SUBLANE)
            cls_ref[s, 0:span, :] = ext_ref[pl.ds(pad - hist + s, span), :]
    for r0 in range(0, rows, rc):
        acc = jnp.zeros((rc, D_A), F32) + bias
        if nb == 1:
            for k in range(CONV_A_WIDTH):
                s, jt = k % SUBLANE, k // SUBLANE
                acc = acc + w_ref[0, k:k + 1, :] * cls_ref[s, r0 + SUBLANE * jt:r0 + SUBLANE * jt + rc, :]
        else:
            for k in range(CONV_A_WIDTH):
                acc = acc + w_ref[0, k:k + 1, :] * ext_ref[pl.ds(pad - hist + k * nb + r0, rc), :]
        mu = jnp.mean(acc, axis=-1, keepdims=True)
        xc = acc - mu
        var = jnp.mean(xc * xc, axis=-1, keepdims=True)
        c = xc * lax.rsqrt(var + EPS) * lnw + lnb
        y_ref[r0:r0 + rc, :] = _silu(c).astype(y_ref.dtype)
    return ext_ref[pl.ds(pad + rows - hist, hist), :]


def _conva_scratch(nb, rows):
    hist = (CONV_A_WIDTH - 1) * nb
    pad = _round_up(hist, SUBLANE)
    shapes = [pltpu.VMEM((pad + rows, D_A), F32)]
    if nb == 1:
        shapes.append(pltpu.VMEM((SUBLANE, rows + SUBLANE * ((CONV_A_WIDTH - 1) // SUBLANE), D_A), F32))
    return shapes


def _conva_kernel(*refs, nb, lt, n_tiles, has_state):
    refs = list(refs)
    cls_ref = refs.pop() if nb == 1 else None
    if has_state:
        h_ref, w_ref, b_ref, lnw_ref, lnb_ref, st_ref, y_ref, nst_ref, ext_ref = refs
    else:
        h_ref, w_ref, b_ref, lnw_ref, lnb_ref, y_ref, nst_ref, ext_ref = refs
    hist = (CONV_A_WIDTH - 1) * nb
    pad = _round_up(hist, SUBLANE)
    rows = lt * nb
    j = pl.program_id(1)

    @pl.when(j == 0)
    def _():
        if has_state:
            ext_ref[pad - hist:pad, :] = st_ref[0]
        else:
            ext_ref[0:pad, :] = jnp.zeros((pad, D_A), F32)

    ext_ref[pad:pad + rows, :] = h_ref[:, 0:D_A] * _sigmoid(h_ref[:, D_A:2 * D_A])
    new_hist = _conva_tile(ext_ref, cls_ref, w_ref, b_ref, lnw_ref, lnb_ref, y_ref, nb=nb, rows=rows)
    nst_ref[0] = new_hist
    if n_tiles > 1:
        ext_ref[pad - hist:pad, :] = new_hist


def conva_mixer(h_a, p, layer, state, *, n_seq, nb, lt, n_tiles):
    rows = lt * nb
    hist = (CONV_A_WIDTH - 1) * nb
    pad = _round_up(hist, SUBLANE)
    has_state = state is not None
    in_specs = [pl.BlockSpec((rows, 2 * D_A), lambda s, j: (s * n_tiles + j, 0)),
                _layer_spec((CONV_A_WIDTH, D_A), layer),
                _layer_spec((1, D_A), layer), _layer_spec((1, D_A), layer), _layer_spec((1, D_A), layer)]
    args = [h_a, p['conv_a_w'], p['conv_a_b'], p['ln_a_w'], p['ln_a_b']]
    if has_state:
        assert n_seq == 1
        in_specs.append(_layer_spec((hist, D_A), layer))
        args.append(state)
    return pl.pallas_call(
        functools.partial(_conva_kernel, nb=nb, lt=lt, n_tiles=n_tiles, has_state=has_state),
        out_shape=(jax.ShapeDtypeStruct((h_a.shape[0], D_A), BF16),
                   jax.ShapeDtypeStruct((n_seq, hist, D_A), F32)),
        grid=(n_seq, n_tiles),
        in_specs=in_specs,
        out_specs=(pl.BlockSpec((rows, D_A), lambda s, j: (s * n_tiles + j, 0)),
                   pl.BlockSpec((1, hist, D_A), lambda s, j: (s, 0, 0))),
        scratch_shapes=_conva_scratch(nb, rows),
        compiler_params=_cp("parallel", "arbitrary"),
        name="conva_mixer",
    )(*args)


def _in_proj_conva_kernel(x_ref, nw_ref, w_ref, cw_ref, cb_ref, lnw_ref, lnb_ref,
                          ya_ref, hb_ref, hc_ref, nst_ref, ext_ref, cls_ref, *, tiles_per_seq):
    rows = x_ref.shape[0]
    hist = CONV_A_WIDTH - 1
    pad = _round_up(hist, SUBLANE)

    @pl.when(pl.program_id(0) % tiles_per_seq == 0)
    def _():
        ext_ref[0:pad, :] = jnp.zeros((pad, D_A), F32)

    xn = _rmsnorm_rows(x_ref[...], nw_ref[0]).astype(BF16)
    h_a = _dot(xn, w_ref[0, :, 0:IN_SPLITS[0]])
    ext_ref[pad:pad + rows, :] = h_a[:, 0:D_A] * _sigmoid(h_a[:, D_A:2 * D_A])
    hb_ref[...] = _dot(xn, w_ref[0, :, IN_SPLITS[0]:IN_SPLITS[1]])
    hc_ref[...] = _dot(xn, w_ref[0, :, IN_SPLITS[1]:IN_SPLITS[2]])
    new_hist = _conva_tile(ext_ref, cls_ref, cw_ref, cb_ref, lnw_ref, lnb_ref, ya_ref, nb=1, rows=rows)
    nst_ref[0] = new_hist
    ext_ref[pad - hist:pad, :] = new_hist


def in_proj_conva(x, p, layer, *, tm, tiles_per_seq):
    m, k = x.shape
    hist = CONV_A_WIDTH - 1
    wb, wc = IN_SPLITS[1] - IN_SPLITS[0], IN_SPLITS[2] - IN_SPLITS[1]
    return pl.pallas_call(
        functools.partial(_in_proj_conva_kernel, tiles_per_seq=tiles_per_seq),
        out_shape=(jax.ShapeDtypeStruct((m, D_A), BF16),
                   jax.ShapeDtypeStruct((m, wb), F32),
                   jax.ShapeDtypeStruct((m, wc), F32),
                   jax.ShapeDtypeStruct((m // tm, hist, D_A), F32)),
        grid=(m // tm,),
        in_specs=[pl.BlockSpec((tm, k), lambda i: (i, 0)),
                  _layer_spec((1, k), layer),
                  pl.BlockSpec((1, k, IN_SPLITS[2]), lambda i: (layer, 0, 0), pipeline_mode=pl.Buffered(1)),
                  _layer_spec((CONV_A_WIDTH, D_A), layer),
                  _layer_spec((1, D_A), layer), _layer_spec((1, D_A), layer), _layer_spec((1, D_A), layer)],
        out_specs=(pl.BlockSpec((tm, D_A), lambda i: (i, 0)),
                   pl.BlockSpec((tm, wb), lambda i: (i, 0)),
                   pl.BlockSpec((tm, wc), lambda i: (i, 0)),
                   pl.BlockSpec((1, hist, D_A), lambda i: (i, 0, 0))),
        scratch_shapes=_conva_scratch(1, tm),
        compiler_params=_cp("arbitrary"),
        name="in_proj_conva",
    )(x, p['norm_mix_w'], p['w_in'], p['conv_a_w'], p['conv_a_b'], p['ln_a_w'], p['ln_a_b'])


def _gelu_tanh(x):
    return x * (0.5 * (1.0 + jnp.tanh(math.sqrt(2.0 / math.pi) * (x + 0.044715 * (x * x * x)))))


S5_BLOCKS = S5_LANES // LANE
S5_SUPER = 2
S5_SUP_CH = D_B // S5_SUPER
S5_SUP_ST = S5_LANES // S5_SUPER


def _s5_b_proj(u, bb_ref, hs_ref):
    n = S5_LANES
    for sb in range(S5_SUPER):
        bu = _dot(u[:, sb * S5_SUP_CH:(sb + 1) * S5_SUP_CH].astype(BF16), bb_ref[0, sb])
        hs_ref[:, sb * S5_SUP_ST:(sb + 1) * S5_SUP_ST] = bu[:, 0:S5_SUP_ST]
        hs_ref[:, n + sb * S5_SUP_ST:n + (sb + 1) * S5_SUP_ST] = bu[:, S5_SUP_ST:2 * S5_SUP_ST]


def _s5_glu_out(hs_ref, u, cc_ref, d_ref, gw_ref, gb_ref, y_ref):
    n = S5_LANES
    ys = []
    for sb in range(S5_SUPER):
        h16 = jnp.concatenate([hs_ref[:, sb * S5_SUP_ST:(sb + 1) * S5_SUP_ST],
                               hs_ref[:, n + sb * S5_SUP_ST:n + (sb + 1) * S5_SUP_ST]], axis=1).astype(BF16)
        ys.append(_dot(h16, cc_ref[0, sb]))
    y = jnp.concatenate(ys, axis=1) + d_ref[0] * u
    y = _gelu_tanh(y)
    gate = _dot(y.astype(BF16), gw_ref[0]) + gb_ref[0]
    y_ref[...] = (y * _sigmoid(gate)).astype(y_ref.dtype)


def _s5_seq_kernel(u_ref, bb_ref, tab_ref, cc_ref, d_ref, gw_ref, gb_ref,
                   y_ref, nre_ref, nim_ref, hs_ref, cre_ref, cim_ref, *, lt):
    n = S5_LANES
    j = pl.program_id(1)

    @pl.when(j == 0)
    def _():
        cre_ref[...] = jnp.zeros(cre_ref.shape, F32)
        cim_ref[...] = jnp.zeros(cim_ref.shape, F32)

    u = u_ref[...]
    _s5_b_proj(u, bb_ref, hs_ref)

    def group(i, carry):
        r0 = pl.multiple_of(i * SUBLANE, SUBLANE)
        new = []
        for c in range(S5_BLOCKS):
            lr = slice(c * LANE, (c + 1) * LANE)
            li = slice(n + c * LANE, n + (c + 1) * LANE)
            xr = hs_ref[pl.ds(r0, SUBLANE), lr]
            xi = hs_ref[pl.ds(r0, SUBLANE), li]
            for lev in range(3):
                ar = tab_ref[0, lev, :, lr]
                ai = tab_ref[0, lev, :, li]
                sr = pltpu.roll(xr, 1 << lev, 0)
                si = pltpu.roll(xi, 1 << lev, 0)
                xr, xi = xr + ar * sr - ai * si, xi + ar * si + ai * sr
            pr = tab_ref[0, 3, :, lr]
            pi = tab_ref[0, 3, :, li]
            er, ei = carry[2 * c], carry[2 * c + 1]
            hr = xr + pr * er - pi * ei
            hi = xi + pr * ei + pi * er
            hs_ref[pl.ds(r0, SUBLANE), lr] = hr
            hs_ref[pl.ds(r0, SUBLANE), li] = hi
            new += [jnp.broadcast_to(hr[SUBLANE - 1:SUBLANE, :], (SUBLANE, LANE)),
                    jnp.broadcast_to(hi[SUBLANE - 1:SUBLANE, :], (SUBLANE, LANE))]
        return tuple(new)

    init = []
    for c in range(S5_BLOCKS):
        init += [cre_ref[:, c * LANE:(c + 1) * LANE], cim_ref[:, c * LANE:(c + 1) * LANE]]
    last = lax.fori_loop(0, lt // SUBLANE, group, tuple(init))
    for c in range(S5_BLOCKS):
        cre_ref[:, c * LANE:(c + 1) * LANE] = last[2 * c]
        cim_ref[:, c * LANE:(c + 1) * LANE] = last[2 * c + 1]
    nre_ref[0] = cre_ref[0:1, :]
    nim_ref[0] = cim_ref[0:1, :]
    _s5_glu_out(hs_ref, u, cc_ref, d_ref, gw_ref, gb_ref, y_ref)


def _s5_step_kernel(u_ref, bb_ref, ab_ref, cc_ref, d_ref, gw_ref, gb_ref, sre_ref, sim_ref,
                    y_ref, nre_ref, nim_ref, hs_ref, *, nb, lt):
    n = S5_LANES
    nblk = S5_BLOCKS
    u = u_ref[...]
    _s5_b_proj(u, bb_ref, hs_ref)
    ab_re = jnp.concatenate([ab_ref[0, c:c + 1, :] for c in range(nblk)], axis=1)
    ab_im = jnp.concatenate([ab_ref[0, nblk + c:nblk + c + 1, :] for c in range(nblk)], axis=1)
    hr = sre_ref[0]
    hi = sim_ref[0]
    for t in range(lt):
        rs = slice(t * nb, (t + 1) * nb)
        nr = ab_re * hr - ab_im * hi + hs_ref[rs, 0:n]
        ni = ab_re * hi + ab_im * hr + hs_ref[rs, n:2 * n]
        hr, hi = nr, ni
        hs_ref[rs, 0:n] = hr
        hs_ref[rs, n:2 * n] = hi
    nre_ref[0] = hr
    nim_ref[0] = hi
    _s5_glu_out(hs_ref, u, cc_ref, d_ref, gw_ref, gb_ref, y_ref)


def s5_mixer(h_b, p, layer, s_re, s_im, *, n_seq, nb, lt, n_tiles):
    rows = lt * nb
    n = S5_LANES
    has_state = s_re is not None
    in_specs = [pl.BlockSpec((rows, D_B), lambda s, j: (s * n_tiles + j, 0)),
                _layer_spec((S5_SUPER, S5_SUP_CH, 2 * S5_SUP_ST), layer),
                _layer_spec((2 * S5_BLOCKS, LANE), layer)]
    args = [h_b, p['s5_bb'], p['s5_ab']]
    if not has_state:
        in_specs[2] = _layer_spec((4, SUBLANE, 2 * n), layer)
        args[2] = p['s5_tab']
    in_specs += [_layer_spec((S5_SUPER, 2 * S5_SUP_ST, S5_SUP_CH), layer),
                 _layer_spec((1, D_B), layer),
                 _layer_spec((D_B, D_B), layer),
                 _layer_spec((1, D_B), layer)]
    args += [p['s5_cc'], p['s5_d'], p['s5_glu_w'], p['s5_glu_b']]
    if has_state:
        assert n_seq == 1 and n_tiles == 1
        in_specs += [_layer_spec((nb, n), layer)] * 2
        args += [s_re, s_im]
        body = functools.partial(_s5_step_kernel, nb=nb, lt=lt)
        scratch = [pltpu.VMEM((rows, 2 * n), F32)]
    else:
        assert nb == 1 and lt % SUBLANE == 0
        body = functools.partial(_s5_seq_kernel, lt=lt)
        scratch = [pltpu.VMEM((rows, 2 * n), F32),
                   pltpu.VMEM((SUBLANE, n), F32),
                   pltpu.VMEM((SUBLANE, n), F32)]
    st_spec = pl.BlockSpec((1, nb, n), lambda s, j: (s, 0, 0))
    return pl.pallas_call(
        body,
        out_shape=(jax.ShapeDtypeStruct((h_b.shape[0], D_B), BF16),
                   jax.ShapeDtypeStruct((n_seq, nb, n), F32),
                   jax.ShapeDtypeStruct((n_seq, nb, n), F32)),
        grid=(n_seq, n_tiles),
        in_specs=in_specs,
        out_specs=(pl.BlockSpec((rows, D_B), lambda s, j: (s * n_tiles + j, 0)), st_spec, st_spec),
        scratch_shapes=scratch,
        compiler_params=_cp("parallel", "arbitrary"),
        name="s5_mixer",
    )(*args)


def _group_rmsnorm(y, nw):
    half = D_C // SSD_GROUPS
    outs = []
    for g in range(SSD_GROUPS):
        yg = y[:, g * half:(g + 1) * half]
        outs.append(yg * lax.rsqrt(jnp.mean(yg * yg, axis=-1, keepdims=True) + EPS))
    return jnp.concatenate(outs, axis=1) * nw


def _mamba_p_kernel(h_ref, cw_ref, cb_ref, dtb_ref, alog_ref, dexp_ref, nw_ref, e_ref, tril_ref,
                    y_ref, ncst_ref, nsst_ref, ext_ref, st_ref, *, lt, n_tiles):
    q = SSD_CHUNK
    hist = SSD_CONV_WIDTH - 1
    pad = SUBLANE
    half = D_C // SSD_GROUPS
    hpg = SSD_HEADS // SSD_GROUPS
    j = pl.program_id(1)

    @pl.when(j == 0)
    def _():
        ext_ref[0:pad, :] = jnp.zeros((pad, D_XBC), F32)
        st_ref[...] = jnp.zeros(st_ref.shape, F32)

    ext_ref[pad:pad + lt, :] = h_ref[:, D_C:D_C + D_XBC]

    e = e_ref[...]
    tril = tril_ref[...]
    a_neg = -jnp.exp(alog_ref[0])
    li = lax.broadcasted_iota(jnp.int32, (q, q), 0)
    si = lax.broadcasted_iota(jnp.int32, (q, q), 1)
    causal = li >= si
    lane = lax.broadcasted_iota(jnp.int32, (q, LANE), 1)

    for c in range(lt // q):
        r0 = c * q
        acc = jnp.zeros((q, D_XBC), F32) + cb_ref[0]
        for k in range(SSD_CONV_WIDTH):
            acc = acc + cw_ref[0, k:k + 1, :] * ext_ref[pl.ds(pad - hist + k + r0, q), :]
        xc = _silu(acc)
        xs = xc[:, 0:D_C]
        z = h_ref[r0:r0 + q, 0:D_C]
        dt = _softplus(h_ref[r0:r0 + q, D_C + D_XBC:D_C + D_XBC + LANE] + dtb_ref[0])
        a = dt * a_neg
        hi_, mid_, lo_ = _split3(a)
        cs = _dot(tril, hi_) + _dot(tril, mid_) + _dot(tril, lo_)
        cs_last = cs[q - 1:q, :]
        dt_x = _expand(dt, e)
        ecs_x = _expand(jnp.exp(cs), e)
        edl_x = _expand(jnp.exp(cs_last - cs), e)
        xdt = xs * dt_x
        cs_t = cs.T

        y_parts = []
        for g in range(SSD_GROUPS):
            bm = xc[:, D_C + g * SSD_STATE:D_C + (g + 1) * SSD_STATE]
            cm = xc[:, D_C + SSD_GROUPS * SSD_STATE + g * SSD_STATE:
                    D_C + SSD_GROUPS * SSD_STATE + (g + 1) * SSD_STATE]
            bm16 = bm.astype(BF16)
            cm16 = cm.astype(BF16)
            cb = _dot_nt(cm16, bm16)
            for pr in range(hpg // 2):
                r_even = g * hpg + 2 * pr
                xpair = xdt[:, r_even * SSD_HEAD_DIM:(r_even + 2) * SSD_HEAD_DIM].astype(BF16)
                ys = []
                for r in (r_even, r_even + 1):
                    seg = cs[:, r:r + 1] - cs_t[r:r + 1, :]
                    dec = jnp.exp(jnp.where(causal, seg, -jnp.inf))
                    ys.append(_dot((cb * dec).astype(BF16), xpair))
                y_parts.append(jnp.where(lane < SSD_HEAD_DIM, ys[0], ys[1]))
        y_diag = jnp.concatenate(y_parts, axis=1)
        y_off = jnp.concatenate(
            [_dot(xc[:, D_C + SSD_GROUPS * SSD_STATE + g * SSD_STATE:
                      D_C + SSD_GROUPS * SSD_STATE + (g + 1) * SSD_STATE].astype(BF16),
                  st_ref[:, g * half:(g + 1) * half].astype(BF16)) for g in range(SSD_GROUPS)],
            axis=1) * ecs_x
        y = y_diag + y_off + dexp_ref[0] * xs
        y = y * _silu(z)
        y_ref[r0:r0 + q, :] = _group_rmsnorm(y, nw_ref[0]).astype(y_ref.dtype)

        xw = (xdt * edl_x).astype(BF16)
        dec_row = ecs_x[q - 1:q, :]
        for g in range(SSD_GROUPS):
            bm_t = xc[:, D_C + g * SSD_STATE:D_C + (g + 1) * SSD_STATE].T.astype(BF16)
            upd = _dot(bm_t, xw[:, g * half:(g + 1) * half])
            st_ref[:, g * half:(g + 1) * half] = (
                st_ref[:, g * half:(g + 1) * half] * dec_row[:, g * half:(g + 1) * half] + upd)

    new_hist = ext_ref[pl.ds(pad + lt - hist, hist), :]
    ncst_ref[0] = new_hist
    if n_tiles > 1:
        ext_ref[pad - hist:pad, :] = new_hist

    @pl.when(j == n_tiles - 1)
    def _():
        for blk in range(D_C // LANE):
            nsst_ref[0, blk * LANE:(blk + 1) * LANE, :] = st_ref[:, blk * LANE:(blk + 1) * LANE].T


def _ssd_consts():
    head_of_lane = jnp.arange(D_C) // SSD_HEAD_DIM
    e = (jnp.arange(LANE)[:, None] == head_of_lane[None, :]).astype(BF16)
    tril = (jnp.arange(SSD_CHUNK)[:, None] >= jnp.arange(SSD_CHUNK)[None, :]).astype(BF16)
    return e, tril


def _ssd_param_specs(layer):
    return [_layer_spec((SSD_CONV_WIDTH, D_XBC), layer),
            _layer_spec((1, D_XBC), layer),
            _layer_spec((1, LANE), layer),
            _layer_spec((1, LANE), layer),
            _layer_spec((1, D_C), layer),
            _layer_spec((1, D_C), layer)]


def _ssd_param_args(p):
    return [p['conv_c_w'], p['conv_c_b'], p['ssd_dt_bias'], p['ssd_a_log'], p['ssd_d'], p['ssd_norm_w']]


def mamba_prompt(h_c, p, layer, *, n_seq, lt, n_tiles):
    hist = SSD_CONV_WIDTH - 1
    const = lambda s, j: (0, 0)
    return pl.pallas_call(
        functools.partial(_mamba_p_kernel, lt=lt, n_tiles=n_tiles),
        out_shape=(jax.ShapeDtypeStruct((h_c.shape[0], D_C), BF16),
                   jax.ShapeDtypeStruct((n_seq, hist, D_XBC), F32),
                   jax.ShapeDtypeStruct((n_seq, D_C, SSD_STATE), F32)),
        grid=(n_seq, n_tiles),
        in_specs=[pl.BlockSpec((lt, D_HC), lambda s, j: (s * n_tiles + j, 0))]
        + _ssd_param_specs(layer)
        + [pl.BlockSpec((LANE, D_C), const), pl.BlockSpec((SSD_CHUNK, SSD_CHUNK), const)],
        out_specs=(pl.BlockSpec((lt, D_C), lambda s, j: (s * n_tiles + j, 0)),
                   pl.BlockSpec((1, hist, D_XBC), lambda s, j: (s, 0, 0)),
                   pl.BlockSpec((1, D_C, SSD_STATE), lambda s, j: (s, 0, 0))),
        scratch_shapes=[pltpu.VMEM((SUBLANE + lt, D_XBC), F32),
                        pltpu.VMEM((SSD_STATE, D_C), F32)],
        compiler_params=_cp("parallel", "arbitrary"),
        name="mamba_prompt",
    )(h_c, *_ssd_param_args(p), p['ssd_e'], p['ssd_tril'])


def _ks(c, k, nb):
    return slice((c * SUBLANE + k) * nb, (c * SUBLANE + k + 1) * nb)


def _slab_put(ref, k, slab, nb):
    for c in range(slab.shape[1] // LANE):
        ref[_ks(c, k, nb), :] = slab[:, c * LANE:(c + 1) * LANE]


def _slab_get(ref, k, n_blocks, nb):
    return jnp.concatenate([ref[_ks(c, k, nb), :] for c in range(n_blocks)], axis=1)


def _seq_get(ref, b, n_blocks, nb):
    return jnp.concatenate(
        [ref[pl.ds(c * SUBLANE * nb + b, SUBLANE, stride=nb), :] for c in range(n_blocks)], axis=1)


def _seq_put(ref, b, val, nb, c0=0):
    for c in range(val.shape[1] // LANE):
        ref[pl.ds((c0 + c) * SUBLANE * nb + b, SUBLANE, stride=nb), :] = val[:, c * LANE:(c + 1) * LANE]


def _mamba_s_kernel(h_ref, cw_ref, cb_ref, dtb_ref, alog_ref, dexp_ref, nw_ref, e_ref, cst_ref, sst_ref,
                    y_ref, ncst_ref, nsst_ref,
                    ext_ref, xs_ref, dt_ref, cs_ref, lhs_ref, rhs_ref, c8_ref, yoff_ref,
                    *, nb, lt, bb, lsel, passthrough):
    hist = (SSD_CONV_WIDTH - 1) * nb
    rows = lt * nb
    half = D_C // SSD_GROUPS
    hpg = SSD_HEADS // SSD_GROUPS
    xblk = D_C // LANE
    hblk = half // LANE
    i = pl.program_id(0)
    n_steps = pl.num_programs(0)
    bc_off = D_C
    cc_off = D_C + SSD_GROUPS * SSD_STATE

    @pl.when(i == 0)
    def _phase1():
        e = e_ref[...]
        ext_ref[0:hist, :] = cst_ref[0]
        ext_ref[hist:hist + rows, :] = h_ref[:, D_C:D_C + D_XBC]
        ncst_ref[...] = ext_ref[rows:rows + hist, :]
        a_neg = -jnp.exp(alog_ref[0])
        lhs_ref[...] = jnp.zeros(lhs_ref.shape, F32)
        rhs_ref[...] = jnp.zeros(rhs_ref.shape, F32)
        c8_ref[...] = jnp.zeros(c8_ref.shape, F32)
        cs = jnp.zeros((nb, LANE), F32)
        for t in range(lt):
            rs = slice(t * nb, (t + 1) * nb)
            acc = jnp.zeros((nb, D_XBC), F32) + cb_ref[0]
            for k in range(SSD_CONV_WIDTH):
                acc = acc + cw_ref[0, k:k + 1, :] * ext_ref[(t + k) * nb:(t + k + 1) * nb, :]
            xc = _silu(acc)
            xs_ref[rs, :] = xc[:, 0:D_C]
            for g in range(SSD_GROUPS):
                rhs_ref[_ks(2 * g, t, nb), :] = xc[:, bc_off + g * SSD_STATE:bc_off + (g + 1) * SSD_STATE]
            _slab_put(c8_ref, t, xc[:, cc_off:cc_off + SSD_GROUPS * SSD_STATE], nb)
            dt = _softplus(h_ref[rs, D_C + D_XBC:D_C + D_XBC + LANE] + dtb_ref[0])
            dt_ref[rs, :] = dt
            cs = cs + dt * a_neg
            cs_ref[rs, :] = cs
        cs_last = cs
        for t in range(lt):
            rs = slice(t * nb, (t + 1) * nb)
            wt = jnp.exp(cs_last - cs_ref[rs, :]) * dt_ref[rs, :]
            _slab_put(lhs_ref, t, xs_ref[rs, :] * _expand(wt, e), nb)
        dec = _expand(jnp.exp(cs_last), e)
        d_hi = dec.astype(BF16).astype(F32)
        d_r = dec - d_hi
        d_mid = d_r.astype(BF16).astype(F32)
        d_lo = d_r - d_mid
        ones = jnp.ones((nb, SSD_STATE), F32)
        for k, piece in enumerate((d_hi, d_mid, d_lo)):
            _slab_put(lhs_ref, lt + k, piece, nb)
            for g in range(SSD_GROUPS):
                rhs_ref[_ks(2 * g + 1, lt + k, nb), :] = ones

    for jb in range(bb):
        b = i * bb + jb
        l8 = _seq_get(lhs_ref, b, xblk, nb).astype(BF16)
        r8 = _seq_get(rhs_ref, b, 2 * SSD_GROUPS, nb).astype(BF16)
        c8 = _seq_get(c8_ref, b, SSD_GROUPS, nb).astype(BF16)
        for g in range(SSD_GROUPS):
            s = sst_ref[lsel, jb, g * half:(g + 1) * half, :]
            yo = _dot_nt(c8[:, g * SSD_STATE:(g + 1) * SSD_STATE], s.astype(BF16))
            _seq_put(yoff_ref, b, yo, nb, c0=g * hblk)
            upd = _dot_tn(l8[:, g * half:(g + 1) * half],
                          r8[:, g * 2 * SSD_STATE:(g + 1) * 2 * SSD_STATE])
            nsst_ref[lsel, jb, g * half:(g + 1) * half, :] = upd[:, SSD_STATE:] * s + upd[:, :SSD_STATE]
    for d in passthrough:
        nsst_ref[d] = sst_ref[d]

    @pl.when(i == n_steps - 1)
    def _phase3():
        e = e_ref[...]
        lane = lax.broadcasted_iota(jnp.int32, (nb, LANE), 1)
        for t in range(lt):
            rt = slice(t * nb, (t + 1) * nb)
            cs_t = cs_ref[rt, :]
            y = (_slab_get(yoff_ref, t, xblk, nb) * _expand(jnp.exp(cs_t), e)
                 + dexp_ref[0] * xs_ref[rt, :])
            for s_ in range(t + 1):
                rsl = slice(s_ * nb, (s_ + 1) * nb)
                cbs = []
                for g in range(SSD_GROUPS):
                    cm = c8_ref[_ks(g, t, nb), :]
                    bm = rhs_ref[_ks(2 * g, s_, nb), :]
                    cbs.append(jnp.sum(cm * bm, axis=-1, keepdims=True))
                cb = jnp.where(lane < hpg, cbs[0], cbs[1])
                m = jnp.exp(cs_t - cs_ref[rsl, :]) * dt_ref[rsl, :] * cb
                y = y + _expand(m, e) * xs_ref[rsl, :]
            y = y * _silu(h_ref[rt, 0:D_C])
            y_ref[rt, :] = _group_rmsnorm(y, nw_ref[0]).astype(y_ref.dtype)


def mamba_sample(h_c, p, layer, cst, sst, *, nb, lt, bb, in_place):
    rows = lt * nb
    hist = (SSD_CONV_WIDTH - 1) * nb
    depth = sst.shape[0]
    const = lambda i: (0, 0)
    if in_place:
        sst_spec = pl.BlockSpec((1, bb, D_C, SSD_STATE), lambda i: (layer, i, 0, 0))
        lsel, passthrough = 0, ()
    else:
        sst_spec = pl.BlockSpec((depth, bb, D_C, SSD_STATE), lambda i: (0, i, 0, 0))
        lsel, passthrough = layer, tuple(d for d in range(depth) if d != layer)
    in_specs = ([pl.BlockSpec((rows, D_HC), const)] + _ssd_param_specs(layer)
                + [pl.BlockSpec((LANE, D_C), const), _layer_spec((hist, D_XBC), layer), sst_spec])
    return pl.pallas_call(
        functools.partial(_mamba_s_kernel, nb=nb, lt=lt, bb=bb, lsel=lsel, passthrough=passthrough),
        out_shape=(jax.ShapeDtypeStruct((rows, D_C), BF16),
                   jax.ShapeDtypeStruct((hist, D_XBC), F32),
                   jax.ShapeDtypeStruct(sst.shape, F32)),
        grid=(nb // bb,),
        in_specs=in_specs,
        out_specs=(pl.BlockSpec((rows, D_C), const),
                   pl.BlockSpec((hist, D_XBC), const),
                   sst_spec),
        input_output_aliases={len(in_specs) - 1: 2} if in_place else {},
        scratch_shapes=[pltpu.VMEM((hist + rows, D_XBC), F32),
                        pltpu.VMEM((rows, D_C), F32),
                        pltpu.VMEM((rows, LANE), F32),
                        pltpu.VMEM((rows, LANE), F32),
                        pltpu.VMEM((D_C // LANE * SUBLANE * nb, LANE), F32),
                        pltpu.VMEM((2 * SSD_GROUPS * SUBLANE * nb, LANE), F32),
                        pltpu.VMEM((SSD_GROUPS * SUBLANE * nb, LANE), F32),
                        pltpu.VMEM((D_C // LANE * SUBLANE * nb, LANE), F32)],
        compiler_params=_cp("arbitrary"),
        name="mamba_sample",
    )(h_c, *_ssd_param_args(p), p['ssd_e'], cst, sst)


def _softmax_rows(s):
    m = jnp.max(s, axis=-1, keepdims=True)
    ex = jnp.exp(s - m)
    return ex / jnp.sum(ex, axis=-1, keepdims=True)


def _attn_p_kernel(q_ref, k_ref, v_ref, o_ref):
    for h in range(XA_HEADS):
        hs = slice(h * XA_HEAD_DIM, (h + 1) * XA_HEAD_DIM)
        s = _dot_nt(q_ref[:, hs].astype(BF16), k_ref[0, :, hs].astype(BF16)) / math.sqrt(XA_HEAD_DIM)
        p = _softmax_rows(s)
        o_ref[:, hs] = _dot(p.astype(BF16), v_ref[0, :, hs].astype(BF16)).astype(o_ref.dtype)


def attn_prompt(q, k, v, layer, *, n_seq, seq, tq):
    n_tiles = seq // tq
    return pl.pallas_call(
        _attn_p_kernel,
        out_shape=jax.ShapeDtypeStruct(q.shape, BF16),
        grid=(n_seq, n_tiles),
        in_specs=[pl.BlockSpec((tq, D_MODEL), lambda s, j: (s * n_tiles + j, 0)),
                  pl.BlockSpec((1, N_MEM, D_MODEL), lambda s, j: (layer * n_seq + s, 0, 0)),
                  pl.BlockSpec((1, N_MEM, D_MODEL), lambda s, j: (layer * n_seq + s, 0, 0))],
        out_specs=pl.BlockSpec((tq, D_MODEL), lambda s, j: (s * n_tiles + j, 0)),
        compiler_params=_cp("parallel", "arbitrary"),
        name="attn_prompt",
    )(q, k, v)


def _attn_s_kernel(q_ref, k_ref, v_ref, o_ref, *, bb, lt):
    rows = XA_HEADS * lt
    n = N_MEM * XA_HEADS
    col_head = lax.broadcasted_iota(jnp.int32, (rows, n), 1) % XA_HEADS
    row_head = lax.broadcasted_iota(jnp.int32, (rows, n), 0) // lt
    same_head = col_head == row_head
    for jb in range(bb):
        k = k_ref[0, jb].reshape(n, XA_HEAD_DIM).astype(BF16)
        v = v_ref[0, jb].reshape(n, XA_HEAD_DIM).astype(BF16)
        s = _dot_nt(q_ref[jb].astype(BF16), k) / math.sqrt(XA_HEAD_DIM)
        p = _softmax_rows(jnp.where(same_head, s, -jnp.inf))
        o_ref[jb] = _dot(p.astype(BF16), v)


def attn_sample(q, k, v, layer, *, bb):
    nb, rows, _ = q.shape
    kv_spec = pl.BlockSpec((1, bb, N_MEM, XA_HEADS, XA_HEAD_DIM), lambda i: (layer, i, 0, 0, 0))
    return pl.pallas_call(
        functools.partial(_attn_s_kernel, bb=bb, lt=rows // XA_HEADS),
        out_shape=jax.ShapeDtypeStruct((nb, rows, XA_HEAD_DIM), F32),
        grid=(nb // bb,),
        in_specs=[pl.BlockSpec((bb, rows, XA_HEAD_DIM), lambda i: (i, 0, 0)), kv_spec, kv_spec],
        out_specs=pl.BlockSpec((bb, rows, XA_HEAD_DIM), lambda i: (i, 0, 0)),
        compiler_params=_cp("parallel"),
        name="attn_sample",
    )(q, k, v)


def _ffn_kernel(*refs, nb, tiles_per_seq, has_state, final_norm):
    refs = list(refs)
    x_ref, nw_ref, wg_ref, wu_ref, cw_ref, cb_ref, wd_ref = refs[:7]
    pos = 7
    st_ref = None
    if has_state:
        st_ref = refs[pos]
        pos += 1
    fw_ref = None
    if final_norm:
        fw_ref = refs[pos]
        pos += 1
    o_ref, nst_ref, xn_ref, gext_ref, carry_ref = refs[pos:pos + 5]

    hist = (FFN_CONV_WIDTH - 1) * nb
    pad = _round_up(hist, SUBLANE)
    tm = x_ref.shape[0]
    tf = wg_ref.shape[3]
    i = pl.program_id(0)
    f = pl.program_id(1)
    n_f = pl.num_programs(1)

    @pl.when(f == 0)
    def _():
        x = x_ref[...]
        xn_ref[...] = _rmsnorm_rows(x, nw_ref[0]).astype(BF16)
        o_ref[...] = x

    if tiles_per_seq > 1:
        first = (i % tiles_per_seq) == 0

        @pl.when(first)
        def _():
            if has_state:
                gext_ref[pad - hist:pad, :] = st_ref[0]
            else:
                gext_ref[0:pad, :] = jnp.zeros((pad, tf), F32)

        @pl.when(jnp.logical_not(first))
        def _():
            gext_ref[0:pad, :] = carry_ref[f]
    else:
        if has_state:
            gext_ref[pad - hist:pad, :] = st_ref[0]
        else:
            gext_ref[0:pad, :] = jnp.zeros((pad, tf), F32)

    xn = xn_ref[...]
    g = _dot(xn, wg_ref[0, 0])
    up = _dot(xn, wu_ref[0, 0])
    gext_ref[pad:pad + tm, :] = g
    conv = (cw_ref[0, 0:1, :] * gext_ref[pl.ds(pad - 2 * nb, tm), :]
            + cw_ref[0, 1:2, :] * gext_ref[pl.ds(pad - nb, tm), :]
            + cw_ref[0, 2:3, :] * g + cb_ref[0])
    act = _silu(conv) * up
    o_ref[...] += _dot(act.astype(BF16), wd_ref[0])

    nst_ref[0] = gext_ref[pl.ds(pad + tm - hist, hist), :]
    if tiles_per_seq > 1:
        carry_ref[f] = gext_ref[pl.ds(tm, pad), :]

    if final_norm:
        @pl.when(f == n_f - 1)
        def _():
            o_ref[...] = _rmsnorm_rows(o_ref[...], fw_ref[...])


def conv_ffn(x, p, layer, state, final_w, *, n_seq, nb, tm, tiles_per_seq):
    m = x.shape[0]
    tf = FF_TILE
    n_f = D_FF_PAD // tf
    hist = (FFN_CONV_WIDTH - 1) * nb
    pad = _round_up(hist, SUBLANE)
    has_state = state is not None
    final_norm = final_w is not None
    in_specs = [pl.BlockSpec((tm, D_MODEL), lambda i, f: (i, 0)),
                _layer_spec((1, D_MODEL), layer),
                pl.BlockSpec((1, 1, D_MODEL, tf), lambda i, f: (layer, f, 0, 0)),
                pl.BlockSpec((1, 1, D_MODEL, tf), lambda i, f: (layer, f, 0, 0)),
                pl.BlockSpec((1, FFN_CONV_WIDTH, tf), lambda i, f: (layer, 0, f)),
                pl.BlockSpec((1, 1, tf), lambda i, f: (layer, 0, f)),
                pl.BlockSpec((1, tf, D_MODEL), lambda i, f: (layer, f, 0))]
    args = [x, p['norm_ffn_w'], p['ffn_wg'], p['ffn_wu'], p['ffn_conv_w'], p['ffn_conv_b'], p['ffn_wd']]
    if has_state:
        assert n_seq == 1 and tiles_per_seq == 1
        in_specs.append(pl.BlockSpec((1, hist, tf), lambda i, f: (layer, 0, f)))
        args.append(state)
    if final_norm:
        in_specs.append(pl.BlockSpec((1, D_MODEL), lambda i, f: (0, 0)))
        args.append(final_w.reshape(1, D_MODEL))
    return pl.pallas_call(
        functools.partial(_ffn_kernel, nb=nb, tiles_per_seq=tiles_per_seq, has_state=has_state,
                          final_norm=final_norm),
        out_shape=(jax.ShapeDtypeStruct((m, D_MODEL), F32),
                   jax.ShapeDtypeStruct((m // tm, hist, D_FF_PAD), F32)),
        grid=(m // tm, n_f),
        in_specs=in_specs,
        out_specs=(pl.BlockSpec((tm, D_MODEL), lambda i, f: (i, 0)),
                   pl.BlockSpec((1, hist, tf), lambda i, f: (i, 0, f))),
        scratch_shapes=[pltpu.VMEM((tm, D_MODEL), BF16),
                        pltpu.VMEM((pad + tm, tf), F32),
                        pltpu.VMEM((n_f, pad, tf), F32)],
        compiler_params=_cp("arbitrary", "arbitrary"),
        name="conv_ffn",
    )(*args)


def _s5_params(lam_re, lam_im, log_dt, b_re, b_im, c_re, c_im):
    depth = lam_re.shape[0]
    dt = jnp.exp(log_dt)[..., None]
    mag = jnp.exp(lam_re * dt)
    ang = lam_im * dt
    ab_re, ab_im = mag * jnp.cos(ang), mag * jnp.sin(ang)
    blocks = lambda re, im: jnp.concatenate(
        [re.reshape(*re.shape[:-2], S5_BLOCKS, LANE), im.reshape(*im.shape[:-2], S5_BLOCKS, LANE)], axis=-2)
    row = jnp.arange(SUBLANE, dtype=F32)
    expo = jnp.stack([jnp.full((SUBLANE,), 1.0), jnp.full((SUBLANE,), 2.0), jnp.full((SUBLANE,), 4.0),
                      row + 1.0])
    keep = jnp.stack([row >= 1, row >= 2, row >= 4, row >= 0]).astype(F32)
    lam_dt = (lam_re * dt).reshape(depth, 1, 1, S5_LANES)
    pang = ang.reshape(depth, 1, 1, S5_LANES) * expo[None, :, :, None]
    pmag = jnp.exp(lam_dt * expo[None, :, :, None]) * keep[None, :, :, None]
    tab = jnp.concatenate([pmag * jnp.cos(pang), pmag * jnp.sin(pang)], axis=-1)
    den = lam_re * lam_re + lam_im * lam_im
    nr, ni = ab_re - 1.0, ab_im
    co_re = (nr * lam_re + ni * lam_im) / den
    co_im = (ni * lam_re - nr * lam_im) / den
    bb_re = co_re[..., None] * b_re - co_im[..., None] * b_im
    bb_im = co_re[..., None] * b_im + co_im[..., None] * b_re
    gps = S5_GROUPS // S5_SUPER
    eye = jnp.eye(gps, dtype=F32)
    sup = lambda m: m.reshape(depth, S5_SUPER, gps, *m.shape[2:])
    dense_b = lambda m: jnp.einsum('lsgph,gk->lsghkp', sup(m), eye).reshape(depth, S5_SUPER, S5_SUP_CH, S5_SUP_ST)
    dense_c = lambda m: jnp.einsum('lsghp,gk->lskpgh', sup(m), eye).reshape(depth, S5_SUPER, S5_SUP_ST, S5_SUP_CH)
    bb = jnp.concatenate([dense_b(bb_re), dense_b(bb_im)], axis=3).astype(BF16)
    cc = jnp.concatenate([dense_c(c_re), -dense_c(c_im)], axis=2).astype(BF16)
    return bb, blocks(ab_re, ab_im), tab, cc


def _wprep_kernel(w_ref, o_ref, *, axis, valid_last):
    f = pl.program_id(1)
    last = pl.num_programs(1) - 1
    o = o_ref.at[0, 0] if axis == 1 else o_ref.at[0]

    @pl.when(f < last)
    def _():
        o[...] = w_ref[0].astype(BF16)

    @pl.when(f == last)
    def _():
        if axis == 1:
            o[:, :valid_last] = w_ref[0, :, :valid_last].astype(BF16)
            o[:, valid_last:] = jnp.zeros((o.shape[0], o.shape[1] - valid_last), BF16)
        else:
            o[:valid_last, :] = w_ref[0, :valid_last, :].astype(BF16)
            o[valid_last:, :] = jnp.zeros((o.shape[0] - valid_last, o.shape[1]), BF16)


def ffn_weight_cols(w, tf):
    depth, k, n = w.shape
    n_f = pl.cdiv(n, tf)
    return pl.pallas_call(
        functools.partial(_wprep_kernel, axis=1, valid_last=n - (n_f - 1) * tf),
        out_shape=jax.ShapeDtypeStruct((depth, n_f, k, tf), BF16),
        grid=(depth, n_f),
        in_specs=[pl.BlockSpec((1, k, tf), lambda l, f: (l, 0, f))],
        out_specs=pl.BlockSpec((1, 1, k, tf), lambda l, f: (l, f, 0, 0)),
        compiler_params=_cp("parallel", "parallel"),
        name="ffn_weight_cols",
    )(w)


def ffn_weight_rows(w, tf):
    depth, k, n = w.shape
    n_f = pl.cdiv(k, tf)
    return pl.pallas_call(
        functools.partial(_wprep_kernel, axis=0, valid_last=k - (n_f - 1) * tf),
        out_shape=jax.ShapeDtypeStruct((depth, n_f * tf, n), BF16),
        grid=(depth, n_f),
        in_specs=[pl.BlockSpec((1, tf, n), lambda l, f: (l, f, 0))],
        out_specs=pl.BlockSpec((1, tf, n), lambda l, f: (l, f, 0)),
        compiler_params=_cp("parallel", "parallel"),
        name="ffn_weight_rows",
    )(w)


def kernel(x_prompt, x_sample, mem_prompt, cache_mem_k, cache_mem_v, state_conv_a, state_s5_re, state_s5_im, state_conv_c, state_ssd, state_ffn_conv, norm_mix_w, w_in, conv_a_w, conv_a_b, ln_a_w, ln_a_b, s5_lam_re, s5_lam_im, s5_log_dt, s5_b_re, s5_b_im, s5_c_re, s5_c_im, s5_d, s5_glu_w, s5_glu_b, conv_c_w, conv_c_b, ssd_dt_bias, ssd_a_log, ssd_d, ssd_norm_w, w_out, norm_xa_w, norm_mem_w, xa_wq, xa_wk, xa_wv, xa_wo, norm_ffn_w, ffn_w_gate, ffn_w_up, ffn_conv_w, ffn_conv_b, ffn_w_down, final_norm_w):
    bp, seq, _ = x_prompt.shape
    nbs, lts, _ = x_sample.shape
    depth = w_in.shape[0]
    n_mem = mem_prompt.shape[1]
    lt_p = 512 if seq % 512 == 0 else seq
    n_tiles_p = seq // lt_p
    tm_p = lt_p
    tm_s = lts * nbs
    tm_f = 1024 if seq % 1024 == 0 else tm_p
    tm_m = min(256, bp * n_mem)

    vec = lambda a: a.reshape(depth, 1, a.shape[-1])
    pad_lanes = lambda a: vec(jnp.pad(a, ((0, 0), (0, LANE - a.shape[-1]))))
    ff_pad = D_FF_PAD - D_FF
    s5_bb, s5_ab, s5_tab, s5_cc = _s5_params(s5_lam_re, s5_lam_im, s5_log_dt, s5_b_re, s5_b_im, s5_c_re, s5_c_im)
    ssd_e, ssd_tril = _ssd_consts()
    p = {
        'norm_mix_w': vec(norm_mix_w), 'norm_xa_w': vec(norm_xa_w), 'norm_mem_w': vec(norm_mem_w),
        'norm_ffn_w': vec(norm_ffn_w),
        'w_in': jnp.pad(w_in, ((0, 0), (0, 0), (0, IN_SPLITS[2] - w_in.shape[2]))).astype(BF16),
        'conv_a_w': conv_a_w, 'conv_a_b': vec(conv_a_b), 'ln_a_w': vec(ln_a_w), 'ln_a_b': vec(ln_a_b),
        's5_bb': s5_bb, 's5_ab': s5_ab, 's5_tab': s5_tab, 's5_cc': s5_cc,
        's5_d': vec(s5_d), 's5_glu_w': s5_glu_w.astype(BF16), 's5_glu_b': vec(s5_glu_b),
        'conv_c_w': conv_c_w, 'conv_c_b': vec(conv_c_b),
        'ssd_dt_bias': pad_lanes(ssd_dt_bias), 'ssd_a_log': pad_lanes(ssd_a_log),
        'ssd_d': vec(jnp.repeat(ssd_d, SSD_HEAD_DIM, axis=1)), 'ssd_norm_w': vec(ssd_norm_w),
        'ssd_e': ssd_e, 'ssd_tril': ssd_tril,
        'w_out': w_out.astype(BF16), 'wq': xa_wq.astype(BF16), 'wo': xa_wo.astype(BF16),
        'wk': xa_wk.astype(BF16), 'wv': xa_wv.astype(BF16),
        'ffn_wg': ffn_weight_cols(ffn_w_gate, FF_TILE),
        'ffn_wu': ffn_weight_cols(ffn_w_up, FF_TILE),
        'ffn_wd': ffn_weight_rows(ffn_w_down, FF_TILE),
        'ffn_conv_w': jnp.pad(ffn_conv_w, ((0, 0), (0, 0), (0, ff_pad))),
        'ffn_conv_b': vec(jnp.pad(ffn_conv_b, ((0, 0), (0, ff_pad)))),
    }

    tmaj = lambda a: a.transpose(0, 2, 1, 3).reshape(depth, a.shape[2] * nbs, a.shape[3])
    st_conv_a = tmaj(state_conv_a)
    st_conv_c = tmaj(state_conv_c)
    st_ffn = jnp.pad(tmaj(state_ffn_conv), ((0, 0), (0, 0), (0, ff_pad)))
    st_re = state_s5_re.reshape(depth, nbs, S5_LANES)
    st_im = state_s5_im.reshape(depth, nbs, S5_LANES)
    ssd_all = state_ssd.reshape(depth, nbs, D_C, SSD_STATE)

    xp = x_prompt.reshape(bp * seq, D_MODEL)
    xs = x_sample.transpose(1, 0, 2).reshape(lts * nbs, D_MODEL)
    mem2d = mem_prompt.reshape(bp * n_mem, D_MODEL)

    def mixers(x, l, *, n_seq, nb, lt, n_tiles, tm, sample):
        if sample:
            h_a, h_b, h_c = in_proj(x, p['norm_mix_w'], p['w_in'], l, tm=tm)
            ya, n_conv_a = conva_mixer(h_a, p, l, st_conv_a, n_seq=n_seq, nb=nb, lt=lt, n_tiles=n_tiles)
        else:
            ya, h_b, h_c, n_conv_a = in_proj_conva(x, p, l, tm=tm, tiles_per_seq=n_tiles)
            n_conv_a = n_conv_a[n_tiles - 1::n_tiles]
        lt_b = lt if sample else tq_p
        yb, n_re, n_im = s5_mixer(h_b, p, l, st_re if sample else None, st_im if sample else None,
                                  n_seq=n_seq, nb=nb, lt=lt_b, n_tiles=lt * n_tiles // lt_b)
        return h_c, ya, yb, n_conv_a, n_re, n_im

    mk, mv, p_mk, p_mv = mem_kv(mem2d, p['norm_mem_w'], p['wk'], p['wv'], tm=tm_m)
    mk = mk.reshape(depth * bp, n_mem, D_MODEL)
    mv = mv.reshape(depth * bp, n_mem, D_MODEL)
    tq_p = 1024 if seq % 1024 == 0 else lt_p

    outs_p = [[] for _ in range(6)]
    outs_s = [[] for _ in range(5)]
    for l in range(depth):
        last = l == depth - 1

        h_c, ya, yb, p_conv_a, p_re, p_im = mixers(xp, l, n_seq=bp, nb=1, lt=lt_p, n_tiles=n_tiles_p,
                                                   tm=tm_p, sample=False)
        yc, p_conv_c, p_ssd = mamba_prompt(h_c, p, l, n_seq=bp, lt=lt_p, n_tiles=n_tiles_p)
        xp, q = out_q_proj(ya, yb, yc, p['w_out'], xp, p['norm_xa_w'], p['wq'], l, tm=tm_p)
        o = attn_prompt(q, mk, mv, l, n_seq=bp, seq=seq, tq=tq_p)
        xp = proj_res([o], p['wo'], l, xp, tm=tq_p, name="attn_out")
        xp, p_ffn = conv_ffn(xp, p, l, None, final_norm_w if last else None, n_seq=bp, nb=1, tm=tm_f,
                             tiles_per_seq=seq // tm_f)
        for lst, v in zip(outs_p, (p_conv_a, p_re, p_im, p_conv_c, p_ssd,
                                   p_ffn[seq // tm_f - 1::seq // tm_f])):
            lst.append(v)

        h_c, ya, yb, s_conv_a, s_re, s_im = mixers(xs, l, n_seq=1, nb=nbs, lt=lts, n_tiles=1, tm=tm_s,
                                                   sample=True)
        yc, s_conv_c, ssd_all = mamba_sample(h_c, p, l, st_conv_c, ssd_all, nb=nbs, lt=lts,
                                             bb=8 if l > 0 else 4, in_place=l > 0)
        xs, q = out_q_proj(ya, yb, yc, p['w_out'], xs, p['norm_xa_w'], p['wq'], l, tm=tm_s,
                           q_by_seq=(nbs, lts))
        o = attn_sample(q, cache_mem_k, cache_mem_v, l, bb=4)
        xs = attn_out_seq(o, p['wo'], l, xs, nb=nbs, lt=lts)
        xs, s_ffn = conv_ffn(xs, p, l, st_ffn, final_norm_w if last else None, n_seq=1, nb=nbs, tm=tm_s,
                             tiles_per_seq=1)
        for lst, v in zip(outs_s, (s_conv_a[0], s_re[0], s_im[0], s_conv_c, s_ffn[0])):
            lst.append(v)

    p_conv_a, p_re, p_im, p_conv_c, p_ssd, p_ffn = [jnp.stack(o) for o in outs_p]
    s_conv_a, s_re, s_im, s_conv_c, s_ffn = [jnp.stack(o) for o in outs_s]
    bmaj = lambda a, w: a.reshape(depth, w, nbs, a.shape[-1]).transpose(0, 2, 1, 3)
    y_prompt = xp.reshape(bp, seq, D_MODEL)
    y_sample = xs.reshape(lts, nbs, D_MODEL).transpose(1, 0, 2)
    return (y_prompt, y_sample,
            p_conv_a,
            p_re.reshape(depth, bp, S5_GROUPS, S5_STATE), p_im.reshape(depth, bp, S5_GROUPS, S5_STATE),
            p_conv_c,
            p_ssd.reshape(depth, bp, SSD_HEADS, SSD_HEAD_DIM, SSD_STATE),
            p_ffn[..., :D_FF],
            p_mk.reshape(depth, bp, n_mem, XA_HEADS, XA_HEAD_DIM),
            p_mv.reshape(depth, bp, n_mem, XA_HEADS, XA_HEAD_DIM),
            bmaj(s_conv_a, CONV_A_WIDTH - 1),
            s_re.reshape(depth, nbs, S5_GROUPS, S5_STATE), s_im.reshape(depth, nbs, S5_GROUPS, S5_STATE),
            bmaj(s_conv_c, SSD_CONV_WIDTH - 1),
            ssd_all.reshape(state_ssd.shape),
            bmaj(s_ffn, FFN_CONV_WIDTH - 1)[..., :D_FF])
```
